```python
import jax, jax.numpy as jnp
from jax import lax
import numpy as np

D_MODEL = 1024
BATCH = 8
SEQ = 2048
DEPTH = 4
DEC_BATCH = 8
DEC_SEQ = 32
PAST_LEN = 1024

CHUNK = 64
Q_BLOCK = 128
D_HEAD = 64
H_A = (D_MODEL // 2) // D_HEAD
B_WIDTH = D_MODEL // 2
G_B = 4
B_CHUNK = 128
H_C = (D_MODEL // 2) // D_HEAD
KV_C = max(1, H_C // 4)
H_IDX = 4
D_IDX = 64
TOPK_MAX = 256
D_D = D_MODEL // 2
CONV_W = 3
D_FF = 4 * D_MODEL
N_EVEN = (DEPTH + 1) // 2
N_ODD = DEPTH // 2
EPS = 1e-6
NEG = -1e30
EVEN_SPLITS = (H_A * D_HEAD, H_A * D_HEAD, H_A * D_HEAD, B_WIDTH, B_WIDTH)
ODD_SPLITS = (H_C * D_HEAD, KV_C * D_HEAD, KV_C * D_HEAD, H_IDX * D_IDX, D_IDX, H_IDX, D_D, D_D, D_D)
EVEN_IN = sum(EVEN_SPLITS)
ODD_IN = sum(ODD_SPLITS)
MIX_OUT = H_A * D_HEAD + B_WIDTH

kernel_name = 'hybrid_streaming_encoder_step'


def rms_norm(x, g):
    xf = x.astype(jnp.float32)
    y = xf * lax.rsqrt(jnp.mean(xf * xf, axis=-1, keepdims=True) + EPS)
    return (y * g.astype(jnp.float32)).astype(x.dtype)


def split_cols(z, sizes):
    offs = np.cumsum((0,) + tuple(sizes))
    return [z[..., int(offs[i]):int(offs[i + 1])] for i in range(len(sizes))]


def to_blocks(a):
    b, s = a.shape[:2]
    return jnp.moveaxis(a.reshape((b, s // Q_BLOCK, Q_BLOCK) + a.shape[2:]), 1, 0)


def from_blocks(a):
    nb, b, qb = a.shape[:3]
    return jnp.moveaxis(a, 0, 1).reshape((b, nb * qb) + a.shape[3:])


def stick_breaking(q, k, v, q_pos, k_pos):
    f32 = jnp.float32
    z = jnp.einsum('bqhd,bkhd->bhqk', q.astype(f32), k.astype(f32)) * (D_HEAD ** -0.5)
    mask = k_pos[None, :] < q_pos[:, None]
    log_stay = jnp.where(mask, jax.nn.log_sigmoid(-z), 0.0)
    csum = jnp.cumsum(log_stay, axis=-1)
    after = csum[..., -1:] - csum
    w = jnp.where(mask, jnp.exp(jax.nn.log_sigmoid(z) + after), 0.0)
    return jnp.einsum('bhqk,bkhd->bqhd', w.astype(v.dtype), v)


def dsa_attend(q, qi, wi, q_pos, k, v, ki, k_pos, top_k):
    f32 = jnp.float32
    qk_idx = jnp.einsum('bqhe,bke->bqhk', qi.astype(f32), ki.astype(f32)) * (D_IDX ** -0.5)
    score = jnp.einsum('bqhk,bqh->bqk', jax.nn.relu(qk_idx), wi.astype(f32)) * (H_IDX ** -0.5)
    admissible = (k_pos[None, :] // CHUNK) <= (q_pos[:, None] // CHUNK)
    score = jnp.where(admissible, score, NEG)
    _, idx = lax.top_k(score, top_k)
    gather = jax.vmap(lambda rows, ii: rows[ii])
    k_sel = gather(k, idx)
    v_sel = gather(v, idx)
    sel_ok = (k_pos[idx] // CHUNK) <= (q_pos[None, :, None] // CHUNK)
    b_, nq = q.shape[:2]
    qg = q.reshape(b_, nq, KV_C, H_C // KV_C, D_HEAD)
    logits = jnp.einsum('bqgrd,bqkgd->bqgrk', qg.astype(f32), k_sel.astype(f32)) * (D_HEAD ** -0.5)
    logits = jnp.where(sel_ok[:, :, None, None, :], logits, NEG)
    p = jax.nn.softmax(logits, axis=-1)
    out = jnp.einsum('bqgrk,bqkgd->bqgrd', p.astype(v.dtype), v_sel)
    return out.reshape(b_, nq, H_C, D_HEAD)


def spatial_gate(u, vb, ws, bs, fresh_partial):
    pos = jnp.arange(B_CHUNK)
    causal = (pos[None, :] // CHUNK) <= (pos[:, None] // CHUNK)
    w = ws * causal.astype(ws.dtype)
    b_, t = vb.shape[:2]
    cg = B_WIDTH // G_B
    if fresh_partial:
        vr = vb.reshape(b_, 1, t, G_B, cg)
        w = w[:, :t, :t]
        bias = bs[:, :t]
    else:
        vr = vb.reshape(b_, t // B_CHUNK, B_CHUNK, G_B, cg)
        bias = bs
    mixed = jnp.einsum('gpq,bnqgc->bnpgc', w, vr) + bias.T[:, :, None]
    return u * mixed.reshape(b_, t, B_WIDTH)


def short_conv(cin, conv_w, prev):
    b_, t = cin.shape[:2]
    if prev is None:
        prev = jnp.zeros((b_, CONV_W - 1, D_D), cin.dtype)
    padded = jnp.concatenate([prev, cin], axis=1)
    y = conv_w[0] * padded[:, 0:t]
    for j in range(1, CONV_W):
        y = y + conv_w[j] * padded[:, j:j + t]
    return y, padded[:, t:]


def even_mixer(h, w_in, gq, gk, gb, ws, bs, w_out, past_k=None, past_v=None):
    b_, t = h.shape[:2]
    q, k, v, u, vb = split_cols(h @ w_in, EVEN_SPLITS)
    q = rms_norm(q.reshape(b_, t, H_A, D_HEAD), gq)
    k = rms_norm(k.reshape(b_, t, H_A, D_HEAD), gk)
    v = v.reshape(b_, t, H_A, D_HEAD)
    if past_k is None:
        pos = jnp.arange(t)
        att = from_blocks(lax.map(lambda a: stick_breaking(a[0], k, v, a[1], pos),
                                  (to_blocks(q), pos.reshape(-1, Q_BLOCK))))
    else:
        p = past_k.shape[1]
        att = stick_breaking(q, jnp.concatenate([past_k, k], axis=1),
                             jnp.concatenate([past_v, v], axis=1),
                             p + jnp.arange(t), jnp.arange(p + t))
    u = jax.nn.gelu(u)
    vb = rms_norm(jax.nn.gelu(vb), gb)
    gated = spatial_gate(u, vb, ws, bs, past_k is not None)
    out = jnp.concatenate([att.reshape(b_, t, -1), gated], axis=-1) @ w_out
    return out, k, v, vb


def odd_mixer(h, w_in, gq, gk, conv_w, w_out, past_k=None, past_v=None, past_ki=None, past_conv=None):
    b_, t = h.shape[:2]
    q, k, v, qi, ki, wi, gate_b, gate_c, hd = split_cols(h @ w_in, ODD_SPLITS)
    q = rms_norm(q.reshape(b_, t, H_C, D_HEAD), gq)
    k = rms_norm(k.reshape(b_, t, KV_C, D_HEAD), gk)
    v = v.reshape(b_, t, KV_C, D_HEAD)
    qi = qi.reshape(b_, t, H_IDX, D_IDX)
    if past_k is None:
        pos = jnp.arange(t)
        top_k = min(TOPK_MAX, t // 4)
        att = from_blocks(lax.map(lambda a: dsa_attend(a[0], a[1], a[2], a[3], k, v, ki, pos, top_k),
                                  (to_blocks(q), to_blocks(qi), to_blocks(wi), pos.reshape(-1, Q_BLOCK))))
    else:
        p = past_k.shape[1]
        top_k = min(TOPK_MAX, (p + t) // 4)
        att = dsa_attend(q, qi, wi, p + jnp.arange(t),
                         jnp.concatenate([past_k, k], axis=1),
                         jnp.concatenate([past_v, v], axis=1),
                         jnp.concatenate([past_ki, ki], axis=1),
                         jnp.arange(p + t), top_k)
    conv, new_conv = short_conv(gate_c * hd, conv_w, past_conv)
    out = jnp.concatenate([att.reshape(b_, t, -1), gate_b * conv], axis=-1) @ w_out
    return out, k, v, ki, new_conv


def sq_relu_ffn(h, w1, w2):
    return jnp.square(jax.nn.relu(h @ w1)) @ w2


def setup_inputs(seed: int = 0) -> dict:
    key = jax.random.key(seed)
    ks = iter(jax.random.split(key, 32))

    def nrm(shape, scale=1.0):
        return jax.random.normal(next(ks), shape, jnp.float32) * scale

    def gain(shape):
        return 1.0 + 0.05 * nrm(shape)

    return {
        'x_prompt': nrm((BATCH, SEQ, D_MODEL)),
        'x_sample': nrm((DEC_BATCH, DEC_SEQ, D_MODEL)),
        'cache_a_k': nrm((N_EVEN, DEC_BATCH, PAST_LEN, H_A, D_HEAD)),
        'cache_a_v': nrm((N_EVEN, DEC_BATCH, PAST_LEN, H_A, D_HEAD)),
        'cache_c_k': nrm((N_ODD, DEC_BATCH, PAST_LEN, KV_C, D_HEAD)),
        'cache_c_v': nrm((N_ODD, DEC_BATCH, PAST_LEN, KV_C, D_HEAD)),
        'cache_c_kidx': nrm((N_ODD, DEC_BATCH, PAST_LEN, D_IDX)),
        'state_d_conv': nrm((N_ODD, DEC_BATCH, CONV_W - 1, D_D)),
        'norm_mix_g': gain((DEPTH, D_MODEL)),
        'norm_ffn_g': gain((DEPTH, D_MODEL)),
        'w_in_even': nrm((N_EVEN, D_MODEL, EVEN_IN), D_MODEL ** -0.5),
        'gq_a': gain((N_EVEN, D_HEAD)),
        'gk_a': gain((N_EVEN, D_HEAD)),
        'g_b': gain((N_EVEN, B_WIDTH)),
        'ws_b': nrm((N_EVEN, G_B, B_CHUNK, B_CHUNK), B_CHUNK ** -0.5),
        'bs_b': gain((N_EVEN, G_B, B_CHUNK)),
        'w_out_even': nrm((N_EVEN, MIX_OUT, D_MODEL), MIX_OUT ** -0.5),
        'w_in_odd': nrm((N_ODD, D_MODEL, ODD_IN), D_MODEL ** -0.5),
        'gq_c': gain((N_ODD, D_HEAD)),
        'gk_c': gain((N_ODD, D_HEAD)),
        'conv_w_d': nrm((N_ODD, CONV_W, D_D), CONV_W ** -0.5),
        'w_out_odd': nrm((N_ODD, MIX_OUT, D_MODEL), MIX_OUT ** -0.5),
        'w_ffn1': nrm((DEPTH, D_MODEL, D_FF), D_MODEL ** -0.5),
        'w_ffn2': nrm((DEPTH, D_FF, D_MODEL), D_FF ** -0.5),
    }


def reference(x_prompt, x_sample, cache_a_k, cache_a_v, cache_c_k, cache_c_v, cache_c_kidx, state_d_conv,
              norm_mix_g, norm_ffn_g, w_in_even, gq_a, gk_a, g_b, ws_b, bs_b, w_out_even,
              w_in_odd, gq_c, gk_c, conv_w_d, w_out_odd, w_ffn1, w_ffn2):
    xp, xs = x_prompt, x_sample
    ak_p, av_p, ak_s, av_s, bv_s = [], [], [], [], []
    ck_p, cv_p, ci_p, ck_s, cv_s, ci_s, dc_p, dc_s = [], [], [], [], [], [], [], []
    for i in range(DEPTH):
        j = i // 2
        hp = rms_norm(xp, norm_mix_g[i])
        hs = rms_norm(xs, norm_mix_g[i])
        if i % 2 == 0:
            lw = (w_in_even[j], gq_a[j], gk_a[j], g_b[j], ws_b[j], bs_b[j], w_out_even[j])
            mp, kp, vp, _ = even_mixer(hp, *lw)
            ms, ks_, vs_, vbs = even_mixer(hs, *lw, past_k=cache_a_k[j], past_v=cache_a_v[j])
            ak_p.append(kp); av_p.append(vp)
            ak_s.append(ks_); av_s.append(vs_); bv_s.append(vbs)
        else:
            lw = (w_in_odd[j], gq_c[j], gk_c[j], conv_w_d[j], w_out_odd[j])
            mp, kp, vp, kip, cvp = odd_mixer(hp, *lw)
            ms, ks_, vs_, kis, cvs = odd_mixer(hs, *lw, past_k=cache_c_k[j], past_v=cache_c_v[j],
                                               past_ki=cache_c_kidx[j], past_conv=state_d_conv[j])
            ck_p.append(kp); cv_p.append(vp); ci_p.append(kip); dc_p.append(cvp)
            ck_s.append(ks_); cv_s.append(vs_); ci_s.append(kis); dc_s.append(cvs)
        xp = xp + mp
        xs = xs + ms
        xp = xp + sq_relu_ffn(rms_norm(xp, norm_ffn_g[i]), w_ffn1[i], w_ffn2[i])
        xs = xs + sq_relu_ffn(rms_norm(xs, norm_ffn_g[i]), w_ffn1[i], w_ffn2[i])
    return (xp, xs,
            jnp.stack(ak_p), jnp.stack(av_p), jnp.stack(ak_s), jnp.stack(av_s),
            jnp.stack(bv_s),
            jnp.stack(ck_p), jnp.stack(cv_p), jnp.stack(ci_p),
            jnp.stack(ck_s), jnp.stack(cv_s), jnp.stack(ci_s),
            jnp.stack(dc_p), jnp.stack(dc_s))
```

```python
import functools

import numpy as np
import jax
import jax.numpy as jnp
from jax import lax
from jax.experimental import pallas as pl
from jax.experimental.pallas import tpu as pltpu

F32 = jnp.float32
BF16 = jnp.bfloat16

EPS = 1e-6
NEG = -1e30
CHUNK = 64
CHUNK_SHIFT = 6
D_HEAD = 64
G_B = 4
B_CHUNK = 128
H_IDX = 4
D_IDX = 64
TOPK_MAX = 256
CONV_W = 3
LANES = 128
KEY_TILE = 128
VMEM_LIMIT = 56 * 1024 * 1024


def _cparams(sem):
    return pltpu.CompilerParams(dimension_semantics=sem, vmem_limit_bytes=VMEM_LIMIT)


def _dot(a, b):
    return jnp.dot(a, b, preferred_element_type=F32)


def _dot_t(a, b):
    return lax.dot_general(a, b, (((1,), (1,)), ((), ())), preferred_element_type=F32)


def _split(x):
    hi = x.astype(BF16)
    lo = (x - hi.astype(F32)).astype(BF16)
    return hi, lo


def _rms(x, g):
    return x * lax.rsqrt(jnp.mean(x * x, axis=-1, keepdims=True) + EPS) * g


def _head_rms(x, hmean, g):
    hi, lo = _split(x * x)
    ms = _dot(hi, hmean) + _dot(lo, hmean)
    return x * lax.rsqrt(ms + EPS) * g


def _gelu(x):
    return jax.nn.gelu(x)


def _even_in_kernel(x_ref, g_ref, w_ref, hm_ref, gq_ref, gk_ref, gb_ref,
                    q_ref, k_ref, v_ref, u_ref, vb_ref):
    h = _rms(x_ref[...], g_ref[...]).astype(BF16)
    n = q_ref.shape[-1]

    def seg(i):
        return _dot(h, w_ref[:, i * n:(i + 1) * n])

    hm = hm_ref[...]
    q_ref[...] = _head_rms(seg(0), hm, gq_ref[...])
    k_ref[...] = _head_rms(seg(1), hm, gk_ref[...])
    v_ref[...] = seg(2)
    u_ref[...] = _gelu(seg(3))
    vb_ref[...] = _rms(_gelu(seg(4)), gb_ref[...])


def _even_in(x, g, w, hm, gq, gk, gb, tm):
    n_rows, d = x.shape
    n = w.shape[1] // 5
    row = lambda i: (i, 0)
    const = lambda i: (0, 0)
    out = jax.ShapeDtypeStruct((n_rows, n), F32)
    return pl.pallas_call(
        _even_in_kernel,
        grid=(n_rows // tm,),
        in_specs=[pl.BlockSpec((tm, d), row), pl.BlockSpec((1, d), const),
                  pl.BlockSpec(w.shape, const), pl.BlockSpec(hm.shape, const),
                  pl.BlockSpec((1, n), const), pl.BlockSpec((1, n), const), pl.BlockSpec((1, n), const)],
        out_specs=[pl.BlockSpec((tm, n), row)] * 5,
        out_shape=[out] * 5,
        compiler_params=_cparams(("parallel",)),
        name="even_in",
    )(x, g, w, hm, gq, gk, gb)


ODD_Q, ODD_K, ODD_V, ODD_QI, ODD_KI, ODD_WI, ODD_GB, ODD_GC, ODD_HD = (
    0, 512, 640, 768, 1024, 1152, 1280, 1792, 2304)
ODD_COLS = 2816


def _odd_in_kernel(x_ref, g_ref, w_ref, hm_ref, gq_ref, gk_ref,
                   q_ref, k_ref, v_ref, qi_ref, ki_ref, wi_ref, gb_ref, cin_ref):
    h = _rms(x_ref[...], g_ref[...]).astype(BF16)

    def seg(lo, width):
        return _dot(h, w_ref[:, lo:lo + width])

    hm = hm_ref[...]
    q_ref[...] = _head_rms(seg(ODD_Q, 512), hm, gq_ref[...])
    k_ref[...] = _head_rms(seg(ODD_K, 128), hm[:128, :128], gk_ref[...])
    v_ref[...] = seg(ODD_V, 128)
    qi_ref[...] = seg(ODD_QI, 256)
    ki_ref[...] = seg(ODD_KI, 128)
    wi_ref[...] = seg(ODD_WI, 128)
    gb_ref[...] = seg(ODD_GB, 512)
    cin_ref[...] = seg(ODD_GC, 512) * seg(ODD_HD, 512)


def _odd_in(x, g, w, hm, gq, gk, tm):
    n_rows, d = x.shape
    row = lambda i: (i, 0)
    const = lambda i: (0, 0)
    widths = (512, 128, 128, 256, 128, 128, 512, 512)
    return pl.pallas_call(
        _odd_in_kernel,
        grid=(n_rows // tm,),
        in_specs=[pl.BlockSpec((tm, d), row), pl.BlockSpec((1, d), const),
                  pl.BlockSpec(w.shape, const), pl.BlockSpec(hm.shape, const),
                  pl.BlockSpec((1, 512), const), pl.BlockSpec((1, 128), const)],
        out_specs=[pl.BlockSpec((tm, n), row) for n in widths],
        out_shape=[jax.ShapeDtypeStruct((n_rows, n), F32) for n in widths],
        compiler_params=_cparams(("parallel",)),
        name="odd_in",
    )(x, g, w, hm, gq, gk)


def _stick_kernel(q_ref, k_ref, v_ref, o_ref, *, tq, tk, q_pos0):
    q_start = q_pos0 + pl.program_id(2) * tq
    n_full = q_start // tk
    lane = lax.broadcasted_iota(jnp.int32, (1, LANES), 1)
    first = lane < D_HEAD
    q = q_ref[...] * (D_HEAD ** -0.5)
    q2 = jnp.concatenate([jnp.where(first, q, 0.0), jnp.where(first, 0.0, q)], axis=0).astype(BF16)
    rr = lax.broadcasted_iota(jnp.int32, (tk, tk), 0)
    cc = lax.broadcasted_iota(jnp.int32, (tk, tk), 1)
    later = jnp.where(rr > cc, 1.0, 0.0).astype(BF16)

    def block(kb, carry, acc, masked):
        ks = pl.multiple_of(kb * tk, tk)
        kblk = k_ref[pl.ds(ks, tk), :].astype(BF16)
        vblk = v_ref[pl.ds(ks, tk), :]
        z = _dot_t(q2, kblk)
        stay = -(jnp.maximum(z, 0.0) + jnp.log(1.0 + jnp.exp(-jnp.abs(z))))
        if masked:
            row = lax.broadcasted_iota(jnp.int32, (2 * tq, tk), 0)
            qpos = q_start + jnp.where(row >= tq, row - tq, row)
            kpos = ks + lax.broadcasted_iota(jnp.int32, (2 * tq, tk), 1)
            ok = kpos < qpos
            stay = jnp.where(ok, stay, 0.0)
        hi, lo = _split(stay)
        after = _dot(hi, later) + _dot(lo, later) + carry
        w = jnp.exp(z + stay + after)
        if masked:
            w = jnp.where(ok, w, 0.0)
        w = w.astype(BF16)
        v0 = jnp.where(first, vblk, 0.0).astype(BF16)
        v1 = jnp.where(first, 0.0, vblk).astype(BF16)
        acc = acc + _dot(w[:tq], v0) + _dot(w[tq:], v1)
        carry = carry + jnp.sum(stay, axis=1, keepdims=True)
        return carry, acc

    carry = jnp.zeros((2 * tq, 1), F32)
    acc = jnp.zeros((tq, LANES), F32)
    n_masked = max(1, tq // tk)
    for i in range(n_masked):
        carry, acc = block(n_full + (n_masked - 1 - i), carry, acc, True)

    def body(i, c):
        return block(n_full - 1 - i, c[0], c[1], False)

    carry, acc = lax.fori_loop(0, n_full, body, (carry, acc))
    o_ref[...] = acc


def _stick(q, k, v, tq, q_pos0):
    b, t, c = q.shape
    lp = k.shape[1]
    tk = KEY_TILE
    assert q_pos0 % tk == 0 and (tq % tk == 0 or tk % tq == 0) and lp % tk == 0
    assert q_pos0 + t <= lp
    kern = functools.partial(_stick_kernel, tq=tq, tk=tk, q_pos0=q_pos0)
    return pl.pallas_call(
        kern,
        grid=(b, c // LANES, t // tq),
        in_specs=[pl.BlockSpec((None, tq, LANES), lambda bi, h, i: (bi, i, h)),
                  pl.BlockSpec((None, lp, LANES), lambda bi, h, i: (bi, 0, h)),
                  pl.BlockSpec((None, lp, LANES), lambda bi, h, i: (bi, 0, h))],
        out_specs=pl.BlockSpec((None, tq, LANES), lambda bi, h, i: (bi, i, h)),
        out_shape=jax.ShapeDtypeStruct((b, t, c), F32),
        compiler_params=_cparams(("parallel", "parallel", "arbitrary")),
        name="stick_attn",
    )(q, k, v)


def _dsa_kernel(q_ref, qi_ref, wi_ref, k_ref, v_ref, ki_ref, o_ref, key_sc, bias_sc,
                *, tq, l_valid, q_pos0, top_k):
    lp = k_ref.shape[0]
    kt = KEY_TILE
    n_kt = lp // kt
    lane = lax.broadcasted_iota(jnp.int32, (1, LANES), 1)
    first = lane < D_HEAD
    q_start = q_pos0 + pl.program_id(1) * tq
    qpos = q_start + lax.broadcasted_iota(jnp.int32, (tq, 1), 0)
    qchunk = jnp.right_shift(qpos, CHUNK_SHIFT)

    def admissible(lo, width):
        kpos = lo + lax.broadcasted_iota(jnp.int32, (1, width), 1)
        return (jnp.right_shift(kpos, CHUNK_SHIFT) <= qchunk) & (kpos < l_valid)

    ki_hi, ki_lo = _split(ki_ref[...])
    qi = qi_ref[...]
    wi = wi_ref[...]
    score = jnp.zeros((tq, lp), F32)
    for h in range(H_IDX):
        grp = qi[:, (h // 2) * LANES:(h // 2 + 1) * LANES]
        qm = jnp.where(first, grp, 0.0) if h % 2 == 0 else jnp.where(first, 0.0, grp)
        q_hi, q_lo = _split(qm)
        s = _dot_t(q_hi, ki_hi) + (_dot_t(q_hi, ki_lo) + _dot_t(q_lo, ki_hi))
        score = score + jnp.maximum(s, 0.0) * wi[:, h:h + 1]
    score = score * ((D_IDX ** -0.5) * (H_IDX ** -0.5)) + 0.0
    score = jnp.where(admissible(0, lp), score, NEG)
    bits = lax.bitcast_convert_type(score, jnp.int32)
    key_sc[...] = jnp.where(bits < 0, bits ^ jnp.int32(0x7FFFFFFF), bits)

    def count_ge(trial):
        part = jnp.zeros((tq, kt), F32)
        for c in range(n_kt):
            part = part + jnp.where(key_sc[:, c * kt:(c + 1) * kt] >= trial, 1.0, 0.0)
        return jnp.sum(part, axis=1, keepdims=True)

    kf = jnp.float32(top_k)
    int_min = jnp.int32(-2 ** 31)
    thr = jnp.where(count_ge(jnp.zeros((tq, 1), jnp.int32)) >= kf, jnp.int32(0), int_min)

    def bit_step(i, t):
        trial = t + jnp.left_shift(jnp.int32(1), jnp.int32(30) - i)
        return jnp.where(count_ge(trial) >= kf, trial, t)

    thr = lax.fori_loop(0, 31, bit_step, thr)

    n_gt = jnp.zeros((tq, kt), F32)
    for c in range(n_kt):
        n_gt = n_gt + jnp.where(key_sc[:, c * kt:(c + 1) * kt] > thr, 1.0, 0.0)
    room = kf - jnp.sum(n_gt, axis=1, keepdims=True)
    rr = lax.broadcasted_iota(jnp.int32, (kt, kt), 0)
    cc = lax.broadcasted_iota(jnp.int32, (kt, kt), 1)
    before = jnp.where(rr < cc, 1.0, 0.0).astype(BF16)
    run = jnp.zeros((tq, 1), F32)
    for c in range(n_kt):
        keyc = key_sc[:, c * kt:(c + 1) * kt]
        eqf = jnp.where(keyc == thr, 1.0, 0.0)
        rank = _dot(eqf.astype(BF16), before) + run
        admc = admissible(c * kt, kt)
        tie_ok = jnp.where(keyc == thr, jnp.where(rank < room, 0.0, NEG), NEG)
        sel = jnp.where(keyc > thr, 0.0, tie_ok)
        bias_sc[:, c * kt:(c + 1) * kt] = jnp.where(admc, sel, NEG)
        run = run + jnp.sum(eqf, axis=1, keepdims=True)

    kb = k_ref[...].astype(BF16)
    vf = v_ref[...]
    q = q_ref[...] * (D_HEAD ** -0.5)
    n_rep = q.shape[1] // LANES
    bias = bias_sc[...]
    out = jnp.zeros((n_rep * tq, LANES), F32)
    for g in range(2):
        keep = first if g == 0 else jnp.logical_not(first)
        vg = jnp.where(keep, vf, 0.0).astype(BF16)
        qs = jnp.concatenate(
            [jnp.where(keep, q[:, r * LANES:(r + 1) * LANES], 0.0) for r in range(n_rep)], axis=0).astype(BF16)
        logits = _dot_t(qs, kb)
        logits = (logits.reshape(n_rep, tq, lp) + bias[None]).reshape(n_rep * tq, lp)
        m = jnp.max(logits, axis=1, keepdims=True)
        p = jnp.exp(logits - m)
        den = jnp.sum(p, axis=1, keepdims=True)
        out = out + _dot(p.astype(BF16), vg) / den
    for r in range(n_rep):
        o_ref[:, r * LANES:(r + 1) * LANES] = out[r * tq:(r + 1) * tq]


def _dsa(q, qi, wi, k, v, ki, tq, q_pos0, l_valid):
    b, t, c = q.shape
    lp = k.shape[1]
    assert lp % KEY_TILE == 0 and l_valid <= lp and t % tq == 0
    top_k = min(TOPK_MAX, l_valid // 4)
    kern = functools.partial(_dsa_kernel, tq=tq, l_valid=l_valid, q_pos0=q_pos0, top_k=top_k)
    qmap = lambda bi, i: (bi, i, 0)
    kmap = lambda bi, i: (bi, 0, 0)
    return pl.pallas_call(
        kern,
        grid=(b, t // tq),
        in_specs=[pl.BlockSpec((None, tq, c), qmap), pl.BlockSpec((None, tq, qi.shape[2]), qmap),
                  pl.BlockSpec((None, tq, LANES), qmap),
                  pl.BlockSpec((None, lp, LANES), kmap), pl.BlockSpec((None, lp, LANES), kmap),
                  pl.BlockSpec((None, lp, LANES), kmap)],
        out_specs=pl.BlockSpec((None, tq, c), qmap),
        out_shape=jax.ShapeDtypeStruct((b, t, c), F32),
        scratch_shapes=[pltpu.VMEM((tq, lp), jnp.int32), pltpu.VMEM((tq, lp), F32)],
        compiler_params=_cparams(("parallel", "arbitrary")),
        name="dsa_attn",
    )(q, qi, wi, k, v, ki)


def _even_out_kernel(x_ref, att_ref, u_ref, vb_ref, ws_ref, bst_ref, w_ref, o_ref, *, p_len):
    tm = x_ref.shape[0]
    cg = vb_ref.shape[1] // G_B
    rr = lax.broadcasted_iota(jnp.int32, (p_len, p_len), 0)
    cc = lax.broadcasted_iota(jnp.int32, (p_len, p_len), 1)
    causal = jnp.right_shift(cc, CHUNK_SHIFT) <= jnp.right_shift(rr, CHUNK_SHIFT)
    u = u_ref[...]
    vb = vb_ref[...].astype(BF16)
    bst = bst_ref[...]
    mixed_rows = []
    for c in range(tm // p_len):
        cols = []
        for g in range(G_B):
            wg = jnp.where(causal, ws_ref[g], 0.0).astype(BF16)
            mg = _dot(wg, vb[c * p_len:(c + 1) * p_len, g * cg:(g + 1) * cg]) + bst[:, g:g + 1]
            cols.append(mg)
        mixed_rows.append(jnp.concatenate(cols, axis=1))
    gated = u * jnp.concatenate(mixed_rows, axis=0)
    na = att_ref.shape[1]
    y = _dot(att_ref[...].astype(BF16), w_ref[:na, :]) + _dot(gated.astype(BF16), w_ref[na:, :])
    o_ref[...] = x_ref[...] + y


def _even_out(x, att, u, vb, ws, bst, w, tm, p_len):
    n_rows, d = x.shape
    n = att.shape[1]
    row = lambda i: (i, 0)
    kern = functools.partial(_even_out_kernel, p_len=p_len)
    return pl.pallas_call(
        kern,
        grid=(n_rows // tm,),
        in_specs=[pl.BlockSpec((tm, d), row), pl.BlockSpec((tm, n), row), pl.BlockSpec((tm, n), row),
                  pl.BlockSpec((tm, n), row), pl.BlockSpec(ws.shape, lambda i: (0, 0, 0)),
                  pl.BlockSpec(bst.shape, lambda i: (0, 0)), pl.BlockSpec(w.shape, lambda i: (0, 0))],
        out_specs=pl.BlockSpec((tm, d), row),
        out_shape=jax.ShapeDtypeStruct((n_rows, d), F32),
        compiler_params=_cparams(("parallel",)),
        name="even_out",
    )(x, att, u, vb, ws, bst, w)


def _odd_out_kernel(x_ref, att_ref, gb_ref, cin_ref, prev_ref, cw_ref, w_ref, o_ref, *, t_len, has_prev):
    tm = x_ref.shape[0]
    cin = cin_ref[...]
    prev = prev_ref[...]
    if not has_prev:
        start = (pl.program_id(0) * tm) % t_len == 0
        prev = jnp.where(start, 0.0, prev)
    ext = jnp.concatenate([prev, cin], axis=0)
    cw = cw_ref[...]
    conv = cw[CONV_W - 1:CONV_W] * cin
    for j in range(1, CONV_W):
        conv = conv + cw[CONV_W - 1 - j:CONV_W - j] * pltpu.roll(ext, j, 0)[8:]
    gated = gb_ref[...] * conv
    na = att_ref.shape[1]
    y = _dot(att_ref[...].astype(BF16), w_ref[:na, :]) + _dot(gated.astype(BF16), w_ref[na:, :])
    o_ref[...] = x_ref[...] + y


def _odd_out(x, att, gb, cin, prev, cw, w, tm, t_len):
    n_rows, d = x.shape
    n = att.shape[1]
    row = lambda i: (i, 0)
    has_prev = prev is not None
    if has_prev:
        assert tm == t_len
        prev_arr, prev_map = prev, row
    else:
        assert t_len % tm == 0 and tm % 8 == 0
        prev_arr, prev_map = cin, (lambda i: (jnp.maximum(i * (tm // 8) - 1, 0), 0))
    kern = functools.partial(_odd_out_kernel, t_len=t_len, has_prev=has_prev)
    return pl.pallas_call(
        kern,
        grid=(n_rows // tm,),
        in_specs=[pl.BlockSpec((tm, d), row), pl.BlockSpec((tm, n), row), pl.BlockSpec((tm, n), row),
                  pl.BlockSpec((tm, n), row), pl.BlockSpec((8, n), prev_map),
                  pl.BlockSpec(cw.shape, lambda i: (0, 0)), pl.BlockSpec(w.shape, lambda i: (0, 0))],
        out_specs=pl.BlockSpec((tm, d), row),
        out_shape=jax.ShapeDtypeStruct((n_rows, d), F32),
        compiler_params=_cparams(("parallel",)),
        name="odd_out",
    )(x, att, gb, cin, prev_arr, cw, w)


def _ffn_kernel(x_ref, g_ref, w1_ref, w2_ref, o_ref, h_sc, acc_sc):
    j = pl.program_id(1)

    @pl.when(j == 0)
    def _():
        h_sc[...] = _rms(x_ref[...], g_ref[...]).astype(BF16)
        acc_sc[...] = jnp.zeros_like(acc_sc)

    a = jnp.square(jnp.maximum(_dot(h_sc[...], w1_ref[...]), 0.0))
    acc_sc[...] += _dot(a.astype(BF16), w2_ref[...])

    @pl.when(j == pl.num_programs(1) - 1)
    def _():
        o_ref[...] = x_ref[...] + acc_sc[...]


def _ffn(x, g, w1, w2, tm, tf):
    n_rows, d = x.shape
    dff = w1.shape[1]
    return pl.pallas_call(
        _ffn_kernel,
        grid=(n_rows // tm, dff // tf),
        in_specs=[pl.BlockSpec((tm, d), lambda i, j: (i, 0)), pl.BlockSpec((1, d), lambda i, j: (0, 0)),
                  pl.BlockSpec((d, tf), lambda i, j: (0, j)), pl.BlockSpec((tf, d), lambda i, j: (j, 0))],
        out_specs=pl.BlockSpec((tm, d), lambda i, j: (i, 0)),
        out_shape=jax.ShapeDtypeStruct((n_rows, d), F32),
        scratch_shapes=[pltpu.VMEM((tm, d), BF16), pltpu.VMEM((tm, d), F32)],
        compiler_params=_cparams(("parallel", "arbitrary")),
        name="ffn",
    )(x, g, w1, w2)


def _round_up(n, m):
    return (n + m - 1) // m * m


def _row_tile(n_rows, cap):
    t = min(cap, n_rows)
    while n_rows % t:
        t //= 2
    return t


def _odd_weight_layout(n_q):
    n_heads = n_q // D_HEAD
    rep = n_heads // 2
    perm = np.zeros((n_q,), np.int32)
    for r in range(rep):
        for g in range(2):
            for dd in range(D_HEAD):
                perm[r * LANES + g * D_HEAD + dd] = (g * rep + r) * D_HEAD + dd
    return perm


def kernel(x_prompt, x_sample, cache_a_k, cache_a_v, cache_c_k, cache_c_v, cache_c_kidx, state_d_conv,
           norm_mix_g, norm_ffn_g, w_in_even, gq_a, gk_a, g_b, ws_b, bs_b, w_out_even,
           w_in_odd, gq_c, gk_c, conv_w_d, w_out_odd, w_ffn1, w_ffn2):
    bp, tp, d = x_prompt.shape
    bs, ts, _ = x_sample.shape
    depth = norm_mix_g.shape[0]
    past = cache_a_k.shape[2]
    n_a = cache_a_k.shape[3] * cache_a_k.shape[4]
    n_kv = cache_c_k.shape[3] * cache_c_k.shape[4]
    n_b = g_b.shape[1]
    n_d = conv_w_d.shape[2]
    n_q = w_out_odd.shape[1] - n_d
    assert n_a == 512 and n_b == 512 and n_kv == 128 and n_q == 512 and n_d == 512 and d == 1024

    xp = x_prompt.reshape(bp * tp, d)
    xs = x_sample.reshape(bs * ts, d)
    tm_p = _row_tile(bp * tp, 512)
    tm_s = _row_tile(bs * ts, 512)
    ls = past + ts
    ls_pad = _round_up(ls, KEY_TILE)

    hmean = jnp.asarray(np.kron(np.eye(n_a // D_HEAD), np.full((D_HEAD, D_HEAD), 1.0 / D_HEAD)), BF16)
    perm = _odd_weight_layout(n_q)
    tile8 = lambda gvec: jnp.tile(gvec, n_a // D_HEAD).reshape(1, n_a)

    ak_p, av_p, ak_s, av_s, bv_s = [], [], [], [], []
    ck_p, cv_p, ci_p, ck_s, cv_s, ci_s, dc_p, dc_s = [], [], [], [], [], [], [], []

    def pad_keys(past_rows, new_rows):
        width = new_rows.shape[-1]
        zeros = jnp.zeros((bs, ls_pad - ls, width), F32)
        return jnp.concatenate([past_rows.reshape(bs, past, width), new_rows.reshape(bs, ts, width), zeros], axis=1)

    for i in range(depth):
        j = i // 2
        g_mix = norm_mix_g[i].reshape(1, d)
        if i % 2 == 0:
            w_in = w_in_even[j].astype(BF16)
            w_out = w_out_even[j].astype(BF16)
            gq, gk, gb = tile8(gq_a[j]), tile8(gk_a[j]), g_b[j].reshape(1, n_b)
            q, k, v, u, vb = _even_in(xp, g_mix, w_in, hmean, gq, gk, gb, tm_p)
            att = _stick(q.reshape(bp, tp, n_a), k.reshape(bp, tp, n_a), v.reshape(bp, tp, n_a), KEY_TILE, 0)
            xp = _even_out(xp, att.reshape(bp * tp, n_a), u, vb, ws_b[j], bs_b[j].T, w_out, tm_p, B_CHUNK)
            ak_p.append(k.reshape(bp, tp, n_a // D_HEAD, D_HEAD))
            av_p.append(v.reshape(bp, tp, n_a // D_HEAD, D_HEAD))
            q, k, v, u, vb = _even_in(xs, g_mix, w_in, hmean, gq, gk, gb, tm_s)
            att = _stick(q.reshape(bs, ts, n_a), pad_keys(cache_a_k[j], k), pad_keys(cache_a_v[j], v), ts, past)
            xs = _even_out(xs, att.reshape(bs * ts, n_a), u, vb, ws_b[j][:, :ts, :ts], bs_b[j][:, :ts].T,
                           w_out, tm_s, ts)
            ak_s.append(k.reshape(bs, ts, n_a // D_HEAD, D_HEAD))
            av_s.append(v.reshape(bs, ts, n_a // D_HEAD, D_HEAD))
            bv_s.append(vb.reshape(bs, ts, n_b))
        else:
            wi_full = w_in_odd[j]
            offs = np.cumsum([0, n_q, n_kv, n_kv, H_IDX * D_IDX, D_IDX, H_IDX, n_d, n_d, n_d])
            col = lambda s: wi_full[:, int(offs[s]):int(offs[s + 1])]
            w_in = jnp.concatenate(
                [col(0)[:, perm], col(1), col(2), col(3), col(4), col(4), col(5),
                 jnp.zeros((d, LANES - H_IDX), F32), col(6), col(7), col(8)], axis=1).astype(BF16)
            assert w_in.shape[1] == ODD_COLS
            w_out = jnp.concatenate([w_out_odd[j][:n_q][perm], w_out_odd[j][n_q:]], axis=0).astype(BF16)
            gq, gk = tile8(gq_c[j]), jnp.tile(gk_c[j], n_kv // D_HEAD).reshape(1, n_kv)
            cw = conv_w_d[j]
            q, k, v, qi, ki, wi, gbt, cin = _odd_in(xp, g_mix, w_in, hmean, gq, gk, tm_p)
            r3 = lambda a: a.reshape(bp, tp, a.shape[-1])
            att = _dsa(r3(q), r3(qi), r3(wi), r3(k), r3(v), r3(ki), KEY_TILE, 0, tp)
            xp = _odd_out(xp, att.reshape(bp * tp, n_q), gbt, cin, None, cw, w_out, tm_p, tp)
            ck_p.append(k.reshape(bp, tp, n_kv // D_HEAD, D_HEAD))
            cv_p.append(v.reshape(bp, tp, n_kv // D_HEAD, D_HEAD))
            ci_p.append(ki[:, :D_IDX].reshape(bp, tp, D_IDX))
            dc_p.append(cin.reshape(bp, tp, n_d)[:, tp - (CONV_W - 1):])
            q, k, v, qi, ki, wi, gbt, cin = _odd_in(xs, g_mix, w_in, hmean, gq, gk, tm_s)
            r3 = lambda a: a.reshape(bs, ts, a.shape[-1])
            past_ki = jnp.concatenate([cache_c_kidx[j], cache_c_kidx[j]], axis=-1)
            att = _dsa(r3(q), r3(qi), r3(wi), pad_keys(cache_c_k[j], k), pad_keys(cache_c_v[j], v),
                       pad_keys(past_ki, ki), ts, past, ls)
            prev = jnp.concatenate([jnp.zeros((bs, 8 - (CONV_W - 1), n_d), F32), state_d_conv[j]], axis=1)
            xs = _odd_out(xs, att.reshape(bs * ts, n_q), gbt, cin, prev.reshape(bs * 8, n_d), cw, w_out, ts, ts)
            ck_s.append(k.reshape(bs, ts, n_kv // D_HEAD, D_HEAD))
            cv_s.append(v.reshape(bs, ts, n_kv // D_HEAD, D_HEAD))
            ci_s.append(ki[:, :D_IDX].reshape(bs, ts, D_IDX))
            dc_s.append(cin.reshape(bs, ts, n_d)[:, ts - (CONV_W - 1):])
        g_ffn = norm_ffn_g[i].reshape(1, d)
        w1 = w_ffn1[i].astype(BF16)
        w2 = w_ffn2[i].astype(BF16)
        xp = _ffn(xp, g_ffn, w1, w2, _row_tile(bp * tp, 1024), 512)
        xs = _ffn(xs, g_ffn, w1, w2, _row_tile(bs * ts, 1024), 512)

    return (xp.reshape(bp, tp, d), xs.reshape(bs, ts, d),
            jnp.stack(ak_p), jnp.stack(av_p), jnp.stack(ak_s), jnp.stack(av_s),
            jnp.stack(bv_s),
            jnp.stack(ck_p), jnp.stack(cv_p), jnp.stack(ci_p),
            jnp.stack(ck_s), jnp.stack(cv_s), jnp.stack(ci_s),
            jnp.stack(dc_p), jnp.stack(dc_s))
```

```python
import functools

import numpy as np
import jax
import jax.numpy as jnp
from jax import lax
from jax.experimental import pallas as pl
from jax.experimental.pallas import tpu as pltpu

F32 = jnp.float32
BF16 = jnp.bfloat16

EPS = 1e-6
NEG = -1e30
CHUNK = 64
CHUNK_SHIFT = 6
D_HEAD = 64
G_B = 4
B_CHUNK = 128
H_IDX = 4
D_IDX = 64
TOPK_MAX = 256
CONV_W = 3
LANES = 128
KEY_TILE = 128
VMEM_LIMIT = 56 * 1024 * 1024


def _cparams(sem):
    return pltpu.CompilerParams(dimension_semantics=sem, vmem_limit_bytes=VMEM_LIMIT)


def _dot(a, b):
    return jnp.dot(a, b, preferred_element_type=F32)


def _dot_t(a, b):
    return lax.dot_general(a, b, (((1,), (1,)), ((), ())), preferred_element_type=F32)


def _split(x):
    hi = x.astype(BF16)
    lo = (x - hi.astype(F32)).astype(BF16)
    return hi, lo


def _rms(x, g):
    return x * lax.rsqrt(jnp.mean(x * x, axis=-1, keepdims=True) + EPS) * g


def _head_rms(x, hmean, g):
    hi, lo = _split(x * x)
    ms = _dot(hi, hmean) + _dot(lo, hmean)
    return x * lax.rsqrt(ms + EPS) * g


def _gelu(x):
    return jax.nn.gelu(x)


def _even_in_kernel(x_ref, g_ref, w_ref, hm_ref, gq_ref, gk_ref, gb_ref,
                    q_ref, k_ref, v_ref, u_ref, vb_ref, kb_ref, vm_ref):
    h = _rms(x_ref[...], g_ref[...]).astype(BF16)
    n = q_ref.shape[-1]

    def seg(i):
        return _dot(h, w_ref[:, i * n:(i + 1) * n])

    hm = hm_ref[...]
    q_ref[...] = _head_rms(seg(0), hm, gq_ref[...])
    k = _head_rms(seg(1), hm, gk_ref[...])
    k_ref[...] = k
    kb_ref[...] = k.astype(BF16)
    v = seg(2)
    v_ref[...] = v
    first = lax.broadcasted_iota(jnp.int32, (1, LANES), 1) < D_HEAD
    for p in range(n // LANES):
        vp = v[:, p * LANES:(p + 1) * LANES]
        vm_ref[:, 2 * p * LANES:(2 * p + 1) * LANES] = jnp.where(first, vp, 0.0).astype(BF16)
        vm_ref[:, (2 * p + 1) * LANES:(2 * p + 2) * LANES] = jnp.where(first, 0.0, vp).astype(BF16)
    u_ref[...] = _gelu(seg(3))
    vb_ref[...] = _rms(_gelu(seg(4)), gb_ref[...])


def _even_in(x, g, w, hm, gq, gk, gb, tm):
    n_rows, d = x.shape
    n = w.shape[1] // 5
    row = lambda i: (i, 0)
    const = lambda i: (0, 0)
    out = jax.ShapeDtypeStruct((n_rows, n), F32)
    return pl.pallas_call(
        _even_in_kernel,
        grid=(n_rows // tm,),
        in_specs=[pl.BlockSpec((tm, d), row), pl.BlockSpec((1, d), const),
                  pl.BlockSpec(w.shape, const), pl.BlockSpec(hm.shape, const),
                  pl.BlockSpec((1, n), const), pl.BlockSpec((1, n), const), pl.BlockSpec((1, n), const)],
        out_specs=[pl.BlockSpec((tm, n), row)] * 6 + [pl.BlockSpec((tm, 2 * n), row)],
        out_shape=[out] * 5 + [jax.ShapeDtypeStruct((n_rows, n), BF16), jax.ShapeDtypeStruct((n_rows, 2 * n), BF16)],
        compiler_params=_cparams(("parallel",)),
        name="even_in",
    )(x, g, w, hm, gq, gk, gb)


ODD_Q, ODD_K, ODD_V, ODD_QI, ODD_KI, ODD_WI, ODD_GB, ODD_GC, ODD_HD = (
    0, 512, 640, 768, 1024, 1152, 1280, 1792, 2304)
ODD_COLS = 2816


def _odd_in_kernel(x_ref, g_ref, w_ref, hm_ref, gq_ref, gk_ref,
                   q_ref, k_ref, v_ref, qi_ref, ki_ref, wi_ref, gb_ref, cin_ref):
    h = _rms(x_ref[...], g_ref[...]).astype(BF16)

    def seg(lo, width):
        return _dot(h, w_ref[:, lo:lo + width])

    hm = hm_ref[...]
    q_ref[...] = _head_rms(seg(ODD_Q, 512), hm, gq_ref[...])
    k_ref[...] = _head_rms(seg(ODD_K, 128), hm[:128, :128], gk_ref[...])
    v_ref[...] = seg(ODD_V, 128)
    qi_ref[...] = seg(ODD_QI, 256)
    ki_ref[...] = seg(ODD_KI, 128)
    wi_ref[...] = seg(ODD_WI, 128)
    gb_ref[...] = seg(ODD_GB, 512)
    cin_ref[...] = seg(ODD_GC, 512) * seg(ODD_HD, 512)


def _odd_in(x, g, w, hm, gq, gk, tm):
    n_rows, d = x.shape
    row = lambda i: (i, 0)
    const = lambda i: (0, 0)
    widths = (512, 128, 128, 256, 128, 128, 512, 512)
    return pl.pallas_call(
        _odd_in_kernel,
        grid=(n_rows // tm,),
        in_specs=[pl.BlockSpec((tm, d), row), pl.BlockSpec((1, d), const),
                  pl.BlockSpec(w.shape, const), pl.BlockSpec(hm.shape, const),
                  pl.BlockSpec((1, 512), const), pl.BlockSpec((1, 128), const)],
        out_specs=[pl.BlockSpec((tm, n), row) for n in widths],
        out_shape=[jax.ShapeDtypeStruct((n_rows, n), F32) for n in widths],
        compiler_params=_cparams(("parallel",)),
        name="odd_in",
    )(x, g, w, hm, gq, gk)


STICK_TK = 256


def _stick_tri():
    half = STICK_TK // 2
    later = (np.arange(half)[:, None] > np.arange(half)[None, :]).astype(np.float32)
    blk = np.concatenate([later, np.ones((half, half), np.float32)], axis=1)
    return jnp.asarray(np.concatenate([blk, blk], axis=0), BF16)


def _stick_kernel(q_ref, k_ref, v_ref, tri_ref, o_ref, carry_sc, acc_sc, *, tq, q_pos0):
    tk = STICK_TK
    half = tk // 2
    n_pairs = q_ref.shape[1] // LANES
    v_split = v_ref.shape[1] == 2 * q_ref.shape[1]
    q_start = q_pos0 + pl.program_id(1) * tq
    n_full = q_start // tk
    lane = lax.broadcasted_iota(jnp.int32, (1, LANES), 1)
    first = lane < D_HEAD
    q = q_ref[...] * (D_HEAD ** -0.5)
    q2 = []
    for p in range(n_pairs):
        qp = q[:, p * LANES:(p + 1) * LANES]
        q2.append(jnp.concatenate([jnp.where(first, qp, 0.0), jnp.where(first, 0.0, qp)], axis=0).astype(BF16))
    tri = tri_ref[...]
    carry_sc[...] = jnp.zeros_like(carry_sc)
    acc_sc[...] = jnp.zeros_like(acc_sc)

    def block(kb, masked):
        ks = pl.multiple_of(kb * tk, tk)
        if masked:
            row = lax.broadcasted_iota(jnp.int32, (2 * tq, tk), 0)
            qpos = q_start + jnp.where(row >= tq, row - tq, row)
            kpos = ks + lax.broadcasted_iota(jnp.int32, (2 * tq, tk), 1)
            ok = kpos < qpos
        for p in range(n_pairs):
            cols = slice(p * LANES, (p + 1) * LANES)
            z = _dot_t(q2[p], k_ref[pl.ds(ks, tk), cols].astype(BF16))
            nz = -z
            stay = jnp.minimum(nz, 0.0) - jnp.log(1.0 + jnp.exp(jnp.minimum(z, nz)))
            if masked:
                stay = jnp.where(ok, stay, 0.0)
            carry = carry_sc[p]
            after = [None, None]
            for h in (1, 0):
                hi, lo = _split(stay[:, h * half:(h + 1) * half])
                r = _dot(jnp.concatenate([hi, lo], axis=1), tri)
                after[h] = r[:, :half] + carry
                carry = carry + r[:, half:]
            carry_sc[p] = carry
            w = jnp.exp(z + stay + jnp.concatenate(after, axis=1))
            if masked:
                w = jnp.where(ok, w, 0.0)
            w = w.astype(BF16)
            if v_split:
                v0 = v_ref[pl.ds(ks, tk), 2 * p * LANES:(2 * p + 1) * LANES]
                v1 = v_ref[pl.ds(ks, tk), (2 * p + 1) * LANES:(2 * p + 2) * LANES]
            else:
                vblk = v_ref[pl.ds(ks, tk), cols]
                v0 = jnp.where(first, vblk, 0.0).astype(BF16)
                v1 = jnp.where(first, 0.0, vblk).astype(BF16)
            acc_sc[p] += _dot(jnp.concatenate([w[:tq], w[tq:]], axis=1), jnp.concatenate([v0, v1], axis=0))

    block(n_full, True)

    def body(i, c):
        block(n_full - 1 - i, False)
        return c

    lax.fori_loop(0, n_full, body, 0)
    for p in range(n_pairs):
        o_ref[:, p * LANES:(p + 1) * LANES] = acc_sc[p]


def _stick(q, k, v, tri, tq, q_pos0):
    b, t, c = q.shape
    lp = k.shape[1]
    tk = STICK_TK
    assert lp % tk == 0 and q_pos0 % tq == 0 and tk % tq == 0 and q_pos0 + t <= lp
    kern = functools.partial(_stick_kernel, tq=tq, q_pos0=q_pos0)
    kmap = lambda bi, i: (bi, 0, 0)
    return pl.pallas_call(
        kern,
        grid=(b, t // tq),
        in_specs=[pl.BlockSpec((None, tq, c), lambda bi, i: (bi, i, 0)),
                  pl.BlockSpec((None, lp, c), kmap),
                  pl.BlockSpec((None, lp, v.shape[2]), kmap),
                  pl.BlockSpec(tri.shape, lambda bi, i: (0, 0))],
        out_specs=pl.BlockSpec((None, tq, c), lambda bi, i: (bi, i, 0)),
        out_shape=jax.ShapeDtypeStruct((b, t, c), F32),
        scratch_shapes=[pltpu.VMEM((c // LANES, 2 * tq, LANES), F32), pltpu.VMEM((c // LANES, tq, LANES), F32)],
        compiler_params=_cparams(("parallel", "arbitrary")),
        name="stick_attn",
    )(q, k, v, tri)


def _dsa_kernel(q_ref, qi_ref, wi_ref, k_ref, v_ref, ki_ref, o_ref, key_sc, bias_sc,
                *, tq, l_valid, q_pos0, top_k):
    lp = k_ref.shape[0]
    kt = KEY_TILE
    n_kt = lp // kt
    lane = lax.broadcasted_iota(jnp.int32, (1, LANES), 1)
    first = lane < D_HEAD
    q_start = q_pos0 + pl.program_id(1) * tq
    qpos = q_start + lax.broadcasted_iota(jnp.int32, (tq, 1), 0)
    qchunk = jnp.right_shift(qpos, CHUNK_SHIFT)

    def admissible(lo, width):
        kpos = lo + lax.broadcasted_iota(jnp.int32, (1, width), 1)
        return (jnp.right_shift(kpos, CHUNK_SHIFT) <= qchunk) & (kpos < l_valid)

    ki_hi, ki_lo = _split(ki_ref[...])
    qi = qi_ref[...]
    wi = wi_ref[...]
    score = jnp.zeros((tq, lp), F32)
    for h in range(H_IDX):
        grp = qi[:, (h // 2) * LANES:(h // 2 + 1) * LANES]
        qm = jnp.where(first, grp, 0.0) if h % 2 == 0 else jnp.where(first, 0.0, grp)
        q_hi, q_lo = _split(qm)
        s = _dot_t(q_hi, ki_hi) + (_dot_t(q_hi, ki_lo) + _dot_t(q_lo, ki_hi))
        score = score + jnp.maximum(s, 0.0) * wi[:, h:h + 1]
    score = score * ((D_IDX ** -0.5) * (H_IDX ** -0.5)) + 0.0
    score = jnp.where(admissible(0, lp), score, NEG)
    bits = lax.bitcast_convert_type(score, jnp.int32)
    key_sc[...] = jnp.where(bits < 0, bits ^ jnp.int32(0x7FFFFFFF), bits)

    def count_ge(trial):
        part = jnp.zeros((tq, kt), F32)
        for c in range(n_kt):
            part = part + jnp.where(key_sc[:, c * kt:(c + 1) * kt] >= trial, 1.0, 0.0)
        return jnp.sum(part, axis=1, keepdims=True)

    kf = jnp.float32(top_k)
    int_min = jnp.int32(-2 ** 31)
    thr = jnp.where(count_ge(jnp.zeros((tq, 1), jnp.int32)) >= kf, jnp.int32(0), int_min)

    def bit_step(i, t):
        trial = t + jnp.left_shift(jnp.int32(1), jnp.int32(30) - i)
        return jnp.where(count_ge(trial) >= kf, trial, t)

    thr = lax.fori_loop(0, 31, bit_step, thr)

    n_gt = jnp.zeros((tq, kt), F32)
    for c in range(n_kt):
        n_gt = n_gt + jnp.where(key_sc[:, c * kt:(c + 1) * kt] > thr, 1.0, 0.0)
    room = kf - jnp.sum(n_gt, axis=1, keepdims=True)
    rr = lax.broadcasted_iota(jnp.int32, (kt, kt), 0)
    cc = lax.broadcasted_iota(jnp.int32, (kt, kt), 1)
    before = jnp.where(rr < cc, 1.0, 0.0).astype(BF16)
    run = jnp.zeros((tq, 1), F32)
    for c in range(n_kt):
        keyc = key_sc[:, c * kt:(c + 1) * kt]
        eqf = jnp.where(keyc == thr, 1.0, 0.0)
        rank = _dot(eqf.astype(BF16), before) + run
        admc = admissible(c * kt, kt)
        tie_ok = jnp.where(keyc == thr, jnp.where(rank < room, 0.0, NEG), NEG)
        sel = jnp.where(keyc > thr, 0.0, tie_ok)
        bias_sc[:, c * kt:(c + 1) * kt] = jnp.where(admc, sel, NEG)
        run = run + jnp.sum(eqf, axis=1, keepdims=True)

    kb = k_ref[...].astype(BF16)
    vf = v_ref[...]
    q = q_ref[...] * (D_HEAD ** -0.5)
    n_rep = q.shape[1] // LANES
    bias = bias_sc[...]
    out = jnp.zeros((n_rep * tq, LANES), F32)
    for g in range(2):
        keep = first if g == 0 else jnp.logical_not(first)
        vg = jnp.where(keep, vf, 0.0).astype(BF16)
        qs = jnp.concatenate(
            [jnp.where(keep, q[:, r * LANES:(r + 1) * LANES], 0.0) for r in range(n_rep)], axis=0).astype(BF16)
        logits = _dot_t(qs, kb)
        logits = (logits.reshape(n_rep, tq, lp) + bias[None]).reshape(n_rep * tq, lp)
        m = jnp.max(logits, axis=1, keepdims=True)
        p = jnp.exp(logits - m)
        den = jnp.sum(p, axis=1, keepdims=True)
        out = out + _dot(p.astype(BF16), vg) / den
    for r in range(n_rep):
        o_ref[:, r * LANES:(r + 1) * LANES] = out[r * tq:(r + 1) * tq]


def _dsa(q, qi, wi, k, v, ki, tq, q_pos0, l_valid):
    b, t, c = q.shape
    lp = k.shape[1]
    assert lp % KEY_TILE == 0 and l_valid <= lp and t % tq == 0
    top_k = min(TOPK_MAX, l_valid // 4)
    kern = functools.partial(_dsa_kernel, tq=tq, l_valid=l_valid, q_pos0=q_pos0, top_k=top_k)
    qmap = lambda bi, i: (bi, i, 0)
    kmap = lambda bi, i: (bi, 0, 0)
    return pl.pallas_call(
        kern,
        grid=(b, t // tq),
        in_specs=[pl.BlockSpec((None, tq, c), qmap), pl.BlockSpec((None, tq, qi.shape[2]), qmap),
                  pl.BlockSpec((None, tq, LANES), qmap),
                  pl.BlockSpec((None, lp, LANES), kmap), pl.BlockSpec((None, lp, LANES), kmap),
                  pl.BlockSpec((None, lp, LANES), kmap)],
        out_specs=pl.BlockSpec((None, tq, c), qmap),
        out_shape=jax.ShapeDtypeStruct((b, t, c), F32),
        scratch_shapes=[pltpu.VMEM((tq, lp), jnp.int32), pltpu.VMEM((tq, lp), F32)],
        compiler_params=_cparams(("parallel", "arbitrary")),
        name="dsa_attn",
    )(q, qi, wi, k, v, ki)


def _even_out_kernel(x_ref, att_ref, u_ref, vb_ref, ws_ref, bst_ref, w_ref, o_ref, *, p_len):
    tm = x_ref.shape[0]
    cg = vb_ref.shape[1] // G_B
    rr = lax.broadcasted_iota(jnp.int32, (p_len, p_len), 0)
    cc = lax.broadcasted_iota(jnp.int32, (p_len, p_len), 1)
    causal = jnp.right_shift(cc, CHUNK_SHIFT) <= jnp.right_shift(rr, CHUNK_SHIFT)
    u = u_ref[...]
    vb = vb_ref[...].astype(BF16)
    bst = bst_ref[...]
    mixed_rows = []
    for c in range(tm // p_len):
        cols = []
        for g in range(G_B):
            wg = jnp.where(causal, ws_ref[g], 0.0).astype(BF16)
            mg = _dot(wg, vb[c * p_len:(c + 1) * p_len, g * cg:(g + 1) * cg]) + bst[:, g:g + 1]
            cols.append(mg)
        mixed_rows.append(jnp.concatenate(cols, axis=1))
    gated = u * jnp.concatenate(mixed_rows, axis=0)
    na = att_ref.shape[1]
    y = _dot(att_ref[...].astype(BF16), w_ref[:na, :]) + _dot(gated.astype(BF16), w_ref[na:, :])
    o_ref[...] = x_ref[...] + y


def _even_out(x, att, u, vb, ws, bst, w, tm, p_len):
    n_rows, d = x.shape
    n = att.shape[1]
    row = lambda i: (i, 0)
    kern = functools.partial(_even_out_kernel, p_len=p_len)
    return pl.pallas_call(
        kern,
        grid=(n_rows // tm,),
        in_specs=[pl.BlockSpec((tm, d), row), pl.BlockSpec((tm, n), row), pl.BlockSpec((tm, n), row),
                  pl.BlockSpec((tm, n), row), pl.BlockSpec(ws.shape, lambda i: (0, 0, 0)),
                  pl.BlockSpec(bst.shape, lambda i: (0, 0)), pl.BlockSpec(w.shape, lambda i: (0, 0))],
        out_specs=pl.BlockSpec((tm, d), row),
        out_shape=jax.ShapeDtypeStruct((n_rows, d), F32),
        compiler_params=_cparams(("parallel",)),
        name="even_out",
    )(x, att, u, vb, ws, bst, w)


def _odd_out_kernel(x_ref, att_ref, gb_ref, cin_ref, prev_ref, cw_ref, w_ref, o_ref, *, t_len, has_prev):
    tm = x_ref.shape[0]
    cin = cin_ref[...]
    prev = prev_ref[...]
    if not has_prev:
        start = (pl.program_id(0) * tm) % t_len == 0
        prev = jnp.where(start, 0.0, prev)
    ext = jnp.concatenate([prev, cin], axis=0)
    cw = cw_ref[...]
    conv = cw[CONV_W - 1:CONV_W] * cin
    for j in range(1, CONV_W):
        conv = conv + cw[CONV_W - 1 - j:CONV_W - j] * pltpu.roll(ext, j, 0)[8:]
    gated = gb_ref[...] * conv
    na = att_ref.shape[1]
    y = _dot(att_ref[...].astype(BF16), w_ref[:na, :]) + _dot(gated.astype(BF16), w_ref[na:, :])
    o_ref[...] = x_ref[...] + y


def _odd_out(x, att, gb, cin, prev, cw, w, tm, t_len):
    n_rows, d = x.shape
    n = att.shape[1]
    row = lambda i: (i, 0)
    has_prev = prev is not None
    if has_prev:
        assert tm == t_len
        prev_arr, prev_map = prev, row
    else:
        assert t_len % tm == 0 and tm % 8 == 0
        prev_arr, prev_map = cin, (lambda i: (jnp.maximum(i * (tm // 8) - 1, 0), 0))
    kern = functools.partial(_odd_out_kernel, t_len=t_len, has_prev=has_prev)
    return pl.pallas_call(
        kern,
        grid=(n_rows // tm,),
        in_specs=[pl.BlockSpec((tm, d), row), pl.BlockSpec((tm, n), row), pl.BlockSpec((tm, n), row),
                  pl.BlockSpec((tm, n), row), pl.BlockSpec((8, n), prev_map),
                  pl.BlockSpec(cw.shape, lambda i: (0, 0)), pl.BlockSpec(w.shape, lambda i: (0, 0))],
        out_specs=pl.BlockSpec((tm, d), row),
        out_shape=jax.ShapeDtypeStruct((n_rows, d), F32),
        compiler_params=_cparams(("parallel",)),
        name="odd_out",
    )(x, att, gb, cin, prev_arr, cw, w)


def _ffn_kernel(x_ref, g_ref, w1_ref, w2_ref, o_ref, h_sc, acc_sc):
    j = pl.program_id(1)

    @pl.when(j == 0)
    def _():
        h_sc[...] = _rms(x_ref[...], g_ref[...]).astype(BF16)
        acc_sc[...] = jnp.zeros_like(acc_sc)

    a = jnp.square(jnp.maximum(_dot(h_sc[...], w1_ref[...]), 0.0))
    acc_sc[...] += _dot(a.astype(BF16), w2_ref[...])

    @pl.when(j == pl.num_programs(1) - 1)
    def _():
        o_ref[...] = x_ref[...] + acc_sc[...]


def _ffn(x, g, w1, w2, tm, tf):
    n_rows, d = x.shape
    dff = w1.shape[1]
    return pl.pallas_call(
        _ffn_kernel,
        grid=(n_rows // tm, dff // tf),
        in_specs=[pl.BlockSpec((tm, d), lambda i, j: (i, 0)), pl.BlockSpec((1, d), lambda i, j: (0, 0)),
                  pl.BlockSpec((d, tf), lambda i, j: (0, j)), pl.BlockSpec((tf, d), lambda i, j: (j, 0))],
        out_specs=pl.BlockSpec((tm, d), lambda i, j: (i, 0)),
        out_shape=jax.ShapeDtypeStruct((n_rows, d), F32),
        scratch_shapes=[pltpu.VMEM((tm, d), BF16), pltpu.VMEM((tm, d), F32)],
        compiler_params=_cparams(("parallel", "arbitrary")),
        name="ffn",
    )(x, g, w1, w2)


def _round_up(n, m):
    return (n + m - 1) // m * m


def _row_tile(n_rows, cap):
    t = min(cap, n_rows)
    while n_rows % t:
        t //= 2
    return t


def _odd_weight_layout(n_q):
    n_heads = n_q // D_HEAD
    rep = n_heads // 2
    perm = np.zeros((n_q,), np.int32)
    for r in range(rep):
        for g in range(2):
            for dd in range(D_HEAD):
                perm[r * LANES + g * D_HEAD + dd] = (g * rep + r) * D_HEAD + dd
    return perm


def kernel(x_prompt, x_sample, cache_a_k, cache_a_v, cache_c_k, cache_c_v, cache_c_kidx, state_d_conv,
           norm_mix_g, norm_ffn_g, w_in_even, gq_a, gk_a, g_b, ws_b, bs_b, w_out_even,
           w_in_odd, gq_c, gk_c, conv_w_d, w_out_odd, w_ffn1, w_ffn2):
    bp, tp, d = x_prompt.shape
    bs, ts, _ = x_sample.shape
    depth = norm_mix_g.shape[0]
    past = cache_a_k.shape[2]
    n_a = cache_a_k.shape[3] * cache_a_k.shape[4]
    n_kv = cache_c_k.shape[3] * cache_c_k.shape[4]
    n_b = g_b.shape[1]
    n_d = conv_w_d.shape[2]
    n_q = w_out_odd.shape[1] - n_d
    assert n_a == 512 and n_b == 512 and n_kv == 128 and n_q == 512 and n_d == 512 and d == 1024

    xp = x_prompt.reshape(bp * tp, d)
    xs = x_sample.reshape(bs * ts, d)
    tm_p = _row_tile(bp * tp, 512)
    tm_s = _row_tile(bs * ts, 512)
    ls = past + ts
    ls_pad = _round_up(ls, KEY_TILE)

    hmean = jnp.asarray(np.kron(np.eye(n_a // D_HEAD), np.full((D_HEAD, D_HEAD), 1.0 / D_HEAD)), BF16)
    perm = _odd_weight_layout(n_q)
    tile8 = lambda gvec: jnp.tile(gvec, n_a // D_HEAD).reshape(1, n_a)

    ak_p, av_p, ak_s, av_s, bv_s = [], [], [], [], []
    ck_p, cv_p, ci_p, ck_s, cv_s, ci_s, dc_p, dc_s = [], [], [], [], [], [], [], []

    tri = _stick_tri()

    def pad_keys(past_rows, new_rows, multiple=KEY_TILE):
        width = new_rows.shape[-1]
        zeros = jnp.zeros((bs, _round_up(ls, multiple) - ls, width), F32)
        return jnp.concatenate([past_rows.reshape(bs, past, width), new_rows.reshape(bs, ts, width), zeros], axis=1)

    for i in range(depth):
        j = i // 2
        g_mix = norm_mix_g[i].reshape(1, d)
        if i % 2 == 0:
            w_in = w_in_even[j].astype(BF16)
            w_out = w_out_even[j].astype(BF16)
            gq, gk, gb = tile8(gq_a[j]), tile8(gk_a[j]), g_b[j].reshape(1, n_b)
            q, k, v, u, vb, kb, vm = _even_in(xp, g_mix, w_in, hmean, gq, gk, gb, tm_p)
            att = _stick(q.reshape(bp, tp, n_a), kb.reshape(bp, tp, n_a), vm.reshape(bp, tp, 2 * n_a), tri,
                         min(tp, KEY_TILE), 0)
            xp = _even_out(xp, att.reshape(bp * tp, n_a), u, vb, ws_b[j], bs_b[j].T, w_out, tm_p, B_CHUNK)
            ak_p.append(k.reshape(bp, tp, n_a // D_HEAD, D_HEAD))
            av_p.append(v.reshape(bp, tp, n_a // D_HEAD, D_HEAD))
            q, k, v, u, vb, _, _ = _even_in(xs, g_mix, w_in, hmean, gq, gk, gb, tm_s)
            att = _stick(q.reshape(bs, ts, n_a), pad_keys(cache_a_k[j], k, STICK_TK),
                         pad_keys(cache_a_v[j], v, STICK_TK), tri, ts, past)
            xs = _even_out(xs, att.reshape(bs * ts, n_a), u, vb, ws_b[j][:, :ts, :ts], bs_b[j][:, :ts].T,
                           w_out, tm_s, ts)
            ak_s.append(k.reshape(bs, ts, n_a // D_HEAD, D_HEAD))
            av_s.append(v.reshape(bs, ts, n_a // D_HEAD, D_HEAD))
            bv_s.append(vb.reshape(bs, ts, n_b))
        else:
            wi_full = w_in_odd[j]
            offs = np.cumsum([0, n_q, n_kv, n_kv, H_IDX * D_IDX, D_IDX, H_IDX, n_d, n_d, n_d])
            col = lambda s: wi_full[:, int(offs[s]):int(offs[s + 1])]
            w_in = jnp.concatenate(
                [col(0)[:, perm], col(1), col(2), col(3), col(4), col(4), col(5),
                 jnp.zeros((d, LANES - H_IDX), F32), col(6), col(7), col(8)], axis=1).astype(BF16)
            assert w_in.shape[1] == ODD_COLS
            w_out = jnp.concatenate([w_out_odd[j][:n_q][perm], w_out_odd[j][n_q:]], axis=0).astype(BF16)
            gq, gk = tile8(gq_c[j]), jnp.tile(gk_c[j], n_kv // D_HEAD).reshape(1, n_kv)
            cw = conv_w_d[j]
            q, k, v, qi, ki, wi, gbt, cin = _odd_in(xp, g_mix, w_in, hmean, gq, gk, tm_p)
            r3 = lambda a: a.reshape(bp, tp, a.shape[-1])
            att = _dsa(r3(q), r3(qi), r3(wi), r3(k), r3(v), r3(ki), KEY_TILE, 0, tp)
            xp = _odd_out(xp, att.reshape(bp * tp, n_q), gbt, cin, None, cw, w_out, tm_p, tp)
            ck_p.append(k.reshape(bp, tp, n_kv // D_HEAD, D_HEAD))
            cv_p.append(v.reshape(bp, tp, n_kv // D_HEAD, D_HEAD))
            ci_p.append(ki[:, :D_IDX].reshape(bp, tp, D_IDX))
            dc_p.append(cin.reshape(bp, tp, n_d)[:, tp - (CONV_W - 1):])
            q, k, v, qi, ki, wi, gbt, cin = _odd_in(xs, g_mix, w_in, hmean, gq, gk, tm_s)
            r3 = lambda a: a.reshape(bs, ts, a.shape[-1])
            past_ki = jnp.concatenate([cache_c_kidx[j], cache_c_kidx[j]], axis=-1)
            att = _dsa(r3(q), r3(qi), r3(wi), pad_keys(cache_c_k[j], k), pad_keys(cache_c_v[j], v),
                       pad_keys(past_ki, ki), ts, past, ls)
            prev = jnp.concatenate([jnp.zeros((bs, 8 - (CONV_W - 1), n_d), F32), state_d_conv[j]], axis=1)
            xs = _odd_out(xs, att.reshape(bs * ts, n_q), gbt, cin, prev.reshape(bs * 8, n_d), cw, w_out, ts, ts)
            ck_s.append(k.reshape(bs, ts, n_kv // D_HEAD, D_HEAD))
            cv_s.append(v.reshape(bs, ts, n_kv // D_HEAD, D_HEAD))
            ci_s.append(ki[:, :D_IDX].reshape(bs, ts, D_IDX))
            dc_s.append(cin.reshape(bs, ts, n_d)[:, ts - (CONV_W - 1):])
        g_ffn = norm_ffn_g[i].reshape(1, d)
        w1 = w_ffn1[i].astype(BF16)
        w2 = w_ffn2[i].astype(BF16)
        xp = _ffn(xp, g_ffn, w1, w2, _row_tile(bp * tp, 1024), 512)
        xs = _ffn(xs, g_ffn, w1, w2, _row_tile(bs * ts, 1024), 512)

    return (xp.reshape(bp, tp, d), xs.reshape(bs, ts, d),
            jnp.stack(ak_p), jnp.stack(av_p), jnp.stack(ak_s), jnp.stack(av_s),
            jnp.stack(bv_s),
            jnp.stack(ck_p), jnp.stack(cv_p), jnp.stack(ci_p),
            jnp.stack(ck_s), jnp.stack(cv_s), jnp.stack(ci_s),
            jnp.stack(dc_p), jnp.stack(dc_s))
```

```python
import functools

import numpy as np
import jax
import jax.numpy as jnp
from jax import lax
from jax.experimental import pallas as pl
from jax.experimental.pallas import tpu as pltpu

F32 = jnp.float32
BF16 = jnp.bfloat16

EPS = 1e-6
NEG = -1e30
CHUNK = 64
CHUNK_SHIFT = 6
D_HEAD = 64
G_B = 4
B_CHUNK = 128
H_IDX = 4
D_IDX = 64
TOPK_MAX = 256
CONV_W = 3
LANES = 128
KEY_TILE = 128
VMEM_LIMIT = 56 * 1024 * 1024


def _cparams(sem):
    return pltpu.CompilerParams(dimension_semantics=sem, vmem_limit_bytes=VMEM_LIMIT)


def _dot(a, b):
    return jnp.dot(a, b, preferred_element_type=F32)


def _dot_t(a, b):
    return lax.dot_general(a, b, (((1,), (1,)), ((), ())), preferred_element_type=F32)


def _split(x):
    hi = x.astype(BF16)
    lo = (x - hi.astype(F32)).astype(BF16)
    return hi, lo


def _rms(x, g):
    return x * lax.rsqrt(jnp.mean(x * x, axis=-1, keepdims=True) + EPS) * g


def _head_rms(x, hmean, g):
    hi, lo = _split(x * x)
    ms = _dot(hi, hmean) + _dot(lo, hmean)
    return x * lax.rsqrt(ms + EPS) * g


def _gelu(x):
    return jax.nn.gelu(x)


def _even_in_kernel(x_ref, g_ref, w_ref, hm_ref, gq_ref, gk_ref, gb_ref,
                    q_ref, k_ref, v_ref, u_ref, vb_ref, kb_ref, vm_ref):
    h = _rms(x_ref[...], g_ref[...]).astype(BF16)
    n = q_ref.shape[-1]

    def seg(i):
        return _dot(h, w_ref[:, i * n:(i + 1) * n])

    hm = hm_ref[...]
    q_ref[...] = _head_rms(seg(0), hm, gq_ref[...])
    k = _head_rms(seg(1), hm, gk_ref[...])
    k_ref[...] = k
    kb_ref[...] = k.astype(BF16)
    v = seg(2)
    v_ref[...] = v
    first = lax.broadcasted_iota(jnp.int32, (1, LANES), 1) < D_HEAD
    for p in range(n // LANES):
        vp = v[:, p * LANES:(p + 1) * LANES]
        vm_ref[:, 2 * p * LANES:(2 * p + 1) * LANES] = jnp.where(first, vp, 0.0).astype(BF16)
        vm_ref[:, (2 * p + 1) * LANES:(2 * p + 2) * LANES] = jnp.where(first, 0.0, vp).astype(BF16)
    u_ref[...] = _gelu(seg(3))
    vb_ref[...] = _rms(_gelu(seg(4)), gb_ref[...])


def _even_in(x, g, w, hm, gq, gk, gb, tm):
    n_rows, d = x.shape
    n = w.shape[1] // 5
    row = lambda i: (i, 0)
    const = lambda i: (0, 0)
    out = jax.ShapeDtypeStruct((n_rows, n), F32)
    return pl.pallas_call(
        _even_in_kernel,
        grid=(n_rows // tm,),
        in_specs=[pl.BlockSpec((tm, d), row), pl.BlockSpec((1, d), const),
                  pl.BlockSpec(w.shape, const), pl.BlockSpec(hm.shape, const),
                  pl.BlockSpec((1, n), const), pl.BlockSpec((1, n), const), pl.BlockSpec((1, n), const)],
        out_specs=[pl.BlockSpec((tm, n), row)] * 6 + [pl.BlockSpec((tm, 2 * n), row)],
        out_shape=[out] * 5 + [jax.ShapeDtypeStruct((n_rows, n), BF16), jax.ShapeDtypeStruct((n_rows, 2 * n), BF16)],
        compiler_params=_cparams(("parallel",)),
        name="even_in",
    )(x, g, w, hm, gq, gk, gb)


ODD_Q, ODD_K, ODD_V, ODD_QI, ODD_KI, ODD_WI, ODD_GB, ODD_GC, ODD_HD = (
    0, 512, 640, 768, 1024, 1152, 1280, 1792, 2304)
ODD_COLS = 2816


def _odd_in_kernel(x_ref, g_ref, w_ref, hm_ref, gq_ref, gk_ref,
                   q_ref, k_ref, v_ref, qi_ref, ki_ref, wi_ref, gb_ref, cin_ref):
    h = _rms(x_ref[...], g_ref[...]).astype(BF16)

    def seg(lo, width):
        return _dot(h, w_ref[:, lo:lo + width])

    hm = hm_ref[...]
    q_ref[...] = _head_rms(seg(ODD_Q, 512), hm, gq_ref[...])
    k_ref[...] = _head_rms(seg(ODD_K, 128), hm[:128, :128], gk_ref[...])
    v_ref[...] = seg(ODD_V, 128)
    qi_ref[...] = seg(ODD_QI, 256)
    ki_ref[...] = seg(ODD_KI, 128)
    wi_ref[...] = seg(ODD_WI, 128)
    gb_ref[...] = seg(ODD_GB, 512)
    cin_ref[...] = seg(ODD_GC, 512) * seg(ODD_HD, 512)


def _odd_in(x, g, w, hm, gq, gk, tm):
    n_rows, d = x.shape
    row = lambda i: (i, 0)
    const = lambda i: (0, 0)
    widths = (512, 128, 128, 256, 128, 128, 512, 512)
    return pl.pallas_call(
        _odd_in_kernel,
        grid=(n_rows // tm,),
        in_specs=[pl.BlockSpec((tm, d), row), pl.BlockSpec((1, d), const),
                  pl.BlockSpec(w.shape, const), pl.BlockSpec(hm.shape, const),
                  pl.BlockSpec((1, 512), const), pl.BlockSpec((1, 128), const)],
        out_specs=[pl.BlockSpec((tm, n), row) for n in widths],
        out_shape=[jax.ShapeDtypeStruct((n_rows, n), F32) for n in widths],
        compiler_params=_cparams(("parallel",)),
        name="odd_in",
    )(x, g, w, hm, gq, gk)


STICK_TK = 256


def _stick_tri():
    half = STICK_TK // 2
    later = (np.arange(half)[:, None] > np.arange(half)[None, :]).astype(np.float32)
    blk = np.concatenate([later, np.ones((half, half), np.float32)], axis=1)
    return jnp.asarray(np.concatenate([blk, blk], axis=0), BF16)


def _stick_kernel(q_ref, k_ref, v_ref, tri_ref, o_ref, carry_sc, acc_sc, *, tq, q_pos0):
    tk = STICK_TK
    half = tk // 2
    n_pairs = q_ref.shape[1] // LANES
    v_split = v_ref.shape[1] == 2 * q_ref.shape[1]
    q_start = q_pos0 + pl.program_id(1) * tq
    n_full = q_start // tk
    lane = lax.broadcasted_iota(jnp.int32, (1, LANES), 1)
    first = lane < D_HEAD
    q = q_ref[...] * (D_HEAD ** -0.5)
    q2 = []
    for p in range(n_pairs):
        qp = q[:, p * LANES:(p + 1) * LANES]
        q2.append(jnp.concatenate([jnp.where(first, qp, 0.0), jnp.where(first, 0.0, qp)], axis=0).astype(BF16))
    tri = tri_ref[...]
    carry_sc[...] = jnp.zeros_like(carry_sc)
    acc_sc[...] = jnp.zeros_like(acc_sc)

    def block(kb, masked):
        ks = pl.multiple_of(kb * tk, tk)
        if masked:
            row = lax.broadcasted_iota(jnp.int32, (2 * tq, tk), 0)
            qpos = q_start + jnp.where(row >= tq, row - tq, row)
            kpos = ks + lax.broadcasted_iota(jnp.int32, (2 * tq, tk), 1)
            ok = kpos < qpos
        for p in range(n_pairs):
            cols = slice(p * LANES, (p + 1) * LANES)
            z = _dot_t(q2[p], k_ref[pl.ds(ks, tk), cols].astype(BF16))
            nz = -z
            stay = jnp.minimum(nz, 0.0) - jnp.log(1.0 + jnp.exp(jnp.minimum(z, nz)))
            if masked:
                stay = jnp.where(ok, stay, 0.0)
            carry = carry_sc[p]
            after = [None, None]
            for h in (1, 0):
                hi, lo = _split(stay[:, h * half:(h + 1) * half])
                r = _dot(jnp.concatenate([hi, lo], axis=1), tri)
                after[h] = r[:, :half] + carry
                carry = carry + r[:, half:]
            carry_sc[p] = carry
            w = jnp.exp(z + stay + jnp.concatenate(after, axis=1))
            if masked:
                w = jnp.where(ok, w, 0.0)
            w = w.astype(BF16)
            if v_split:
                v0 = v_ref[pl.ds(ks, tk), 2 * p * LANES:(2 * p + 1) * LANES]
                v1 = v_ref[pl.ds(ks, tk), (2 * p + 1) * LANES:(2 * p + 2) * LANES]
            else:
                vblk = v_ref[pl.ds(ks, tk), cols]
                v0 = jnp.where(first, vblk, 0.0).astype(BF16)
                v1 = jnp.where(first, 0.0, vblk).astype(BF16)
            acc_sc[p] += _dot(jnp.concatenate([w[:tq], w[tq:]], axis=1), jnp.concatenate([v0, v1], axis=0))

    block(n_full, True)

    def body(i, c):
        block(n_full - 1 - i, False)
        return c

    lax.fori_loop(0, n_full, body, 0)
    for p in range(n_pairs):
        o_ref[:, p * LANES:(p + 1) * LANES] = acc_sc[p]


def _stick(q, k, v, tri, tq, q_pos0):
    b, t, c = q.shape
    lp = k.shape[1]
    tk = STICK_TK
    assert lp % tk == 0 and q_pos0 % tq == 0 and tk % tq == 0 and q_pos0 + t <= lp
    kern = functools.partial(_stick_kernel, tq=tq, q_pos0=q_pos0)
    kmap = lambda bi, i: (bi, 0, 0)
    return pl.pallas_call(
        kern,
        grid=(b, t // tq),
        in_specs=[pl.BlockSpec((None, tq, c), lambda bi, i: (bi, i, 0)),
                  pl.BlockSpec((None, lp, c), kmap),
                  pl.BlockSpec((None, lp, v.shape[2]), kmap),
                  pl.BlockSpec(tri.shape, lambda bi, i: (0, 0))],
        out_specs=pl.BlockSpec((None, tq, c), lambda bi, i: (bi, i, 0)),
        out_shape=jax.ShapeDtypeStruct((b, t, c), F32),
        scratch_shapes=[pltpu.VMEM((c // LANES, 2 * tq, LANES), F32), pltpu.VMEM((c // LANES, tq, LANES), F32)],
        compiler_params=_cparams(("parallel", "arbitrary")),
        name="stick_attn",
    )(q, k, v, tri)


DSA_LEVEL = 512


def _dsa_kernel(q_ref, qi_ref, wi_ref, k_ref, v_ref, ki_ref, o_ref, key_sc, bias_sc,
                *, tq, l_valid, q_pos0, top_k, levels):
    kt = KEY_TILE
    th = min(tq, 128)
    parts = [slice(i * th, (i + 1) * th) for i in range(tq // th)]
    n_rep = q_ref.shape[1] // LANES
    lane = lax.broadcasted_iota(jnp.int32, (1, LANES), 1)
    first = lane < D_HEAD
    q_start = q_pos0 + pl.program_id(1) * tq
    kf = jnp.float32(top_k)

    def body(lv):
        n_kt = lv // kt

        def admissible(rows, lo, width):
            qpos = q_start + rows.start + lax.broadcasted_iota(jnp.int32, (th, 1), 0)
            kpos = lo + lax.broadcasted_iota(jnp.int32, (1, width), 1)
            return (jnp.right_shift(kpos, CHUNK_SHIFT) <= jnp.right_shift(qpos, CHUNK_SHIFT)) & (kpos < l_valid)

        ki_hi, ki_lo = _split(ki_ref[:lv, :])
        for rows in parts:
            qi = qi_ref[rows, :]
            wi = wi_ref[rows, :]
            qm = []
            for h in range(H_IDX):
                grp = qi[:, (h // 2) * LANES:(h // 2 + 1) * LANES]
                qm.append(jnp.where(first, grp, 0.0) if h % 2 == 0 else jnp.where(first, 0.0, grp))
            q_hi, q_lo = _split(jnp.concatenate(qm, axis=0))
            s = _dot_t(q_hi, ki_hi) + (_dot_t(q_hi, ki_lo) + _dot_t(q_lo, ki_hi))
            score = jnp.zeros((th, lv), F32)
            for h in range(H_IDX):
                score = score + jnp.maximum(s[h * th:(h + 1) * th], 0.0) * wi[:, h:h + 1]
            score = score * ((D_IDX ** -0.5) * (H_IDX ** -0.5)) + 0.0
            score = jnp.where(admissible(rows, 0, lv), score, NEG)
            bits = lax.bitcast_convert_type(score, jnp.int32)
            key_sc[rows, :lv] = jnp.where(bits < 0, bits ^ jnp.int32(0x7FFFFFFF), bits)

        def count_ge(rows, trial):
            part = jnp.zeros((th, kt), F32)
            for c in range(n_kt):
                part = part + jnp.where(key_sc[rows, c * kt:(c + 1) * kt] >= trial, 1.0, 0.0)
            return jnp.sum(part, axis=1, keepdims=True)

        int_min = jnp.int32(-2 ** 31)
        zero = jnp.zeros((th, 1), jnp.int32)
        thr0 = tuple(jnp.where(count_ge(rows, zero) >= kf, jnp.int32(0), int_min) for rows in parts)

        def bit_step(i, thr):
            bit = jnp.left_shift(jnp.int32(1), jnp.int32(30) - i)
            return tuple(jnp.where(count_ge(rows, t + bit) >= kf, t + bit, t) for rows, t in zip(parts, thr))

        thrs = lax.fori_loop(0, 31, bit_step, thr0)

        rr = lax.broadcasted_iota(jnp.int32, (kt, kt), 0)
        cc = lax.broadcasted_iota(jnp.int32, (kt, kt), 1)
        before = jnp.where(rr < cc, 1.0, 0.0).astype(BF16)
        for rows, thr in zip(parts, thrs):
            n_gt = jnp.zeros((th, kt), F32)
            for c in range(n_kt):
                n_gt = n_gt + jnp.where(key_sc[rows, c * kt:(c + 1) * kt] > thr, 1.0, 0.0)
            room = kf - jnp.sum(n_gt, axis=1, keepdims=True)
            run = jnp.zeros((th, 1), F32)
            for c in range(n_kt):
                keyc = key_sc[rows, c * kt:(c + 1) * kt]
                eqf = jnp.where(keyc == thr, 1.0, 0.0)
                rank = _dot(eqf.astype(BF16), before) + run
                tie_ok = jnp.where(keyc == thr, jnp.where(rank < room, 0.0, NEG), NEG)
                sel = jnp.where(keyc > thr, 0.0, tie_ok)
                bias_sc[rows, c * kt:(c + 1) * kt] = jnp.where(admissible(rows, c * kt, kt), sel, NEG)
                run = run + jnp.sum(eqf, axis=1, keepdims=True)

        kb = k_ref[:lv, :].astype(BF16)
        vf = v_ref[:lv, :]
        vg = [jnp.where(first, vf, 0.0).astype(BF16), jnp.where(first, 0.0, vf).astype(BF16)]
        for rows in parts:
            q = q_ref[rows, :] * (D_HEAD ** -0.5)
            bias = bias_sc[rows, :lv]
            out = jnp.zeros((n_rep * th, LANES), F32)
            for g in range(2):
                keep = first if g == 0 else jnp.logical_not(first)
                qs = jnp.concatenate(
                    [jnp.where(keep, q[:, r * LANES:(r + 1) * LANES], 0.0) for r in range(n_rep)],
                    axis=0).astype(BF16)
                logits = _dot_t(qs, kb)
                logits = (logits.reshape(n_rep, th, lv) + bias[None]).reshape(n_rep * th, lv)
                m = jnp.max(logits, axis=1, keepdims=True)
                p = jnp.exp(logits - m)
                den = jnp.sum(p, axis=1, keepdims=True)
                out = out + _dot(p.astype(BF16), vg[g]) / den
            for r in range(n_rep):
                o_ref[rows, r * LANES:(r + 1) * LANES] = out[r * th:(r + 1) * th]

    if len(levels) == 1:
        body(levels[0])
    else:
        cls = (q_start + tq - 1) // DSA_LEVEL
        for c, lv in enumerate(levels):
            pl.when(cls == c)(functools.partial(body, lv))


def _dsa(q, qi, wi, k, v, ki, tq, q_pos0, l_valid):
    b, t, c = q.shape
    lp = k.shape[1]
    assert lp % KEY_TILE == 0 and l_valid <= lp and t % tq == 0
    top_k = min(TOPK_MAX, l_valid // 4)
    if q_pos0 == 0 and lp % DSA_LEVEL == 0 and tq % CHUNK == 0 and DSA_LEVEL >= top_k:
        levels = tuple(range(DSA_LEVEL, lp + 1, DSA_LEVEL))
    else:
        levels = (lp,)
    kern = functools.partial(_dsa_kernel, tq=tq, l_valid=l_valid, q_pos0=q_pos0, top_k=top_k, levels=levels)
    qmap = lambda bi, i: (bi, i, 0)
    kmap = lambda bi, i: (bi, 0, 0)
    return pl.pallas_call(
        kern,
        grid=(b, t // tq),
        in_specs=[pl.BlockSpec((None, tq, c), qmap), pl.BlockSpec((None, tq, qi.shape[2]), qmap),
                  pl.BlockSpec((None, tq, LANES), qmap),
                  pl.BlockSpec((None, lp, LANES), kmap), pl.BlockSpec((None, lp, LANES), kmap),
                  pl.BlockSpec((None, lp, LANES), kmap)],
        out_specs=pl.BlockSpec((None, tq, c), qmap),
        out_shape=jax.ShapeDtypeStruct((b, t, c), F32),
        scratch_shapes=[pltpu.VMEM((tq, lp), jnp.int32), pltpu.VMEM((tq, lp), F32)],
        compiler_params=_cparams(("parallel", "arbitrary")),
        name="dsa_attn",
    )(q, qi, wi, k, v, ki)


def _even_out_kernel(x_ref, att_ref, u_ref, vb_ref, ws_ref, bst_ref, w_ref, o_ref, *, p_len):
    tm = x_ref.shape[0]
    cg = vb_ref.shape[1] // G_B
    rr = lax.broadcasted_iota(jnp.int32, (p_len, p_len), 0)
    cc = lax.broadcasted_iota(jnp.int32, (p_len, p_len), 1)
    causal = jnp.right_shift(cc, CHUNK_SHIFT) <= jnp.right_shift(rr, CHUNK_SHIFT)
    u = u_ref[...]
    vb = vb_ref[...].astype(BF16)
    bst = bst_ref[...]
    mixed_rows = []
    for c in range(tm // p_len):
        cols = []
        for g in range(G_B):
            wg = jnp.where(causal, ws_ref[g], 0.0).astype(BF16)
            mg = _dot(wg, vb[c * p_len:(c + 1) * p_len, g * cg:(g + 1) * cg]) + bst[:, g:g + 1]
            cols.append(mg)
        mixed_rows.append(jnp.concatenate(cols, axis=1))
    gated = u * jnp.concatenate(mixed_rows, axis=0)
    na = att_ref.shape[1]
    y = _dot(att_ref[...].astype(BF16), w_ref[:na, :]) + _dot(gated.astype(BF16), w_ref[na:, :])
    o_ref[...] = x_ref[...] + y


def _even_out(x, att, u, vb, ws, bst, w, tm, p_len):
    n_rows, d = x.shape
    n = att.shape[1]
    row = lambda i: (i, 0)
    kern = functools.partial(_even_out_kernel, p_len=p_len)
    return pl.pallas_call(
        kern,
        grid=(n_rows // tm,),
        in_specs=[pl.BlockSpec((tm, d), row), pl.BlockSpec((tm, n), row), pl.BlockSpec((tm, n), row),
                  pl.BlockSpec((tm, n), row), pl.BlockSpec(ws.shape, lambda i: (0, 0, 0)),
                  pl.BlockSpec(bst.shape, lambda i: (0, 0)), pl.BlockSpec(w.shape, lambda i: (0, 0))],
        out_specs=pl.BlockSpec((tm, d), row),
        out_shape=jax.ShapeDtypeStruct((n_rows, d), F32),
        compiler_params=_cparams(("parallel",)),
        name="even_out",
    )(x, att, u, vb, ws, bst, w)


def _odd_out_kernel(x_ref, att_ref, gb_ref, cin_ref, prev_ref, cw_ref, w_ref, o_ref, *, t_len, has_prev):
    tm = x_ref.shape[0]
    cin = cin_ref[...]
    prev = prev_ref[...]
    if not has_prev:
        start = (pl.program_id(0) * tm) % t_len == 0
        prev = jnp.where(start, 0.0, prev)
    ext = jnp.concatenate([prev, cin], axis=0)
    cw = cw_ref[...]
    conv = cw[CONV_W - 1:CONV_W] * cin
    for j in range(1, CONV_W):
        conv = conv + cw[CONV_W - 1 - j:CONV_W - j] * pltpu.roll(ext, j, 0)[8:]
    gated = gb_ref[...] * conv
    na = att_ref.shape[1]
    y = _dot(att_ref[...].astype(BF16), w_ref[:na, :]) + _dot(gated.astype(BF16), w_ref[na:, :])
    o_ref[...] = x_ref[...] + y


def _odd_out(x, att, gb, cin, prev, cw, w, tm, t_len):
    n_rows, d = x.shape
    n = att.shape[1]
    row = lambda i: (i, 0)
    has_prev = prev is not None
    if has_prev:
        assert tm == t_len
        prev_arr, prev_map = prev, row
    else:
        assert t_len % tm == 0 and tm % 8 == 0
        prev_arr, prev_map = cin, (lambda i: (jnp.maximum(i * (tm // 8) - 1, 0), 0))
    kern = functools.partial(_odd_out_kernel, t_len=t_len, has_prev=has_prev)
    return pl.pallas_call(
        kern,
        grid=(n_rows // tm,),
        in_specs=[pl.BlockSpec((tm, d), row), pl.BlockSpec((tm, n), row), pl.BlockSpec((tm, n), row),
                  pl.BlockSpec((tm, n), row), pl.BlockSpec((8, n), prev_map),
                  pl.BlockSpec(cw.shape, lambda i: (0, 0)), pl.BlockSpec(w.shape, lambda i: (0, 0))],
        out_specs=pl.BlockSpec((tm, d), row),
        out_shape=jax.ShapeDtypeStruct((n_rows, d), F32),
        compiler_params=_cparams(("parallel",)),
        name="odd_out",
    )(x, att, gb, cin, prev_arr, cw, w)


def _ffn_kernel(x_ref, g_ref, w1_ref, w2_ref, o_ref, h_sc, acc_sc):
    j = pl.program_id(1)

    @pl.when(j == 0)
    def _():
        h_sc[...] = _rms(x_ref[...], g_ref[...]).astype(BF16)
        acc_sc[...] = jnp.zeros_like(acc_sc)

    a = jnp.square(jnp.maximum(_dot(h_sc[...], w1_ref[...]), 0.0))
    acc_sc[...] += _dot(a.astype(BF16), w2_ref[...])

    @pl.when(j == pl.num_programs(1) - 1)
    def _():
        o_ref[...] = x_ref[...] + acc_sc[...]


def _ffn(x, g, w1, w2, tm, tf):
    n_rows, d = x.shape
    dff = w1.shape[1]
    return pl.pallas_call(
        _ffn_kernel,
        grid=(n_rows // tm, dff // tf),
        in_specs=[pl.BlockSpec((tm, d), lambda i, j: (i, 0)), pl.BlockSpec((1, d), lambda i, j: (0, 0)),
                  pl.BlockSpec((d, tf), lambda i, j: (0, j)), pl.BlockSpec((tf, d), lambda i, j: (j, 0))],
        out_specs=pl.BlockSpec((tm, d), lambda i, j: (i, 0)),
        out_shape=jax.ShapeDtypeStruct((n_rows, d), F32),
        scratch_shapes=[pltpu.VMEM((tm, d), BF16), pltpu.VMEM((tm, d), F32)],
        compiler_params=_cparams(("parallel", "arbitrary")),
        name="ffn",
    )(x, g, w1, w2)


def _round_up(n, m):
    return (n + m - 1) // m * m


def _row_tile(n_rows, cap):
    t = min(cap, n_rows)
    while n_rows % t:
        t //= 2
    return t


def _odd_weight_layout(n_q):
    n_heads = n_q // D_HEAD
    rep = n_heads // 2
    perm = np.zeros((n_q,), np.int32)
    for r in range(rep):
        for g in range(2):
            for dd in range(D_HEAD):
                perm[r * LANES + g * D_HEAD + dd] = (g * rep + r) * D_HEAD + dd
    return perm


def kernel(x_prompt, x_sample, cache_a_k, cache_a_v, cache_c_k, cache_c_v, cache_c_kidx, state_d_conv,
           norm_mix_g, norm_ffn_g, w_in_even, gq_a, gk_a, g_b, ws_b, bs_b, w_out_even,
           w_in_odd, gq_c, gk_c, conv_w_d, w_out_odd, w_ffn1, w_ffn2):
    bp, tp, d = x_prompt.shape
    bs, ts, _ = x_sample.shape
    depth = norm_mix_g.shape[0]
    past = cache_a_k.shape[2]
    n_a = cache_a_k.shape[3] * cache_a_k.shape[4]
    n_kv = cache_c_k.shape[3] * cache_c_k.shape[4]
    n_b = g_b.shape[1]
    n_d = conv_w_d.shape[2]
    n_q = w_out_odd.shape[1] - n_d
    assert n_a == 512 and n_b == 512 and n_kv == 128 and n_q == 512 and n_d == 512 and d == 1024

    xp = x_prompt.reshape(bp * tp, d)
    xs = x_sample.reshape(bs * ts, d)
    tm_p = _row_tile(bp * tp, 512)
    tm_s = _row_tile(bs * ts, 512)
    ls = past + ts
    ls_pad = _round_up(ls, KEY_TILE)

    hmean = jnp.asarray(np.kron(np.eye(n_a // D_HEAD), np.full((D_HEAD, D_HEAD), 1.0 / D_HEAD)), BF16)
    perm = _odd_weight_layout(n_q)
    tile8 = lambda gvec: jnp.tile(gvec, n_a // D_HEAD).reshape(1, n_a)

    ak_p, av_p, ak_s, av_s, bv_s = [], [], [], [], []
    ck_p, cv_p, ci_p, ck_s, cv_s, ci_s, dc_p, dc_s = [], [], [], [], [], [], [], []

    tri = _stick_tri()

    def pad_keys(past_rows, new_rows, multiple=KEY_TILE):
        width = new_rows.shape[-1]
        zeros = jnp.zeros((bs, _round_up(ls, multiple) - ls, width), F32)
        return jnp.concatenate([past_rows.reshape(bs, past, width), new_rows.reshape(bs, ts, width), zeros], axis=1)

    for i in range(depth):
        j = i // 2
        g_mix = norm_mix_g[i].reshape(1, d)
        if i % 2 == 0:
            w_in = w_in_even[j].astype(BF16)
            w_out = w_out_even[j].astype(BF16)
            gq, gk, gb = tile8(gq_a[j]), tile8(gk_a[j]), g_b[j].reshape(1, n_b)
            q, k, v, u, vb, kb, vm = _even_in(xp, g_mix, w_in, hmean, gq, gk, gb, tm_p)
            att = _stick(q.reshape(bp, tp, n_a), kb.reshape(bp, tp, n_a), vm.reshape(bp, tp, 2 * n_a), tri,
                         min(tp, KEY_TILE), 0)
            xp = _even_out(xp, att.reshape(bp * tp, n_a), u, vb, ws_b[j], bs_b[j].T, w_out, tm_p, B_CHUNK)
            ak_p.append(k.reshape(bp, tp, n_a // D_HEAD, D_HEAD))
            av_p.append(v.reshape(bp, tp, n_a // D_HEAD, D_HEAD))
            q, k, v, u, vb, _, _ = _even_in(xs, g_mix, w_in, hmean, gq, gk, gb, tm_s)
            att = _stick(q.reshape(bs, ts, n_a), pad_keys(cache_a_k[j], k, STICK_TK),
                         pad_keys(cache_a_v[j], v, STICK_TK), tri, ts, past)
            xs = _even_out(xs, att.reshape(bs * ts, n_a), u, vb, ws_b[j][:, :ts, :ts], bs_b[j][:, :ts].T,
                           w_out, tm_s, ts)
            ak_s.append(k.reshape(bs, ts, n_a // D_HEAD, D_HEAD))
            av_s.append(v.reshape(bs, ts, n_a // D_HEAD, D_HEAD))
            bv_s.append(vb.reshape(bs, ts, n_b))
        else:
            wi_full = w_in_odd[j]
            offs = np.cumsum([0, n_q, n_kv, n_kv, H_IDX * D_IDX, D_IDX, H_IDX, n_d, n_d, n_d])
            col = lambda s: wi_full[:, int(offs[s]):int(offs[s + 1])]
            w_in = jnp.concatenate(
                [col(0)[:, perm], col(1), col(2), col(3), col(4), col(4), col(5),
                 jnp.zeros((d, LANES - H_IDX), F32), col(6), col(7), col(8)], axis=1).astype(BF16)
            assert w_in.shape[1] == ODD_COLS
            w_out = jnp.concatenate([w_out_odd[j][:n_q][perm], w_out_odd[j][n_q:]], axis=0).astype(BF16)
            gq, gk = tile8(gq_c[j]), jnp.tile(gk_c[j], n_kv // D_HEAD).reshape(1, n_kv)
            cw = conv_w_d[j]
            q, k, v, qi, ki, wi, gbt, cin = _odd_in(xp, g_mix, w_in, hmean, gq, gk, tm_p)
            r3 = lambda a: a.reshape(bp, tp, a.shape[-1])
            att = _dsa(r3(q), r3(qi), r3(wi), r3(k), r3(v), r3(ki), min(tp, 2 * KEY_TILE), 0, tp)
            xp = _odd_out(xp, att.reshape(bp * tp, n_q), gbt, cin, None, cw, w_out, tm_p, tp)
            ck_p.append(k.reshape(bp, tp, n_kv // D_HEAD, D_HEAD))
            cv_p.append(v.reshape(bp, tp, n_kv // D_HEAD, D_HEAD))
            ci_p.append(ki[:, :D_IDX].reshape(bp, tp, D_IDX))
            dc_p.append(cin.reshape(bp, tp, n_d)[:, tp - (CONV_W - 1):])
            q, k, v, qi, ki, wi, gbt, cin = _odd_in(xs, g_mix, w_in, hmean, gq, gk, tm_s)
            r3 = lambda a: a.reshape(bs, ts, a.shape[-1])
            past_ki = jnp.concatenate([cache_c_kidx[j], cache_c_kidx[j]], axis=-1)
            att = _dsa(r3(q), r3(qi), r3(wi), pad_keys(cache_c_k[j], k), pad_keys(cache_c_v[j], v),
                       pad_keys(past_ki, ki), ts, past, ls)
            prev = jnp.concatenate([jnp.zeros((bs, 8 - (CONV_W - 1), n_d), F32), state_d_conv[j]], axis=1)
            xs = _odd_out(xs, att.reshape(bs * ts, n_q), gbt, cin, prev.reshape(bs * 8, n_d), cw, w_out, ts, ts)
            ck_s.append(k.reshape(bs, ts, n_kv // D_HEAD, D_HEAD))
            cv_s.append(v.reshape(bs, ts, n_kv // D_HEAD, D_HEAD))
            ci_s.append(ki[:, :D_IDX].reshape(bs, ts, D_IDX))
            dc_s.append(cin.reshape(bs, ts, n_d)[:, ts - (CONV_W - 1):])
        g_ffn = norm_ffn_g[i].reshape(1, d)
        w1 = w_ffn1[i].astype(BF16)
        w2 = w_ffn2[i].astype(BF16)
        xp = _ffn(xp, g_ffn, w1, w2, _row_tile(bp * tp, 1024), 512)
        xs = _ffn(xs, g_ffn, w1, w2, _row_tile(bs * ts, 1024), 512)

    return (xp.reshape(bp, tp, d), xs.reshape(bs, ts, d),
            jnp.stack(ak_p), jnp.stack(av_p), jnp.stack(ak_s), jnp.stack(av_s),
            jnp.stack(bv_s),
            jnp.stack(ck_p), jnp.stack(cv_p), jnp.stack(ci_p),
            jnp.stack(ck_s), jnp.stack(cv_s), jnp.stack(ci_s),
            jnp.stack(dc_p), jnp.stack(dc_s))
```

```python
import functools

import numpy as np
import jax
import jax.numpy as jnp
from jax import lax
from jax.experimental import pallas as pl
from jax.experimental.pallas import tpu as pltpu

F32 = jnp.float32
BF16 = jnp.bfloat16

EPS = 1e-6
NEG = -1e30
CHUNK = 64
CHUNK_SHIFT = 6
D_HEAD = 64
G_B = 4
B_CHUNK = 128
H_IDX = 4
D_IDX = 64
TOPK_MAX = 256
CONV_W = 3
LANES = 128
KEY_TILE = 128
VMEM_LIMIT = 56 * 1024 * 1024


def _cparams(sem):
    return pltpu.CompilerParams(dimension_semantics=sem, vmem_limit_bytes=VMEM_LIMIT)


def _dot(a, b):
    return jnp.dot(a, b, preferred_element_type=F32)


def _dot_t(a, b):
    return lax.dot_general(a, b, (((1,), (1,)), ((), ())), preferred_element_type=F32)


def _split(x):
    hi = x.astype(BF16)
    lo = (x - hi.astype(F32)).astype(BF16)
    return hi, lo


def _rms(x, g):
    return x * lax.rsqrt(jnp.mean(x * x, axis=-1, keepdims=True) + EPS) * g


def _head_rms(x, hmean, g):
    hi, lo = _split(x * x)
    ms = _dot(hi, hmean) + _dot(lo, hmean)
    return x * lax.rsqrt(ms + EPS) * g


def _gelu(x):
    return jax.nn.gelu(x)


def _even_in_kernel(x_ref, g_ref, w_ref, hm_ref, gq_ref, gk_ref, gb_ref,
                    q_ref, k_ref, v_ref, u_ref, vb_ref, kb_ref, vm_ref):
    h = _rms(x_ref[...], g_ref[...]).astype(BF16)
    n = q_ref.shape[-1]

    def seg(i):
        return _dot(h, w_ref[:, i * n:(i + 1) * n])

    hm = hm_ref[...]
    q_ref[...] = _head_rms(seg(0), hm, gq_ref[...])
    k = _head_rms(seg(1), hm, gk_ref[...])
    k_ref[...] = k
    kb_ref[...] = k.astype(BF16)
    v = seg(2)
    v_ref[...] = v
    first = lax.broadcasted_iota(jnp.int32, (1, LANES), 1) < D_HEAD
    for p in range(n // LANES):
        vp = v[:, p * LANES:(p + 1) * LANES]
        vm_ref[:, 2 * p * LANES:(2 * p + 1) * LANES] = jnp.where(first, vp, 0.0).astype(BF16)
        vm_ref[:, (2 * p + 1) * LANES:(2 * p + 2) * LANES] = jnp.where(first, 0.0, vp).astype(BF16)
    u_ref[...] = _gelu(seg(3))
    vb_ref[...] = _rms(_gelu(seg(4)), gb_ref[...])


def _even_in(x, g, w, hm, gq, gk, gb, tm):
    n_rows, d = x.shape
    n = w.shape[1] // 5
    row = lambda i: (i, 0)
    const = lambda i: (0, 0)
    out = jax.ShapeDtypeStruct((n_rows, n), F32)
    return pl.pallas_call(
        _even_in_kernel,
        grid=(n_rows // tm,),
        in_specs=[pl.BlockSpec((tm, d), row), pl.BlockSpec((1, d), const),
                  pl.BlockSpec(w.shape, const), pl.BlockSpec(hm.shape, const),
                  pl.BlockSpec((1, n), const), pl.BlockSpec((1, n), const), pl.BlockSpec((1, n), const)],
        out_specs=[pl.BlockSpec((tm, n), row)] * 6 + [pl.BlockSpec((tm, 2 * n), row)],
        out_shape=[out] * 5 + [jax.ShapeDtypeStruct((n_rows, n), BF16), jax.ShapeDtypeStruct((n_rows, 2 * n), BF16)],
        compiler_params=_cparams(("parallel",)),
        name="even_in",
    )(x, g, w, hm, gq, gk, gb)


ODD_Q, ODD_K, ODD_V, ODD_QI, ODD_KI, ODD_WI, ODD_GB, ODD_GC, ODD_HD = (
    0, 512, 640, 768, 1024, 1152, 1280, 1792, 2304)
ODD_COLS = 2816


def _odd_in_kernel(x_ref, g_ref, w_ref, hm_ref, gq_ref, gk_ref,
                   q_ref, k_ref, v_ref, qi_ref, ki_ref, wi_ref, gb_ref, cin_ref):
    h = _rms(x_ref[...], g_ref[...]).astype(BF16)

    def seg(lo, width):
        return _dot(h, w_ref[:, lo:lo + width])

    hm = hm_ref[...]
    q_ref[...] = _head_rms(seg(ODD_Q, 512), hm, gq_ref[...])
    k_ref[...] = _head_rms(seg(ODD_K, 128), hm[:128, :128], gk_ref[...])
    v_ref[...] = seg(ODD_V, 128)
    qi_ref[...] = seg(ODD_QI, 256)
    ki_ref[...] = seg(ODD_KI, 128)
    wi_ref[...] = seg(ODD_WI, 128)
    gb_ref[...] = seg(ODD_GB, 512)
    cin_ref[...] = seg(ODD_GC, 512) * seg(ODD_HD, 512)


def _odd_in(x, g, w, hm, gq, gk, tm):
    n_rows, d = x.shape
    row = lambda i: (i, 0)
    const = lambda i: (0, 0)
    widths = (512, 128, 128, 256, 128, 128, 512, 512)
    return pl.pallas_call(
        _odd_in_kernel,
        grid=(n_rows // tm,),
        in_specs=[pl.BlockSpec((tm, d), row), pl.BlockSpec((1, d), const),
                  pl.BlockSpec(w.shape, const), pl.BlockSpec(hm.shape, const),
                  pl.BlockSpec((1, 512), const), pl.BlockSpec((1, 128), const)],
        out_specs=[pl.BlockSpec((tm, n), row) for n in widths],
        out_shape=[jax.ShapeDtypeStruct((n_rows, n), F32) for n in widths],
        compiler_params=_cparams(("parallel",)),
        name="odd_in",
    )(x, g, w, hm, gq, gk)


STICK_TK = 256


def _stick_tri():
    half = STICK_TK // 2
    later = (np.arange(half)[:, None] > np.arange(half)[None, :]).astype(np.float32)
    blk = np.concatenate([later, np.ones((half, half), np.float32)], axis=1)
    return jnp.asarray(np.concatenate([blk, blk], axis=0), BF16)


def _stick_kernel(q_ref, k_ref, v_ref, tri_ref, o_ref, carry_sc, acc_sc, *, tq, q_pos0):
    tk = STICK_TK
    half = tk // 2
    n_pairs = q_ref.shape[1] // LANES
    v_split = v_ref.shape[1] == 2 * q_ref.shape[1]
    q_start = q_pos0 + pl.program_id(1) * tq
    n_full = q_start // tk
    lane = lax.broadcasted_iota(jnp.int32, (1, LANES), 1)
    first = lane < D_HEAD
    q = q_ref[...] * (D_HEAD ** -0.5)
    q2 = []
    for p in range(n_pairs):
        qp = q[:, p * LANES:(p + 1) * LANES]
        q2.append(jnp.concatenate([jnp.where(first, qp, 0.0), jnp.where(first, 0.0, qp)], axis=0).astype(BF16))
    tri = tri_ref[...]
    carry_sc[...] = jnp.zeros_like(carry_sc)
    acc_sc[...] = jnp.zeros_like(acc_sc)

    def block(kb, masked):
        ks = pl.multiple_of(kb * tk, tk)
        if masked:
            row = lax.broadcasted_iota(jnp.int32, (2 * tq, tk), 0)
            qpos = q_start + jnp.where(row >= tq, row - tq, row)
            kpos = ks + lax.broadcasted_iota(jnp.int32, (2 * tq, tk), 1)
            ok = kpos < qpos
        for p in range(n_pairs):
            cols = slice(p * LANES, (p + 1) * LANES)
            z = _dot_t(q2[p], k_ref[pl.ds(ks, tk), cols].astype(BF16))
            nz = -z
            stay = jnp.minimum(nz, 0.0) - jnp.log(1.0 + jnp.exp(jnp.minimum(z, nz)))
            if masked:
                stay = jnp.where(ok, stay, 0.0)
            carry = carry_sc[p]
            after = [None, None]
            for h in (1, 0):
                hi, lo = _split(stay[:, h * half:(h + 1) * half])
                r = _dot(jnp.concatenate([hi, lo], axis=1), tri)
                after[h] = r[:, :half] + carry
                carry = carry + r[:, half:]
            carry_sc[p] = carry
            w = jnp.exp(z + stay + jnp.concatenate(after, axis=1))
            if masked:
                w = jnp.where(ok, w, 0.0)
            w = w.astype(BF16)
            if v_split:
                v0 = v_ref[pl.ds(ks, tk), 2 * p * LANES:(2 * p + 1) * LANES]
                v1 = v_ref[pl.ds(ks, tk), (2 * p + 1) * LANES:(2 * p + 2) * LANES]
            else:
                vblk = v_ref[pl.ds(ks, tk), cols]
                v0 = jnp.where(first, vblk, 0.0).astype(BF16)
                v1 = jnp.where(first, 0.0, vblk).astype(BF16)
            acc_sc[p] += _dot(jnp.concatenate([w[:tq], w[tq:]], axis=1), jnp.concatenate([v0, v1], axis=0))

    block(n_full, True)

    def body(i, c):
        block(n_full - 1 - i, False)
        return c

    lax.fori_loop(0, n_full, body, 0)
    for p in range(n_pairs):
        o_ref[:, p * LANES:(p + 1) * LANES] = acc_sc[p]


def _stick(q, k, v, tri, tq, q_pos0):
    b, t, c = q.shape
    lp = k.shape[1]
    tk = STICK_TK
    assert lp % tk == 0 and q_pos0 % tq == 0 and tk % tq == 0 and q_pos0 + t <= lp
    kern = functools.partial(_stick_kernel, tq=tq, q_pos0=q_pos0)
    kmap = lambda bi, i: (bi, 0, 0)
    return pl.pallas_call(
        kern,
        grid=(b, t // tq),
        in_specs=[pl.BlockSpec((None, tq, c), lambda bi, i: (bi, i, 0)),
                  pl.BlockSpec((None, lp, c), kmap),
                  pl.BlockSpec((None, lp, v.shape[2]), kmap),
                  pl.BlockSpec(tri.shape, lambda bi, i: (0, 0))],
        out_specs=pl.BlockSpec((None, tq, c), lambda bi, i: (bi, i, 0)),
        out_shape=jax.ShapeDtypeStruct((b, t, c), F32),
        scratch_shapes=[pltpu.VMEM((c // LANES, 2 * tq, LANES), F32), pltpu.VMEM((c // LANES, tq, LANES), F32)],
        compiler_params=_cparams(("parallel", "arbitrary")),
        name="stick_attn",
    )(q, k, v, tri)


DSA_LEVEL = 512


def _dsa_kernel(q_ref, qi_ref, wi_ref, k_ref, v_ref, ki_ref, o_ref, key_sc, bias_sc,
                *, tq, l_valid, q_pos0, top_k, levels):
    kt = KEY_TILE
    th = min(tq, 128)
    parts = [slice(i * th, (i + 1) * th) for i in range(tq // th)]
    n_rep = q_ref.shape[1] // LANES
    lane = lax.broadcasted_iota(jnp.int32, (1, LANES), 1)
    first = lane < D_HEAD
    q_start = q_pos0 + pl.program_id(1) * tq
    kf = jnp.float32(top_k)

    def body(lv):
        n_kt = lv // kt

        def admissible(rows, lo, width):
            qpos = q_start + rows.start + lax.broadcasted_iota(jnp.int32, (th, 1), 0)
            kpos = lo + lax.broadcasted_iota(jnp.int32, (1, width), 1)
            return (jnp.right_shift(kpos, CHUNK_SHIFT) <= jnp.right_shift(qpos, CHUNK_SHIFT)) & (kpos < l_valid)

        ki_hi, ki_lo = _split(ki_ref[:lv, :])
        for rows in parts:
            qi = qi_ref[rows, :]
            wi = wi_ref[rows, :]
            qm = []
            for h in range(H_IDX):
                grp = qi[:, (h // 2) * LANES:(h // 2 + 1) * LANES]
                qm.append(jnp.where(first, grp, 0.0) if h % 2 == 0 else jnp.where(first, 0.0, grp))
            q_hi, q_lo = _split(jnp.concatenate(qm, axis=0))
            s = _dot_t(q_hi, ki_hi) + (_dot_t(q_hi, ki_lo) + _dot_t(q_lo, ki_hi))
            score = jnp.zeros((th, lv), F32)
            for h in range(H_IDX):
                score = score + jnp.maximum(s[h * th:(h + 1) * th], 0.0) * wi[:, h:h + 1]
            score = score * ((D_IDX ** -0.5) * (H_IDX ** -0.5)) + 0.0
            score = jnp.where(admissible(rows, 0, lv), score, NEG)
            bits = lax.bitcast_convert_type(score, jnp.int32)
            key_sc[rows, :lv] = jnp.where(bits < 0, bits ^ jnp.int32(0x7FFFFFFF), bits)

        def count_ge(rows, trial):
            part = jnp.zeros((th, kt), F32)
            for c in range(n_kt):
                part = part + jnp.where(key_sc[rows, c * kt:(c + 1) * kt] >= trial, 1.0, 0.0)
            return jnp.sum(part, axis=1, keepdims=True)

        int_min = jnp.int32(-2 ** 31)
        zero = jnp.zeros((th, 1), jnp.int32)
        thr0 = tuple(jnp.where(count_ge(rows, zero) >= kf, jnp.int32(0), int_min) for rows in parts)

        def bit_step(i, thr):
            bit = jnp.left_shift(jnp.int32(1), jnp.int32(30) - i)
            return tuple(jnp.where(count_ge(rows, t + bit) >= kf, t + bit, t) for rows, t in zip(parts, thr))

        thrs = lax.fori_loop(0, 31, bit_step, thr0)

        rr = lax.broadcasted_iota(jnp.int32, (kt, kt), 0)
        cc = lax.broadcasted_iota(jnp.int32, (kt, kt), 1)
        before = jnp.where(rr < cc, 1.0, 0.0).astype(BF16)
        for rows, thr in zip(parts, thrs):
            n_gt = jnp.zeros((th, kt), F32)
            for c in range(n_kt):
                n_gt = n_gt + jnp.where(key_sc[rows, c * kt:(c + 1) * kt] > thr, 1.0, 0.0)
            room = kf - jnp.sum(n_gt, axis=1, keepdims=True)
            run = jnp.zeros((th, 1), F32)
            for c in range(n_kt):
                keyc = key_sc[rows, c * kt:(c + 1) * kt]
                eqf = jnp.where(keyc == thr, 1.0, 0.0)
                rank = _dot(eqf.astype(BF16), before) + run
                tie_ok = jnp.where(keyc == thr, jnp.where(rank < room, 0.0, NEG), NEG)
                sel = jnp.where(keyc > thr, 0.0, tie_ok)
                bias_sc[rows, c * kt:(c + 1) * kt] = jnp.where(admissible(rows, c * kt, kt), sel, NEG)
                run = run + jnp.sum(eqf, axis=1, keepdims=True)

        kb = k_ref[:lv, :].astype(BF16)
        vf = v_ref[:lv, :]
        vg = [jnp.where(first, vf, 0.0).astype(BF16), jnp.where(first, 0.0, vf).astype(BF16)]
        for rows in parts:
            q = q_ref[rows, :] * (D_HEAD ** -0.5)
            bias = bias_sc[rows, :lv]
            out = jnp.zeros((n_rep * th, LANES), F32)
            for g in range(2):
                keep = first if g == 0 else jnp.logical_not(first)
                qs = jnp.concatenate(
                    [jnp.where(keep, q[:, r * LANES:(r + 1) * LANES], 0.0) for r in range(n_rep)],
                    axis=0).astype(BF16)
                logits = _dot_t(qs, kb)
                logits = (logits.reshape(n_rep, th, lv) + bias[None]).reshape(n_rep * th, lv)
                m = jnp.max(logits, axis=1, keepdims=True)
                p = jnp.exp(logits - m)
                den = jnp.sum(p, axis=1, keepdims=True)
                out = out + _dot(p.astype(BF16), vg[g]) / den
            for r in range(n_rep):
                o_ref[rows, r * LANES:(r + 1) * LANES] = out[r * th:(r + 1) * th]

    if len(levels) == 1:
        body(levels[0])
    else:
        cls = (q_start + tq - 1) // DSA_LEVEL
        for c, lv in enumerate(levels):
            pl.when(cls == c)(functools.partial(body, lv))


def _dsa(q, qi, wi, k, v, ki, tq, q_pos0, l_valid):
    b, t, c = q.shape
    lp = k.shape[1]
    assert lp % KEY_TILE == 0 and l_valid <= lp and t % tq == 0
    top_k = min(TOPK_MAX, l_valid // 4)
    if q_pos0 == 0 and lp % DSA_LEVEL == 0 and tq % CHUNK == 0 and DSA_LEVEL >= top_k:
        levels = tuple(range(DSA_LEVEL, lp + 1, DSA_LEVEL))
    else:
        levels = (lp,)
    kern = functools.partial(_dsa_kernel, tq=tq, l_valid=l_valid, q_pos0=q_pos0, top_k=top_k, levels=levels)
    qmap = lambda bi, i: (bi, i, 0)
    kmap = lambda bi, i: (bi, 0, 0)
    return pl.pallas_call(
        kern,
        grid=(b, t // tq),
        in_specs=[pl.BlockSpec((None, tq, c), qmap), pl.BlockSpec((None, tq, qi.shape[2]), qmap),
                  pl.BlockSpec((None, tq, LANES), qmap),
                  pl.BlockSpec((None, lp, LANES), kmap), pl.BlockSpec((None, lp, LANES), kmap),
                  pl.BlockSpec((None, lp, LANES), kmap)],
        out_specs=pl.BlockSpec((None, tq, c), qmap),
        out_shape=jax.ShapeDtypeStruct((b, t, c), F32),
        scratch_shapes=[pltpu.VMEM((tq, lp), jnp.int32), pltpu.VMEM((tq, lp), F32)],
        compiler_params=_cparams(("parallel", "arbitrary")),
        name="dsa_attn",
    )(q, qi, wi, k, v, ki)


FFN_CHUNK = 1024


def _ffn_tail(x1, gf_ref, w1_ref, w2_ref, o_ref):
    h = _rms(x1, gf_ref[...]).astype(BF16)
    acc = None
    for c in range(w1_ref.shape[1] // FFN_CHUNK):
        cols = slice(c * FFN_CHUNK, (c + 1) * FFN_CHUNK)
        a = jnp.square(jnp.maximum(_dot(h, w1_ref[:, cols]), 0.0))
        part = _dot(a.astype(BF16), w2_ref[cols, :])
        acc = part if acc is None else acc + part
    o_ref[...] = x1 + acc


def _even_out_kernel(x_ref, att_ref, u_ref, vb_ref, ws_ref, bst_ref, w_ref, gf_ref, w1_ref, w2_ref, o_ref,
                     *, p_len):
    tm = x_ref.shape[0]
    cg = vb_ref.shape[1] // G_B
    rr = lax.broadcasted_iota(jnp.int32, (p_len, p_len), 0)
    cc = lax.broadcasted_iota(jnp.int32, (p_len, p_len), 1)
    causal = jnp.right_shift(cc, CHUNK_SHIFT) <= jnp.right_shift(rr, CHUNK_SHIFT)
    u = u_ref[...]
    vb = vb_ref[...].astype(BF16)
    bst = bst_ref[...]
    mixed_rows = []
    for c in range(tm // p_len):
        cols = []
        for g in range(G_B):
            wg = jnp.where(causal, ws_ref[g], 0.0).astype(BF16)
            mg = _dot(wg, vb[c * p_len:(c + 1) * p_len, g * cg:(g + 1) * cg]) + bst[:, g:g + 1]
            cols.append(mg)
        mixed_rows.append(jnp.concatenate(cols, axis=1))
    gated = u * jnp.concatenate(mixed_rows, axis=0)
    na = att_ref.shape[1]
    y = _dot(att_ref[...].astype(BF16), w_ref[:na, :]) + _dot(gated.astype(BF16), w_ref[na:, :])
    _ffn_tail(x_ref[...] + y, gf_ref, w1_ref, w2_ref, o_ref)


def _resident(shape):
    return pl.BlockSpec(shape, lambda i: (0,) * len(shape), pipeline_mode=pl.Buffered(1))


def _even_out_ffn(x, att, u, vb, ws, bst, w, gf, w1, w2, tm, p_len):
    n_rows, d = x.shape
    n = att.shape[1]
    row = lambda i: (i, 0)
    kern = functools.partial(_even_out_kernel, p_len=p_len)
    return pl.pallas_call(
        kern,
        grid=(n_rows // tm,),
        in_specs=[pl.BlockSpec((tm, d), row), pl.BlockSpec((tm, n), row), pl.BlockSpec((tm, n), row),
                  pl.BlockSpec((tm, n), row), _resident(ws.shape), _resident(bst.shape), _resident(w.shape),
                  _resident(gf.shape), _resident(w1.shape), _resident(w2.shape)],
        out_specs=pl.BlockSpec((tm, d), row),
        out_shape=jax.ShapeDtypeStruct((n_rows, d), F32),
        compiler_params=_cparams(("parallel",)),
        name="even_out_ffn",
    )(x, att, u, vb, ws, bst, w, gf, w1, w2)


def _odd_out_kernel(x_ref, att_ref, gb_ref, cin_ref, prev_ref, cw_ref, w_ref, gf_ref, w1_ref, w2_ref, o_ref,
                    *, t_len, has_prev):
    tm = x_ref.shape[0]
    prev = prev_ref[...]
    if not has_prev:
        start = (pl.program_id(0) * tm) % t_len == 0
        prev = jnp.where(start, 0.0, prev)
    cw = cw_ref[...]
    seg = t_len if has_prev else tm
    convs = []
    for s in range(tm // seg):
        cin = cin_ref[s * seg:(s + 1) * seg, :]
        ext = jnp.concatenate([prev[8 * s:8 * (s + 1)], cin], axis=0)
        conv = cw[CONV_W - 1:CONV_W] * cin
        for j in range(1, CONV_W):
            conv = conv + cw[CONV_W - 1 - j:CONV_W - j] * pltpu.roll(ext, j, 0)[8:]
        convs.append(conv)
    gated = gb_ref[...] * (convs[0] if len(convs) == 1 else jnp.concatenate(convs, axis=0))
    na = att_ref.shape[1]
    y = _dot(att_ref[...].astype(BF16), w_ref[:na, :]) + _dot(gated.astype(BF16), w_ref[na:, :])
    _ffn_tail(x_ref[...] + y, gf_ref, w1_ref, w2_ref, o_ref)


def _odd_out_ffn(x, att, gb, cin, prev, cw, w, gf, w1, w2, tm, t_len):
    n_rows, d = x.shape
    n = att.shape[1]
    row = lambda i: (i, 0)
    has_prev = prev is not None
    if has_prev:
        assert tm % t_len == 0 and t_len % 8 == 0
        prev_arr, prev_map, prev_rows = prev, row, 8 * (tm // t_len)
    else:
        assert t_len % tm == 0 and tm % 8 == 0
        prev_arr, prev_map, prev_rows = cin, (lambda i: (jnp.maximum(i * (tm // 8) - 1, 0), 0)), 8
    kern = functools.partial(_odd_out_kernel, t_len=t_len, has_prev=has_prev)
    return pl.pallas_call(
        kern,
        grid=(n_rows // tm,),
        in_specs=[pl.BlockSpec((tm, d), row), pl.BlockSpec((tm, n), row), pl.BlockSpec((tm, n), row),
                  pl.BlockSpec((tm, n), row), pl.BlockSpec((prev_rows, n), prev_map),
                  _resident(cw.shape), _resident(w.shape),
                  _resident(gf.shape), _resident(w1.shape), _resident(w2.shape)],
        out_specs=pl.BlockSpec((tm, d), row),
        out_shape=jax.ShapeDtypeStruct((n_rows, d), F32),
        compiler_params=_cparams(("parallel",)),
        name="odd_out_ffn",
    )(x, att, gb, cin, prev_arr, cw, w, gf, w1, w2)


def _round_up(n, m):
    return (n + m - 1) // m * m


def _row_tile(n_rows, cap):
    t = min(cap, n_rows)
    while n_rows % t:
        t //= 2
    return t


def _odd_weight_layout(n_q):
    n_heads = n_q // D_HEAD
    rep = n_heads // 2
    perm = np.zeros((n_q,), np.int32)
    for r in range(rep):
        for g in range(2):
            for dd in range(D_HEAD):
                perm[r * LANES + g * D_HEAD + dd] = (g * rep + r) * D_HEAD + dd
    return perm


def kernel(x_prompt, x_sample, cache_a_k, cache_a_v, cache_c_k, cache_c_v, cache_c_kidx, state_d_conv,
           norm_mix_g, norm_ffn_g, w_in_even, gq_a, gk_a, g_b, ws_b, bs_b, w_out_even,
           w_in_odd, gq_c, gk_c, conv_w_d, w_out_odd, w_ffn1, w_ffn2):
    bp, tp, d = x_prompt.shape
    bs, ts, _ = x_sample.shape
    depth = norm_mix_g.shape[0]
    past = cache_a_k.shape[2]
    n_a = cache_a_k.shape[3] * cache_a_k.shape[4]
    n_kv = cache_c_k.shape[3] * cache_c_k.shape[4]
    n_b = g_b.shape[1]
    n_d = conv_w_d.shape[2]
    n_q = w_out_odd.shape[1] - n_d
    assert n_a == 512 and n_b == 512 and n_kv == 128 and n_q == 512 and n_d == 512 and d == 1024

    xp = x_prompt.reshape(bp * tp, d)
    xs = x_sample.reshape(bs * ts, d)
    tm_p = _row_tile(bp * tp, 512)
    tm_s = _row_tile(bs * ts, 512)
    ls = past + ts
    ls_pad = _round_up(ls, KEY_TILE)

    hmean = jnp.asarray(np.kron(np.eye(n_a // D_HEAD), np.full((D_HEAD, D_HEAD), 1.0 / D_HEAD)), BF16)
    perm = _odd_weight_layout(n_q)
    tile8 = lambda gvec: jnp.tile(gvec, n_a // D_HEAD).reshape(1, n_a)

    ak_p, av_p, ak_s, av_s, bv_s = [], [], [], [], []
    ck_p, cv_p, ci_p, ck_s, cv_s, ci_s, dc_p, dc_s = [], [], [], [], [], [], [], []

    tri = _stick_tri()

    def pad_keys(past_rows, new_rows, multiple=KEY_TILE):
        width = new_rows.shape[-1]
        zeros = jnp.zeros((bs, _round_up(ls, multiple) - ls, width), F32)
        return jnp.concatenate([past_rows.reshape(bs, past, width), new_rows.reshape(bs, ts, width), zeros], axis=1)

    for i in range(depth):
        j = i // 2
        g_mix = norm_mix_g[i].reshape(1, d)
        ffn = (norm_ffn_g[i].reshape(1, d), w_ffn1[i].astype(BF16), w_ffn2[i].astype(BF16))
        if i % 2 == 0:
            w_in = w_in_even[j].astype(BF16)
            w_out = w_out_even[j].astype(BF16)
            gq, gk, gb = tile8(gq_a[j]), tile8(gk_a[j]), g_b[j].reshape(1, n_b)
            q, k, v, u, vb, kb, vm = _even_in(xp, g_mix, w_in, hmean, gq, gk, gb, tm_p)
            att = _stick(q.reshape(bp, tp, n_a), kb.reshape(bp, tp, n_a), vm.reshape(bp, tp, 2 * n_a), tri,
                         min(tp, KEY_TILE), 0)
            xp = _even_out_ffn(xp, att.reshape(bp * tp, n_a), u, vb, ws_b[j], bs_b[j].T, w_out, *ffn, tm_p, B_CHUNK)
            ak_p.append(k.reshape(bp, tp, n_a // D_HEAD, D_HEAD))
            av_p.append(v.reshape(bp, tp, n_a // D_HEAD, D_HEAD))
            q, k, v, u, vb, _, _ = _even_in(xs, g_mix, w_in, hmean, gq, gk, gb, tm_s)
            att = _stick(q.reshape(bs, ts, n_a), pad_keys(cache_a_k[j], k, STICK_TK),
                         pad_keys(cache_a_v[j], v, STICK_TK), tri, ts, past)
            xs = _even_out_ffn(xs, att.reshape(bs * ts, n_a), u, vb, ws_b[j][:, :ts, :ts], bs_b[j][:, :ts].T,
                               w_out, *ffn, tm_s, ts)
            ak_s.append(k.reshape(bs, ts, n_a // D_HEAD, D_HEAD))
            av_s.append(v.reshape(bs, ts, n_a // D_HEAD, D_HEAD))
            bv_s.append(vb.reshape(bs, ts, n_b))
        else:
            wi_full = w_in_odd[j]
            offs = np.cumsum([0, n_q, n_kv, n_kv, H_IDX * D_IDX, D_IDX, H_IDX, n_d, n_d, n_d])
            col = lambda s: wi_full[:, int(offs[s]):int(offs[s + 1])]
            w_in = jnp.concatenate(
                [col(0)[:, perm], col(1), col(2), col(3), col(4), col(4), col(5),
                 jnp.zeros((d, LANES - H_IDX), F32), col(6), col(7), col(8)], axis=1).astype(BF16)
            assert w_in.shape[1] == ODD_COLS
            w_out = jnp.concatenate([w_out_odd[j][:n_q][perm], w_out_odd[j][n_q:]], axis=0).astype(BF16)
            gq, gk = tile8(gq_c[j]), jnp.tile(gk_c[j], n_kv // D_HEAD).reshape(1, n_kv)
            cw = conv_w_d[j]
            q, k, v, qi, ki, wi, gbt, cin = _odd_in(xp, g_mix, w_in, hmean, gq, gk, tm_p)
            r3 = lambda a: a.reshape(bp, tp, a.shape[-1])
            att = _dsa(r3(q), r3(qi), r3(wi), r3(k), r3(v), r3(ki), min(tp, 2 * KEY_TILE), 0, tp)
            xp = _odd_out_ffn(xp, att.reshape(bp * tp, n_q), gbt, cin, None, cw, w_out, *ffn, tm_p, tp)
            ck_p.append(k.reshape(bp, tp, n_kv // D_HEAD, D_HEAD))
            cv_p.append(v.reshape(bp, tp, n_kv // D_HEAD, D_HEAD))
            ci_p.append(ki[:, :D_IDX].reshape(bp, tp, D_IDX))
            dc_p.append(cin.reshape(bp, tp, n_d)[:, tp - (CONV_W - 1):])
            q, k, v, qi, ki, wi, gbt, cin = _odd_in(xs, g_mix, w_in, hmean, gq, gk, tm_s)
            r3 = lambda a: a.reshape(bs, ts, a.shape[-1])
            past_ki = jnp.concatenate([cache_c_kidx[j], cache_c_kidx[j]], axis=-1)
            att = _dsa(r3(q), r3(qi), r3(wi), pad_keys(cache_c_k[j], k), pad_keys(cache_c_v[j], v),
                       pad_keys(past_ki, ki), ts, past, ls)
            prev = jnp.concatenate([jnp.zeros((bs, 8 - (CONV_W - 1), n_d), F32), state_d_conv[j]], axis=1)
            xs = _odd_out_ffn(xs, att.reshape(bs * ts, n_q), gbt, cin, prev.reshape(bs * 8, n_d), cw, w_out, *ffn,
                              tm_s, ts)
            ck_s.append(k.reshape(bs, ts, n_kv // D_HEAD, D_HEAD))
            cv_s.append(v.reshape(bs, ts, n_kv // D_HEAD, D_HEAD))
            ci_s.append(ki[:, :D_IDX].reshape(bs, ts, D_IDX))
            dc_s.append(cin.reshape(bs, ts, n_d)[:, ts - (CONV_W - 1):])

    return (xp.reshape(bp, tp, d), xs.reshape(bs, ts, d),
            jnp.stack(ak_p), jnp.stack(av_p), jnp.stack(ak_s), jnp.stack(av_s),
            jnp.stack(bv_s),
            jnp.stack(ck_p), jnp.stack(cv_p), jnp.stack(ci_p),
            jnp.stack(ck_s), jnp.stack(cv_s), jnp.stack(ci_s),
            jnp.stack(dc_p), jnp.stack(dc_s))
```

```python
import functools

import numpy as np
import jax
import jax.numpy as jnp
from jax import lax
from jax.experimental import pallas as pl
from jax.experimental.pallas import tpu as pltpu

F32 = jnp.float32
BF16 = jnp.bfloat16

EPS = 1e-6
NEG = -1e30
CHUNK = 64
CHUNK_SHIFT = 6
D_HEAD = 64
G_B = 4
B_CHUNK = 128
H_IDX = 4
D_IDX = 64
TOPK_MAX = 256
CONV_W = 3
LANES = 128
KEY_TILE = 128
VMEM_LIMIT = 56 * 1024 * 1024


def _cparams(sem):
    return pltpu.CompilerParams(dimension_semantics=sem, vmem_limit_bytes=VMEM_LIMIT)


def _dot(a, b):
    return jnp.dot(a, b, preferred_element_type=F32)


def _dot_t(a, b):
    return lax.dot_general(a, b, (((1,), (1,)), ((), ())), preferred_element_type=F32)


def _split(x):
    hi = x.astype(BF16)
    lo = (x - hi.astype(F32)).astype(BF16)
    return hi, lo


def _rms(x, g):
    return x * lax.rsqrt(jnp.mean(x * x, axis=-1, keepdims=True) + EPS) * g


def _head_rms(x, hmean, g):
    hi, lo = _split(x * x)
    ms = _dot(hi, hmean) + _dot(lo, hmean)
    return x * lax.rsqrt(ms + EPS) * g


def _gelu(x):
    return jax.nn.gelu(x)


def _even_in_kernel(x_ref, g_ref, w_ref, hm_ref, gq_ref, gk_ref, gb_ref,
                    q_ref, k_ref, v_ref, u_ref, vb_ref, kb_ref, vm_ref):
    h = _rms(x_ref[...], g_ref[...]).astype(BF16)
    n = q_ref.shape[-1]

    def seg(i):
        return _dot(h, w_ref[:, i * n:(i + 1) * n])

    hm = hm_ref[...]
    q_ref[...] = _head_rms(seg(0), hm, gq_ref[...])
    k = _head_rms(seg(1), hm, gk_ref[...])
    k_ref[...] = k
    kb_ref[...] = k.astype(BF16)
    v = seg(2)
    v_ref[...] = v
    first = lax.broadcasted_iota(jnp.int32, (1, LANES), 1) < D_HEAD
    for p in range(n // LANES):
        vp = v[:, p * LANES:(p + 1) * LANES]
        vm_ref[:, 2 * p * LANES:(2 * p + 1) * LANES] = jnp.where(first, vp, 0.0).astype(BF16)
        vm_ref[:, (2 * p + 1) * LANES:(2 * p + 2) * LANES] = jnp.where(first, 0.0, vp).astype(BF16)
    u_ref[...] = _gelu(seg(3))
    vb_ref[...] = _rms(_gelu(seg(4)), gb_ref[...])


def _even_in(x, g, w, hm, gq, gk, gb, tm):
    n_rows, d = x.shape
    n = w.shape[1] // 5
    row = lambda i: (i, 0)
    const = lambda i: (0, 0)
    out = jax.ShapeDtypeStruct((n_rows, n), F32)
    return pl.pallas_call(
        _even_in_kernel,
        grid=(n_rows // tm,),
        in_specs=[pl.BlockSpec((tm, d), row), pl.BlockSpec((1, d), const),
                  pl.BlockSpec(w.shape, const), pl.BlockSpec(hm.shape, const),
                  pl.BlockSpec((1, n), const), pl.BlockSpec((1, n), const), pl.BlockSpec((1, n), const)],
        out_specs=[pl.BlockSpec((tm, n), row)] * 6 + [pl.BlockSpec((tm, 2 * n), row)],
        out_shape=[out] * 5 + [jax.ShapeDtypeStruct((n_rows, n), BF16), jax.ShapeDtypeStruct((n_rows, 2 * n), BF16)],
        compiler_params=_cparams(("parallel",)),
        name="even_in",
    )(x, g, w, hm, gq, gk, gb)


ODD_Q, ODD_K, ODD_V, ODD_QI, ODD_KI, ODD_WI, ODD_GB, ODD_GC, ODD_HD = (
    0, 512, 640, 768, 1024, 1152, 1280, 1792, 2304)
ODD_COLS = 2816


def _odd_in_kernel(x_ref, g_ref, w_ref, hm_ref, gq_ref, gk_ref,
                   q_ref, k_ref, v_ref, qi_ref, ki_ref, wi_ref, gb_ref, cin_ref):
    h = _rms(x_ref[...], g_ref[...]).astype(BF16)

    def seg(lo, width):
        return _dot(h, w_ref[:, lo:lo + width])

    hm = hm_ref[...]
    q_ref[...] = _head_rms(seg(ODD_Q, 512), hm, gq_ref[...])
    k_ref[...] = _head_rms(seg(ODD_K, 128), hm[:128, :128], gk_ref[...])
    v_ref[...] = seg(ODD_V, 128)
    qi_ref[...] = seg(ODD_QI, 256)
    ki_ref[...] = seg(ODD_KI, 128)
    wi_ref[...] = seg(ODD_WI, 128)
    gb_ref[...] = seg(ODD_GB, 512)
    cin_ref[...] = seg(ODD_GC, 512) * seg(ODD_HD, 512)


def _odd_in(x, g, w, hm, gq, gk, tm):
    n_rows, d = x.shape
    row = lambda i: (i, 0)
    const = lambda i: (0, 0)
    widths = (512, 128, 128, 256, 128, 128, 512, 512)
    return pl.pallas_call(
        _odd_in_kernel,
        grid=(n_rows // tm,),
        in_specs=[pl.BlockSpec((tm, d), row), pl.BlockSpec((1, d), const),
                  pl.BlockSpec(w.shape, const), pl.BlockSpec(hm.shape, const),
                  pl.BlockSpec((1, 512), const), pl.BlockSpec((1, 128), const)],
        out_specs=[pl.BlockSpec((tm, n), row) for n in widths],
        out_shape=[jax.ShapeDtypeStruct((n_rows, n), F32) for n in widths],
        compiler_params=_cparams(("parallel",)),
        name="odd_in",
    )(x, g, w, hm, gq, gk)


STICK_TK = 256
F32_EXP_ZERO = -104.0


def _stick_tri():
    half = STICK_TK // 2
    later = (np.arange(half)[:, None] > np.arange(half)[None, :]).astype(np.float32)
    blk = np.concatenate([later, np.ones((half, half), np.float32)], axis=1)
    return jnp.asarray(np.concatenate([blk, blk], axis=0), BF16)


def _stick_kernel(q_ref, k_ref, v_ref, tri_ref, o_ref, carry_sc, acc_sc, *, tq, q_pos0):
    tk = STICK_TK
    half = tk // 2
    n_pairs = q_ref.shape[1] // LANES
    v_split = v_ref.shape[1] == 2 * q_ref.shape[1]
    q_start = q_pos0 + pl.program_id(1) * tq
    n_full = q_start // tk
    lane = lax.broadcasted_iota(jnp.int32, (1, LANES), 1)
    first = lane < D_HEAD
    q = q_ref[...] * (D_HEAD ** -0.5)
    q2 = []
    for p in range(n_pairs):
        qp = q[:, p * LANES:(p + 1) * LANES]
        q2.append(jnp.concatenate([jnp.where(first, qp, 0.0), jnp.where(first, 0.0, qp)], axis=0).astype(BF16))
    tri = tri_ref[...]
    carry_sc[...] = jnp.zeros_like(carry_sc)
    acc_sc[...] = jnp.zeros_like(acc_sc)

    def block(kb, masked):
        ks = pl.multiple_of(kb * tk, tk)
        if masked:
            row = lax.broadcasted_iota(jnp.int32, (2 * tq, tk), 0)
            qpos = q_start + jnp.where(row >= tq, row - tq, row)
            kpos = ks + lax.broadcasted_iota(jnp.int32, (2 * tq, tk), 1)
            ok = kpos < qpos
        for p in range(n_pairs):
            cols = slice(p * LANES, (p + 1) * LANES)
            z = _dot_t(q2[p], k_ref[pl.ds(ks, tk), cols].astype(BF16))
            nz = -z
            stay = jnp.minimum(nz, 0.0) - jnp.log(1.0 + jnp.exp(jnp.minimum(z, nz)))
            if masked:
                stay = jnp.where(ok, stay, 0.0)
            carry = carry_sc[p]
            after = [None, None]
            for h in (1, 0):
                hi, lo = _split(stay[:, h * half:(h + 1) * half])
                r = _dot(jnp.concatenate([hi, lo], axis=1), tri)
                after[h] = r[:, :half] + carry
                carry = carry + r[:, half:]
            carry_sc[p] = carry
            w = jnp.exp(z + stay + jnp.concatenate(after, axis=1))
            if masked:
                w = jnp.where(ok, w, 0.0)
            w = w.astype(BF16)
            if v_split:
                v0 = v_ref[pl.ds(ks, tk), 2 * p * LANES:(2 * p + 1) * LANES]
                v1 = v_ref[pl.ds(ks, tk), (2 * p + 1) * LANES:(2 * p + 2) * LANES]
            else:
                vblk = v_ref[pl.ds(ks, tk), cols]
                v0 = jnp.where(first, vblk, 0.0).astype(BF16)
                v1 = jnp.where(first, 0.0, vblk).astype(BF16)
            acc_sc[p] += _dot(jnp.concatenate([w[:tq], w[tq:]], axis=1), jnp.concatenate([v0, v1], axis=0))

    block(n_full, True)

    def any_weight_left():
        m = jnp.max(jnp.max(carry_sc[...], axis=0), axis=0, keepdims=True)
        return jnp.max(m, axis=1, keepdims=True)[0, 0] > F32_EXP_ZERO

    def body(s):
        block(s[0], False)
        return s[0] - 1, any_weight_left()

    lax.while_loop(lambda s: (s[0] >= 0) & s[1], body, (n_full - 1, any_weight_left()))
    for p in range(n_pairs):
        o_ref[:, p * LANES:(p + 1) * LANES] = acc_sc[p]


def _stick(q, k, v, tri, tq, q_pos0):
    b, t, c = q.shape
    lp = k.shape[1]
    tk = STICK_TK
    assert lp % tk == 0 and q_pos0 % tq == 0 and tk % tq == 0 and q_pos0 + t <= lp
    kern = functools.partial(_stick_kernel, tq=tq, q_pos0=q_pos0)
    kmap = lambda bi, i: (bi, 0, 0)
    return pl.pallas_call(
        kern,
        grid=(b, t // tq),
        in_specs=[pl.BlockSpec((None, tq, c), lambda bi, i: (bi, i, 0)),
                  pl.BlockSpec((None, lp, c), kmap),
                  pl.BlockSpec((None, lp, v.shape[2]), kmap),
                  pl.BlockSpec(tri.shape, lambda bi, i: (0, 0))],
        out_specs=pl.BlockSpec((None, tq, c), lambda bi, i: (bi, i, 0)),
        out_shape=jax.ShapeDtypeStruct((b, t, c), F32),
        scratch_shapes=[pltpu.VMEM((c // LANES, 2 * tq, LANES), F32), pltpu.VMEM((c // LANES, tq, LANES), F32)],
        compiler_params=_cparams(("parallel", "arbitrary")),
        name="stick_attn",
    )(q, k, v, tri)


DSA_LEVEL = 256


def _dsa_kernel(q_ref, qi_ref, wi_ref, k_ref, v_ref, ki_ref, o_ref, key_sc, bias_sc,
                *, tq, l_valid, q_pos0, top_k, levels):
    kt = KEY_TILE
    th = min(tq, 128)
    parts = [slice(i * th, (i + 1) * th) for i in range(tq // th)]
    n_rep = q_ref.shape[1] // LANES
    lane = lax.broadcasted_iota(jnp.int32, (1, LANES), 1)
    first = lane < D_HEAD
    q_start = q_pos0 + pl.program_id(1) * tq
    kf = jnp.float32(top_k)

    def body(lv):
        n_kt = lv // kt

        def admissible(rows, lo, width):
            qpos = q_start + rows.start + lax.broadcasted_iota(jnp.int32, (th, 1), 0)
            kpos = lo + lax.broadcasted_iota(jnp.int32, (1, width), 1)
            return (jnp.right_shift(kpos, CHUNK_SHIFT) <= jnp.right_shift(qpos, CHUNK_SHIFT)) & (kpos < l_valid)

        ki_hi, ki_lo = _split(ki_ref[:lv, :])
        for rows in parts:
            qi = qi_ref[rows, :]
            wi = wi_ref[rows, :]
            qm = []
            for h in range(H_IDX):
                grp = qi[:, (h // 2) * LANES:(h // 2 + 1) * LANES]
                qm.append(jnp.where(first, grp, 0.0) if h % 2 == 0 else jnp.where(first, 0.0, grp))
            q_hi, q_lo = _split(jnp.concatenate(qm, axis=0))
            s = _dot_t(q_hi, ki_hi) + (_dot_t(q_hi, ki_lo) + _dot_t(q_lo, ki_hi))
            score = jnp.zeros((th, lv), F32)
            for h in range(H_IDX):
                score = score + jnp.maximum(s[h * th:(h + 1) * th], 0.0) * wi[:, h:h + 1]
            score = score * ((D_IDX ** -0.5) * (H_IDX ** -0.5)) + 0.0
            score = jnp.where(admissible(rows, 0, lv), score, NEG)
            bits = lax.bitcast_convert_type(score, jnp.int32)
            key_sc[rows, :lv] = jnp.where(bits < 0, bits ^ jnp.int32(0x7FFFFFFF), bits)

        def count_ge(rows, trial):
            part = jnp.zeros((th, kt), F32)
            for c in range(n_kt):
                part = part + jnp.where(key_sc[rows, c * kt:(c + 1) * kt] >= trial, 1.0, 0.0)
            return jnp.sum(part, axis=1, keepdims=True)

        int_min = jnp.int32(-2 ** 31)
        zero = jnp.zeros((th, 1), jnp.int32)
        thr0 = tuple(jnp.where(count_ge(rows, zero) >= kf, jnp.int32(0), int_min) for rows in parts)

        def bit_step(i, thr):
            bit = jnp.left_shift(jnp.int32(1), jnp.int32(30) - i)
            return tuple(jnp.where(count_ge(rows, t + bit) >= kf, t + bit, t) for rows, t in zip(parts, thr))

        thrs = lax.fori_loop(0, 31, bit_step, thr0)

        rr = lax.broadcasted_iota(jnp.int32, (kt, kt), 0)
        cc = lax.broadcasted_iota(jnp.int32, (kt, kt), 1)
        before = jnp.where(rr < cc, 1.0, 0.0).astype(BF16)
        for rows, thr in zip(parts, thrs):
            n_gt = jnp.zeros((th, kt), F32)
            for c in range(n_kt):
                n_gt = n_gt + jnp.where(key_sc[rows, c * kt:(c + 1) * kt] > thr, 1.0, 0.0)
            room = kf - jnp.sum(n_gt, axis=1, keepdims=True)
            run = jnp.zeros((th, 1), F32)
            for c in range(n_kt):
                keyc = key_sc[rows, c * kt:(c + 1) * kt]
                eqf = jnp.where(keyc == thr, 1.0, 0.0)
                rank = _dot(eqf.astype(BF16), before) + run
                tie_ok = jnp.where(keyc == thr, jnp.where(rank < room, 0.0, NEG), NEG)
                sel = jnp.where(keyc > thr, 0.0, tie_ok)
                bias_sc[rows, c * kt:(c + 1) * kt] = jnp.where(admissible(rows, c * kt, kt), sel, NEG)
                run = run + jnp.sum(eqf, axis=1, keepdims=True)

        kb = k_ref[:lv, :].astype(BF16)
        vf = v_ref[:lv, :]
        vg = [jnp.where(first, vf, 0.0).astype(BF16), jnp.where(first, 0.0, vf).astype(BF16)]
        for rows in parts:
            q = q_ref[rows, :] * (D_HEAD ** -0.5)
            bias = bias_sc[rows, :lv]
            out = jnp.zeros((n_rep * th, LANES), F32)
            for g in range(2):
                keep = first if g == 0 else jnp.logical_not(first)
                qs = jnp.concatenate(
                    [jnp.where(keep, q[:, r * LANES:(r + 1) * LANES], 0.0) for r in range(n_rep)],
                    axis=0).astype(BF16)
                logits = _dot_t(qs, kb)
                logits = (logits.reshape(n_rep, th, lv) + bias[None]).reshape(n_rep * th, lv)
                m = jnp.max(logits, axis=1, keepdims=True)
                p = jnp.exp(logits - m)
                den = jnp.sum(p, axis=1, keepdims=True)
                out = out + _dot(p.astype(BF16), vg[g]) / den
            for r in range(n_rep):
                o_ref[rows, r * LANES:(r + 1) * LANES] = out[r * th:(r + 1) * th]

    if len(levels) == 1:
        body(levels[0])
    else:
        cls = (q_start + tq - 1) // DSA_LEVEL
        for c, lv in enumerate(levels):
            pl.when(cls == c)(functools.partial(body, lv))


def _dsa(q, qi, wi, k, v, ki, tq, q_pos0, l_valid):
    b, t, c = q.shape
    lp = k.shape[1]
    assert lp % KEY_TILE == 0 and l_valid <= lp and t % tq == 0
    top_k = min(TOPK_MAX, l_valid // 4)
    if q_pos0 == 0 and lp % DSA_LEVEL == 0 and tq % CHUNK == 0 and DSA_LEVEL >= top_k:
        levels = tuple(range(DSA_LEVEL, lp + 1, DSA_LEVEL))
    else:
        levels = (lp,)
    kern = functools.partial(_dsa_kernel, tq=tq, l_valid=l_valid, q_pos0=q_pos0, top_k=top_k, levels=levels)
    qmap = lambda bi, i: (bi, i, 0)
    kmap = lambda bi, i: (bi, 0, 0)
    return pl.pallas_call(
        kern,
        grid=(b, t // tq),
        in_specs=[pl.BlockSpec((None, tq, c), qmap), pl.BlockSpec((None, tq, qi.shape[2]), qmap),
                  pl.BlockSpec((None, tq, LANES), qmap),
                  pl.BlockSpec((None, lp, LANES), kmap), pl.BlockSpec((None, lp, LANES), kmap),
                  pl.BlockSpec((None, lp, LANES), kmap)],
        out_specs=pl.BlockSpec((None, tq, c), qmap),
        out_shape=jax.ShapeDtypeStruct((b, t, c), F32),
        scratch_shapes=[pltpu.VMEM((tq, lp), jnp.int32), pltpu.VMEM((tq, lp), F32)],
        compiler_params=_cparams(("parallel", "arbitrary")),
        name="dsa_attn",
    )(q, qi, wi, k, v, ki)


FFN_CHUNK = 1024


def _ffn_tail(x1, gf_ref, w1_ref, w2_ref, o_ref):
    h = _rms(x1, gf_ref[...]).astype(BF16)
    acc = None
    for c in range(w1_ref.shape[1] // FFN_CHUNK):
        cols = slice(c * FFN_CHUNK, (c + 1) * FFN_CHUNK)
        a = jnp.square(jnp.maximum(_dot(h, w1_ref[:, cols]), 0.0))
        part = _dot(a.astype(BF16), w2_ref[cols, :])
        acc = part if acc is None else acc + part
    o_ref[...] = x1 + acc


def _even_out_kernel(x_ref, att_ref, u_ref, vb_ref, ws_ref, bst_ref, w_ref, gf_ref, w1_ref, w2_ref, o_ref,
                     *, p_len):
    tm = x_ref.shape[0]
    cg = vb_ref.shape[1] // G_B
    rr = lax.broadcasted_iota(jnp.int32, (p_len, p_len), 0)
    cc = lax.broadcasted_iota(jnp.int32, (p_len, p_len), 1)
    causal = jnp.right_shift(cc, CHUNK_SHIFT) <= jnp.right_shift(rr, CHUNK_SHIFT)
    u = u_ref[...]
    vb = vb_ref[...].astype(BF16)
    bst = bst_ref[...]
    mixed_rows = []
    for c in range(tm // p_len):
        cols = []
        for g in range(G_B):
            wg = jnp.where(causal, ws_ref[g], 0.0).astype(BF16)
            mg = _dot(wg, vb[c * p_len:(c + 1) * p_len, g * cg:(g + 1) * cg]) + bst[:, g:g + 1]
            cols.append(mg)
        mixed_rows.append(jnp.concatenate(cols, axis=1))
    gated = u * jnp.concatenate(mixed_rows, axis=0)
    na = att_ref.shape[1]
    y = _dot(att_ref[...].astype(BF16), w_ref[:na, :]) + _dot(gated.astype(BF16), w_ref[na:, :])
    _ffn_tail(x_ref[...] + y, gf_ref, w1_ref, w2_ref, o_ref)


def _resident(shape):
    return pl.BlockSpec(shape, lambda i: (0,) * len(shape), pipeline_mode=pl.Buffered(1))


def _even_out_ffn(x, att, u, vb, ws, bst, w, gf, w1, w2, tm, p_len):
    n_rows, d = x.shape
    n = att.shape[1]
    row = lambda i: (i, 0)
    kern = functools.partial(_even_out_kernel, p_len=p_len)
    return pl.pallas_call(
        kern,
        grid=(n_rows // tm,),
        in_specs=[pl.BlockSpec((tm, d), row), pl.BlockSpec((tm, n), row), pl.BlockSpec((tm, n), row),
                  pl.BlockSpec((tm, n), row), _resident(ws.shape), _resident(bst.shape), _resident(w.shape),
                  _resident(gf.shape), _resident(w1.shape), _resident(w2.shape)],
        out_specs=pl.BlockSpec((tm, d), row),
        out_shape=jax.ShapeDtypeStruct((n_rows, d), F32),
        compiler_params=_cparams(("parallel",)),
        name="even_out_ffn",
    )(x, att, u, vb, ws, bst, w, gf, w1, w2)


def _odd_out_kernel(x_ref, att_ref, gb_ref, cin_ref, prev_ref, cw_ref, w_ref, gf_ref, w1_ref, w2_ref, o_ref,
                    *, t_len, has_prev):
    tm = x_ref.shape[0]
    prev = prev_ref[...]
    if not has_prev:
        start = (pl.program_id(0) * tm) % t_len == 0
        prev = jnp.where(start, 0.0, prev)
    cw = cw_ref[...]
    seg = t_len if has_prev else tm
    convs = []
    for s in range(tm // seg):
        cin = cin_ref[s * seg:(s + 1) * seg, :]
        ext = jnp.concatenate([prev[8 * s:8 * (s + 1)], cin], axis=0)
        conv = cw[CONV_W - 1:CONV_W] * cin
        for j in range(1, CONV_W):
            conv = conv + cw[CONV_W - 1 - j:CONV_W - j] * pltpu.roll(ext, j, 0)[8:]
        convs.append(conv)
    gated = gb_ref[...] * (convs[0] if len(convs) == 1 else jnp.concatenate(convs, axis=0))
    na = att_ref.shape[1]
    y = _dot(att_ref[...].astype(BF16), w_ref[:na, :]) + _dot(gated.astype(BF16), w_ref[na:, :])
    _ffn_tail(x_ref[...] + y, gf_ref, w1_ref, w2_ref, o_ref)


def _odd_out_ffn(x, att, gb, cin, prev, cw, w, gf, w1, w2, tm, t_len):
    n_rows, d = x.shape
    n = att.shape[1]
    row = lambda i: (i, 0)
    has_prev = prev is not None
    if has_prev:
        assert tm % t_len == 0 and t_len % 8 == 0
        prev_arr, prev_map, prev_rows = prev, row, 8 * (tm // t_len)
    else:
        assert t_len % tm == 0 and tm % 8 == 0
        prev_arr, prev_map, prev_rows = cin, (lambda i: (jnp.maximum(i * (tm // 8) - 1, 0), 0)), 8
    kern = functools.partial(_odd_out_kernel, t_len=t_len, has_prev=has_prev)
    return pl.pallas_call(
        kern,
        grid=(n_rows // tm,),
        in_specs=[pl.BlockSpec((tm, d), row), pl.BlockSpec((tm, n), row), pl.BlockSpec((tm, n), row),
                  pl.BlockSpec((tm, n), row), pl.BlockSpec((prev_rows, n), prev_map),
                  _resident(cw.shape), _resident(w.shape),
                  _resident(gf.shape), _resident(w1.shape), _resident(w2.shape)],
        out_specs=pl.BlockSpec((tm, d), row),
        out_shape=jax.ShapeDtypeStruct((n_rows, d), F32),
        compiler_params=_cparams(("parallel",)),
        name="odd_out_ffn",
    )(x, att, gb, cin, prev_arr, cw, w, gf, w1, w2)


def _round_up(n, m):
    return (n + m - 1) // m * m


def _row_tile(n_rows, cap):
    t = min(cap, n_rows)
    while n_rows % t:
        t //= 2
    return t


def _odd_weight_layout(n_q):
    n_heads = n_q // D_HEAD
    rep = n_heads // 2
    perm = np.zeros((n_q,), np.int32)
    for r in range(rep):
        for g in range(2):
            for dd in range(D_HEAD):
                perm[r * LANES + g * D_HEAD + dd] = (g * rep + r) * D_HEAD + dd
    return perm


def kernel(x_prompt, x_sample, cache_a_k, cache_a_v, cache_c_k, cache_c_v, cache_c_kidx, state_d_conv,
           norm_mix_g, norm_ffn_g, w_in_even, gq_a, gk_a, g_b, ws_b, bs_b, w_out_even,
           w_in_odd, gq_c, gk_c, conv_w_d, w_out_odd, w_ffn1, w_ffn2):
    bp, tp, d = x_prompt.shape
    bs, ts, _ = x_sample.shape
    depth = norm_mix_g.shape[0]
    past = cache_a_k.shape[2]
    n_a = cache_a_k.shape[3] * cache_a_k.shape[4]
    n_kv = cache_c_k.shape[3] * cache_c_k.shape[4]
    n_b = g_b.shape[1]
    n_d = conv_w_d.shape[2]
    n_q = w_out_odd.shape[1] - n_d
    assert n_a == 512 and n_b == 512 and n_kv == 128 and n_q == 512 and n_d == 512 and d == 1024

    xp = x_prompt.reshape(bp * tp, d)
    xs = x_sample.reshape(bs * ts, d)
    tm_p = _row_tile(bp * tp, 512)
    tm_s = _row_tile(bs * ts, 512)
    ls = past + ts
    ls_pad = _round_up(ls, KEY_TILE)

    hmean = jnp.asarray(np.kron(np.eye(n_a // D_HEAD), np.full((D_HEAD, D_HEAD), 1.0 / D_HEAD)), BF16)
    perm = _odd_weight_layout(n_q)
    tile8 = lambda gvec: jnp.tile(gvec, n_a // D_HEAD).reshape(1, n_a)

    ak_p, av_p, ak_s, av_s, bv_s = [], [], [], [], []
    ck_p, cv_p, ci_p, ck_s, cv_s, ci_s, dc_p, dc_s = [], [], [], [], [], [], [], []

    tri = _stick_tri()

    def pad_keys(past_rows, new_rows, multiple=KEY_TILE):
        width = new_rows.shape[-1]
        zeros = jnp.zeros((bs, _round_up(ls, multiple) - ls, width), F32)
        return jnp.concatenate([past_rows.reshape(bs, past, width), new_rows.reshape(bs, ts, width), zeros], axis=1)

    for i in range(depth):
        j = i // 2
        g_mix = norm_mix_g[i].reshape(1, d)
        ffn = (norm_ffn_g[i].reshape(1, d), w_ffn1[i].astype(BF16), w_ffn2[i].astype(BF16))
        if i % 2 == 0:
            w_in = w_in_even[j].astype(BF16)
            w_out = w_out_even[j].astype(BF16)
            gq, gk, gb = tile8(gq_a[j]), tile8(gk_a[j]), g_b[j].reshape(1, n_b)
            q, k, v, u, vb, kb, vm = _even_in(xp, g_mix, w_in, hmean, gq, gk, gb, tm_p)
            att = _stick(q.reshape(bp, tp, n_a), kb.reshape(bp, tp, n_a), vm.reshape(bp, tp, 2 * n_a), tri,
                         min(tp, KEY_TILE), 0)
            xp = _even_out_ffn(xp, att.reshape(bp * tp, n_a), u, vb, ws_b[j], bs_b[j].T, w_out, *ffn, tm_p, B_CHUNK)
            ak_p.append(k.reshape(bp, tp, n_a // D_HEAD, D_HEAD))
            av_p.append(v.reshape(bp, tp, n_a // D_HEAD, D_HEAD))
            q, k, v, u, vb, _, _ = _even_in(xs, g_mix, w_in, hmean, gq, gk, gb, tm_s)
            att = _stick(q.reshape(bs, ts, n_a), pad_keys(cache_a_k[j], k, STICK_TK),
                         pad_keys(cache_a_v[j], v, STICK_TK), tri, ts, past)
            xs = _even_out_ffn(xs, att.reshape(bs * ts, n_a), u, vb, ws_b[j][:, :ts, :ts], bs_b[j][:, :ts].T,
                               w_out, *ffn, tm_s, ts)
            ak_s.append(k.reshape(bs, ts, n_a // D_HEAD, D_HEAD))
            av_s.append(v.reshape(bs, ts, n_a // D_HEAD, D_HEAD))
            bv_s.append(vb.reshape(bs, ts, n_b))
        else:
            wi_full = w_in_odd[j]
            offs = np.cumsum([0, n_q, n_kv, n_kv, H_IDX * D_IDX, D_IDX, H_IDX, n_d, n_d, n_d])
            col = lambda s: wi_full[:, int(offs[s]):int(offs[s + 1])]
            w_in = jnp.concatenate(
                [col(0)[:, perm], col(1), col(2), col(3), col(4), col(4), col(5),
                 jnp.zeros((d, LANES - H_IDX), F32), col(6), col(7), col(8)], axis=1).astype(BF16)
            assert w_in.shape[1] == ODD_COLS
            w_out = jnp.concatenate([w_out_odd[j][:n_q][perm], w_out_odd[j][n_q:]], axis=0).astype(BF16)
            gq, gk = tile8(gq_c[j]), jnp.tile(gk_c[j], n_kv // D_HEAD).reshape(1, n_kv)
            cw = conv_w_d[j]
            q, k, v, qi, ki, wi, gbt, cin = _odd_in(xp, g_mix, w_in, hmean, gq, gk, tm_p)
            r3 = lambda a: a.reshape(bp, tp, a.shape[-1])
            att = _dsa(r3(q), r3(qi), r3(wi), r3(k), r3(v), r3(ki), min(tp, 2 * KEY_TILE), 0, tp)
            xp = _odd_out_ffn(xp, att.reshape(bp * tp, n_q), gbt, cin, None, cw, w_out, *ffn, tm_p, tp)
            ck_p.append(k.reshape(bp, tp, n_kv // D_HEAD, D_HEAD))
            cv_p.append(v.reshape(bp, tp, n_kv // D_HEAD, D_HEAD))
            ci_p.append(ki[:, :D_IDX].reshape(bp, tp, D_IDX))
            dc_p.append(cin.reshape(bp, tp, n_d)[:, tp - (CONV_W - 1):])
            q, k, v, qi, ki, wi, gbt, cin = _odd_in(xs, g_mix, w_in, hmean, gq, gk, tm_s)
            r3 = lambda a: a.reshape(bs, ts, a.shape[-1])
            past_ki = jnp.concatenate([cache_c_kidx[j], cache_c_kidx[j]], axis=-1)
            att = _dsa(r3(q), r3(qi), r3(wi), pad_keys(cache_c_k[j], k), pad_keys(cache_c_v[j], v),
                       pad_keys(past_ki, ki), ts, past, ls)
            prev = jnp.concatenate([jnp.zeros((bs, 8 - (CONV_W - 1), n_d), F32), state_d_conv[j]], axis=1)
            xs = _odd_out_ffn(xs, att.reshape(bs * ts, n_q), gbt, cin, prev.reshape(bs * 8, n_d), cw, w_out, *ffn,
                              tm_s, ts)
            ck_s.append(k.reshape(bs, ts, n_kv // D_HEAD, D_HEAD))
            cv_s.append(v.reshape(bs, ts, n_kv // D_HEAD, D_HEAD))
            ci_s.append(ki[:, :D_IDX].reshape(bs, ts, D_IDX))
            dc_s.append(cin.reshape(bs, ts, n_d)[:, ts - (CONV_W - 1):])

    return (xp.reshape(bp, tp, d), xs.reshape(bs, ts, d),
            jnp.stack(ak_p), jnp.stack(av_p), jnp.stack(ak_s), jnp.stack(av_s),
            jnp.stack(bv_s),
            jnp.stack(ck_p), jnp.stack(cv_p), jnp.stack(ci_p),
            jnp.stack(ck_s), jnp.stack(cv_s), jnp.stack(ci_s),
            jnp.stack(dc_p), jnp.stack(dc_s))
```

```python
import functools

import numpy as np
import jax
import jax.numpy as jnp
from jax import lax
from jax.experimental import pallas as pl
from jax.experimental.pallas import tpu as pltpu

F32 = jnp.float32
BF16 = jnp.bfloat16

EPS = 1e-6
NEG = -1e30
CHUNK = 64
CHUNK_SHIFT = 6
D_HEAD = 64
G_B = 4
B_CHUNK = 128
H_IDX = 4
D_IDX = 64
TOPK_MAX = 256
CONV_W = 3
LANES = 128
KEY_TILE = 128
VMEM_LIMIT = 56 * 1024 * 1024


def _cparams(sem):
    return pltpu.CompilerParams(dimension_semantics=sem, vmem_limit_bytes=VMEM_LIMIT)


def _dot(a, b):
    return jnp.dot(a, b, preferred_element_type=F32)


def _dot_t(a, b):
    return lax.dot_general(a, b, (((1,), (1,)), ((), ())), preferred_element_type=F32)


def _split(x):
    hi = x.astype(BF16)
    lo = (x - hi.astype(F32)).astype(BF16)
    return hi, lo


def _rms(x, g):
    return x * lax.rsqrt(jnp.mean(x * x, axis=-1, keepdims=True) + EPS) * g


def _head_rms(x, hmean, g):
    hi, lo = _split(x * x)
    ms = _dot(hi, hmean) + _dot(lo, hmean)
    return x * lax.rsqrt(ms + EPS) * g


def _gelu(x):
    return jax.nn.gelu(x)


def _even_in_kernel(x_ref, g_ref, w_ref, hm_ref, gq_ref, gk_ref, gb_ref,
                    q_ref, k_ref, v_ref, u_ref, vb_ref, kb_ref, vm_ref):
    h = _rms(x_ref[...], g_ref[...]).astype(BF16)
    n = q_ref.shape[-1]

    def seg(i):
        return _dot(h, w_ref[:, i * n:(i + 1) * n])

    hm = hm_ref[...]
    q_ref[...] = _head_rms(seg(0), hm, gq_ref[...])
    k = _head_rms(seg(1), hm, gk_ref[...])
    k_ref[...] = k
    kb_ref[...] = k.astype(BF16)
    v = seg(2)
    v_ref[...] = v
    first = lax.broadcasted_iota(jnp.int32, (1, LANES), 1) < D_HEAD
    for p in range(n // LANES):
        vp = v[:, p * LANES:(p + 1) * LANES]
        vm_ref[:, 2 * p * LANES:(2 * p + 1) * LANES] = jnp.where(first, vp, 0.0).astype(BF16)
        vm_ref[:, (2 * p + 1) * LANES:(2 * p + 2) * LANES] = jnp.where(first, 0.0, vp).astype(BF16)
    u_ref[...] = _gelu(seg(3))
    vb_ref[...] = _rms(_gelu(seg(4)), gb_ref[...])


def _even_in(x, g, w, hm, gq, gk, gb, tm):
    n_rows, d = x.shape
    n = w.shape[1] // 5
    row = lambda i: (i, 0)
    const = lambda i: (0, 0)
    out = jax.ShapeDtypeStruct((n_rows, n), F32)
    return pl.pallas_call(
        _even_in_kernel,
        grid=(n_rows // tm,),
        in_specs=[pl.BlockSpec((tm, d), row), pl.BlockSpec((1, d), const),
                  pl.BlockSpec(w.shape, const), pl.BlockSpec(hm.shape, const),
                  pl.BlockSpec((1, n), const), pl.BlockSpec((1, n), const), pl.BlockSpec((1, n), const)],
        out_specs=[pl.BlockSpec((tm, n), row)] * 6 + [pl.BlockSpec((tm, 2 * n), row)],
        out_shape=[out] * 5 + [jax.ShapeDtypeStruct((n_rows, n), BF16), jax.ShapeDtypeStruct((n_rows, 2 * n), BF16)],
        compiler_params=_cparams(("parallel",)),
        name="even_in",
    )(x, g, w, hm, gq, gk, gb)


ODD_Q, ODD_K, ODD_V, ODD_QI, ODD_KI, ODD_WI, ODD_GB, ODD_GC, ODD_HD = (
    0, 512, 640, 768, 1024, 1152, 1280, 1792, 2304)
ODD_COLS = 2816


def _odd_in_kernel(x_ref, g_ref, w_ref, hm_ref, gq_ref, gk_ref,
                   q_ref, k_ref, v_ref, qi_ref, ki_ref, wi_ref, gb_ref, cin_ref):
    h = _rms(x_ref[...], g_ref[...]).astype(BF16)

    def seg(lo, width):
        return _dot(h, w_ref[:, lo:lo + width])

    hm = hm_ref[...]
    q_ref[...] = _head_rms(seg(ODD_Q, 512), hm, gq_ref[...])
    k_ref[...] = _head_rms(seg(ODD_K, 128), hm[:128, :128], gk_ref[...])
    v_ref[...] = seg(ODD_V, 128)
    qi_ref[...] = seg(ODD_QI, 256)
    ki_ref[...] = seg(ODD_KI, 128)
    wi_ref[...] = seg(ODD_WI, 128)
    gb_ref[...] = seg(ODD_GB, 512)
    cin_ref[...] = seg(ODD_GC, 512) * seg(ODD_HD, 512)


def _odd_in(x, g, w, hm, gq, gk, tm):
    n_rows, d = x.shape
    row = lambda i: (i, 0)
    const = lambda i: (0, 0)
    widths = (512, 128, 128, 256, 128, 128, 512, 512)
    return pl.pallas_call(
        _odd_in_kernel,
        grid=(n_rows // tm,),
        in_specs=[pl.BlockSpec((tm, d), row), pl.BlockSpec((1, d), const),
                  pl.BlockSpec(w.shape, const), pl.BlockSpec(hm.shape, const),
                  pl.BlockSpec((1, 512), const), pl.BlockSpec((1, 128), const)],
        out_specs=[pl.BlockSpec((tm, n), row) for n in widths],
        out_shape=[jax.ShapeDtypeStruct((n_rows, n), F32) for n in widths],
        compiler_params=_cparams(("parallel",)),
        name="odd_in",
    )(x, g, w, hm, gq, gk)


STICK_TK = 256
F32_EXP_ZERO = -104.0


def _stick_tri():
    half = STICK_TK // 2
    later = (np.arange(half)[:, None] > np.arange(half)[None, :]).astype(np.float32)
    blk = np.concatenate([later, np.ones((half, half), np.float32)], axis=1)
    return jnp.asarray(np.concatenate([blk, blk], axis=0), BF16)


def _stick_kernel(q_ref, k_ref, v_ref, tri_ref, o_ref, carry_sc, acc_sc, *, tq, q_pos0):
    tk = STICK_TK
    half = tk // 2
    n_pairs = q_ref.shape[1] // LANES
    v_split = v_ref.shape[1] == 2 * q_ref.shape[1]
    q_start = q_pos0 + pl.program_id(1) * tq
    n_full = q_start // tk
    lane = lax.broadcasted_iota(jnp.int32, (1, LANES), 1)
    first = lane < D_HEAD
    q = q_ref[...] * (D_HEAD ** -0.5)
    q2 = []
    for p in range(n_pairs):
        qp = q[:, p * LANES:(p + 1) * LANES]
        q2.append(jnp.concatenate([jnp.where(first, qp, 0.0), jnp.where(first, 0.0, qp)], axis=0).astype(BF16))
    tri = tri_ref[...]
    carry_sc[...] = jnp.zeros_like(carry_sc)
    acc_sc[...] = jnp.zeros_like(acc_sc)

    def block(kb, masked):
        ks = pl.multiple_of(kb * tk, tk)
        if masked:
            row = lax.broadcasted_iota(jnp.int32, (2 * tq, tk), 0)
            qpos = q_start + jnp.where(row >= tq, row - tq, row)
            kpos = ks + lax.broadcasted_iota(jnp.int32, (2 * tq, tk), 1)
            ok = kpos < qpos
        for p in range(n_pairs):
            cols = slice(p * LANES, (p + 1) * LANES)
            z = _dot_t(q2[p], k_ref[pl.ds(ks, tk), cols].astype(BF16))
            nz = -z
            stay = jnp.minimum(nz, 0.0) - jnp.log(1.0 + jnp.exp(jnp.minimum(z, nz)))
            if masked:
                stay = jnp.where(ok, stay, 0.0)
            carry = carry_sc[p]
            after = [None, None]
            for h in (1, 0):
                hi, lo = _split(stay[:, h * half:(h + 1) * half])
                r = _dot(jnp.concatenate([hi, lo], axis=1), tri)
                after[h] = r[:, :half] + carry
                carry = carry + r[:, half:]
            carry_sc[p] = carry
            w = jnp.exp(z + stay + jnp.concatenate(after, axis=1))
            if masked:
                w = jnp.where(ok, w, 0.0)
            w = w.astype(BF16)
            if v_split:
                v0 = v_ref[pl.ds(ks, tk), 2 * p * LANES:(2 * p + 1) * LANES]
                v1 = v_ref[pl.ds(ks, tk), (2 * p + 1) * LANES:(2 * p + 2) * LANES]
            else:
                vblk = v_ref[pl.ds(ks, tk), cols]
                v0 = jnp.where(first, vblk, 0.0).astype(BF16)
                v1 = jnp.where(first, 0.0, vblk).astype(BF16)
            acc_sc[p] += _dot(jnp.concatenate([w[:tq], w[tq:]], axis=1), jnp.concatenate([v0, v1], axis=0))

    block(n_full, True)

    def any_weight_left():
        m = jnp.max(jnp.max(carry_sc[...], axis=0), axis=0, keepdims=True)
        return jnp.max(m, axis=1, keepdims=True)[0, 0] > F32_EXP_ZERO

    def body(s):
        block(s[0], False)
        return s[0] - 1, any_weight_left()

    lax.while_loop(lambda s: (s[0] >= 0) & s[1], body, (n_full - 1, any_weight_left()))
    for p in range(n_pairs):
        o_ref[:, p * LANES:(p + 1) * LANES] = acc_sc[p]


def _stick(q, k, v, tri, tq, q_pos0):
    b, t, c = q.shape
    lp = k.shape[1]
    tk = STICK_TK
    assert lp % tk == 0 and q_pos0 % tq == 0 and tk % tq == 0 and q_pos0 + t <= lp
    kern = functools.partial(_stick_kernel, tq=tq, q_pos0=q_pos0)
    kmap = lambda bi, i: (bi, 0, 0)
    return pl.pallas_call(
        kern,
        grid=(b, t // tq),
        in_specs=[pl.BlockSpec((None, tq, c), lambda bi, i: (bi, i, 0)),
                  pl.BlockSpec((None, lp, c), kmap),
                  pl.BlockSpec((None, lp, v.shape[2]), kmap),
                  pl.BlockSpec(tri.shape, lambda bi, i: (0, 0))],
        out_specs=pl.BlockSpec((None, tq, c), lambda bi, i: (bi, i, 0)),
        out_shape=jax.ShapeDtypeStruct((b, t, c), F32),
        scratch_shapes=[pltpu.VMEM((c // LANES, 2 * tq, LANES), F32), pltpu.VMEM((c // LANES, tq, LANES), F32)],
        compiler_params=_cparams(("parallel", "arbitrary")),
        name="stick_attn",
    )(q, k, v, tri)


DSA_LEVEL = 512


def _dsa_kernel(q_ref, qi_ref, wi_ref, k_ref, v_ref, ki_ref, o_ref, key_sc, bias_sc,
                *, tq, l_valid, q_pos0, top_k, levels):
    kt = KEY_TILE
    th = min(tq, 128)
    parts = [slice(i * th, (i + 1) * th) for i in range(tq // th)]
    n_rep = q_ref.shape[1] // LANES
    lane = lax.broadcasted_iota(jnp.int32, (1, LANES), 1)
    first = lane < D_HEAD
    q_start = q_pos0 + pl.program_id(1) * tq
    kf = jnp.float32(top_k)

    def body(lv):
        n_kt = lv // kt

        def admissible(rows, lo, width):
            qpos = q_start + rows.start + lax.broadcasted_iota(jnp.int32, (th, 1), 0)
            kpos = lo + lax.broadcasted_iota(jnp.int32, (1, width), 1)
            return (jnp.right_shift(kpos, CHUNK_SHIFT) <= jnp.right_shift(qpos, CHUNK_SHIFT)) & (kpos < l_valid)

        ki_hi, ki_lo = _split(ki_ref[:lv, :])
        for rows in parts:
            qi = qi_ref[rows, :]
            wi = wi_ref[rows, :]
            qm = []
            for h in range(H_IDX):
                grp = qi[:, (h // 2) * LANES:(h // 2 + 1) * LANES]
                qm.append(jnp.where(first, grp, 0.0) if h % 2 == 0 else jnp.where(first, 0.0, grp))
            q_hi, q_lo = _split(jnp.concatenate(qm, axis=0))
            s = _dot_t(q_hi, ki_hi) + (_dot_t(q_hi, ki_lo) + _dot_t(q_lo, ki_hi))
            score = jnp.zeros((th, lv), F32)
            for h in range(H_IDX):
                score = score + jnp.maximum(s[h * th:(h + 1) * th], 0.0) * wi[:, h:h + 1]
            score = score * ((D_IDX ** -0.5) * (H_IDX ** -0.5)) + 0.0
            score = jnp.where(admissible(rows, 0, lv), score, NEG)
            bits = lax.bitcast_convert_type(score, jnp.int32)
            key_sc[rows, :lv] = jnp.where(bits < 0, bits ^ jnp.int32(0x7FFFFFFF), bits)

        def count_ge(rows, trial):
            part = jnp.zeros((th, kt), F32)
            for c in range(n_kt):
                part = part + jnp.where(key_sc[rows, c * kt:(c + 1) * kt] >= trial, 1.0, 0.0)
            return jnp.sum(part, axis=1, keepdims=True)

        int_min = jnp.int32(-2 ** 31)
        zero = jnp.zeros((th, 1), jnp.int32)
        thr0 = tuple(jnp.where(count_ge(rows, zero) >= kf, jnp.int32(0), int_min) for rows in parts)

        def bit_step(i, thr):
            bit = jnp.left_shift(jnp.int32(1), jnp.int32(30) - i)
            return tuple(jnp.where(count_ge(rows, t + bit) >= kf, t + bit, t) for rows, t in zip(parts, thr))

        thrs = lax.fori_loop(0, 31, bit_step, thr0)

        rr = lax.broadcasted_iota(jnp.int32, (kt, kt), 0)
        cc = lax.broadcasted_iota(jnp.int32, (kt, kt), 1)
        before = jnp.where(rr < cc, 1.0, 0.0).astype(BF16)
        for rows, thr in zip(parts, thrs):
            n_gt = jnp.zeros((th, kt), F32)
            for c in range(n_kt):
                n_gt = n_gt + jnp.where(key_sc[rows, c * kt:(c + 1) * kt] > thr, 1.0, 0.0)
            room = kf - jnp.sum(n_gt, axis=1, keepdims=True)
            run = jnp.zeros((th, 1), F32)
            for c in range(n_kt):
                keyc = key_sc[rows, c * kt:(c + 1) * kt]
                eqf = jnp.where(keyc == thr, 1.0, 0.0)
                rank = _dot(eqf.astype(BF16), before) + run
                tie_ok = jnp.where(keyc == thr, jnp.where(rank < room, 0.0, NEG), NEG)
                sel = jnp.where(keyc > thr, 0.0, tie_ok)
                bias_sc[rows, c * kt:(c + 1) * kt] = jnp.where(admissible(rows, c * kt, kt), sel, NEG)
                run = run + jnp.sum(eqf, axis=1, keepdims=True)

        kb = k_ref[:lv, :].astype(BF16)
        vf = v_ref[:lv, :]
        vg = [jnp.where(first, vf, 0.0).astype(BF16), jnp.where(first, 0.0, vf).astype(BF16)]
        for rows in parts:
            q = q_ref[rows, :] * (D_HEAD ** -0.5)
            bias = bias_sc[rows, :lv]
            out = jnp.zeros((n_rep * th, LANES), F32)
            for g in range(2):
                keep = first if g == 0 else jnp.logical_not(first)
                qs = jnp.concatenate(
                    [jnp.where(keep, q[:, r * LANES:(r + 1) * LANES], 0.0) for r in range(n_rep)],
                    axis=0).astype(BF16)
                logits = _dot_t(qs, kb)
                logits = (logits.reshape(n_rep, th, lv) + bias[None]).reshape(n_rep * th, lv)
                m = jnp.max(logits, axis=1, keepdims=True)
                p = jnp.exp(logits - m)
                den = jnp.sum(p, axis=1, keepdims=True)
                out = out + _dot(p.astype(BF16), vg[g]) / den
            for r in range(n_rep):
                o_ref[rows, r * LANES:(r + 1) * LANES] = out[r * th:(r + 1) * th]

    if len(levels) == 1:
        body(levels[0])
    else:
        cls = (q_start + tq - 1) // DSA_LEVEL
        for c, lv in enumerate(levels):
            pl.when(cls == c)(functools.partial(body, lv))


def _dsa(q, qi, wi, k, v, ki, tq, q_pos0, l_valid):
    b, t, c = q.shape
    lp = k.shape[1]
    assert lp % KEY_TILE == 0 and l_valid <= lp and t % tq == 0
    top_k = min(TOPK_MAX, l_valid // 4)
    if q_pos0 == 0 and lp % DSA_LEVEL == 0 and tq % CHUNK == 0 and DSA_LEVEL >= top_k:
        levels = tuple(range(DSA_LEVEL, lp + 1, DSA_LEVEL))
    else:
        levels = (lp,)
    kern = functools.partial(_dsa_kernel, tq=tq, l_valid=l_valid, q_pos0=q_pos0, top_k=top_k, levels=levels)
    qmap = lambda bi, i: (bi, i, 0)
    kmap = lambda bi, i: (bi, 0, 0)
    return pl.pallas_call(
        kern,
        grid=(b, t // tq),
        in_specs=[pl.BlockSpec((None, tq, c), qmap), pl.BlockSpec((None, tq, qi.shape[2]), qmap),
                  pl.BlockSpec((None, tq, LANES), qmap),
                  pl.BlockSpec((None, lp, LANES), kmap), pl.BlockSpec((None, lp, LANES), kmap),
                  pl.BlockSpec((None, lp, LANES), kmap)],
        out_specs=pl.BlockSpec((None, tq, c), qmap),
        out_shape=jax.ShapeDtypeStruct((b, t, c), F32),
        scratch_shapes=[pltpu.VMEM((tq, lp), jnp.int32), pltpu.VMEM((tq, lp), F32)],
        compiler_params=_cparams(("parallel", "arbitrary")),
        name="dsa_attn",
    )(q, qi, wi, k, v, ki)


FFN_CHUNK = 1024


def _ffn_tail(x1, gf_ref, w1_ref, w2_ref, o_ref):
    h = _rms(x1, gf_ref[...]).astype(BF16)
    acc = None
    for c in range(w1_ref.shape[1] // FFN_CHUNK):
        cols = slice(c * FFN_CHUNK, (c + 1) * FFN_CHUNK)
        a = jnp.square(jnp.maximum(_dot(h, w1_ref[:, cols]), 0.0))
        part = _dot(a.astype(BF16), w2_ref[cols, :])
        acc = part if acc is None else acc + part
    o_ref[...] = x1 + acc


def _even_out_kernel(x_ref, att_ref, u_ref, vb_ref, ws_ref, bst_ref, w_ref, gf_ref, w1_ref, w2_ref, o_ref,
                     *, p_len):
    tm = x_ref.shape[0]
    cg = vb_ref.shape[1] // G_B
    rr = lax.broadcasted_iota(jnp.int32, (p_len, p_len), 0)
    cc = lax.broadcasted_iota(jnp.int32, (p_len, p_len), 1)
    causal = jnp.right_shift(cc, CHUNK_SHIFT) <= jnp.right_shift(rr, CHUNK_SHIFT)
    u = u_ref[...]
    vb = vb_ref[...].astype(BF16)
    bst = bst_ref[...]
    mixed_rows = []
    for c in range(tm // p_len):
        cols = []
        for g in range(G_B):
            wg = jnp.where(causal, ws_ref[g], 0.0).astype(BF16)
            mg = _dot(wg, vb[c * p_len:(c + 1) * p_len, g * cg:(g + 1) * cg]) + bst[:, g:g + 1]
            cols.append(mg)
        mixed_rows.append(jnp.concatenate(cols, axis=1))
    gated = u * jnp.concatenate(mixed_rows, axis=0)
    na = att_ref.shape[1]
    y = _dot(att_ref[...].astype(BF16), w_ref[:na, :]) + _dot(gated.astype(BF16), w_ref[na:, :])
    _ffn_tail(x_ref[...] + y, gf_ref, w1_ref, w2_ref, o_ref)


def _resident(shape):
    return pl.BlockSpec(shape, lambda i: (0,) * len(shape), pipeline_mode=pl.Buffered(1))


def _even_out_ffn(x, att, u, vb, ws, bst, w, gf, w1, w2, tm, p_len):
    n_rows, d = x.shape
    n = att.shape[1]
    row = lambda i: (i, 0)
    kern = functools.partial(_even_out_kernel, p_len=p_len)
    return pl.pallas_call(
        kern,
        grid=(n_rows // tm,),
        in_specs=[pl.BlockSpec((tm, d), row), pl.BlockSpec((tm, n), row), pl.BlockSpec((tm, n), row),
                  pl.BlockSpec((tm, n), row), _resident(ws.shape), _resident(bst.shape), _resident(w.shape),
                  _resident(gf.shape), _resident(w1.shape), _resident(w2.shape)],
        out_specs=pl.BlockSpec((tm, d), row),
        out_shape=jax.ShapeDtypeStruct((n_rows, d), F32),
        compiler_params=_cparams(("parallel",)),
        name="even_out_ffn",
    )(x, att, u, vb, ws, bst, w, gf, w1, w2)


def _odd_out_kernel(x_ref, att_ref, gb_ref, cin_ref, prev_ref, cw_ref, w_ref, gf_ref, w1_ref, w2_ref, o_ref,
                    *, t_len, has_prev):
    tm = x_ref.shape[0]
    prev = prev_ref[...]
    if not has_prev:
        start = (pl.program_id(0) * tm) % t_len == 0
        prev = jnp.where(start, 0.0, prev)
    cw = cw_ref[...]
    seg = t_len if has_prev else tm
    convs = []
    for s in range(tm // seg):
        cin = cin_ref[s * seg:(s + 1) * seg, :]
        ext = jnp.concatenate([prev[8 * s:8 * (s + 1)], cin], axis=0)
        conv = cw[CONV_W - 1:CONV_W] * cin
        for j in range(1, CONV_W):
            conv = conv + cw[CONV_W - 1 - j:CONV_W - j] * pltpu.roll(ext, j, 0)[8:]
        convs.append(conv)
    gated = gb_ref[...] * (convs[0] if len(convs) == 1 else jnp.concatenate(convs, axis=0))
    na = att_ref.shape[1]
    y = _dot(att_ref[...].astype(BF16), w_ref[:na, :]) + _dot(gated.astype(BF16), w_ref[na:, :])
    _ffn_tail(x_ref[...] + y, gf_ref, w1_ref, w2_ref, o_ref)


def _odd_out_ffn(x, att, gb, cin, prev, cw, w, gf, w1, w2, tm, t_len):
    n_rows, d = x.shape
    n = att.shape[1]
    row = lambda i: (i, 0)
    has_prev = prev is not None
    if has_prev:
        assert tm % t_len == 0 and t_len % 8 == 0
        prev_arr, prev_map, prev_rows = prev, row, 8 * (tm // t_len)
    else:
        assert t_len % tm == 0 and tm % 8 == 0
        prev_arr, prev_map, prev_rows = cin, (lambda i: (jnp.maximum(i * (tm // 8) - 1, 0), 0)), 8
    kern = functools.partial(_odd_out_kernel, t_len=t_len, has_prev=has_prev)
    return pl.pallas_call(
        kern,
        grid=(n_rows // tm,),
        in_specs=[pl.BlockSpec((tm, d), row), pl.BlockSpec((tm, n), row), pl.BlockSpec((tm, n), row),
                  pl.BlockSpec((tm, n), row), pl.BlockSpec((prev_rows, n), prev_map),
                  _resident(cw.shape), _resident(w.shape),
                  _resident(gf.shape), _resident(w1.shape), _resident(w2.shape)],
        out_specs=pl.BlockSpec((tm, d), row),
        out_shape=jax.ShapeDtypeStruct((n_rows, d), F32),
        compiler_params=_cparams(("parallel",)),
        name="odd_out_ffn",
    )(x, att, gb, cin, prev_arr, cw, w, gf, w1, w2)


def _round_up(n, m):
    return (n + m - 1) // m * m


def _row_tile(n_rows, cap):
    t = min(cap, n_rows)
    while n_rows % t:
        t //= 2
    return t


def _odd_weight_layout(n_q):
    n_heads = n_q // D_HEAD
    rep = n_heads // 2
    perm = np.zeros((n_q,), np.int32)
    for r in range(rep):
        for g in range(2):
            for dd in range(D_HEAD):
                perm[r * LANES + g * D_HEAD + dd] = (g * rep + r) * D_HEAD + dd
    return perm


def kernel(x_prompt, x_sample, cache_a_k, cache_a_v, cache_c_k, cache_c_v, cache_c_kidx, state_d_conv,
           norm_mix_g, norm_ffn_g, w_in_even, gq_a, gk_a, g_b, ws_b, bs_b, w_out_even,
           w_in_odd, gq_c, gk_c, conv_w_d, w_out_odd, w_ffn1, w_ffn2):
    bp, tp, d = x_prompt.shape
    bs, ts, _ = x_sample.shape
    depth = norm_mix_g.shape[0]
    past = cache_a_k.shape[2]
    n_a = cache_a_k.shape[3] * cache_a_k.shape[4]
    n_kv = cache_c_k.shape[3] * cache_c_k.shape[4]
    n_b = g_b.shape[1]
    n_d = conv_w_d.shape[2]
    n_q = w_out_odd.shape[1] - n_d
    assert n_a == 512 and n_b == 512 and n_kv == 128 and n_q == 512 and n_d == 512 and d == 1024

    xp = x_prompt.reshape(bp * tp, d)
    xs = x_sample.reshape(bs * ts, d)
    tm_p = _row_tile(bp * tp, 512)
    tm_s = _row_tile(bs * ts, 512)
    ls = past + ts
    ls_pad = _round_up(ls, KEY_TILE)

    hmean = jnp.asarray(np.kron(np.eye(n_a // D_HEAD), np.full((D_HEAD, D_HEAD), 1.0 / D_HEAD)), BF16)
    perm = _odd_weight_layout(n_q)
    tile8 = lambda gvec: jnp.tile(gvec, n_a // D_HEAD).reshape(1, n_a)

    ak_p, av_p, ak_s, av_s, bv_s = [], [], [], [], []
    ck_p, cv_p, ci_p, ck_s, cv_s, ci_s, dc_p, dc_s = [], [], [], [], [], [], [], []

    tri = _stick_tri()

    def pad_keys(past_rows, new_rows, multiple=KEY_TILE):
        width = new_rows.shape[-1]
        zeros = jnp.zeros((bs, _round_up(ls, multiple) - ls, width), F32)
        return jnp.concatenate([past_rows.reshape(bs, past, width), new_rows.reshape(bs, ts, width), zeros], axis=1)

    for i in range(depth):
        j = i // 2
        g_mix = norm_mix_g[i].reshape(1, d)
        ffn = (norm_ffn_g[i].reshape(1, d), w_ffn1[i].astype(BF16), w_ffn2[i].astype(BF16))
        if i % 2 == 0:
            w_in = w_in_even[j].astype(BF16)
            w_out = w_out_even[j].astype(BF16)
            gq, gk, gb = tile8(gq_a[j]), tile8(gk_a[j]), g_b[j].reshape(1, n_b)
            q, k, v, u, vb, kb, vm = _even_in(xp, g_mix, w_in, hmean, gq, gk, gb, tm_p)
            att = _stick(q.reshape(bp, tp, n_a), kb.reshape(bp, tp, n_a), vm.reshape(bp, tp, 2 * n_a), tri,
                         min(tp, KEY_TILE), 0)
            xp = _even_out_ffn(xp, att.reshape(bp * tp, n_a), u, vb, ws_b[j], bs_b[j].T, w_out, *ffn, tm_p, B_CHUNK)
            ak_p.append(k.reshape(bp, tp, n_a // D_HEAD, D_HEAD))
            av_p.append(v.reshape(bp, tp, n_a // D_HEAD, D_HEAD))
            q, k, v, u, vb, _, _ = _even_in(xs, g_mix, w_in, hmean, gq, gk, gb, tm_s)
            att = _stick(q.reshape(bs, ts, n_a), pad_keys(cache_a_k[j], k, STICK_TK),
                         pad_keys(cache_a_v[j], v, STICK_TK), tri, ts, past)
            xs = _even_out_ffn(xs, att.reshape(bs * ts, n_a), u, vb, ws_b[j][:, :ts, :ts], bs_b[j][:, :ts].T,
                               w_out, *ffn, tm_s, ts)
            ak_s.append(k.reshape(bs, ts, n_a // D_HEAD, D_HEAD))
            av_s.append(v.reshape(bs, ts, n_a // D_HEAD, D_HEAD))
            bv_s.append(vb.reshape(bs, ts, n_b))
        else:
            wi_full = w_in_odd[j]
            offs = np.cumsum([0, n_q, n_kv, n_kv, H_IDX * D_IDX, D_IDX, H_IDX, n_d, n_d, n_d])
            col = lambda s: wi_full[:, int(offs[s]):int(offs[s + 1])]
            w_in = jnp.concatenate(
                [col(0)[:, perm], col(1), col(2), col(3), col(4), col(4), col(5),
                 jnp.zeros((d, LANES - H_IDX), F32), col(6), col(7), col(8)], axis=1).astype(BF16)
            assert w_in.shape[1] == ODD_COLS
            w_out = jnp.concatenate([w_out_odd[j][:n_q][perm], w_out_odd[j][n_q:]], axis=0).astype(BF16)
            gq, gk = tile8(gq_c[j]), jnp.tile(gk_c[j], n_kv // D_HEAD).reshape(1, n_kv)
            cw = conv_w_d[j]
            q, k, v, qi, ki, wi, gbt, cin = _odd_in(xp, g_mix, w_in, hmean, gq, gk, tm_p)
            r3 = lambda a: a.reshape(bp, tp, a.shape[-1])
            att = _dsa(r3(q), r3(qi), r3(wi), r3(k), r3(v), r3(ki), min(tp, 2 * KEY_TILE), 0, tp)
            xp = _odd_out_ffn(xp, att.reshape(bp * tp, n_q), gbt, cin, None, cw, w_out, *ffn, tm_p, tp)
            ck_p.append(k.reshape(bp, tp, n_kv // D_HEAD, D_HEAD))
            cv_p.append(v.reshape(bp, tp, n_kv // D_HEAD, D_HEAD))
            ci_p.append(ki[:, :D_IDX].reshape(bp, tp, D_IDX))
            dc_p.append(cin.reshape(bp, tp, n_d)[:, tp - (CONV_W - 1):])
            q, k, v, qi, ki, wi, gbt, cin = _odd_in(xs, g_mix, w_in, hmean, gq, gk, tm_s)
            r3 = lambda a: a.reshape(bs, ts, a.shape[-1])
            past_ki = jnp.concatenate([cache_c_kidx[j], cache_c_kidx[j]], axis=-1)
            att = _dsa(r3(q), r3(qi), r3(wi), pad_keys(cache_c_k[j], k), pad_keys(cache_c_v[j], v),
                       pad_keys(past_ki, ki), ts, past, ls)
            prev = jnp.concatenate([jnp.zeros((bs, 8 - (CONV_W - 1), n_d), F32), state_d_conv[j]], axis=1)
            xs = _odd_out_ffn(xs, att.reshape(bs * ts, n_q), gbt, cin, prev.reshape(bs * 8, n_d), cw, w_out, *ffn,
                              tm_s, ts)
            ck_s.append(k.reshape(bs, ts, n_kv // D_HEAD, D_HEAD))
            cv_s.append(v.reshape(bs, ts, n_kv // D_HEAD, D_HEAD))
            ci_s.append(ki[:, :D_IDX].reshape(bs, ts, D_IDX))
            dc_s.append(cin.reshape(bs, ts, n_d)[:, ts - (CONV_W - 1):])

    return (xp.reshape(bp, tp, d), xs.reshape(bs, ts, d),
            jnp.stack(ak_p), jnp.stack(av_p), jnp.stack(ak_s), jnp.stack(av_s),
            jnp.stack(bv_s),
            jnp.stack(ck_p), jnp.stack(cv_p), jnp.stack(ci_p),
            jnp.stack(ck_s), jnp.stack(cv_s), jnp.stack(ci_s),
            jnp.stack(dc_p), jnp.stack(dc_s))
```

```python
import functools

import numpy as np
import jax
import jax.numpy as jnp
from jax import lax
from jax.experimental import pallas as pl
from jax.experimental.pallas import tpu as pltpu

F32 = jnp.float32
BF16 = jnp.bfloat16

EPS = 1e-6
NEG = -1e30
CHUNK = 64
CHUNK_SHIFT = 6
D_HEAD = 64
G_B = 4
B_CHUNK = 128
H_IDX = 4
D_IDX = 64
TOPK_MAX = 256
CONV_W = 3
LANES = 128
KEY_TILE = 128
VMEM_LIMIT = 56 * 1024 * 1024


def _cparams(sem):
    return pltpu.CompilerParams(dimension_semantics=sem, vmem_limit_bytes=VMEM_LIMIT)


def _resident(a):
    if isinstance(a, tuple):
        arr, layer = a
        shape = arr.shape[1:]
        return pl.BlockSpec((None,) + shape, lambda *_: (layer,) + (0,) * len(shape),
                            pipeline_mode=pl.Buffered(1))
    return pl.BlockSpec(a.shape, lambda *_: (0,) * a.ndim, pipeline_mode=pl.Buffered(1))


def _operand(a):
    return a[0] if isinstance(a, tuple) else a


def _dot(a, b):
    return jnp.dot(a, b, preferred_element_type=F32)


def _dot_t(a, b):
    return lax.dot_general(a, b, (((1,), (1,)), ((), ())), preferred_element_type=F32)


def _split(x):
    hi = x.astype(BF16)
    lo = (x - hi.astype(F32)).astype(BF16)
    return hi, lo


def _rms(x, g):
    return x * lax.rsqrt(jnp.mean(x * x, axis=-1, keepdims=True) + EPS) * g


def _head_rms(x, hmean, g):
    hi, lo = _split(x * x)
    ms = _dot(hi, hmean) + _dot(lo, hmean)
    return x * lax.rsqrt(ms + EPS) * g


def _gelu(x):
    return jax.nn.gelu(x)


def _even_in_kernel(x_ref, g_ref, w_ref, hm_ref, gq_ref, gk_ref, gb_ref,
                    q_ref, k_ref, v_ref, u_ref, vb_ref, kb_ref, vm_ref):
    h = _rms(x_ref[...], g_ref[...]).astype(BF16)
    n = q_ref.shape[-1]

    def seg(i):
        return _dot(h, w_ref[:, i * n:(i + 1) * n])

    hm = hm_ref[...]
    q_ref[...] = _head_rms(seg(0), hm, gq_ref[...])
    k = _head_rms(seg(1), hm, gk_ref[...])
    k_ref[...] = k
    kb_ref[...] = k.astype(BF16)
    v = seg(2)
    v_ref[...] = v
    first = lax.broadcasted_iota(jnp.int32, (1, LANES), 1) < D_HEAD
    for p in range(n // LANES):
        vp = v[:, p * LANES:(p + 1) * LANES]
        vm_ref[:, 2 * p * LANES:(2 * p + 1) * LANES] = jnp.where(first, vp, 0.0).astype(BF16)
        vm_ref[:, (2 * p + 1) * LANES:(2 * p + 2) * LANES] = jnp.where(first, 0.0, vp).astype(BF16)
    u_ref[...] = _gelu(seg(3))
    vb_ref[...] = _rms(_gelu(seg(4)), gb_ref[...])


def _even_in(x, g, w, hm, gq, gk, gb, tm):
    n_rows, d = x.shape
    n = _operand(w).shape[-1] // 5
    row = lambda i: (i, 0)
    out = jax.ShapeDtypeStruct((n_rows, n), F32)
    return pl.pallas_call(
        _even_in_kernel,
        grid=(n_rows // tm,),
        in_specs=[pl.BlockSpec((tm, d), row), _resident(g), _resident(w), _resident(hm),
                  _resident(gq), _resident(gk), _resident(gb)],
        out_specs=[pl.BlockSpec((tm, n), row)] * 6 + [pl.BlockSpec((tm, 2 * n), row)],
        out_shape=[out] * 5 + [jax.ShapeDtypeStruct((n_rows, n), BF16), jax.ShapeDtypeStruct((n_rows, 2 * n), BF16)],
        compiler_params=_cparams(("parallel",)),
        name="even_in",
    )(x, g, _operand(w), hm, gq, gk, gb)


ODD_Q, ODD_K, ODD_V, ODD_QI, ODD_KI, ODD_WI, ODD_GB, ODD_GC, ODD_HD = (
    0, 512, 640, 768, 1024, 1152, 1280, 1792, 2304)
ODD_COLS = 2816


def _odd_in_kernel(x_ref, g_ref, w_ref, hm_ref, gq_ref, gk_ref,
                   q_ref, k_ref, v_ref, qi_ref, ki_ref, wi_ref, gb_ref, cin_ref):
    h = _rms(x_ref[...], g_ref[...]).astype(BF16)

    def seg(lo, width):
        return _dot(h, w_ref[:, lo:lo + width])

    hm = hm_ref[...]
    q_ref[...] = _head_rms(seg(ODD_Q, 512), hm, gq_ref[...])
    k_ref[...] = _head_rms(seg(ODD_K, 128), hm[:128, :128], gk_ref[...])
    v_ref[...] = seg(ODD_V, 128)
    qi_ref[...] = seg(ODD_QI, 256)
    ki_ref[...] = seg(ODD_KI, 128)
    wi_ref[...] = seg(ODD_WI, 128)
    gb_ref[...] = seg(ODD_GB, 512)
    cin_ref[...] = seg(ODD_GC, 512) * seg(ODD_HD, 512)


def _odd_in(x, g, w, hm, gq, gk, tm):
    n_rows, d = x.shape
    row = lambda i: (i, 0)
    widths = (512, 128, 128, 256, 128, 128, 512, 512)
    return pl.pallas_call(
        _odd_in_kernel,
        grid=(n_rows // tm,),
        in_specs=[pl.BlockSpec((tm, d), row), _resident(g), _resident(w), _resident(hm),
                  _resident(gq), _resident(gk)],
        out_specs=[pl.BlockSpec((tm, n), row) for n in widths],
        out_shape=[jax.ShapeDtypeStruct((n_rows, n), F32) for n in widths],
        compiler_params=_cparams(("parallel",)),
        name="odd_in",
    )(x, g, _operand(w), hm, gq, gk)


STICK_TK = 256
F32_EXP_ZERO = -104.0


def _stick_tri():
    half = STICK_TK // 2
    later = (np.arange(half)[:, None] > np.arange(half)[None, :]).astype(np.float32)
    blk = np.concatenate([later, np.ones((half, half), np.float32)], axis=1)
    return jnp.asarray(np.concatenate([blk, blk], axis=0), BF16)


def _stick_kernel(q_ref, k_ref, v_ref, tri_ref, o_ref, carry_sc, acc_sc, *, tq, q_pos0):
    tk = STICK_TK
    half = tk // 2
    n_pairs = q_ref.shape[1] // LANES
    v_split = v_ref.shape[1] == 2 * q_ref.shape[1]
    q_start = q_pos0 + pl.program_id(1) * tq
    n_full = q_start // tk
    lane = lax.broadcasted_iota(jnp.int32, (1, LANES), 1)
    first = lane < D_HEAD
    q = q_ref[...] * (D_HEAD ** -0.5)
    q2 = []
    for p in range(n_pairs):
        qp = q[:, p * LANES:(p + 1) * LANES]
        q2.append(jnp.concatenate([jnp.where(first, qp, 0.0), jnp.where(first, 0.0, qp)], axis=0).astype(BF16))
    tri = tri_ref[...]
    carry_sc[...] = jnp.zeros_like(carry_sc)
    acc_sc[...] = jnp.zeros_like(acc_sc)

    def block(kb, masked):
        ks = pl.multiple_of(kb * tk, tk)
        if masked:
            row = lax.broadcasted_iota(jnp.int32, (2 * tq, tk), 0)
            qpos = q_start + jnp.where(row >= tq, row - tq, row)
            kpos = ks + lax.broadcasted_iota(jnp.int32, (2 * tq, tk), 1)
            ok = kpos < qpos
        for p in range(n_pairs):
            cols = slice(p * LANES, (p + 1) * LANES)
            z = _dot_t(q2[p], k_ref[pl.ds(ks, tk), cols].astype(BF16))
            nz = -z
            stay = jnp.minimum(nz, 0.0) - jnp.log(1.0 + jnp.exp(jnp.minimum(z, nz)))
            if masked:
                stay = jnp.where(ok, stay, 0.0)
            carry = carry_sc[p]
            after = [None, None]
            for h in (1, 0):
                hi, lo = _split(stay[:, h * half:(h + 1) * half])
                r = _dot(jnp.concatenate([hi, lo], axis=1), tri)
                after[h] = r[:, :half] + carry
                carry = carry + r[:, half:]
            carry_sc[p] = carry
            w = jnp.exp(z + stay + jnp.concatenate(after, axis=1))
            if masked:
                w = jnp.where(ok, w, 0.0)
            w = w.astype(BF16)
            if v_split:
                v0 = v_ref[pl.ds(ks, tk), 2 * p * LANES:(2 * p + 1) * LANES]
                v1 = v_ref[pl.ds(ks, tk), (2 * p + 1) * LANES:(2 * p + 2) * LANES]
            else:
                vblk = v_ref[pl.ds(ks, tk), cols]
                v0 = jnp.where(first, vblk, 0.0).astype(BF16)
                v1 = jnp.where(first, 0.0, vblk).astype(BF16)
            acc_sc[p] += _dot(jnp.concatenate([w[:tq], w[tq:]], axis=1), jnp.concatenate([v0, v1], axis=0))

    block(n_full, True)

    def any_weight_left():
        m = jnp.max(jnp.max(carry_sc[...], axis=0), axis=0, keepdims=True)
        return jnp.max(m, axis=1, keepdims=True)[0, 0] > F32_EXP_ZERO

    def body(s):
        block(s[0], False)
        return s[0] - 1, any_weight_left()

    lax.while_loop(lambda s: (s[0] >= 0) & s[1], body, (n_full - 1, any_weight_left()))
    for p in range(n_pairs):
        o_ref[:, p * LANES:(p + 1) * LANES] = acc_sc[p]


def _stick(q, k, v, tri, tq, q_pos0):
    b, t, c = q.shape
    lp = k.shape[1]
    tk = STICK_TK
    assert lp % tk == 0 and q_pos0 % tq == 0 and tk % tq == 0 and q_pos0 + t <= lp
    kern = functools.partial(_stick_kernel, tq=tq, q_pos0=q_pos0)
    kmap = lambda bi, i: (bi, 0, 0)
    return pl.pallas_call(
        kern,
        grid=(b, t // tq),
        in_specs=[pl.BlockSpec((None, tq, c), lambda bi, i: (bi, i, 0)),
                  pl.BlockSpec((None, lp, c), kmap),
                  pl.BlockSpec((None, lp, v.shape[2]), kmap),
                  pl.BlockSpec(tri.shape, lambda bi, i: (0, 0))],
        out_specs=pl.BlockSpec((None, tq, c), lambda bi, i: (bi, i, 0)),
        out_shape=jax.ShapeDtypeStruct((b, t, c), F32),
        scratch_shapes=[pltpu.VMEM((c // LANES, 2 * tq, LANES), F32), pltpu.VMEM((c // LANES, tq, LANES), F32)],
        compiler_params=_cparams(("parallel", "arbitrary")),
        name="stick_attn",
    )(q, k, v, tri)


DSA_LEVEL = 512


def _dsa_kernel(q_ref, qi_ref, wi_ref, k_ref, v_ref, ki_ref, o_ref, key_sc, bias_sc,
                *, tq, l_valid, q_pos0, top_k, levels):
    kt = KEY_TILE
    th = min(tq, 128)
    parts = [slice(i * th, (i + 1) * th) for i in range(tq // th)]
    n_rep = q_ref.shape[1] // LANES
    lane = lax.broadcasted_iota(jnp.int32, (1, LANES), 1)
    first = lane < D_HEAD
    q_start = q_pos0 + pl.program_id(1) * tq
    kf = jnp.float32(top_k)

    def body(lv):
        n_kt = lv // kt

        def admissible(rows, lo, width):
            qpos = q_start + rows.start + lax.broadcasted_iota(jnp.int32, (th, 1), 0)
            kpos = lo + lax.broadcasted_iota(jnp.int32, (1, width), 1)
            return (jnp.right_shift(kpos, CHUNK_SHIFT) <= jnp.right_shift(qpos, CHUNK_SHIFT)) & (kpos < l_valid)

        ki_hi, ki_lo = _split(ki_ref[:lv, :])
        for rows in parts:
            qi = qi_ref[rows, :]
            wi = wi_ref[rows, :]
            qm = []
            for h in range(H_IDX):
                grp = qi[:, (h // 2) * LANES:(h // 2 + 1) * LANES]
                qm.append(jnp.where(first, grp, 0.0) if h % 2 == 0 else jnp.where(first, 0.0, grp))
            q_hi, q_lo = _split(jnp.concatenate(qm, axis=0))
            s = _dot_t(q_hi, ki_hi) + (_dot_t(q_hi, ki_lo) + _dot_t(q_lo, ki_hi))
            score = jnp.zeros((th, lv), F32)
            for h in range(H_IDX):
                score = score + jnp.maximum(s[h * th:(h + 1) * th], 0.0) * wi[:, h:h + 1]
            score = score * ((D_IDX ** -0.5) * (H_IDX ** -0.5)) + 0.0
            score = jnp.where(admissible(rows, 0, lv), score, NEG)
            bits = lax.bitcast_convert_type(score, jnp.int32)
            key_sc[rows, :lv] = jnp.where(bits < 0, bits ^ jnp.int32(0x7FFFFFFF), bits)

        def count_ge(rows, trial):
            part = jnp.zeros((th, kt), F32)
            for c in range(n_kt):
                part = part + jnp.where(key_sc[rows, c * kt:(c + 1) * kt] >= trial, 1.0, 0.0)
            return jnp.sum(part, axis=1, keepdims=True)

        int_min = jnp.int32(-2 ** 31)
        zero = jnp.zeros((th, 1), jnp.int32)
        thr0 = tuple(jnp.where(count_ge(rows, zero) >= kf, jnp.int32(0), int_min) for rows in parts)

        def bit_step(i, thr):
            bit = jnp.left_shift(jnp.int32(1), jnp.int32(30) - i)
            return tuple(jnp.where(count_ge(rows, t + bit) >= kf, t + bit, t) for rows, t in zip(parts, thr))

        thrs = lax.fori_loop(0, 31, bit_step, thr0)

        rr = lax.broadcasted_iota(jnp.int32, (kt, kt), 0)
        cc = lax.broadcasted_iota(jnp.int32, (kt, kt), 1)
        before = jnp.where(rr < cc, 1.0, 0.0).astype(BF16)
        for rows, thr in zip(parts, thrs):
            n_gt = jnp.zeros((th, kt), F32)
            for c in range(n_kt):
                n_gt = n_gt + jnp.where(key_sc[rows, c * kt:(c + 1) * kt] > thr, 1.0, 0.0)
            room = kf - jnp.sum(n_gt, axis=1, keepdims=True)
            run = jnp.zeros((th, 1), F32)
            for c in range(n_kt):
                keyc = key_sc[rows, c * kt:(c + 1) * kt]
                eqf = jnp.where(keyc == thr, 1.0, 0.0)
                rank = _dot(eqf.astype(BF16), before) + run
                tie_ok = jnp.where(keyc == thr, jnp.where(rank < room, 0.0, NEG), NEG)
                sel = jnp.where(keyc > thr, 0.0, tie_ok)
                bias_sc[rows, c * kt:(c + 1) * kt] = jnp.where(admissible(rows, c * kt, kt), sel, NEG)
                run = run + jnp.sum(eqf, axis=1, keepdims=True)

        kb = k_ref[:lv, :].astype(BF16)
        vf = v_ref[:lv, :]
        vg = [jnp.where(first, vf, 0.0).astype(BF16), jnp.where(first, 0.0, vf).astype(BF16)]
        for rows in parts:
            q = q_ref[rows, :] * (D_HEAD ** -0.5)
            bias = bias_sc[rows, :lv]
            out = jnp.zeros((n_rep * th, LANES), F32)
            for g in range(2):
                keep = first if g == 0 else jnp.logical_not(first)
                qs = jnp.concatenate(
                    [jnp.where(keep, q[:, r * LANES:(r + 1) * LANES], 0.0) for r in range(n_rep)],
                    axis=0).astype(BF16)
                logits = _dot_t(qs, kb)
                logits = (logits.reshape(n_rep, th, lv) + bias[None]).reshape(n_rep * th, lv)
                m = jnp.max(logits, axis=1, keepdims=True)
                p = jnp.exp(logits - m)
                den = jnp.sum(p, axis=1, keepdims=True)
                out = out + _dot(p.astype(BF16), vg[g]) / den
            for r in range(n_rep):
                o_ref[rows, r * LANES:(r + 1) * LANES] = out[r * th:(r + 1) * th]

    if len(levels) == 1:
        body(levels[0])
    else:
        cls = (q_start + tq - 1) // DSA_LEVEL
        for c, lv in enumerate(levels):
            pl.when(cls == c)(functools.partial(body, lv))


def _dsa(q, qi, wi, k, v, ki, tq, q_pos0, l_valid):
    b, t, c = q.shape
    lp = k.shape[1]
    assert lp % KEY_TILE == 0 and l_valid <= lp and t % tq == 0
    top_k = min(TOPK_MAX, l_valid // 4)
    if q_pos0 == 0 and lp % DSA_LEVEL == 0 and tq % CHUNK == 0 and DSA_LEVEL >= top_k:
        levels = tuple(range(DSA_LEVEL, lp + 1, DSA_LEVEL))
    else:
        levels = (lp,)
    kern = functools.partial(_dsa_kernel, tq=tq, l_valid=l_valid, q_pos0=q_pos0, top_k=top_k, levels=levels)
    qmap = lambda bi, i: (bi, i, 0)
    kmap = lambda bi, i: (bi, 0, 0)
    return pl.pallas_call(
        kern,
        grid=(b, t // tq),
        in_specs=[pl.BlockSpec((None, tq, c), qmap), pl.BlockSpec((None, tq, qi.shape[2]), qmap),
                  pl.BlockSpec((None, tq, LANES), qmap),
                  pl.BlockSpec((None, lp, LANES), kmap), pl.BlockSpec((None, lp, LANES), kmap),
                  pl.BlockSpec((None, lp, LANES), kmap)],
        out_specs=pl.BlockSpec((None, tq, c), qmap),
        out_shape=jax.ShapeDtypeStruct((b, t, c), F32),
        scratch_shapes=[pltpu.VMEM((tq, lp), jnp.int32), pltpu.VMEM((tq, lp), F32)],
        compiler_params=_cparams(("parallel", "arbitrary")),
        name="dsa_attn",
    )(q, qi, wi, k, v, ki)


FFN_CHUNK = 1024


def _ffn_tail(x1, gf_ref, w1_ref, w2_ref, o_ref):
    h = _rms(x1, gf_ref[...]).astype(BF16)
    acc = None
    for c in range(w1_ref.shape[1] // FFN_CHUNK):
        cols = slice(c * FFN_CHUNK, (c + 1) * FFN_CHUNK)
        a = jnp.square(jnp.maximum(_dot(h, w1_ref[:, cols]), 0.0))
        part = _dot(a.astype(BF16), w2_ref[cols, :])
        acc = part if acc is None else acc + part
    o_ref[...] = x1 + acc


def _even_out_kernel(x_ref, att_ref, u_ref, vb_ref, ws_ref, bst_ref, w_ref, gf_ref, w1_ref, w2_ref, o_ref,
                     *, p_len):
    tm = x_ref.shape[0]
    cg = vb_ref.shape[1] // G_B
    rr = lax.broadcasted_iota(jnp.int32, (p_len, p_len), 0)
    cc = lax.broadcasted_iota(jnp.int32, (p_len, p_len), 1)
    causal = jnp.right_shift(cc, CHUNK_SHIFT) <= jnp.right_shift(rr, CHUNK_SHIFT)
    u = u_ref[...]
    vb = vb_ref[...].astype(BF16)
    bst = bst_ref[...]
    mixed_rows = []
    for c in range(tm // p_len):
        cols = []
        for g in range(G_B):
            wg = jnp.where(causal, ws_ref[g], 0.0).astype(BF16)
            mg = _dot(wg, vb[c * p_len:(c + 1) * p_len, g * cg:(g + 1) * cg]) + bst[:, g:g + 1]
            cols.append(mg)
        mixed_rows.append(jnp.concatenate(cols, axis=1))
    gated = u * jnp.concatenate(mixed_rows, axis=0)
    na = att_ref.shape[1]
    y = _dot(att_ref[...].astype(BF16), w_ref[:na, :]) + _dot(gated.astype(BF16), w_ref[na:, :])
    _ffn_tail(x_ref[...] + y, gf_ref, w1_ref, w2_ref, o_ref)


def _even_out_ffn(x, att, u, vb, ws, bst, w, gf, w1, w2, tm, p_len):
    n_rows, d = x.shape
    n = att.shape[1]
    row = lambda i: (i, 0)
    kern = functools.partial(_even_out_kernel, p_len=p_len)
    return pl.pallas_call(
        kern,
        grid=(n_rows // tm,),
        in_specs=[pl.BlockSpec((tm, d), row), pl.BlockSpec((tm, n), row), pl.BlockSpec((tm, n), row),
                  pl.BlockSpec((tm, n), row), _resident(ws), _resident(bst), _resident(w),
                  _resident(gf), _resident(w1), _resident(w2)],
        out_specs=pl.BlockSpec((tm, d), row),
        out_shape=jax.ShapeDtypeStruct((n_rows, d), F32),
        compiler_params=_cparams(("parallel",)),
        name="even_out_ffn",
    )(x, att, u, vb, ws, bst, _operand(w), gf, _operand(w1), _operand(w2))


def _odd_out_kernel(x_ref, att_ref, gb_ref, cin_ref, prev_ref, cw_ref, w_ref, gf_ref, w1_ref, w2_ref, o_ref,
                    *, t_len, has_prev):
    tm = x_ref.shape[0]
    prev = prev_ref[...]
    if not has_prev:
        start = (pl.program_id(0) * tm) % t_len == 0
        prev = jnp.where(start, 0.0, prev)
    cw = cw_ref[...]
    seg = t_len if has_prev else tm
    convs = []
    for s in range(tm // seg):
        cin = cin_ref[s * seg:(s + 1) * seg, :]
        ext = jnp.concatenate([prev[8 * s:8 * (s + 1)], cin], axis=0)
        conv = cw[CONV_W - 1:CONV_W] * cin
        for j in range(1, CONV_W):
            conv = conv + cw[CONV_W - 1 - j:CONV_W - j] * pltpu.roll(ext, j, 0)[8:]
        convs.append(conv)
    gated = gb_ref[...] * (convs[0] if len(convs) == 1 else jnp.concatenate(convs, axis=0))
    na = att_ref.shape[1]
    y = _dot(att_ref[...].astype(BF16), w_ref[:na, :]) + _dot(gated.astype(BF16), w_ref[na:, :])
    _ffn_tail(x_ref[...] + y, gf_ref, w1_ref, w2_ref, o_ref)


def _odd_out_ffn(x, att, gb, cin, prev, cw, w, gf, w1, w2, tm, t_len):
    n_rows, d = x.shape
    n = att.shape[1]
    row = lambda i: (i, 0)
    has_prev = prev is not None
    if has_prev:
        assert tm % t_len == 0 and t_len % 8 == 0
        prev_arr, prev_map, prev_rows = prev, row, 8 * (tm // t_len)
    else:
        assert t_len % tm == 0 and tm % 8 == 0
        prev_arr, prev_map, prev_rows = cin, (lambda i: (jnp.maximum(i * (tm // 8) - 1, 0), 0)), 8
    kern = functools.partial(_odd_out_kernel, t_len=t_len, has_prev=has_prev)
    return pl.pallas_call(
        kern,
        grid=(n_rows // tm,),
        in_specs=[pl.BlockSpec((tm, d), row), pl.BlockSpec((tm, n), row), pl.BlockSpec((tm, n), row),
                  pl.BlockSpec((tm, n), row), pl.BlockSpec((prev_rows, n), prev_map),
                  _resident(cw), _resident(w), _resident(gf), _resident(w1), _resident(w2)],
        out_specs=pl.BlockSpec((tm, d), row),
        out_shape=jax.ShapeDtypeStruct((n_rows, d), F32),
        compiler_params=_cparams(("parallel",)),
        name="odd_out_ffn",
    )(x, att, gb, cin, prev_arr, cw, _operand(w), gf, _operand(w1), _operand(w2))


def _round_up(n, m):
    return (n + m - 1) // m * m


def _row_tile(n_rows, cap):
    t = min(cap, n_rows)
    while n_rows % t:
        t //= 2
    return t


def _odd_weight_layout(n_q):
    n_heads = n_q // D_HEAD
    rep = n_heads // 2
    perm = np.zeros((n_q,), np.int32)
    for r in range(rep):
        for g in range(2):
            for dd in range(D_HEAD):
                perm[r * LANES + g * D_HEAD + dd] = (g * rep + r) * D_HEAD + dd
    return perm


def kernel(x_prompt, x_sample, cache_a_k, cache_a_v, cache_c_k, cache_c_v, cache_c_kidx, state_d_conv,
           norm_mix_g, norm_ffn_g, w_in_even, gq_a, gk_a, g_b, ws_b, bs_b, w_out_even,
           w_in_odd, gq_c, gk_c, conv_w_d, w_out_odd, w_ffn1, w_ffn2):
    bp, tp, d = x_prompt.shape
    bs, ts, _ = x_sample.shape
    depth = norm_mix_g.shape[0]
    past = cache_a_k.shape[2]
    n_a = cache_a_k.shape[3] * cache_a_k.shape[4]
    n_kv = cache_c_k.shape[3] * cache_c_k.shape[4]
    n_b = g_b.shape[1]
    n_d = conv_w_d.shape[2]
    n_q = w_out_odd.shape[1] - n_d
    assert n_a == 512 and n_b == 512 and n_kv == 128 and n_q == 512 and n_d == 512 and d == 1024

    xp = x_prompt.reshape(bp * tp, d)
    xs = x_sample.reshape(bs * ts, d)
    tm_p = _row_tile(bp * tp, 512)
    tm_s = _row_tile(bs * ts, 512)
    ls = past + ts
    ls_pad = _round_up(ls, KEY_TILE)

    hmean = jnp.asarray(np.kron(np.eye(n_a // D_HEAD), np.full((D_HEAD, D_HEAD), 1.0 / D_HEAD)), BF16)
    perm = _odd_weight_layout(n_q)
    tile8 = lambda gvec: jnp.tile(gvec, n_a // D_HEAD).reshape(1, n_a)

    ak_p, av_p, ak_s, av_s, bv_s = [], [], [], [], []
    ck_p, cv_p, ci_p, ck_s, cv_s, ci_s, dc_p, dc_s = [], [], [], [], [], [], [], []

    tri = _stick_tri()

    def pad_keys(past_rows, new_rows, multiple=KEY_TILE):
        width = new_rows.shape[-1]
        zeros = jnp.zeros((bs, _round_up(ls, multiple) - ls, width), F32)
        return jnp.concatenate([past_rows.reshape(bs, past, width), new_rows.reshape(bs, ts, width), zeros], axis=1)

    w1_all, w2_all = w_ffn1.astype(BF16), w_ffn2.astype(BF16)
    w_in_even_all, w_out_even_all = w_in_even.astype(BF16), w_out_even.astype(BF16)
    offs = np.cumsum([0, n_q, n_kv, n_kv, H_IDX * D_IDX, D_IDX, H_IDX, n_d, n_d, n_d])
    col = lambda s: w_in_odd[:, :, int(offs[s]):int(offs[s + 1])]
    w_in_odd_all = jnp.concatenate(
        [col(0)[:, :, perm], col(1), col(2), col(3), col(4), col(4), col(5),
         jnp.zeros((w_in_odd.shape[0], d, LANES - H_IDX), F32), col(6), col(7), col(8)], axis=2).astype(BF16)
    assert w_in_odd_all.shape[2] == ODD_COLS
    w_out_odd_all = jnp.concatenate([w_out_odd[:, :n_q][:, perm], w_out_odd[:, n_q:]], axis=1).astype(BF16)

    for i in range(depth):
        j = i // 2
        g_mix = norm_mix_g[i].reshape(1, d)
        ffn = (norm_ffn_g[i].reshape(1, d), (w1_all, i), (w2_all, i))
        if i % 2 == 0:
            w_in = (w_in_even_all, j)
            w_out = (w_out_even_all, j)
            gq, gk, gb = tile8(gq_a[j]), tile8(gk_a[j]), g_b[j].reshape(1, n_b)
            q, k, v, u, vb, kb, vm = _even_in(xp, g_mix, w_in, hmean, gq, gk, gb, tm_p)
            att = _stick(q.reshape(bp, tp, n_a), kb.reshape(bp, tp, n_a), vm.reshape(bp, tp, 2 * n_a), tri,
                         min(tp, KEY_TILE), 0)
            xp = _even_out_ffn(xp, att.reshape(bp * tp, n_a), u, vb, ws_b[j], bs_b[j].T, w_out, *ffn, tm_p, B_CHUNK)
            ak_p.append(k.reshape(bp, tp, n_a // D_HEAD, D_HEAD))
            av_p.append(v.reshape(bp, tp, n_a // D_HEAD, D_HEAD))
            q, k, v, u, vb, _, _ = _even_in(xs, g_mix, w_in, hmean, gq, gk, gb, tm_s)
            att = _stick(q.reshape(bs, ts, n_a), pad_keys(cache_a_k[j], k, STICK_TK),
                         pad_keys(cache_a_v[j], v, STICK_TK), tri, ts, past)
            xs = _even_out_ffn(xs, att.reshape(bs * ts, n_a), u, vb, ws_b[j][:, :ts, :ts], bs_b[j][:, :ts].T,
                               w_out, *ffn, tm_s, ts)
            ak_s.append(k.reshape(bs, ts, n_a // D_HEAD, D_HEAD))
            av_s.append(v.reshape(bs, ts, n_a // D_HEAD, D_HEAD))
            bv_s.append(vb.reshape(bs, ts, n_b))
        else:
            w_in = (w_in_odd_all, j)
            w_out = (w_out_odd_all, j)
            gq, gk = tile8(gq_c[j]), jnp.tile(gk_c[j], n_kv // D_HEAD).reshape(1, n_kv)
            cw = conv_w_d[j]
            q, k, v, qi, ki, wi, gbt, cin = _odd_in(xp, g_mix, w_in, hmean, gq, gk, tm_p)
            r3 = lambda a: a.reshape(bp, tp, a.shape[-1])
            att = _dsa(r3(q), r3(qi), r3(wi), r3(k), r3(v), r3(ki), min(tp, 2 * KEY_TILE), 0, tp)
            xp = _odd_out_ffn(xp, att.reshape(bp * tp, n_q), gbt, cin, None, cw, w_out, *ffn, tm_p, tp)
            ck_p.append(k.reshape(bp, tp, n_kv // D_HEAD, D_HEAD))
            cv_p.append(v.reshape(bp, tp, n_kv // D_HEAD, D_HEAD))
            ci_p.append(ki[:, :D_IDX].reshape(bp, tp, D_IDX))
            dc_p.append(cin.reshape(bp, tp, n_d)[:, tp - (CONV_W - 1):])
            q, k, v, qi, ki, wi, gbt, cin = _odd_in(xs, g_mix, w_in, hmean, gq, gk, tm_s)
            r3 = lambda a: a.reshape(bs, ts, a.shape[-1])
            past_ki = jnp.concatenate([cache_c_kidx[j], cache_c_kidx[j]], axis=-1)
            att = _dsa(r3(q), r3(qi), r3(wi), pad_keys(cache_c_k[j], k), pad_keys(cache_c_v[j], v),
                       pad_keys(past_ki, ki), ts, past, ls)
            prev = jnp.concatenate([jnp.zeros((bs, 8 - (CONV_W - 1), n_d), F32), state_d_conv[j]], axis=1)
            xs = _odd_out_ffn(xs, att.reshape(bs * ts, n_q), gbt, cin, prev.reshape(bs * 8, n_d), cw, w_out, *ffn,
                              tm_s, ts)
            ck_s.append(k.reshape(bs, ts, n_kv // D_HEAD, D_HEAD))
            cv_s.append(v.reshape(bs, ts, n_kv // D_HEAD, D_HEAD))
            ci_s.append(ki[:, :D_IDX].reshape(bs, ts, D_IDX))
            dc_s.append(cin.reshape(bs, ts, n_d)[:, ts - (CONV_W - 1):])

    return (xp.reshape(bp, tp, d), xs.reshape(bs, ts, d),
            jnp.stack(ak_p), jnp.stack(av_p), jnp.stack(ak_s), jnp.stack(av_s),
            jnp.stack(bv_s),
            jnp.stack(ck_p), jnp.stack(cv_p), jnp.stack(ci_p),
            jnp.stack(ck_s), jnp.stack(cv_s), jnp.stack(ci_s),
            jnp.stack(dc_p), jnp.stack(dc_s))
```

```python
import functools

import numpy as np
import jax
import jax.numpy as jnp
from jax import lax
from jax.experimental import pallas as pl
from jax.experimental.pallas import tpu as pltpu

F32 = jnp.float32
BF16 = jnp.bfloat16

EPS = 1e-6
NEG = -1e30
LOG2_E = 1.4426950408889634
CHUNK = 64
CHUNK_SHIFT = 6
D_HEAD = 64
G_B = 4
B_CHUNK = 128
H_IDX = 4
D_IDX = 64
TOPK_MAX = 256
CONV_W = 3
LANES = 128
KEY_TILE = 128
VMEM_LIMIT = 56 * 1024 * 1024


def _cparams(sem):
    return pltpu.CompilerParams(dimension_semantics=sem, vmem_limit_bytes=VMEM_LIMIT)


def _resident(a):
    if isinstance(a, tuple):
        arr, layer = a
        shape = arr.shape[1:]
        return pl.BlockSpec((None,) + shape, lambda *_: (layer,) + (0,) * len(shape),
                            pipeline_mode=pl.Buffered(1))
    return pl.BlockSpec(a.shape, lambda *_: (0,) * a.ndim, pipeline_mode=pl.Buffered(1))


def _operand(a):
    return a[0] if isinstance(a, tuple) else a


def _dot(a, b):
    return jnp.dot(a, b, preferred_element_type=F32)


def _dot_t(a, b):
    return lax.dot_general(a, b, (((1,), (1,)), ((), ())), preferred_element_type=F32)


def _split(x):
    hi = x.astype(BF16)
    lo = (x - hi.astype(F32)).astype(BF16)
    return hi, lo


def _rms(x, g):
    return x * lax.rsqrt(jnp.mean(x * x, axis=-1, keepdims=True) + EPS) * g


def _head_rms(x, hmean, g):
    hi, lo = _split(x * x)
    ms = _dot(hi, hmean) + _dot(lo, hmean)
    return x * lax.rsqrt(ms + EPS) * g


def _gelu(x):
    return jax.nn.gelu(x)


def _even_in_kernel(x_ref, g_ref, w_ref, hm_ref, gq_ref, gk_ref, gb_ref,
                    q_ref, k_ref, v_ref, u_ref, vb_ref, kb_ref, vm_ref):
    h = _rms(x_ref[...], g_ref[...]).astype(BF16)
    n = q_ref.shape[-1]

    def seg(i):
        return _dot(h, w_ref[:, i * n:(i + 1) * n])

    hm = hm_ref[...]
    q_ref[...] = _head_rms(seg(0), hm, gq_ref[...])
    k = _head_rms(seg(1), hm, gk_ref[...])
    k_ref[...] = k
    kb_ref[...] = k.astype(BF16)
    v = seg(2)
    v_ref[...] = v
    first = lax.broadcasted_iota(jnp.int32, (1, LANES), 1) < D_HEAD
    for p in range(n // LANES):
        vp = v[:, p * LANES:(p + 1) * LANES]
        vm_ref[:, 2 * p * LANES:(2 * p + 1) * LANES] = jnp.where(first, vp, 0.0).astype(BF16)
        vm_ref[:, (2 * p + 1) * LANES:(2 * p + 2) * LANES] = jnp.where(first, 0.0, vp).astype(BF16)
    u_ref[...] = _gelu(seg(3))
    vb_ref[...] = _rms(_gelu(seg(4)), gb_ref[...])


def _even_in(x, g, w, hm, gq, gk, gb, tm):
    n_rows, d = x.shape
    n = _operand(w).shape[-1] // 5
    row = lambda i: (i, 0)
    out = jax.ShapeDtypeStruct((n_rows, n), F32)
    return pl.pallas_call(
        _even_in_kernel,
        grid=(n_rows // tm,),
        in_specs=[pl.BlockSpec((tm, d), row), _resident(g), _resident(w), _resident(hm),
                  _resident(gq), _resident(gk), _resident(gb)],
        out_specs=[pl.BlockSpec((tm, n), row)] * 6 + [pl.BlockSpec((tm, 2 * n), row)],
        out_shape=[out] * 5 + [jax.ShapeDtypeStruct((n_rows, n), BF16), jax.ShapeDtypeStruct((n_rows, 2 * n), BF16)],
        compiler_params=_cparams(("parallel",)),
        name="even_in",
    )(x, g, _operand(w), hm, gq, gk, gb)


ODD_Q, ODD_K, ODD_V, ODD_QI, ODD_KI, ODD_WI, ODD_GB, ODD_GC, ODD_HD = (
    0, 512, 640, 768, 1024, 1152, 1280, 1792, 2304)
ODD_COLS = 2816


def _odd_in_kernel(x_ref, g_ref, w_ref, hm_ref, gq_ref, gk_ref,
                   q_ref, k_ref, v_ref, qi_ref, ki_ref, wi_ref, gb_ref, cin_ref):
    h = _rms(x_ref[...], g_ref[...]).astype(BF16)

    def seg(lo, width):
        return _dot(h, w_ref[:, lo:lo + width])

    hm = hm_ref[...]
    q_ref[...] = _head_rms(seg(ODD_Q, 512), hm, gq_ref[...])
    k_ref[...] = _head_rms(seg(ODD_K, 128), hm[:128, :128], gk_ref[...])
    v_ref[...] = seg(ODD_V, 128)
    qi_ref[...] = seg(ODD_QI, 256)
    ki_ref[...] = seg(ODD_KI, 128)
    wi_ref[...] = seg(ODD_WI, 128)
    gb_ref[...] = seg(ODD_GB, 512)
    cin_ref[...] = seg(ODD_GC, 512) * seg(ODD_HD, 512)


def _odd_in(x, g, w, hm, gq, gk, tm):
    n_rows, d = x.shape
    row = lambda i: (i, 0)
    widths = (512, 128, 128, 256, 128, 128, 512, 512)
    return pl.pallas_call(
        _odd_in_kernel,
        grid=(n_rows // tm,),
        in_specs=[pl.BlockSpec((tm, d), row), _resident(g), _resident(w), _resident(hm),
                  _resident(gq), _resident(gk)],
        out_specs=[pl.BlockSpec((tm, n), row) for n in widths],
        out_shape=[jax.ShapeDtypeStruct((n_rows, n), F32) for n in widths],
        compiler_params=_cparams(("parallel",)),
        name="odd_in",
    )(x, g, _operand(w), hm, gq, gk)


STICK_TK = 256
F32_EXP_ZERO = -104.0


def _stick_tri():
    half = STICK_TK // 2
    later = (np.arange(half)[:, None] > np.arange(half)[None, :]).astype(np.float32)
    blk = np.concatenate([later, np.ones((half, half), np.float32)], axis=1)
    return jnp.asarray(np.concatenate([blk, blk], axis=0), BF16)


def _stick_kernel(q_ref, k_ref, v_ref, tri_ref, o_ref, carry_sc, acc_sc, *, tq, q_pos0):
    tk = STICK_TK
    half = tk // 2
    n_pairs = q_ref.shape[1] // LANES
    v_split = v_ref.shape[1] == 2 * q_ref.shape[1]
    q_start = q_pos0 + pl.program_id(1) * tq
    n_full = q_start // tk
    lane = lax.broadcasted_iota(jnp.int32, (1, LANES), 1)
    first = lane < D_HEAD
    q = q_ref[...] * (D_HEAD ** -0.5)
    q2 = []
    for p in range(n_pairs):
        qp = q[:, p * LANES:(p + 1) * LANES]
        q2.append(jnp.concatenate([jnp.where(first, qp, 0.0), jnp.where(first, 0.0, qp)], axis=0).astype(BF16))
    tri = tri_ref[...]
    carry_sc[...] = jnp.zeros_like(carry_sc)
    acc_sc[...] = jnp.zeros_like(acc_sc)

    def block(kb, masked):
        ks = pl.multiple_of(kb * tk, tk)
        if masked:
            row = lax.broadcasted_iota(jnp.int32, (2 * tq, tk), 0)
            qpos = q_start + jnp.where(row >= tq, row - tq, row)
            kpos = ks + lax.broadcasted_iota(jnp.int32, (2 * tq, tk), 1)
            ok = kpos < qpos
        for p in range(n_pairs):
            cols = slice(p * LANES, (p + 1) * LANES)
            z = _dot_t(q2[p], k_ref[pl.ds(ks, tk), cols].astype(BF16))
            nz = -z
            stay = jnp.minimum(nz, 0.0) - jnp.log(1.0 + jnp.exp(jnp.minimum(z, nz)))
            if masked:
                stay = jnp.where(ok, stay, 0.0)
            carry = carry_sc[p]
            after = [None, None]
            for h in (1, 0):
                hi, lo = _split(stay[:, h * half:(h + 1) * half])
                r = _dot(jnp.concatenate([hi, lo], axis=1), tri)
                after[h] = r[:, :half] + carry
                carry = carry + r[:, half:]
            carry_sc[p] = carry
            w = jnp.exp(z + stay + jnp.concatenate(after, axis=1))
            if masked:
                w = jnp.where(ok, w, 0.0)
            w = w.astype(BF16)
            if v_split:
                v0 = v_ref[pl.ds(ks, tk), 2 * p * LANES:(2 * p + 1) * LANES]
                v1 = v_ref[pl.ds(ks, tk), (2 * p + 1) * LANES:(2 * p + 2) * LANES]
            else:
                vblk = v_ref[pl.ds(ks, tk), cols]
                v0 = jnp.where(first, vblk, 0.0).astype(BF16)
                v1 = jnp.where(first, 0.0, vblk).astype(BF16)
            acc_sc[p] += _dot(jnp.concatenate([w[:tq], w[tq:]], axis=1), jnp.concatenate([v0, v1], axis=0))

    block(n_full, True)

    def any_weight_left():
        m = jnp.max(jnp.max(carry_sc[...], axis=0), axis=0, keepdims=True)
        return jnp.max(m, axis=1, keepdims=True)[0, 0] > F32_EXP_ZERO

    def body(s):
        block(s[0], False)
        return s[0] - 1, any_weight_left()

    lax.while_loop(lambda s: (s[0] >= 0) & s[1], body, (n_full - 1, any_weight_left()))
    for p in range(n_pairs):
        o_ref[:, p * LANES:(p + 1) * LANES] = acc_sc[p]


def _stick(q, k, v, tri, tq, q_pos0):
    b, t, c = q.shape
    lp = k.shape[1]
    tk = STICK_TK
    assert lp % tk == 0 and q_pos0 % tq == 0 and tk % tq == 0 and q_pos0 + t <= lp
    kern = functools.partial(_stick_kernel, tq=tq, q_pos0=q_pos0)
    kmap = lambda bi, i: (bi, 0, 0)
    return pl.pallas_call(
        kern,
        grid=(b, t // tq),
        in_specs=[pl.BlockSpec((None, tq, c), lambda bi, i: (bi, i, 0)),
                  pl.BlockSpec((None, lp, c), kmap),
                  pl.BlockSpec((None, lp, v.shape[2]), kmap),
                  pl.BlockSpec(tri.shape, lambda bi, i: (0, 0))],
        out_specs=pl.BlockSpec((None, tq, c), lambda bi, i: (bi, i, 0)),
        out_shape=jax.ShapeDtypeStruct((b, t, c), F32),
        scratch_shapes=[pltpu.VMEM((c // LANES, 2 * tq, LANES), F32), pltpu.VMEM((c // LANES, tq, LANES), F32)],
        compiler_params=_cparams(("parallel", "arbitrary")),
        name="stick_attn",
    )(q, k, v, tri)


DSA_LEVEL = 512


def _dsa_rows_kernel(q_ref, qi_ref, wi_ref, k_ref, v_ref, ki_ref, o_ref, key_sc, bias_sc,
                     *, tq, l_valid, q_pos0, top_k, levels):
    kt = KEY_TILE
    th = min(tq, 128)
    parts = [slice(i * th, (i + 1) * th) for i in range(tq // th)]
    n_rep = q_ref.shape[1] // LANES
    lane = lax.broadcasted_iota(jnp.int32, (1, LANES), 1)
    first = lane < D_HEAD
    q_start = q_pos0 + pl.program_id(1) * tq
    kf = jnp.float32(top_k)

    def body(lv):
        n_kt = lv // kt

        def admissible(rows, lo, width):
            qpos = q_start + rows.start + lax.broadcasted_iota(jnp.int32, (th, 1), 0)
            kpos = lo + lax.broadcasted_iota(jnp.int32, (1, width), 1)
            return (jnp.right_shift(kpos, CHUNK_SHIFT) <= jnp.right_shift(qpos, CHUNK_SHIFT)) & (kpos < l_valid)

        ki_hi, ki_lo = _split(ki_ref[:lv, :])
        for rows in parts:
            qi = qi_ref[rows, :]
            wi = wi_ref[rows, :]
            qm = []
            for h in range(H_IDX):
                grp = qi[:, (h // 2) * LANES:(h // 2 + 1) * LANES]
                qm.append(jnp.where(first, grp, 0.0) if h % 2 == 0 else jnp.where(first, 0.0, grp))
            q_hi, q_lo = _split(jnp.concatenate(qm, axis=0))
            s = _dot_t(q_hi, ki_hi) + (_dot_t(q_hi, ki_lo) + _dot_t(q_lo, ki_hi))
            score = jnp.zeros((th, lv), F32)
            for h in range(H_IDX):
                score = score + jnp.maximum(s[h * th:(h + 1) * th], 0.0) * wi[:, h:h + 1]
            score = score * ((D_IDX ** -0.5) * (H_IDX ** -0.5)) + 0.0
            score = jnp.where(admissible(rows, 0, lv), score, NEG)
            bits = lax.bitcast_convert_type(score, jnp.int32)
            key_sc[rows, :lv] = jnp.where(bits < 0, bits ^ jnp.int32(0x7FFFFFFF), bits)

        def count_ge(rows, trial):
            part = jnp.zeros((th, kt), F32)
            for c in range(n_kt):
                part = part + jnp.where(key_sc[rows, c * kt:(c + 1) * kt] >= trial, 1.0, 0.0)
            return jnp.sum(part, axis=1, keepdims=True)

        int_min = jnp.int32(-2 ** 31)
        zero = jnp.zeros((th, 1), jnp.int32)
        thr0 = tuple(jnp.where(count_ge(rows, zero) >= kf, jnp.int32(0), int_min) for rows in parts)

        def bit_step(i, thr):
            bit = jnp.left_shift(jnp.int32(1), jnp.int32(30) - i)
            return tuple(jnp.where(count_ge(rows, t + bit) >= kf, t + bit, t) for rows, t in zip(parts, thr))

        thrs = lax.fori_loop(0, 31, bit_step, thr0)

        rr = lax.broadcasted_iota(jnp.int32, (kt, kt), 0)
        cc = lax.broadcasted_iota(jnp.int32, (kt, kt), 1)
        before = jnp.where(rr < cc, 1.0, 0.0).astype(BF16)
        for rows, thr in zip(parts, thrs):
            n_gt = jnp.zeros((th, kt), F32)
            for c in range(n_kt):
                n_gt = n_gt + jnp.where(key_sc[rows, c * kt:(c + 1) * kt] > thr, 1.0, 0.0)
            room = kf - jnp.sum(n_gt, axis=1, keepdims=True)
            run = jnp.zeros((th, 1), F32)
            for c in range(n_kt):
                keyc = key_sc[rows, c * kt:(c + 1) * kt]
                eqf = jnp.where(keyc == thr, 1.0, 0.0)
                rank = _dot(eqf.astype(BF16), before) + run
                tie_ok = jnp.where(keyc == thr, jnp.where(rank < room, 0.0, NEG), NEG)
                sel = jnp.where(keyc > thr, 0.0, tie_ok)
                bias_sc[rows, c * kt:(c + 1) * kt] = jnp.where(admissible(rows, c * kt, kt), sel, NEG)
                run = run + jnp.sum(eqf, axis=1, keepdims=True)

        kb = k_ref[:lv, :].astype(BF16)
        vf = v_ref[:lv, :]
        vg = [jnp.where(first, vf, 0.0).astype(BF16), jnp.where(first, 0.0, vf).astype(BF16)]
        for rows in parts:
            q = q_ref[rows, :] * (D_HEAD ** -0.5)
            bias = bias_sc[rows, :lv]
            out = jnp.zeros((n_rep * th, LANES), F32)
            for g in range(2):
                keep = first if g == 0 else jnp.logical_not(first)
                qs = jnp.concatenate(
                    [jnp.where(keep, q[:, r * LANES:(r + 1) * LANES], 0.0) for r in range(n_rep)],
                    axis=0).astype(BF16)
                logits = _dot_t(qs, kb)
                logits = (logits.reshape(n_rep, th, lv) + bias[None]).reshape(n_rep * th, lv)
                m = jnp.max(logits, axis=1, keepdims=True)
                p = jnp.exp(logits - m)
                den = jnp.sum(p, axis=1, keepdims=True)
                out = out + _dot(p.astype(BF16), vg[g]) / den
            for r in range(n_rep):
                o_ref[rows, r * LANES:(r + 1) * LANES] = out[r * th:(r + 1) * th]

    if len(levels) == 1:
        body(levels[0])
    else:
        cls = (q_start + tq - 1) // DSA_LEVEL
        for c, lv in enumerate(levels):
            pl.when(cls == c)(functools.partial(body, lv))


DSA_CK = 256


def _dsa_kernel(*refs, tq, l_valid, q_pos0, top_k, levels):
    if len(levels) == 1:
        _dsa_tile(*refs, tq=tq, l_valid=l_valid, q_pos0=q_pos0, top_k=top_k, n_chunks=levels[0])
        return
    q_last = q_pos0 + pl.program_id(1) * tq + tq - 1
    vis_end = jnp.left_shift(jnp.right_shift(q_last, CHUNK_SHIFT) + 1, CHUNK_SHIFT)
    step = levels[0] * DSA_CK
    for i, n in enumerate(levels):
        pl.when((vis_end + step - 1) // step == i + 1)(functools.partial(
            _dsa_tile, *refs, tq=tq, l_valid=l_valid, q_pos0=q_pos0, top_k=top_k, n_chunks=n))


def _dsa_tile(q_ref, qi_ref, wi_ref, k_ref, v_ref, ki_ref, o_ref,
              kic_sc, kb_sc, vt_sc, key_sc, bias_sc, *, tq, l_valid, q_pos0, top_k, n_chunks):
    ck = DSA_CK
    lp = k_ref.shape[0]
    n_rep = q_ref.shape[1] // LANES
    lane = lax.broadcasted_iota(jnp.int32, (1, LANES), 1)
    first = lane < D_HEAD
    q_start = q_pos0 + pl.program_id(1) * tq
    qchunk = jnp.right_shift(q_start + lax.broadcasted_iota(jnp.int32, (1, tq), 1), CHUNK_SHIFT)
    kf = jnp.float32(top_k)
    top_rows = lax.broadcasted_iota(jnp.int32, (LANES, 1), 0) < D_HEAD

    def rows_of(c):
        return pl.ds(c * ck, ck)

    def admissible(c):
        kpos = c * ck + lax.broadcasted_iota(jnp.int32, (ck, 1), 0)
        return (jnp.right_shift(kpos, CHUNK_SHIFT) <= qchunk) & (kpos < l_valid)

    def chunks(n, step, carry):
        for c in range(n):
            carry = step(c, carry)
        return carry

    @pl.when(pl.program_id(1) == 0)
    def _():
        def prep(c, _):
            rows = rows_of(c)
            hi, lo = _split(ki_ref[rows, :])
            kic_sc[rows, :LANES] = jnp.where(first, hi, lo)
            kic_sc[rows, LANES:] = jnp.where(first, hi, jnp.zeros_like(hi))
            kb_sc[rows, :] = k_ref[rows, :].astype(BF16)
            vt = v_ref[rows, :].T
            vt_sc[c, :, :ck] = jnp.where(top_rows, vt, 0.0).astype(BF16)
            vt_sc[c, :, ck:] = jnp.where(top_rows, 0.0, vt).astype(BF16)
            return 0
        chunks(lp // ck, prep, 0)

    qi = qi_ref[...]
    blocks = []
    for h in range(H_IDX):
        grp = qi[:, (h // 2) * LANES:(h // 2 + 1) * LANES]
        swapped = pltpu.roll(grp, D_IDX, 1)
        twice = jnp.where(first, grp, swapped) if h % 2 == 0 else jnp.where(first, swapped, grp)
        x3 = jnp.concatenate([twice, jnp.where(first, twice, 0.0)], axis=1)
        hi = x3.astype(BF16).astype(F32)
        blocks.append(jnp.concatenate([hi[:, :LANES], (x3 - hi)[:, LANES:]], axis=1))
    qic_t = jnp.concatenate(blocks, axis=0).T.astype(BF16)
    wit = wi_ref[...].T

    def score_chunk(c, _):
        s = _dot(kic_sc[rows_of(c), :], qic_t)
        score = jnp.zeros((ck, tq), F32)
        for h in range(H_IDX):
            score = score + jnp.maximum(s[:, h * tq:(h + 1) * tq], 0.0) * wit[h:h + 1, :]
        score = score * ((D_IDX ** -0.5) * (H_IDX ** -0.5)) + 0.0
        score = jnp.where(admissible(c), score, NEG)
        bits = lax.bitcast_convert_type(score, jnp.int32)
        key_sc[rows_of(c), :] = jnp.where(bits < 0, bits ^ jnp.int32(0x7FFFFFFF), bits)
        return 0

    chunks(n_chunks, score_chunk, 0)

    def count(pred, trial):
        def body(c, acc):
            hit = jnp.where(pred(key_sc[rows_of(c), :], trial), 1.0, 0.0)
            return acc + jnp.sum(hit.reshape(ck // 8, 8, tq), axis=0)
        acc = chunks(n_chunks, body, jnp.zeros((8, tq), F32))
        return jnp.sum(acc, axis=0, keepdims=True)

    ge = lambda a, b: a >= b
    thr = jnp.where(count(ge, jnp.zeros((1, tq), jnp.int32)) >= kf, jnp.int32(0), jnp.int32(-2 ** 31))

    def bit_step(i, t):
        trial = t + jnp.left_shift(jnp.int32(1), jnp.int32(30) - i)
        return jnp.where(count(ge, trial) >= kf, trial, t)

    thr = lax.fori_loop(0, 31, bit_step, thr)

    room = kf - count(lambda a, b: a > b, thr)
    rr = lax.broadcasted_iota(jnp.int32, (ck, ck), 0)
    cc = lax.broadcasted_iota(jnp.int32, (ck, ck), 1)
    earlier = jnp.where(cc < rr, 1.0, 0.0).astype(BF16)

    def tie_chunk(c, run):
        keyc = key_sc[rows_of(c), :]
        eqf = jnp.where(keyc == thr, 1.0, 0.0)
        rank = _dot(earlier, eqf.astype(BF16)) + run
        tie_ok = jnp.where(keyc == thr, jnp.where(rank < room, 0.0, NEG), NEG)
        sel = jnp.where(keyc > thr, 0.0, tie_ok)
        bias_sc[rows_of(c), :] = jnp.where(admissible(c), sel, NEG)
        return run + jnp.sum(eqf, axis=0, keepdims=True)

    chunks(n_chunks, tie_chunk, jnp.zeros((1, tq), F32))

    q = q_ref[...] * (D_HEAD ** -0.5 * LOG2_E)
    qs_t = []
    for g in range(2):
        keep = first if g == 0 else jnp.logical_not(first)
        qs_t.append(jnp.concatenate(
            [jnp.where(keep, q[:, r * LANES:(r + 1) * LANES], 0.0) for r in range(n_rep)], axis=0).T.astype(BF16))
    width = n_rep * tq

    def att_chunk(c, carry):
        m0, m1, d0, d1, acc = carry
        kb = kb_sc[rows_of(c), :]
        bias = bias_sc[rows_of(c), :]
        bias = jnp.concatenate([bias] * n_rep, axis=1)
        l0 = _dot(kb, qs_t[0]) + bias
        l1 = _dot(kb, qs_t[1]) + bias
        n0 = jnp.maximum(m0, jnp.max(l0, axis=0, keepdims=True))
        n1 = jnp.maximum(m1, jnp.max(l1, axis=0, keepdims=True))
        a0, a1 = jnp.exp2(m0 - n0), jnp.exp2(m1 - n1)
        p0, p1 = jnp.exp2(l0 - n0), jnp.exp2(l1 - n1)
        d0 = d0 * a0 + jnp.sum(p0, axis=0, keepdims=True)
        d1 = d1 * a1 + jnp.sum(p1, axis=0, keepdims=True)
        pv = _dot(vt_sc[c], jnp.concatenate([p0.astype(BF16), p1.astype(BF16)], axis=0))
        acc = acc * jnp.where(top_rows, a0, a1) + pv
        return n0, n1, d0, d1, acc

    neg = jnp.full((1, width), NEG, F32)
    zero = jnp.zeros((1, width), F32)
    _, _, d0, d1, acc = chunks(n_chunks, att_chunk, (neg, neg, zero, zero, jnp.zeros((LANES, width), F32)))
    out_t = acc / jnp.where(top_rows, d0, d1)
    for r in range(n_rep):
        o_ref[:, r * LANES:(r + 1) * LANES] = out_t[:, r * tq:(r + 1) * tq].T


def _dsa(q, qi, wi, k, v, ki, tq, q_pos0, l_valid):
    b, t, c = q.shape
    lp = k.shape[1]
    assert lp % DSA_CK == 0 and l_valid <= lp and t % tq == 0 and tq % LANES == 0
    top_k = min(TOPK_MAX, l_valid // 4)
    assert DSA_CK >= top_k
    n_all = lp // DSA_CK
    if q_pos0 == 0 and n_all % 4 == 0:
        levels = tuple(range(n_all // 4, n_all + 1, n_all // 4))
    else:
        levels = (n_all,)
    kern = functools.partial(_dsa_kernel, tq=tq, l_valid=l_valid, q_pos0=q_pos0, top_k=top_k, levels=levels)
    qmap = lambda bi, i: (bi, i, 0)
    kmap = lambda bi, i: (bi, 0, 0)
    return pl.pallas_call(
        kern,
        grid=(b, t // tq),
        in_specs=[pl.BlockSpec((None, tq, c), qmap), pl.BlockSpec((None, tq, qi.shape[2]), qmap),
                  pl.BlockSpec((None, tq, LANES), qmap),
                  pl.BlockSpec((None, lp, LANES), kmap), pl.BlockSpec((None, lp, LANES), kmap),
                  pl.BlockSpec((None, lp, LANES), kmap)],
        out_specs=pl.BlockSpec((None, tq, c), qmap),
        out_shape=jax.ShapeDtypeStruct((b, t, c), F32),
        scratch_shapes=[pltpu.VMEM((lp, 2 * LANES), BF16), pltpu.VMEM((lp, LANES), BF16),
                        pltpu.VMEM((lp // DSA_CK, LANES, 2 * DSA_CK), BF16), pltpu.VMEM((lp, tq), jnp.int32),
                        pltpu.VMEM((lp, tq), F32)],
        compiler_params=_cparams(("parallel", "arbitrary")),
        name="dsa_attn",
    )(q, qi, wi, k, v, ki)


def _dsa_rows(q, qi, wi, k, v, ki, tq, q_pos0, l_valid):
    b, t, c = q.shape
    lp = k.shape[1]
    assert lp % KEY_TILE == 0 and l_valid <= lp and t % tq == 0
    top_k = min(TOPK_MAX, l_valid // 4)
    if q_pos0 == 0 and lp % DSA_LEVEL == 0 and tq % CHUNK == 0 and DSA_LEVEL >= top_k:
        levels = tuple(range(DSA_LEVEL, lp + 1, DSA_LEVEL))
    else:
        levels = (lp,)
    kern = functools.partial(_dsa_rows_kernel, tq=tq, l_valid=l_valid, q_pos0=q_pos0, top_k=top_k, levels=levels)
    qmap = lambda bi, i: (bi, i, 0)
    kmap = lambda bi, i: (bi, 0, 0)
    return pl.pallas_call(
        kern,
        grid=(b, t // tq),
        in_specs=[pl.BlockSpec((None, tq, c), qmap), pl.BlockSpec((None, tq, qi.shape[2]), qmap),
                  pl.BlockSpec((None, tq, LANES), qmap),
                  pl.BlockSpec((None, lp, LANES), kmap), pl.BlockSpec((None, lp, LANES), kmap),
                  pl.BlockSpec((None, lp, LANES), kmap)],
        out_specs=pl.BlockSpec((None, tq, c), qmap),
        out_shape=jax.ShapeDtypeStruct((b, t, c), F32),
        scratch_shapes=[pltpu.VMEM((tq, lp), jnp.int32), pltpu.VMEM((tq, lp), F32)],
        compiler_params=_cparams(("parallel", "arbitrary")),
        name="dsa_attn",
    )(q, qi, wi, k, v, ki)


FFN_CHUNK = 1024


def _ffn_tail(x1, gf_ref, w1_ref, w2_ref, o_ref):
    h = _rms(x1, gf_ref[...]).astype(BF16)
    acc = None
    for c in range(w1_ref.shape[1] // FFN_CHUNK):
        cols = slice(c * FFN_CHUNK, (c + 1) * FFN_CHUNK)
        a = jnp.square(jnp.maximum(_dot(h, w1_ref[:, cols]), 0.0))
        part = _dot(a.astype(BF16), w2_ref[cols, :])
        acc = part if acc is None else acc + part
    o_ref[...] = x1 + acc


def _even_out_kernel(x_ref, att_ref, u_ref, vb_ref, ws_ref, bst_ref, w_ref, gf_ref, w1_ref, w2_ref, o_ref,
                     *, p_len):
    tm = x_ref.shape[0]
    cg = vb_ref.shape[1] // G_B
    rr = lax.broadcasted_iota(jnp.int32, (p_len, p_len), 0)
    cc = lax.broadcasted_iota(jnp.int32, (p_len, p_len), 1)
    causal = jnp.right_shift(cc, CHUNK_SHIFT) <= jnp.right_shift(rr, CHUNK_SHIFT)
    u = u_ref[...]
    vb = vb_ref[...].astype(BF16)
    bst = bst_ref[...]
    mixed_rows = []
    for c in range(tm // p_len):
        cols = []
        for g in range(G_B):
            wg = jnp.where(causal, ws_ref[g], 0.0).astype(BF16)
            mg = _dot(wg, vb[c * p_len:(c + 1) * p_len, g * cg:(g + 1) * cg]) + bst[:, g:g + 1]
            cols.append(mg)
        mixed_rows.append(jnp.concatenate(cols, axis=1))
    gated = u * jnp.concatenate(mixed_rows, axis=0)
    na = att_ref.shape[1]
    y = _dot(att_ref[...].astype(BF16), w_ref[:na, :]) + _dot(gated.astype(BF16), w_ref[na:, :])
    _ffn_tail(x_ref[...] + y, gf_ref, w1_ref, w2_ref, o_ref)


def _even_out_ffn(x, att, u, vb, ws, bst, w, gf, w1, w2, tm, p_len):
    n_rows, d = x.shape
    n = att.shape[1]
    row = lambda i: (i, 0)
    kern = functools.partial(_even_out_kernel, p_len=p_len)
    return pl.pallas_call(
        kern,
        grid=(n_rows // tm,),
        in_specs=[pl.BlockSpec((tm, d), row), pl.BlockSpec((tm, n), row), pl.BlockSpec((tm, n), row),
                  pl.BlockSpec((tm, n), row), _resident(ws), _resident(bst), _resident(w),
                  _resident(gf), _resident(w1), _resident(w2)],
        out_specs=pl.BlockSpec((tm, d), row),
        out_shape=jax.ShapeDtypeStruct((n_rows, d), F32),
        compiler_params=_cparams(("parallel",)),
        name="even_out_ffn",
    )(x, att, u, vb, ws, bst, _operand(w), gf, _operand(w1), _operand(w2))


def _odd_out_kernel(x_ref, att_ref, gb_ref, cin_ref, prev_ref, cw_ref, w_ref, gf_ref, w1_ref, w2_ref, o_ref,
                    *, t_len, has_prev):
    tm = x_ref.shape[0]
    prev = prev_ref[...]
    if not has_prev:
        start = (pl.program_id(0) * tm) % t_len == 0
        prev = jnp.where(start, 0.0, prev)
    cw = cw_ref[...]
    seg = t_len if has_prev else tm
    convs = []
    for s in range(tm // seg):
        cin = cin_ref[s * seg:(s + 1) * seg, :]
        ext = jnp.concatenate([prev[8 * s:8 * (s + 1)], cin], axis=0)
        conv = cw[CONV_W - 1:CONV_W] * cin
        for j in range(1, CONV_W):
            conv = conv + cw[CONV_W - 1 - j:CONV_W - j] * pltpu.roll(ext, j, 0)[8:]
        convs.append(conv)
    gated = gb_ref[...] * (convs[0] if len(convs) == 1 else jnp.concatenate(convs, axis=0))
    na = att_ref.shape[1]
    y = _dot(att_ref[...].astype(BF16), w_ref[:na, :]) + _dot(gated.astype(BF16), w_ref[na:, :])
    _ffn_tail(x_ref[...] + y, gf_ref, w1_ref, w2_ref, o_ref)


def _odd_out_ffn(x, att, gb, cin, prev, cw, w, gf, w1, w2, tm, t_len):
    n_rows, d = x.shape
    n = att.shape[1]
    row = lambda i: (i, 0)
    has_prev = prev is not None
    if has_prev:
        assert tm % t_len == 0 and t_len % 8 == 0
        prev_arr, prev_map, prev_rows = prev, row, 8 * (tm // t_len)
    else:
        assert t_len % tm == 0 and tm % 8 == 0
        prev_arr, prev_map, prev_rows = cin, (lambda i: (jnp.maximum(i * (tm // 8) - 1, 0), 0)), 8
    kern = functools.partial(_odd_out_kernel, t_len=t_len, has_prev=has_prev)
    return pl.pallas_call(
        kern,
        grid=(n_rows // tm,),
        in_specs=[pl.BlockSpec((tm, d), row), pl.BlockSpec((tm, n), row), pl.BlockSpec((tm, n), row),
                  pl.BlockSpec((tm, n), row), pl.BlockSpec((prev_rows, n), prev_map),
                  _resident(cw), _resident(w), _resident(gf), _resident(w1), _resident(w2)],
        out_specs=pl.BlockSpec((tm, d), row),
        out_shape=jax.ShapeDtypeStruct((n_rows, d), F32),
        compiler_params=_cparams(("parallel",)),
        name="odd_out_ffn",
    )(x, att, gb, cin, prev_arr, cw, _operand(w), gf, _operand(w1), _operand(w2))


def _round_up(n, m):
    return (n + m - 1) // m * m


def _row_tile(n_rows, cap):
    t = min(cap, n_rows)
    while n_rows % t:
        t //= 2
    return t


def _odd_weight_layout(n_q):
    n_heads = n_q // D_HEAD
    rep = n_heads // 2
    perm = np.zeros((n_q,), np.int32)
    for r in range(rep):
        for g in range(2):
            for dd in range(D_HEAD):
                perm[r * LANES + g * D_HEAD + dd] = (g * rep + r) * D_HEAD + dd
    return perm


def kernel(x_prompt, x_sample, cache_a_k, cache_a_v, cache_c_k, cache_c_v, cache_c_kidx, state_d_conv,
           norm_mix_g, norm_ffn_g, w_in_even, gq_a, gk_a, g_b, ws_b, bs_b, w_out_even,
           w_in_odd, gq_c, gk_c, conv_w_d, w_out_odd, w_ffn1, w_ffn2):
    bp, tp, d = x_prompt.shape
    bs, ts, _ = x_sample.shape
    depth = norm_mix_g.shape[0]
    past = cache_a_k.shape[2]
    n_a = cache_a_k.shape[3] * cache_a_k.shape[4]
    n_kv = cache_c_k.shape[3] * cache_c_k.shape[4]
    n_b = g_b.shape[1]
    n_d = conv_w_d.shape[2]
    n_q = w_out_odd.shape[1] - n_d
    assert n_a == 512 and n_b == 512 and n_kv == 128 and n_q == 512 and n_d == 512 and d == 1024

    xp = x_prompt.reshape(bp * tp, d)
    xs = x_sample.reshape(bs * ts, d)
    tm_p = _row_tile(bp * tp, 512)
    tm_s = _row_tile(bs * ts, 512)
    ls = past + ts
    ls_pad = _round_up(ls, KEY_TILE)

    hmean = jnp.asarray(np.kron(np.eye(n_a // D_HEAD), np.full((D_HEAD, D_HEAD), 1.0 / D_HEAD)), BF16)
    perm = _odd_weight_layout(n_q)
    tile8 = lambda gvec: jnp.tile(gvec, n_a // D_HEAD).reshape(1, n_a)

    ak_p, av_p, ak_s, av_s, bv_s = [], [], [], [], []
    ck_p, cv_p, ci_p, ck_s, cv_s, ci_s, dc_p, dc_s = [], [], [], [], [], [], [], []

    tri = _stick_tri()

    def pad_keys(past_rows, new_rows, multiple=KEY_TILE):
        width = new_rows.shape[-1]
        zeros = jnp.zeros((bs, _round_up(ls, multiple) - ls, width), F32)
        return jnp.concatenate([past_rows.reshape(bs, past, width), new_rows.reshape(bs, ts, width), zeros], axis=1)

    w1_all, w2_all = w_ffn1.astype(BF16), w_ffn2.astype(BF16)
    w_in_even_all, w_out_even_all = w_in_even.astype(BF16), w_out_even.astype(BF16)
    offs = np.cumsum([0, n_q, n_kv, n_kv, H_IDX * D_IDX, D_IDX, H_IDX, n_d, n_d, n_d])
    col = lambda s: w_in_odd[:, :, int(offs[s]):int(offs[s + 1])]
    w_in_odd_all = jnp.concatenate(
        [col(0)[:, :, perm], col(1), col(2), col(3), col(4), col(4), col(5),
         jnp.zeros((w_in_odd.shape[0], d, LANES - H_IDX), F32), col(6), col(7), col(8)], axis=2).astype(BF16)
    assert w_in_odd_all.shape[2] == ODD_COLS
    w_out_odd_all = jnp.concatenate([w_out_odd[:, :n_q][:, perm], w_out_odd[:, n_q:]], axis=1).astype(BF16)

    for i in range(depth):
        j = i // 2
        g_mix = norm_mix_g[i].reshape(1, d)
        ffn = (norm_ffn_g[i].reshape(1, d), (w1_all, i), (w2_all, i))
        if i % 2 == 0:
            w_in = (w_in_even_all, j)
            w_out = (w_out_even_all, j)
            gq, gk, gb = tile8(gq_a[j]), tile8(gk_a[j]), g_b[j].reshape(1, n_b)
            q, k, v, u, vb, kb, vm = _even_in(xp, g_mix, w_in, hmean, gq, gk, gb, tm_p)
            att = _stick(q.reshape(bp, tp, n_a), kb.reshape(bp, tp, n_a), vm.reshape(bp, tp, 2 * n_a), tri,
                         min(tp, KEY_TILE), 0)
            xp = _even_out_ffn(xp, att.reshape(bp * tp, n_a), u, vb, ws_b[j], bs_b[j].T, w_out, *ffn, tm_p, B_CHUNK)
            ak_p.append(k.reshape(bp, tp, n_a // D_HEAD, D_HEAD))
            av_p.append(v.reshape(bp, tp, n_a // D_HEAD, D_HEAD))
            q, k, v, u, vb, _, _ = _even_in(xs, g_mix, w_in, hmean, gq, gk, gb, tm_s)
            att = _stick(q.reshape(bs, ts, n_a), pad_keys(cache_a_k[j], k, STICK_TK),
                         pad_keys(cache_a_v[j], v, STICK_TK), tri, ts, past)
            xs = _even_out_ffn(xs, att.reshape(bs * ts, n_a), u, vb, ws_b[j][:, :ts, :ts], bs_b[j][:, :ts].T,
                               w_out, *ffn, tm_s, ts)
            ak_s.append(k.reshape(bs, ts, n_a // D_HEAD, D_HEAD))
            av_s.append(v.reshape(bs, ts, n_a // D_HEAD, D_HEAD))
            bv_s.append(vb.reshape(bs, ts, n_b))
        else:
            w_in = (w_in_odd_all, j)
            w_out = (w_out_odd_all, j)
            gq, gk = tile8(gq_c[j]), jnp.tile(gk_c[j], n_kv // D_HEAD).reshape(1, n_kv)
            cw = conv_w_d[j]
            q, k, v, qi, ki, wi, gbt, cin = _odd_in(xp, g_mix, w_in, hmean, gq, gk, tm_p)
            r3 = lambda a: a.reshape(bp, tp, a.shape[-1])
            att = _dsa(r3(q), r3(qi), r3(wi), r3(k), r3(v), r3(ki), LANES, 0, tp)
            xp = _odd_out_ffn(xp, att.reshape(bp * tp, n_q), gbt, cin, None, cw, w_out, *ffn, tm_p, tp)
            ck_p.append(k.reshape(bp, tp, n_kv // D_HEAD, D_HEAD))
            cv_p.append(v.reshape(bp, tp, n_kv // D_HEAD, D_HEAD))
            ci_p.append(ki[:, :D_IDX].reshape(bp, tp, D_IDX))
            dc_p.append(cin.reshape(bp, tp, n_d)[:, tp - (CONV_W - 1):])
            q, k, v, qi, ki, wi, gbt, cin = _odd_in(xs, g_mix, w_in, hmean, gq, gk, tm_s)
            r3 = lambda a: a.reshape(bs, ts, a.shape[-1])
            past_ki = jnp.concatenate([cache_c_kidx[j], cache_c_kidx[j]], axis=-1)
            pad_q = lambda a: jnp.pad(r3(a), ((0, 0), (0, LANES - ts), (0, 0)))
            att = _dsa(pad_q(q), pad_q(qi), pad_q(wi), pad_keys(cache_c_k[j], k, DSA_CK),
                       pad_keys(cache_c_v[j], v, DSA_CK), pad_keys(past_ki, ki, DSA_CK), LANES, past, ls)[:, :ts]
            prev = jnp.concatenate([jnp.zeros((bs, 8 - (CONV_W - 1), n_d), F32), state_d_conv[j]], axis=1)
            xs = _odd_out_ffn(xs, att.reshape(bs * ts, n_q), gbt, cin, prev.reshape(bs * 8, n_d), cw, w_out, *ffn,
                              tm_s, ts)
            ck_s.append(k.reshape(bs, ts, n_kv // D_HEAD, D_HEAD))
            cv_s.append(v.reshape(bs, ts, n_kv // D_HEAD, D_HEAD))
            ci_s.append(ki[:, :D_IDX].reshape(bs, ts, D_IDX))
            dc_s.append(cin.reshape(bs, ts, n_d)[:, ts - (CONV_W - 1):])

    return (xp.reshape(bp, tp, d), xs.reshape(bs, ts, d),
            jnp.stack(ak_p), jnp.stack(av_p), jnp.stack(ak_s), jnp.stack(av_s),
            jnp.stack(bv_s),
            jnp.stack(ck_p), jnp.stack(cv_p), jnp.stack(ci_p),
            jnp.stack(ck_s), jnp.stack(cv_s), jnp.stack(ci_s),
            jnp.stack(dc_p), jnp.stack(dc_s))
```

```python
import functools

import numpy as np
import jax
import jax.numpy as jnp
from jax import lax
from jax.experimental import pallas as pl
from jax.experimental.pallas import tpu as pltpu

F32 = jnp.float32
BF16 = jnp.bfloat16

EPS = 1e-6
NEG = -1e30
LOG2_E = 1.4426950408889634
CHUNK = 64
CHUNK_SHIFT = 6
D_HEAD = 64
G_B = 4
B_CHUNK = 128
H_IDX = 4
D_IDX = 64
TOPK_MAX = 256
CONV_W = 3
LANES = 128
KEY_TILE = 128
VMEM_LIMIT = 56 * 1024 * 1024


def _cparams(sem):
    return pltpu.CompilerParams(dimension_semantics=sem, vmem_limit_bytes=VMEM_LIMIT)


def _resident(a):
    if isinstance(a, tuple):
        arr, layer = a
        shape = arr.shape[1:]
        return pl.BlockSpec((None,) + shape, lambda *_: (layer,) + (0,) * len(shape),
                            pipeline_mode=pl.Buffered(1))
    return pl.BlockSpec(a.shape, lambda *_: (0,) * a.ndim, pipeline_mode=pl.Buffered(1))


def _operand(a):
    return a[0] if isinstance(a, tuple) else a


def _dot(a, b):
    return jnp.dot(a, b, preferred_element_type=F32)


def _dot_t(a, b):
    return lax.dot_general(a, b, (((1,), (1,)), ((), ())), preferred_element_type=F32)


def _split(x):
    hi = x.astype(BF16)
    lo = (x - hi.astype(F32)).astype(BF16)
    return hi, lo


def _rms(x, g):
    return x * lax.rsqrt(jnp.mean(x * x, axis=-1, keepdims=True) + EPS) * g


def _head_rms(x, hmean, g):
    ms = _dot((x * x).astype(BF16), hmean)
    return x * lax.rsqrt(ms + EPS) * g


def _gelu(x):
    return jax.nn.gelu(x)


def _even_in_kernel(x_ref, g_ref, w_ref, hm_ref, gq_ref, gk_ref, gb_ref,
                    q_ref, k_ref, v_ref, u_ref, vb_ref, kb_ref, vm_ref):
    h = _rms(x_ref[...], g_ref[...]).astype(BF16)
    n = q_ref.shape[-1]

    def seg(i):
        return _dot(h, w_ref[:, i * n:(i + 1) * n])

    hm = hm_ref[...]
    q_ref[...] = _head_rms(seg(0), hm, gq_ref[...])
    k = _head_rms(seg(1), hm, gk_ref[...])
    k_ref[...] = k
    kb_ref[...] = k.astype(BF16)
    v = seg(2)
    v_ref[...] = v
    first = lax.broadcasted_iota(jnp.int32, (1, LANES), 1) < D_HEAD
    for p in range(n // LANES):
        vp = v[:, p * LANES:(p + 1) * LANES]
        vm_ref[:, 2 * p * LANES:(2 * p + 1) * LANES] = jnp.where(first, vp, 0.0).astype(BF16)
        vm_ref[:, (2 * p + 1) * LANES:(2 * p + 2) * LANES] = jnp.where(first, 0.0, vp).astype(BF16)
    u_ref[...] = _gelu(seg(3))
    vb_ref[...] = _rms(_gelu(seg(4)), gb_ref[...])


def _even_in(x, g, w, hm, gq, gk, gb, tm):
    n_rows, d = x.shape
    n = _operand(w).shape[-1] // 5
    row = lambda i: (i, 0)
    out = jax.ShapeDtypeStruct((n_rows, n), F32)
    return pl.pallas_call(
        _even_in_kernel,
        grid=(n_rows // tm,),
        in_specs=[pl.BlockSpec((tm, d), row), _resident(g), _resident(w), _resident(hm),
                  _resident(gq), _resident(gk), _resident(gb)],
        out_specs=[pl.BlockSpec((tm, n), row)] * 6 + [pl.BlockSpec((tm, 2 * n), row)],
        out_shape=[out] * 5 + [jax.ShapeDtypeStruct((n_rows, n), BF16), jax.ShapeDtypeStruct((n_rows, 2 * n), BF16)],
        compiler_params=_cparams(("parallel",)),
        name="even_in",
    )(x, g, _operand(w), hm, gq, gk, gb)


ODD_Q, ODD_K, ODD_V, ODD_QI, ODD_KI, ODD_WI, ODD_GB, ODD_GC, ODD_HD = (
    0, 512, 640, 768, 1024, 1152, 1280, 1792, 2304)
ODD_COLS = 2816


def _odd_in_kernel(x_ref, g_ref, w_ref, hm_ref, gq_ref, gk_ref,
                   q_ref, k_ref, v_ref, qi_ref, ki_ref, wi_ref, gb_ref, cin_ref):
    h = _rms(x_ref[...], g_ref[...]).astype(BF16)

    def seg(lo, width):
        return _dot(h, w_ref[:, lo:lo + width])

    hm = hm_ref[...]
    q_ref[...] = _head_rms(seg(ODD_Q, 512), hm, gq_ref[...])
    k_ref[...] = _head_rms(seg(ODD_K, 128), hm[:128, :128], gk_ref[...])
    v_ref[...] = seg(ODD_V, 128)
    qi_ref[...] = seg(ODD_QI, 256)
    ki_ref[...] = seg(ODD_KI, 128)
    wi_ref[...] = seg(ODD_WI, 128)
    gb_ref[...] = seg(ODD_GB, 512)
    cin_ref[...] = seg(ODD_GC, 512) * seg(ODD_HD, 512)


def _odd_in(x, g, w, hm, gq, gk, tm):
    n_rows, d = x.shape
    row = lambda i: (i, 0)
    widths = (512, 128, 128, 256, 128, 128, 512, 512)
    return pl.pallas_call(
        _odd_in_kernel,
        grid=(n_rows // tm,),
        in_specs=[pl.BlockSpec((tm, d), row), _resident(g), _resident(w), _resident(hm),
                  _resident(gq), _resident(gk)],
        out_specs=[pl.BlockSpec((tm, n), row) for n in widths],
        out_shape=[jax.ShapeDtypeStruct((n_rows, n), F32) for n in widths],
        compiler_params=_cparams(("parallel",)),
        name="odd_in",
    )(x, g, _operand(w), hm, gq, gk)


STICK_TK = 256
F32_EXP2_ZERO = -150.0


def _stick_tri():
    half = STICK_TK // 2
    later = (np.arange(half)[:, None] > np.arange(half)[None, :]).astype(np.float32)
    blk = np.concatenate([later, np.ones((half, half), np.float32)], axis=1)
    return jnp.asarray(np.concatenate([blk, blk], axis=0), BF16)


def _stick_kernel(q_ref, k_ref, v_ref, tri_ref, o_ref, carry_sc, acc_sc, *, tq, q_pos0):
    tk = STICK_TK
    half = tk // 2
    n_pairs = q_ref.shape[1] // LANES
    v_split = v_ref.shape[1] == 2 * q_ref.shape[1]
    q_start = q_pos0 + pl.program_id(1) * tq
    n_full = q_start // tk
    lane = lax.broadcasted_iota(jnp.int32, (1, LANES), 1)
    first = lane < D_HEAD
    q = q_ref[...] * (D_HEAD ** -0.5 * LOG2_E)
    q2 = []
    for p in range(n_pairs):
        qp = q[:, p * LANES:(p + 1) * LANES]
        q2.append(jnp.concatenate([jnp.where(first, qp, 0.0), jnp.where(first, 0.0, qp)], axis=0).astype(BF16))
    tri = tri_ref[...]
    carry_sc[...] = jnp.zeros_like(carry_sc)
    acc_sc[...] = jnp.zeros_like(acc_sc)

    def block(kb, masked):
        ks = pl.multiple_of(kb * tk, tk)
        if masked:
            row = lax.broadcasted_iota(jnp.int32, (2 * tq, tk), 0)
            qpos = q_start + jnp.where(row >= tq, row - tq, row)
            kpos = ks + lax.broadcasted_iota(jnp.int32, (2 * tq, tk), 1)
            ok = kpos < qpos
        for p in range(n_pairs):
            cols = slice(p * LANES, (p + 1) * LANES)
            z = _dot_t(q2[p], k_ref[pl.ds(ks, tk), cols].astype(BF16))
            nz = -z
            stay = jnp.minimum(nz, 0.0) - jnp.log2(1.0 + jnp.exp2(jnp.minimum(z, nz)))
            if masked:
                stay = jnp.where(ok, stay, 0.0)
            carry = carry_sc[p]
            after = [None, None]
            for h in (1, 0):
                hi, lo = _split(stay[:, h * half:(h + 1) * half])
                r = _dot(jnp.concatenate([hi, lo], axis=1), tri)
                after[h] = r[:, :half] + carry
                carry = carry + r[:, half:]
            carry_sc[p] = carry
            w = jnp.exp2(z + stay + jnp.concatenate(after, axis=1))
            if masked:
                w = jnp.where(ok, w, 0.0)
            w = w.astype(BF16)
            if v_split:
                v0 = v_ref[pl.ds(ks, tk), 2 * p * LANES:(2 * p + 1) * LANES]
                v1 = v_ref[pl.ds(ks, tk), (2 * p + 1) * LANES:(2 * p + 2) * LANES]
            else:
                vblk = v_ref[pl.ds(ks, tk), cols]
                v0 = jnp.where(first, vblk, 0.0).astype(BF16)
                v1 = jnp.where(first, 0.0, vblk).astype(BF16)
            acc_sc[p] += _dot(jnp.concatenate([w[:tq], w[tq:]], axis=1), jnp.concatenate([v0, v1], axis=0))

    block(n_full, True)

    def any_weight_left():
        m = jnp.max(jnp.max(carry_sc[...], axis=0), axis=0, keepdims=True)
        return jnp.max(m, axis=1, keepdims=True)[0, 0] > F32_EXP2_ZERO

    def body(s):
        block(s[0], False)
        return s[0] - 1, any_weight_left()

    lax.while_loop(lambda s: (s[0] >= 0) & s[1], body, (n_full - 1, any_weight_left()))
    for p in range(n_pairs):
        o_ref[:, p * LANES:(p + 1) * LANES] = acc_sc[p]


def _stick(q, k, v, tri, tq, q_pos0):
    b, t, c = q.shape
    lp = k.shape[1]
    tk = STICK_TK
    assert lp % tk == 0 and q_pos0 % tq == 0 and tk % tq == 0 and q_pos0 + t <= lp
    kern = functools.partial(_stick_kernel, tq=tq, q_pos0=q_pos0)
    kmap = lambda bi, i: (bi, 0, 0)
    return pl.pallas_call(
        kern,
        grid=(b, t // tq),
        in_specs=[pl.BlockSpec((None, tq, c), lambda bi, i: (bi, i, 0)),
                  pl.BlockSpec((None, lp, c), kmap),
                  pl.BlockSpec((None, lp, v.shape[2]), kmap),
                  pl.BlockSpec(tri.shape, lambda bi, i: (0, 0))],
        out_specs=pl.BlockSpec((None, tq, c), lambda bi, i: (bi, i, 0)),
        out_shape=jax.ShapeDtypeStruct((b, t, c), F32),
        scratch_shapes=[pltpu.VMEM((c // LANES, 2 * tq, LANES), F32), pltpu.VMEM((c // LANES, tq, LANES), F32)],
        compiler_params=_cparams(("parallel", "arbitrary")),
        name="stick_attn",
    )(q, k, v, tri)


DSA_LEVEL = 512


def _dsa_rows_kernel(q_ref, qi_ref, wi_ref, k_ref, v_ref, ki_ref, o_ref, key_sc, bias_sc,
                     *, tq, l_valid, q_pos0, top_k, levels):
    kt = KEY_TILE
    th = min(tq, 128)
    parts = [slice(i * th, (i + 1) * th) for i in range(tq // th)]
    n_rep = q_ref.shape[1] // LANES
    lane = lax.broadcasted_iota(jnp.int32, (1, LANES), 1)
    first = lane < D_HEAD
    q_start = q_pos0 + pl.program_id(1) * tq
    kf = jnp.float32(top_k)

    def body(lv):
        n_kt = lv // kt

        def admissible(rows, lo, width):
            qpos = q_start + rows.start + lax.broadcasted_iota(jnp.int32, (th, 1), 0)
            kpos = lo + lax.broadcasted_iota(jnp.int32, (1, width), 1)
            return (jnp.right_shift(kpos, CHUNK_SHIFT) <= jnp.right_shift(qpos, CHUNK_SHIFT)) & (kpos < l_valid)

        ki_hi, ki_lo = _split(ki_ref[:lv, :])
        for rows in parts:
            qi = qi_ref[rows, :]
            wi = wi_ref[rows, :]
            qm = []
            for h in range(H_IDX):
                grp = qi[:, (h // 2) * LANES:(h // 2 + 1) * LANES]
                qm.append(jnp.where(first, grp, 0.0) if h % 2 == 0 else jnp.where(first, 0.0, grp))
            q_hi, q_lo = _split(jnp.concatenate(qm, axis=0))
            s = _dot_t(q_hi, ki_hi) + (_dot_t(q_hi, ki_lo) + _dot_t(q_lo, ki_hi))
            score = jnp.zeros((th, lv), F32)
            for h in range(H_IDX):
                score = score + jnp.maximum(s[h * th:(h + 1) * th], 0.0) * wi[:, h:h + 1]
            score = score * ((D_IDX ** -0.5) * (H_IDX ** -0.5)) + 0.0
            score = jnp.where(admissible(rows, 0, lv), score, NEG)
            bits = lax.bitcast_convert_type(score, jnp.int32)
            key_sc[rows, :lv] = jnp.where(bits < 0, bits ^ jnp.int32(0x7FFFFFFF), bits)

        def count_ge(rows, trial):
            part = jnp.zeros((th, kt), F32)
            for c in range(n_kt):
                part = part + jnp.where(key_sc[rows, c * kt:(c + 1) * kt] >= trial, 1.0, 0.0)
            return jnp.sum(part, axis=1, keepdims=True)

        int_min = jnp.int32(-2 ** 31)
        zero = jnp.zeros((th, 1), jnp.int32)
        thr0 = tuple(jnp.where(count_ge(rows, zero) >= kf, jnp.int32(0), int_min) for rows in parts)

        def bit_step(i, thr):
            bit = jnp.left_shift(jnp.int32(1), jnp.int32(30) - i)
            return tuple(jnp.where(count_ge(rows, t + bit) >= kf, t + bit, t) for rows, t in zip(parts, thr))

        thrs = lax.fori_loop(0, 31, bit_step, thr0)

        rr = lax.broadcasted_iota(jnp.int32, (kt, kt), 0)
        cc = lax.broadcasted_iota(jnp.int32, (kt, kt), 1)
        before = jnp.where(rr < cc, 1.0, 0.0).astype(BF16)
        for rows, thr in zip(parts, thrs):
            n_gt = jnp.zeros((th, kt), F32)
            for c in range(n_kt):
                n_gt = n_gt + jnp.where(key_sc[rows, c * kt:(c + 1) * kt] > thr, 1.0, 0.0)
            room = kf - jnp.sum(n_gt, axis=1, keepdims=True)
            run = jnp.zeros((th, 1), F32)
            for c in range(n_kt):
                keyc = key_sc[rows, c * kt:(c + 1) * kt]
                eqf = jnp.where(keyc == thr, 1.0, 0.0)
                rank = _dot(eqf.astype(BF16), before) + run
                tie_ok = jnp.where(keyc == thr, jnp.where(rank < room, 0.0, NEG), NEG)
                sel = jnp.where(keyc > thr, 0.0, tie_ok)
                bias_sc[rows, c * kt:(c + 1) * kt] = jnp.where(admissible(rows, c * kt, kt), sel, NEG)
                run = run + jnp.sum(eqf, axis=1, keepdims=True)

        kb = k_ref[:lv, :].astype(BF16)
        vf = v_ref[:lv, :]
        vg = [jnp.where(first, vf, 0.0).astype(BF16), jnp.where(first, 0.0, vf).astype(BF16)]
        for rows in parts:
            q = q_ref[rows, :] * (D_HEAD ** -0.5)
            bias = bias_sc[rows, :lv]
            out = jnp.zeros((n_rep * th, LANES), F32)
            for g in range(2):
                keep = first if g == 0 else jnp.logical_not(first)
                qs = jnp.concatenate(
                    [jnp.where(keep, q[:, r * LANES:(r + 1) * LANES], 0.0) for r in range(n_rep)],
                    axis=0).astype(BF16)
                logits = _dot_t(qs, kb)
                logits = (logits.reshape(n_rep, th, lv) + bias[None]).reshape(n_rep * th, lv)
                m = jnp.max(logits, axis=1, keepdims=True)
                p = jnp.exp(logits - m)
                den = jnp.sum(p, axis=1, keepdims=True)
                out = out + _dot(p.astype(BF16), vg[g]) / den
            for r in range(n_rep):
                o_ref[rows, r * LANES:(r + 1) * LANES] = out[r * th:(r + 1) * th]

    if len(levels) == 1:
        body(levels[0])
    else:
        cls = (q_start + tq - 1) // DSA_LEVEL
        for c, lv in enumerate(levels):
            pl.when(cls == c)(functools.partial(body, lv))


DSA_CK = 256


def _dsa_kernel(*refs, tq, l_valid, q_pos0, top_k, levels):
    if len(levels) == 1:
        _dsa_tile(*refs, tq=tq, l_valid=l_valid, q_pos0=q_pos0, top_k=top_k, n_chunks=levels[0])
        return
    q_last = q_pos0 + pl.program_id(1) * tq + tq - 1
    vis_end = jnp.left_shift(jnp.right_shift(q_last, CHUNK_SHIFT) + 1, CHUNK_SHIFT)
    step = levels[0] * DSA_CK
    for i, n in enumerate(levels):
        pl.when((vis_end + step - 1) // step == i + 1)(functools.partial(
            _dsa_tile, *refs, tq=tq, l_valid=l_valid, q_pos0=q_pos0, top_k=top_k, n_chunks=n))


def _dsa_tile(q_ref, qi_ref, wi_ref, k_ref, v_ref, ki_ref, o_ref,
              kic_sc, kb_sc, vt_sc, key_sc, bias_sc, *, tq, l_valid, q_pos0, top_k, n_chunks):
    ck = DSA_CK
    lp = k_ref.shape[0]
    n_rep = q_ref.shape[1] // LANES
    lane = lax.broadcasted_iota(jnp.int32, (1, LANES), 1)
    first = lane < D_HEAD
    q_start = q_pos0 + pl.program_id(1) * tq
    qchunk = jnp.right_shift(q_start + lax.broadcasted_iota(jnp.int32, (1, tq), 1), CHUNK_SHIFT)
    kf = jnp.float32(top_k)
    top_rows = lax.broadcasted_iota(jnp.int32, (LANES, 1), 0) < D_HEAD

    def rows_of(c):
        return pl.ds(c * ck, ck)

    def admissible(c):
        kpos = c * ck + lax.broadcasted_iota(jnp.int32, (ck, 1), 0)
        return (jnp.right_shift(kpos, CHUNK_SHIFT) <= qchunk) & (kpos < l_valid)

    def chunks(n, step, carry):
        for c in range(n):
            carry = step(c, carry)
        return carry

    @pl.when(pl.program_id(1) == 0)
    def _():
        def prep(c, _):
            rows = rows_of(c)
            hi, lo = _split(ki_ref[rows, :])
            kic_sc[rows, :LANES] = jnp.where(first, hi, lo)
            kic_sc[rows, LANES:] = jnp.where(first, hi, jnp.zeros_like(hi))
            kb_sc[rows, :] = k_ref[rows, :].astype(BF16)
            vt = v_ref[rows, :].T
            vt_sc[c, :, :ck] = jnp.where(top_rows, vt, 0.0).astype(BF16)
            vt_sc[c, :, ck:] = jnp.where(top_rows, 0.0, vt).astype(BF16)
            return 0
        chunks(lp // ck, prep, 0)

    qi = qi_ref[...]
    blocks = []
    for h in range(H_IDX):
        grp = qi[:, (h // 2) * LANES:(h // 2 + 1) * LANES]
        swapped = pltpu.roll(grp, D_IDX, 1)
        twice = jnp.where(first, grp, swapped) if h % 2 == 0 else jnp.where(first, swapped, grp)
        x3 = jnp.concatenate([twice, jnp.where(first, twice, 0.0)], axis=1)
        hi = x3.astype(BF16).astype(F32)
        blocks.append(jnp.concatenate([hi[:, :LANES], (x3 - hi)[:, LANES:]], axis=1))
    qic_t = jnp.concatenate(blocks, axis=0).T.astype(BF16)
    wit = wi_ref[...].T

    def score_chunk(c, _):
        s = _dot(kic_sc[rows_of(c), :], qic_t)
        score = jnp.zeros((ck, tq), F32)
        for h in range(H_IDX):
            score = score + jnp.maximum(s[:, h * tq:(h + 1) * tq], 0.0) * wit[h:h + 1, :]
        score = score * ((D_IDX ** -0.5) * (H_IDX ** -0.5)) + 0.0
        score = jnp.where(admissible(c), score, NEG)
        bits = lax.bitcast_convert_type(score, jnp.int32)
        key_sc[rows_of(c), :] = jnp.where(bits < 0, bits ^ jnp.int32(0x7FFFFFFF), bits)
        return 0

    chunks(n_chunks, score_chunk, 0)

    def count(pred, trial):
        def body(c, acc):
            hit = jnp.where(pred(key_sc[rows_of(c), :], trial), 1.0, 0.0)
            return acc + jnp.sum(hit.reshape(ck // 8, 8, tq), axis=0)
        acc = chunks(n_chunks, body, jnp.zeros((8, tq), F32))
        return jnp.sum(acc, axis=0, keepdims=True)

    ge = lambda a, b: a >= b
    thr = jnp.where(count(ge, jnp.zeros((1, tq), jnp.int32)) >= kf, jnp.int32(0), jnp.int32(-2 ** 31))

    def bit_step(i, t):
        trial = t + jnp.left_shift(jnp.int32(1), jnp.int32(30) - i)
        return jnp.where(count(ge, trial) >= kf, trial, t)

    thr = lax.fori_loop(0, 31, bit_step, thr)

    room = kf - count(lambda a, b: a > b, thr)
    rr = lax.broadcasted_iota(jnp.int32, (ck, ck), 0)
    cc = lax.broadcasted_iota(jnp.int32, (ck, ck), 1)
    earlier = jnp.where(cc < rr, 1.0, 0.0).astype(BF16)

    def tie_chunk(c, run):
        keyc = key_sc[rows_of(c), :]
        eqf = jnp.where(keyc == thr, 1.0, 0.0)
        rank = _dot(earlier, eqf.astype(BF16)) + run
        tie_ok = jnp.where(keyc == thr, jnp.where(rank < room, 0.0, NEG), NEG)
        sel = jnp.where(keyc > thr, 0.0, tie_ok)
        bias_sc[rows_of(c), :] = jnp.where(admissible(c), sel, NEG)
        return run + jnp.sum(eqf, axis=0, keepdims=True)

    chunks(n_chunks, tie_chunk, jnp.zeros((1, tq), F32))

    q = q_ref[...] * (D_HEAD ** -0.5 * LOG2_E)
    qs_t = []
    for g in range(2):
        keep = first if g == 0 else jnp.logical_not(first)
        qs_t.append(jnp.concatenate(
            [jnp.where(keep, q[:, r * LANES:(r + 1) * LANES], 0.0) for r in range(n_rep)], axis=0).T.astype(BF16))
    width = n_rep * tq

    def att_chunk(c, carry):
        m0, m1, d0, d1, acc = carry
        kb = kb_sc[rows_of(c), :]
        bias = bias_sc[rows_of(c), :]
        bias = jnp.concatenate([bias] * n_rep, axis=1)
        l0 = _dot(kb, qs_t[0]) + bias
        l1 = _dot(kb, qs_t[1]) + bias
        n0 = jnp.maximum(m0, jnp.max(l0, axis=0, keepdims=True))
        n1 = jnp.maximum(m1, jnp.max(l1, axis=0, keepdims=True))
        a0, a1 = jnp.exp2(m0 - n0), jnp.exp2(m1 - n1)
        p0, p1 = jnp.exp2(l0 - n0), jnp.exp2(l1 - n1)
        d0 = d0 * a0 + jnp.sum(p0, axis=0, keepdims=True)
        d1 = d1 * a1 + jnp.sum(p1, axis=0, keepdims=True)
        pv = _dot(vt_sc[c], jnp.concatenate([p0.astype(BF16), p1.astype(BF16)], axis=0))
        acc = acc * jnp.where(top_rows, a0, a1) + pv
        return n0, n1, d0, d1, acc

    neg = jnp.full((1, width), NEG, F32)
    zero = jnp.zeros((1, width), F32)
    _, _, d0, d1, acc = chunks(n_chunks, att_chunk, (neg, neg, zero, zero, jnp.zeros((LANES, width), F32)))
    out_t = acc / jnp.where(top_rows, d0, d1)
    for r in range(n_rep):
        o_ref[:, r * LANES:(r + 1) * LANES] = out_t[:, r * tq:(r + 1) * tq].T


def _dsa(q, qi, wi, k, v, ki, tq, q_pos0, l_valid):
    b, t, c = q.shape
    lp = k.shape[1]
    assert lp % DSA_CK == 0 and l_valid <= lp and t % tq == 0 and tq % LANES == 0
    top_k = min(TOPK_MAX, l_valid // 4)
    assert DSA_CK >= top_k
    n_all = lp // DSA_CK
    if q_pos0 == 0 and n_all % 4 == 0:
        levels = tuple(range(n_all // 4, n_all + 1, n_all // 4))
    else:
        levels = (n_all,)
    kern = functools.partial(_dsa_kernel, tq=tq, l_valid=l_valid, q_pos0=q_pos0, top_k=top_k, levels=levels)
    qmap = lambda bi, i: (bi, i, 0)
    kmap = lambda bi, i: (bi, 0, 0)
    return pl.pallas_call(
        kern,
        grid=(b, t // tq),
        in_specs=[pl.BlockSpec((None, tq, c), qmap), pl.BlockSpec((None, tq, qi.shape[2]), qmap),
                  pl.BlockSpec((None, tq, LANES), qmap),
                  pl.BlockSpec((None, lp, LANES), kmap), pl.BlockSpec((None, lp, LANES), kmap),
                  pl.BlockSpec((None, lp, LANES), kmap)],
        out_specs=pl.BlockSpec((None, tq, c), qmap),
        out_shape=jax.ShapeDtypeStruct((b, t, c), F32),
        scratch_shapes=[pltpu.VMEM((lp, 2 * LANES), BF16), pltpu.VMEM((lp, LANES), BF16),
                        pltpu.VMEM((lp // DSA_CK, LANES, 2 * DSA_CK), BF16), pltpu.VMEM((lp, tq), jnp.int32),
                        pltpu.VMEM((lp, tq), F32)],
        compiler_params=_cparams(("parallel", "arbitrary")),
        name="dsa_attn",
    )(q, qi, wi, k, v, ki)


def _dsa_rows(q, qi, wi, k, v, ki, tq, q_pos0, l_valid):
    b, t, c = q.shape
    lp = k.shape[1]
    assert lp % KEY_TILE == 0 and l_valid <= lp and t % tq == 0
    top_k = min(TOPK_MAX, l_valid // 4)
    if q_pos0 == 0 and lp % DSA_LEVEL == 0 and tq % CHUNK == 0 and DSA_LEVEL >= top_k:
        levels = tuple(range(DSA_LEVEL, lp + 1, DSA_LEVEL))
    else:
        levels = (lp,)
    kern = functools.partial(_dsa_rows_kernel, tq=tq, l_valid=l_valid, q_pos0=q_pos0, top_k=top_k, levels=levels)
    qmap = lambda bi, i: (bi, i, 0)
    kmap = lambda bi, i: (bi, 0, 0)
    return pl.pallas_call(
        kern,
        grid=(b, t // tq),
        in_specs=[pl.BlockSpec((None, tq, c), qmap), pl.BlockSpec((None, tq, qi.shape[2]), qmap),
                  pl.BlockSpec((None, tq, LANES), qmap),
                  pl.BlockSpec((None, lp, LANES), kmap), pl.BlockSpec((None, lp, LANES), kmap),
                  pl.BlockSpec((None, lp, LANES), kmap)],
        out_specs=pl.BlockSpec((None, tq, c), qmap),
        out_shape=jax.ShapeDtypeStruct((b, t, c), F32),
        scratch_shapes=[pltpu.VMEM((tq, lp), jnp.int32), pltpu.VMEM((tq, lp), F32)],
        compiler_params=_cparams(("parallel", "arbitrary")),
        name="dsa_attn",
    )(q, qi, wi, k, v, ki)


FFN_CHUNK = 1024


def _ffn_tail(x1, gf_ref, w1_ref, w2_ref, o_ref):
    h = _rms(x1, gf_ref[...]).astype(BF16)
    acc = None
    for c in range(w1_ref.shape[1] // FFN_CHUNK):
        cols = slice(c * FFN_CHUNK, (c + 1) * FFN_CHUNK)
        a = jnp.square(jnp.maximum(_dot(h, w1_ref[:, cols]), 0.0))
        part = _dot(a.astype(BF16), w2_ref[cols, :])
        acc = part if acc is None else acc + part
    o_ref[...] = x1 + acc


def _even_out_kernel(x_ref, att_ref, u_ref, vb_ref, ws_ref, bst_ref, w_ref, gf_ref, w1_ref, w2_ref, o_ref,
                     *, p_len):
    tm = x_ref.shape[0]
    cg = vb_ref.shape[1] // G_B
    rr = lax.broadcasted_iota(jnp.int32, (p_len, p_len), 0)
    cc = lax.broadcasted_iota(jnp.int32, (p_len, p_len), 1)
    causal = jnp.right_shift(cc, CHUNK_SHIFT) <= jnp.right_shift(rr, CHUNK_SHIFT)
    u = u_ref[...]
    vb = vb_ref[...].astype(BF16)
    bst = bst_ref[...]
    mixed_rows = []
    for c in range(tm // p_len):
        cols = []
        for g in range(G_B):
            wg = jnp.where(causal, ws_ref[g], 0.0).astype(BF16)
            mg = _dot(wg, vb[c * p_len:(c + 1) * p_len, g * cg:(g + 1) * cg]) + bst[:, g:g + 1]
            cols.append(mg)
        mixed_rows.append(jnp.concatenate(cols, axis=1))
    gated = u * jnp.concatenate(mixed_rows, axis=0)
    na = att_ref.shape[1]
    y = _dot(att_ref[...].astype(BF16), w_ref[:na, :]) + _dot(gated.astype(BF16), w_ref[na:, :])
    _ffn_tail(x_ref[...] + y, gf_ref, w1_ref, w2_ref, o_ref)


def _even_out_ffn(x, att, u, vb, ws, bst, w, gf, w1, w2, tm, p_len):
    n_rows, d = x.shape
    n = att.shape[1]
    row = lambda i: (i, 0)
    kern = functools.partial(_even_out_kernel, p_len=p_len)
    return pl.pallas_call(
        kern,
        grid=(n_rows // tm,),
        in_specs=[pl.BlockSpec((tm, d), row), pl.BlockSpec((tm, n), row), pl.BlockSpec((tm, n), row),
                  pl.BlockSpec((tm, n), row), _resident(ws), _resident(bst), _resident(w),
                  _resident(gf), _resident(w1), _resident(w2)],
        out_specs=pl.BlockSpec((tm, d), row),
        out_shape=jax.ShapeDtypeStruct((n_rows, d), F32),
        compiler_params=_cparams(("parallel",)),
        name="even_out_ffn",
    )(x, att, u, vb, ws, bst, _operand(w), gf, _operand(w1), _operand(w2))


def _odd_out_kernel(x_ref, att_ref, gb_ref, cin_ref, prev_ref, cw_ref, w_ref, gf_ref, w1_ref, w2_ref, o_ref,
                    *, t_len, has_prev):
    tm = x_ref.shape[0]
    prev = prev_ref[...]
    if not has_prev:
        start = (pl.program_id(0) * tm) % t_len == 0
        prev = jnp.where(start, 0.0, prev)
    cw = cw_ref[...]
    seg = t_len if has_prev else tm
    convs = []
    for s in range(tm // seg):
        cin = cin_ref[s * seg:(s + 1) * seg, :]
        ext = jnp.concatenate([prev[8 * s:8 * (s + 1)], cin], axis=0)
        conv = cw[CONV_W - 1:CONV_W] * cin
        for j in range(1, CONV_W):
            conv = conv + cw[CONV_W - 1 - j:CONV_W - j] * pltpu.roll(ext, j, 0)[8:]
        convs.append(conv)
    gated = gb_ref[...] * (convs[0] if len(convs) == 1 else jnp.concatenate(convs, axis=0))
    na = att_ref.shape[1]
    y = _dot(att_ref[...].astype(BF16), w_ref[:na, :]) + _dot(gated.astype(BF16), w_ref[na:, :])
    _ffn_tail(x_ref[...] + y, gf_ref, w1_ref, w2_ref, o_ref)


def _odd_out_ffn(x, att, gb, cin, prev, cw, w, gf, w1, w2, tm, t_len):
    n_rows, d = x.shape
    n = att.shape[1]
    row = lambda i: (i, 0)
    has_prev = prev is not None
    if has_prev:
        assert tm % t_len == 0 and t_len % 8 == 0
        prev_arr, prev_map, prev_rows = prev, row, 8 * (tm // t_len)
    else:
        assert t_len % tm == 0 and tm % 8 == 0
        prev_arr, prev_map, prev_rows = cin, (lambda i: (jnp.maximum(i * (tm // 8) - 1, 0), 0)), 8
    kern = functools.partial(_odd_out_kernel, t_len=t_len, has_prev=has_prev)
    return pl.pallas_call(
        kern,
        grid=(n_rows // tm,),
        in_specs=[pl.BlockSpec((tm, d), row), pl.BlockSpec((tm, n), row), pl.BlockSpec((tm, n), row),
                  pl.BlockSpec((tm, n), row), pl.BlockSpec((prev_rows, n), prev_map),
                  _resident(cw), _resident(w), _resident(gf), _resident(w1), _resident(w2)],
        out_specs=pl.BlockSpec((tm, d), row),
        out_shape=jax.ShapeDtypeStruct((n_rows, d), F32),
        compiler_params=_cparams(("parallel",)),
        name="odd_out_ffn",
    )(x, att, gb, cin, prev_arr, cw, _operand(w), gf, _operand(w1), _operand(w2))


def _round_up(n, m):
    return (n + m - 1) // m * m


def _row_tile(n_rows, cap):
    t = min(cap, n_rows)
    while n_rows % t:
        t //= 2
    return t


def _odd_weight_layout(n_q):
    n_heads = n_q // D_HEAD
    rep = n_heads // 2
    perm = np.zeros((n_q,), np.int32)
    for r in range(rep):
        for g in range(2):
            for dd in range(D_HEAD):
                perm[r * LANES + g * D_HEAD + dd] = (g * rep + r) * D_HEAD + dd
    return perm


def kernel(x_prompt, x_sample, cache_a_k, cache_a_v, cache_c_k, cache_c_v, cache_c_kidx, state_d_conv,
           norm_mix_g, norm_ffn_g, w_in_even, gq_a, gk_a, g_b, ws_b, bs_b, w_out_even,
           w_in_odd, gq_c, gk_c, conv_w_d, w_out_odd, w_ffn1, w_ffn2):
    bp, tp, d = x_prompt.shape
    bs, ts, _ = x_sample.shape
    depth = norm_mix_g.shape[0]
    past = cache_a_k.shape[2]
    n_a = cache_a_k.shape[3] * cache_a_k.shape[4]
    n_kv = cache_c_k.shape[3] * cache_c_k.shape[4]
    n_b = g_b.shape[1]
    n_d = conv_w_d.shape[2]
    n_q = w_out_odd.shape[1] - n_d
    assert n_a == 512 and n_b == 512 and n_kv == 128 and n_q == 512 and n_d == 512 and d == 1024

    xp = x_prompt.reshape(bp * tp, d)
    xs = x_sample.reshape(bs * ts, d)
    tm_p = _row_tile(bp * tp, 512)
    tm_s = _row_tile(bs * ts, 512)
    ls = past + ts
    ls_pad = _round_up(ls, KEY_TILE)

    hmean = jnp.asarray(np.kron(np.eye(n_a // D_HEAD), np.full((D_HEAD, D_HEAD), 1.0 / D_HEAD)), BF16)
    perm = _odd_weight_layout(n_q)
    tile8 = lambda gvec: jnp.tile(gvec, n_a // D_HEAD).reshape(1, n_a)

    ak_p, av_p, ak_s, av_s, bv_s = [], [], [], [], []
    ck_p, cv_p, ci_p, ck_s, cv_s, ci_s, dc_p, dc_s = [], [], [], [], [], [], [], []

    tri = _stick_tri()

    def pad_keys(past_rows, new_rows, multiple=KEY_TILE):
        width = new_rows.shape[-1]
        zeros = jnp.zeros((bs, _round_up(ls, multiple) - ls, width), F32)
        return jnp.concatenate([past_rows.reshape(bs, past, width), new_rows.reshape(bs, ts, width), zeros], axis=1)

    w1_all, w2_all = w_ffn1.astype(BF16), w_ffn2.astype(BF16)
    w_in_even_all, w_out_even_all = w_in_even.astype(BF16), w_out_even.astype(BF16)
    offs = np.cumsum([0, n_q, n_kv, n_kv, H_IDX * D_IDX, D_IDX, H_IDX, n_d, n_d, n_d])
    col = lambda s: w_in_odd[:, :, int(offs[s]):int(offs[s + 1])]
    w_in_odd_all = jnp.concatenate(
        [col(0)[:, :, perm], col(1), col(2), col(3), col(4), col(4), col(5),
         jnp.zeros((w_in_odd.shape[0], d, LANES - H_IDX), F32), col(6), col(7), col(8)], axis=2).astype(BF16)
    assert w_in_odd_all.shape[2] == ODD_COLS
    w_out_odd_all = jnp.concatenate([w_out_odd[:, :n_q][:, perm], w_out_odd[:, n_q:]], axis=1).astype(BF16)

    for i in range(depth):
        j = i // 2
        g_mix = norm_mix_g[i].reshape(1, d)
        ffn = (norm_ffn_g[i].reshape(1, d), (w1_all, i), (w2_all, i))
        if i % 2 == 0:
            w_in = (w_in_even_all, j)
            w_out = (w_out_even_all, j)
            gq, gk, gb = tile8(gq_a[j]), tile8(gk_a[j]), g_b[j].reshape(1, n_b)
            q, k, v, u, vb, kb, vm = _even_in(xp, g_mix, w_in, hmean, gq, gk, gb, tm_p)
            att = _stick(q.reshape(bp, tp, n_a), kb.reshape(bp, tp, n_a), vm.reshape(bp, tp, 2 * n_a), tri,
                         min(tp, KEY_TILE), 0)
            xp = _even_out_ffn(xp, att.reshape(bp * tp, n_a), u, vb, ws_b[j], bs_b[j].T, w_out, *ffn, tm_p, B_CHUNK)
            ak_p.append(k.reshape(bp, tp, n_a // D_HEAD, D_HEAD))
            av_p.append(v.reshape(bp, tp, n_a // D_HEAD, D_HEAD))
            q, k, v, u, vb, _, _ = _even_in(xs, g_mix, w_in, hmean, gq, gk, gb, tm_s)
            att = _stick(q.reshape(bs, ts, n_a), pad_keys(cache_a_k[j], k, STICK_TK),
                         pad_keys(cache_a_v[j], v, STICK_TK), tri, ts, past)
            xs = _even_out_ffn(xs, att.reshape(bs * ts, n_a), u, vb, ws_b[j][:, :ts, :ts], bs_b[j][:, :ts].T,
                               w_out, *ffn, tm_s, ts)
            ak_s.append(k.reshape(bs, ts, n_a // D_HEAD, D_HEAD))
            av_s.append(v.reshape(bs, ts, n_a // D_HEAD, D_HEAD))
            bv_s.append(vb.reshape(bs, ts, n_b))
        else:
            w_in = (w_in_odd_all, j)
            w_out = (w_out_odd_all, j)
            gq, gk = tile8(gq_c[j]), jnp.tile(gk_c[j], n_kv // D_HEAD).reshape(1, n_kv)
            cw = conv_w_d[j]
            q, k, v, qi, ki, wi, gbt, cin = _odd_in(xp, g_mix, w_in, hmean, gq, gk, tm_p)
            r3 = lambda a: a.reshape(bp, tp, a.shape[-1])
            att = _dsa(r3(q), r3(qi), r3(wi), r3(k), r3(v), r3(ki), LANES, 0, tp)
            xp = _odd_out_ffn(xp, att.reshape(bp * tp, n_q), gbt, cin, None, cw, w_out, *ffn, tm_p, tp)
            ck_p.append(k.reshape(bp, tp, n_kv // D_HEAD, D_HEAD))
            cv_p.append(v.reshape(bp, tp, n_kv // D_HEAD, D_HEAD))
            ci_p.append(ki[:, :D_IDX].reshape(bp, tp, D_IDX))
            dc_p.append(cin.reshape(bp, tp, n_d)[:, tp - (CONV_W - 1):])
            q, k, v, qi, ki, wi, gbt, cin = _odd_in(xs, g_mix, w_in, hmean, gq, gk, tm_s)
            r3 = lambda a: a.reshape(bs, ts, a.shape[-1])
            past_ki = jnp.concatenate([cache_c_kidx[j], cache_c_kidx[j]], axis=-1)
            pad_q = lambda a: jnp.pad(r3(a), ((0, 0), (0, LANES - ts), (0, 0)))
            att = _dsa(pad_q(q), pad_q(qi), pad_q(wi), pad_keys(cache_c_k[j], k, DSA_CK),
                       pad_keys(cache_c_v[j], v, DSA_CK), pad_keys(past_ki, ki, DSA_CK), LANES, past, ls)[:, :ts]
            prev = jnp.concatenate([jnp.zeros((bs, 8 - (CONV_W - 1), n_d), F32), state_d_conv[j]], axis=1)
            xs = _odd_out_ffn(xs, att.reshape(bs * ts, n_q), gbt, cin, prev.reshape(bs * 8, n_d), cw, w_out, *ffn,
                              tm_s, ts)
            ck_s.append(k.reshape(bs, ts, n_kv // D_HEAD, D_HEAD))
            cv_s.append(v.reshape(bs, ts, n_kv // D_HEAD, D_HEAD))
            ci_s.append(ki[:, :D_IDX].reshape(bs, ts, D_IDX))
            dc_s.append(cin.reshape(bs, ts, n_d)[:, ts - (CONV_W - 1):])

    return (xp.reshape(bp, tp, d), xs.reshape(bs, ts, d),
            jnp.stack(ak_p), jnp.stack(av_p), jnp.stack(ak_s), jnp.stack(av_s),
            jnp.stack(bv_s),
            jnp.stack(ck_p), jnp.stack(cv_p), jnp.stack(ci_p),
            jnp.stack(ck_s), jnp.stack(cv_s), jnp.stack(ci_s),
            jnp.stack(dc_p), jnp.stack(dc_s))
```

```python
import functools

import numpy as np
import jax
import jax.numpy as jnp
from jax import lax
from jax.experimental import pallas as pl
from jax.experimental.pallas import tpu as pltpu

F32 = jnp.float32
BF16 = jnp.bfloat16

EPS = 1e-6
NEG = -1e30
LOG2_E = 1.4426950408889634
CHUNK = 64
CHUNK_SHIFT = 6
D_HEAD = 64
G_B = 4
B_CHUNK = 128
H_IDX = 4
D_IDX = 64
TOPK_MAX = 256
CONV_W = 3
LANES = 128
KEY_TILE = 128
VMEM_LIMIT = 56 * 1024 * 1024


def _cparams(sem):
    return pltpu.CompilerParams(dimension_semantics=sem, vmem_limit_bytes=VMEM_LIMIT)


def _resident(a):
    if isinstance(a, tuple):
        arr, layer = a
        shape = arr.shape[1:]
        return pl.BlockSpec((None,) + shape, lambda *_: (layer,) + (0,) * len(shape),
                            pipeline_mode=pl.Buffered(1))
    return pl.BlockSpec(a.shape, lambda *_: (0,) * a.ndim, pipeline_mode=pl.Buffered(1))


def _operand(a):
    return a[0] if isinstance(a, tuple) else a


def _dot(a, b):
    return jnp.dot(a, b, preferred_element_type=F32)


def _dot_t(a, b):
    return lax.dot_general(a, b, (((1,), (1,)), ((), ())), preferred_element_type=F32)


def _split(x):
    hi = x.astype(BF16)
    lo = (x - hi.astype(F32)).astype(BF16)
    return hi, lo


def _rms(x, g):
    return x * lax.rsqrt(jnp.mean(x * x, axis=-1, keepdims=True) + EPS) * g


def _head_rms(x, hmean, g):
    ms = _dot((x * x).astype(BF16), hmean)
    return x * lax.rsqrt(ms + EPS) * g


def _gelu(x):
    return jax.nn.gelu(x)


def _even_in_kernel(x_ref, g_ref, w_ref, hm_ref, gq_ref, gk_ref, gb_ref,
                    q_ref, k_ref, v_ref, u_ref, vb_ref, kb_ref, vm_ref):
    h = _rms(x_ref[...], g_ref[...]).astype(BF16)
    n = q_ref.shape[-1]

    def seg(i):
        return _dot(h, w_ref[:, i * n:(i + 1) * n])

    hm = hm_ref[...]
    q_ref[...] = _head_rms(seg(0), hm, gq_ref[...])
    k = _head_rms(seg(1), hm, gk_ref[...])
    k_ref[...] = k
    kb_ref[...] = k.astype(BF16)
    v = seg(2)
    v_ref[...] = v
    first = lax.broadcasted_iota(jnp.int32, (1, LANES), 1) < D_HEAD
    for p in range(n // LANES):
        vp = v[:, p * LANES:(p + 1) * LANES]
        vm_ref[:, 2 * p * LANES:(2 * p + 1) * LANES] = jnp.where(first, vp, 0.0).astype(BF16)
        vm_ref[:, (2 * p + 1) * LANES:(2 * p + 2) * LANES] = jnp.where(first, 0.0, vp).astype(BF16)
    u_ref[...] = _gelu(seg(3))
    vb_ref[...] = _rms(_gelu(seg(4)), gb_ref[...])


def _even_in(x, g, w, hm, gq, gk, gb, tm):
    n_rows, d = x.shape
    n = _operand(w).shape[-1] // 5
    row = lambda i: (i, 0)
    out = jax.ShapeDtypeStruct((n_rows, n), F32)
    return pl.pallas_call(
        _even_in_kernel,
        grid=(n_rows // tm,),
        in_specs=[pl.BlockSpec((tm, d), row), _resident(g), _resident(w), _resident(hm),
                  _resident(gq), _resident(gk), _resident(gb)],
        out_specs=[pl.BlockSpec((tm, n), row)] * 6 + [pl.BlockSpec((tm, 2 * n), row)],
        out_shape=[out] * 5 + [jax.ShapeDtypeStruct((n_rows, n), BF16), jax.ShapeDtypeStruct((n_rows, 2 * n), BF16)],
        compiler_params=_cparams(("parallel",)),
        name="even_in",
    )(x, g, _operand(w), hm, gq, gk, gb)


ODD_Q, ODD_K, ODD_V, ODD_QI, ODD_KI, ODD_WI, ODD_GB, ODD_GC, ODD_HD = (
    0, 512, 640, 768, 1024, 1152, 1280, 1792, 2304)
ODD_COLS = 2816


def _odd_in_kernel(x_ref, g_ref, w_ref, hm_ref, gq_ref, gk_ref,
                   q_ref, k_ref, v_ref, qi_ref, ki_ref, wi_ref, gb_ref, cin_ref):
    h = _rms(x_ref[...], g_ref[...]).astype(BF16)

    def seg(lo, width):
        return _dot(h, w_ref[:, lo:lo + width])

    hm = hm_ref[...]
    q_ref[...] = _head_rms(seg(ODD_Q, 512), hm, gq_ref[...])
    k_ref[...] = _head_rms(seg(ODD_K, 128), hm[:128, :128], gk_ref[...])
    v_ref[...] = seg(ODD_V, 128)
    qi_ref[...] = seg(ODD_QI, 256)
    ki_ref[...] = seg(ODD_KI, 128)
    wi_ref[...] = seg(ODD_WI, 128)
    gb_ref[...] = seg(ODD_GB, 512)
    cin_ref[...] = seg(ODD_GC, 512) * seg(ODD_HD, 512)


def _odd_in(x, g, w, hm, gq, gk, tm):
    n_rows, d = x.shape
    row = lambda i: (i, 0)
    widths = (512, 128, 128, 256, 128, 128, 512, 512)
    return pl.pallas_call(
        _odd_in_kernel,
        grid=(n_rows // tm,),
        in_specs=[pl.BlockSpec((tm, d), row), _resident(g), _resident(w), _resident(hm),
                  _resident(gq), _resident(gk)],
        out_specs=[pl.BlockSpec((tm, n), row) for n in widths],
        out_shape=[jax.ShapeDtypeStruct((n_rows, n), F32) for n in widths],
        compiler_params=_cparams(("parallel",)),
        name="odd_in",
    )(x, g, _operand(w), hm, gq, gk)


STICK_TK = 256
F32_EXP2_ZERO = -150.0


def _stick_tri():
    half = STICK_TK // 2
    later = (np.arange(half)[:, None] > np.arange(half)[None, :]).astype(np.float32)
    blk = np.concatenate([later, np.ones((half, half), np.float32)], axis=1)
    return jnp.asarray(np.concatenate([blk, blk], axis=0), BF16)


def _stick_kernel(q_ref, k_ref, v_ref, tri_ref, o_ref, carry_sc, acc_sc, *, tq, q_pos0):
    tk = STICK_TK
    half = tk // 2
    n_pairs = q_ref.shape[1] // LANES
    v_split = v_ref.shape[1] == 2 * q_ref.shape[1]
    q_start = q_pos0 + pl.program_id(1) * tq
    n_full = q_start // tk
    lane = lax.broadcasted_iota(jnp.int32, (1, LANES), 1)
    first = lane < D_HEAD
    q = q_ref[...] * (D_HEAD ** -0.5 * LOG2_E)
    q2 = []
    for p in range(n_pairs):
        qp = q[:, p * LANES:(p + 1) * LANES]
        q2.append(jnp.concatenate([jnp.where(first, qp, 0.0), jnp.where(first, 0.0, qp)], axis=0).astype(BF16))
    tri = tri_ref[...]
    carry_sc[...] = jnp.zeros_like(carry_sc)
    acc_sc[...] = jnp.zeros_like(acc_sc)

    def block(kb, masked):
        ks = pl.multiple_of(kb * tk, tk)
        if masked:
            row = lax.broadcasted_iota(jnp.int32, (2 * tq, tk), 0)
            qpos = q_start + jnp.where(row >= tq, row - tq, row)
            kpos = ks + lax.broadcasted_iota(jnp.int32, (2 * tq, tk), 1)
            ok = kpos < qpos
        for p in range(n_pairs):
            cols = slice(p * LANES, (p + 1) * LANES)
            z = _dot_t(q2[p], k_ref[pl.ds(ks, tk), cols].astype(BF16))
            nz = -z
            stay = jnp.minimum(nz, 0.0) - jnp.log2(1.0 + jnp.exp2(jnp.minimum(z, nz)))
            if masked:
                stay = jnp.where(ok, stay, 0.0)
            carry = carry_sc[p]
            after = [None, None]
            for h in (1, 0):
                hi, lo = _split(stay[:, h * half:(h + 1) * half])
                r = _dot(jnp.concatenate([hi, lo], axis=1), tri)
                after[h] = r[:, :half] + carry
                carry = carry + r[:, half:]
            carry_sc[p] = carry
            w = jnp.exp2(z + stay + jnp.concatenate(after, axis=1))
            if masked:
                w = jnp.where(ok, w, 0.0)
            w = w.astype(BF16)
            if v_split:
                v0 = v_ref[pl.ds(ks, tk), 2 * p * LANES:(2 * p + 1) * LANES]
                v1 = v_ref[pl.ds(ks, tk), (2 * p + 1) * LANES:(2 * p + 2) * LANES]
            else:
                vblk = v_ref[pl.ds(ks, tk), cols]
                v0 = jnp.where(first, vblk, 0.0).astype(BF16)
                v1 = jnp.where(first, 0.0, vblk).astype(BF16)
            acc_sc[p] += _dot(jnp.concatenate([w[:tq], w[tq:]], axis=1), jnp.concatenate([v0, v1], axis=0))

    block(n_full, True)

    def any_weight_left():
        m = jnp.max(jnp.max(carry_sc[...], axis=0), axis=0, keepdims=True)
        return jnp.max(m, axis=1, keepdims=True)[0, 0] > F32_EXP2_ZERO

    def body(s):
        block(s[0], False)
        return s[0] - 1, any_weight_left()

    lax.while_loop(lambda s: (s[0] >= 0) & s[1], body, (n_full - 1, any_weight_left()))
    for p in range(n_pairs):
        o_ref[:, p * LANES:(p + 1) * LANES] = acc_sc[p]


def _stick(q, k, v, tri, tq, q_pos0):
    b, t, c = q.shape
    lp = k.shape[1]
    tk = STICK_TK
    assert lp % tk == 0 and q_pos0 % tq == 0 and tk % tq == 0 and q_pos0 + t <= lp
    kern = functools.partial(_stick_kernel, tq=tq, q_pos0=q_pos0)
    kmap = lambda bi, i: (bi, 0, 0)
    return pl.pallas_call(
        kern,
        grid=(b, t // tq),
        in_specs=[pl.BlockSpec((None, tq, c), lambda bi, i: (bi, i, 0)),
                  pl.BlockSpec((None, lp, c), kmap),
                  pl.BlockSpec((None, lp, v.shape[2]), kmap),
                  pl.BlockSpec(tri.shape, lambda bi, i: (0, 0))],
        out_specs=pl.BlockSpec((None, tq, c), lambda bi, i: (bi, i, 0)),
        out_shape=jax.ShapeDtypeStruct((b, t, c), F32),
        scratch_shapes=[pltpu.VMEM((c // LANES, 2 * tq, LANES), F32), pltpu.VMEM((c // LANES, tq, LANES), F32)],
        compiler_params=_cparams(("parallel", "arbitrary")),
        name="stick_attn",
    )(q, k, v, tri)


DSA_LEVEL = 512


def _dsa_rows_kernel(q_ref, qi_ref, wi_ref, k_ref, v_ref, ki_ref, o_ref, key_sc, bias_sc,
                     *, tq, l_valid, q_pos0, top_k, levels):
    kt = KEY_TILE
    th = min(tq, 128)
    parts = [slice(i * th, (i + 1) * th) for i in range(tq // th)]
    n_rep = q_ref.shape[1] // LANES
    lane = lax.broadcasted_iota(jnp.int32, (1, LANES), 1)
    first = lane < D_HEAD
    q_start = q_pos0 + pl.program_id(1) * tq
    kf = jnp.float32(top_k)

    def body(lv):
        n_kt = lv // kt

        def admissible(rows, lo, width):
            qpos = q_start + rows.start + lax.broadcasted_iota(jnp.int32, (th, 1), 0)
            kpos = lo + lax.broadcasted_iota(jnp.int32, (1, width), 1)
            return (jnp.right_shift(kpos, CHUNK_SHIFT) <= jnp.right_shift(qpos, CHUNK_SHIFT)) & (kpos < l_valid)

        ki_hi, ki_lo = _split(ki_ref[:lv, :])
        for rows in parts:
            qi = qi_ref[rows, :]
            wi = wi_ref[rows, :]
            qm = []
            for h in range(H_IDX):
                grp = qi[:, (h // 2) * LANES:(h // 2 + 1) * LANES]
                qm.append(jnp.where(first, grp, 0.0) if h % 2 == 0 else jnp.where(first, 0.0, grp))
            q_hi, q_lo = _split(jnp.concatenate(qm, axis=0))
            s = _dot_t(q_hi, ki_hi) + (_dot_t(q_hi, ki_lo) + _dot_t(q_lo, ki_hi))
            score = jnp.zeros((th, lv), F32)
            for h in range(H_IDX):
                score = score + jnp.maximum(s[h * th:(h + 1) * th], 0.0) * wi[:, h:h + 1]
            score = score * ((D_IDX ** -0.5) * (H_IDX ** -0.5)) + 0.0
            score = jnp.where(admissible(rows, 0, lv), score, NEG)
            bits = lax.bitcast_convert_type(score, jnp.int32)
            key_sc[rows, :lv] = jnp.where(bits < 0, bits ^ jnp.int32(0x7FFFFFFF), bits)

        def count_ge(rows, trial):
            part = jnp.zeros((th, kt), F32)
            for c in range(n_kt):
                part = part + jnp.where(key_sc[rows, c * kt:(c + 1) * kt] >= trial, 1.0, 0.0)
            return jnp.sum(part, axis=1, keepdims=True)

        int_min = jnp.int32(-2 ** 31)
        zero = jnp.zeros((th, 1), jnp.int32)
        thr0 = tuple(jnp.where(count_ge(rows, zero) >= kf, jnp.int32(0), int_min) for rows in parts)

        def bit_step(i, thr):
            bit = jnp.left_shift(jnp.int32(1), jnp.int32(30) - i)
            return tuple(jnp.where(count_ge(rows, t + bit) >= kf, t + bit, t) for rows, t in zip(parts, thr))

        thrs = lax.fori_loop(0, 31, bit_step, thr0)

        rr = lax.broadcasted_iota(jnp.int32, (kt, kt), 0)
        cc = lax.broadcasted_iota(jnp.int32, (kt, kt), 1)
        before = jnp.where(rr < cc, 1.0, 0.0).astype(BF16)
        for rows, thr in zip(parts, thrs):
            n_gt = jnp.zeros((th, kt), F32)
            for c in range(n_kt):
                n_gt = n_gt + jnp.where(key_sc[rows, c * kt:(c + 1) * kt] > thr, 1.0, 0.0)
            room = kf - jnp.sum(n_gt, axis=1, keepdims=True)
            run = jnp.zeros((th, 1), F32)
            for c in range(n_kt):
                keyc = key_sc[rows, c * kt:(c + 1) * kt]
                eqf = jnp.where(keyc == thr, 1.0, 0.0)
                rank = _dot(eqf.astype(BF16), before) + run
                tie_ok = jnp.where(keyc == thr, jnp.where(rank < room, 0.0, NEG), NEG)
                sel = jnp.where(keyc > thr, 0.0, tie_ok)
                bias_sc[rows, c * kt:(c + 1) * kt] = jnp.where(admissible(rows, c * kt, kt), sel, NEG)
                run = run + jnp.sum(eqf, axis=1, keepdims=True)

        kb = k_ref[:lv, :].astype(BF16)
        vf = v_ref[:lv, :]
        vg = [jnp.where(first, vf, 0.0).astype(BF16), jnp.where(first, 0.0, vf).astype(BF16)]
        for rows in parts:
            q = q_ref[rows, :] * (D_HEAD ** -0.5)
            bias = bias_sc[rows, :lv]
            out = jnp.zeros((n_rep * th, LANES), F32)
            for g in range(2):
                keep = first if g == 0 else jnp.logical_not(first)
                qs = jnp.concatenate(
                    [jnp.where(keep, q[:, r * LANES:(r + 1) * LANES], 0.0) for r in range(n_rep)],
                    axis=0).astype(BF16)
                logits = _dot_t(qs, kb)
                logits = (logits.reshape(n_rep, th, lv) + bias[None]).reshape(n_rep * th, lv)
                m = jnp.max(logits, axis=1, keepdims=True)
                p = jnp.exp(logits - m)
                den = jnp.sum(p, axis=1, keepdims=True)
                out = out + _dot(p.astype(BF16), vg[g]) / den
            for r in range(n_rep):
                o_ref[rows, r * LANES:(r + 1) * LANES] = out[r * th:(r + 1) * th]

    if len(levels) == 1:
        body(levels[0])
    else:
        cls = (q_start + tq - 1) // DSA_LEVEL
        for c, lv in enumerate(levels):
            pl.when(cls == c)(functools.partial(body, lv))


DSA_CK = 256


def _dsa_kernel(*refs, tq, l_valid, q_pos0, top_k, levels):
    if len(levels) == 1:
        _dsa_tile(*refs, tq=tq, l_valid=l_valid, q_pos0=q_pos0, top_k=top_k, n_chunks=levels[0])
        return
    q_last = q_pos0 + pl.program_id(1) * tq + tq - 1
    vis_end = jnp.left_shift(jnp.right_shift(q_last, CHUNK_SHIFT) + 1, CHUNK_SHIFT)
    step = levels[0] * DSA_CK
    for i, n in enumerate(levels):
        pl.when((vis_end + step - 1) // step == i + 1)(functools.partial(
            _dsa_tile, *refs, tq=tq, l_valid=l_valid, q_pos0=q_pos0, top_k=top_k, n_chunks=n))


def _dsa_tile(q_ref, qi_ref, wi_ref, k_ref, v_ref, ki_ref, o_ref,
              kic_sc, kb_sc, vt_sc, key_sc, bias_sc, *, tq, l_valid, q_pos0, top_k, n_chunks):
    ck = DSA_CK
    lp = k_ref.shape[0]
    n_rep = q_ref.shape[1] // LANES
    lane = lax.broadcasted_iota(jnp.int32, (1, LANES), 1)
    first = lane < D_HEAD
    q_start = q_pos0 + pl.program_id(1) * tq
    tl = LANES
    parts = range(tq // tl)
    lanes_of = lambda pt: slice(pt * tl, (pt + 1) * tl)
    qchunk = [jnp.right_shift(q_start + pt * tl + lax.broadcasted_iota(jnp.int32, (1, tl), 1), CHUNK_SHIFT)
              for pt in parts]
    kf = jnp.float32(top_k)
    top_rows = lax.broadcasted_iota(jnp.int32, (LANES, 1), 0) < D_HEAD

    def rows_of(c):
        return pl.ds(c * ck, ck)

    def admissible(c, pt):
        kpos = c * ck + lax.broadcasted_iota(jnp.int32, (ck, 1), 0)
        return (jnp.right_shift(kpos, CHUNK_SHIFT) <= qchunk[pt]) & (kpos < l_valid)

    def chunks(n, step, carry):
        for c in range(n):
            carry = step(c, carry)
        return carry

    @pl.when(pl.program_id(1) == 0)
    def _():
        def prep(c, _):
            rows = rows_of(c)
            hi, lo = _split(ki_ref[rows, :])
            kic_sc[rows, :LANES] = jnp.where(first, hi, lo)
            kic_sc[rows, LANES:] = jnp.where(first, hi, jnp.zeros_like(hi))
            kb_sc[rows, :] = k_ref[rows, :].astype(BF16)
            vt = v_ref[rows, :].T
            vt_sc[c, :, :ck] = jnp.where(top_rows, vt, 0.0).astype(BF16)
            vt_sc[c, :, ck:] = jnp.where(top_rows, 0.0, vt).astype(BF16)
            return 0
        chunks(lp // ck, prep, 0)

    qic_t, wit = [], []
    for pt in parts:
        qi = qi_ref[lanes_of(pt), :]
        blocks = []
        for h in range(H_IDX):
            grp = qi[:, (h // 2) * LANES:(h // 2 + 1) * LANES]
            swapped = pltpu.roll(grp, D_IDX, 1)
            twice = jnp.where(first, grp, swapped) if h % 2 == 0 else jnp.where(first, swapped, grp)
            x3 = jnp.concatenate([twice, jnp.where(first, twice, 0.0)], axis=1)
            hi = x3.astype(BF16).astype(F32)
            blocks.append(jnp.concatenate([hi[:, :LANES], (x3 - hi)[:, LANES:]], axis=1))
        qic_t.append(jnp.concatenate(blocks, axis=0).T.astype(BF16))
        wit.append(wi_ref[lanes_of(pt), :].T)

    def score_chunk(c, _):
        for pt in parts:
            s = _dot(kic_sc[rows_of(c), :], qic_t[pt])
            score = jnp.zeros((ck, tl), F32)
            for h in range(H_IDX):
                score = score + jnp.maximum(s[:, h * tl:(h + 1) * tl], 0.0) * wit[pt][h:h + 1, :]
            score = score * ((D_IDX ** -0.5) * (H_IDX ** -0.5)) + 0.0
            score = jnp.where(admissible(c, pt), score, NEG)
            bits = lax.bitcast_convert_type(score, jnp.int32)
            key_sc[rows_of(c), lanes_of(pt)] = jnp.where(bits < 0, bits ^ jnp.int32(0x7FFFFFFF), bits)
        return 0

    chunks(n_chunks, score_chunk, 0)

    def count(pred, trials):
        def body(c, accs):
            out = []
            for pt in parts:
                hit = jnp.where(pred(key_sc[rows_of(c), lanes_of(pt)], trials[pt]), 1.0, 0.0)
                out.append(accs[pt] + jnp.sum(hit.reshape(ck // 8, 8, tl), axis=0))
            return out
        accs = chunks(n_chunks, body, [jnp.zeros((8, tl), F32) for _ in parts])
        return [jnp.sum(a, axis=0, keepdims=True) for a in accs]

    ge = lambda a, b: a >= b
    above = count(ge, [jnp.zeros((1, tl), jnp.int32) for _ in parts])
    thr0 = tuple(jnp.where(n >= kf, jnp.int32(0), jnp.int32(-2 ** 31)) for n in above)

    def bit_step(i, thr):
        bit = jnp.left_shift(jnp.int32(1), jnp.int32(30) - i)
        trials = [t + bit for t in thr]
        return tuple(jnp.where(n >= kf, tr, t) for n, tr, t in zip(count(ge, trials), trials, thr))

    thr = lax.fori_loop(0, 31, bit_step, thr0)

    room = [kf - n for n in count(lambda a, b: a > b, thr)]
    rr = lax.broadcasted_iota(jnp.int32, (ck, ck), 0)
    cc = lax.broadcasted_iota(jnp.int32, (ck, ck), 1)
    earlier = jnp.where(cc < rr, 1.0, 0.0).astype(BF16)

    def tie_chunk(c, runs):
        out = []
        for pt in parts:
            keyc = key_sc[rows_of(c), lanes_of(pt)]
            eqf = jnp.where(keyc == thr[pt], 1.0, 0.0)
            rank = _dot(earlier, eqf.astype(BF16)) + runs[pt]
            tie_ok = jnp.where(keyc == thr[pt], jnp.where(rank < room[pt], 0.0, NEG), NEG)
            sel = jnp.where(keyc > thr[pt], 0.0, tie_ok)
            bias_sc[rows_of(c), lanes_of(pt)] = jnp.where(admissible(c, pt), sel, NEG)
            out.append(runs[pt] + jnp.sum(eqf, axis=0, keepdims=True))
        return out

    chunks(n_chunks, tie_chunk, [jnp.zeros((1, tl), F32) for _ in parts])

    qs_t = []
    for pt in parts:
        q = q_ref[lanes_of(pt), :] * (D_HEAD ** -0.5 * LOG2_E)
        per_group = []
        for g in range(2):
            keep = first if g == 0 else jnp.logical_not(first)
            per_group.append(jnp.concatenate(
                [jnp.where(keep, q[:, r * LANES:(r + 1) * LANES], 0.0) for r in range(n_rep)],
                axis=0).T.astype(BF16))
        qs_t.append(per_group)
    width = n_rep * tl

    def att_chunk(c, carries):
        kb = kb_sc[rows_of(c), :]
        out = []
        for pt in parts:
            m0, m1, d0, d1, acc = carries[pt]
            bias = bias_sc[rows_of(c), lanes_of(pt)]
            bias = jnp.concatenate([bias] * n_rep, axis=1)
            l0 = _dot(kb, qs_t[pt][0]) + bias
            l1 = _dot(kb, qs_t[pt][1]) + bias
            n0 = jnp.maximum(m0, jnp.max(l0, axis=0, keepdims=True))
            n1 = jnp.maximum(m1, jnp.max(l1, axis=0, keepdims=True))
            a0, a1 = jnp.exp2(m0 - n0), jnp.exp2(m1 - n1)
            p0, p1 = jnp.exp2(l0 - n0), jnp.exp2(l1 - n1)
            d0 = d0 * a0 + jnp.sum(p0, axis=0, keepdims=True)
            d1 = d1 * a1 + jnp.sum(p1, axis=0, keepdims=True)
            pv = _dot(vt_sc[c], jnp.concatenate([p0.astype(BF16), p1.astype(BF16)], axis=0))
            out.append((n0, n1, d0, d1, acc * jnp.where(top_rows, a0, a1) + pv))
        return out

    neg = jnp.full((1, width), NEG, F32)
    zero = jnp.zeros((1, width), F32)
    done = chunks(n_chunks, att_chunk, [(neg, neg, zero, zero, jnp.zeros((LANES, width), F32)) for _ in parts])
    for pt in parts:
        _, _, d0, d1, acc = done[pt]
        out_t = acc / jnp.where(top_rows, d0, d1)
        for r in range(n_rep):
            o_ref[lanes_of(pt), r * LANES:(r + 1) * LANES] = out_t[:, r * tl:(r + 1) * tl].T


def _dsa(q, qi, wi, k, v, ki, tq, q_pos0, l_valid):
    b, t, c = q.shape
    lp = k.shape[1]
    assert lp % DSA_CK == 0 and l_valid <= lp and t % tq == 0 and tq % LANES == 0
    top_k = min(TOPK_MAX, l_valid // 4)
    assert DSA_CK >= top_k
    n_all = lp // DSA_CK
    if q_pos0 == 0 and n_all % 4 == 0:
        levels = tuple(range(n_all // 4, n_all + 1, n_all // 4))
    else:
        levels = (n_all,)
    kern = functools.partial(_dsa_kernel, tq=tq, l_valid=l_valid, q_pos0=q_pos0, top_k=top_k, levels=levels)
    qmap = lambda bi, i: (bi, i, 0)
    kmap = lambda bi, i: (bi, 0, 0)
    return pl.pallas_call(
        kern,
        grid=(b, t // tq),
        in_specs=[pl.BlockSpec((None, tq, c), qmap), pl.BlockSpec((None, tq, qi.shape[2]), qmap),
                  pl.BlockSpec((None, tq, LANES), qmap),
                  pl.BlockSpec((None, lp, LANES), kmap), pl.BlockSpec((None, lp, LANES), kmap),
                  pl.BlockSpec((None, lp, LANES), kmap)],
        out_specs=pl.BlockSpec((None, tq, c), qmap),
        out_shape=jax.ShapeDtypeStruct((b, t, c), F32),
        scratch_shapes=[pltpu.VMEM((lp, 2 * LANES), BF16), pltpu.VMEM((lp, LANES), BF16),
                        pltpu.VMEM((lp // DSA_CK, LANES, 2 * DSA_CK), BF16), pltpu.VMEM((lp, tq), jnp.int32),
                        pltpu.VMEM((lp, tq), F32)],
        compiler_params=_cparams(("parallel", "arbitrary")),
        name="dsa_attn",
    )(q, qi, wi, k, v, ki)


def _dsa_rows(q, qi, wi, k, v, ki, tq, q_pos0, l_valid):
    b, t, c = q.shape
    lp = k.shape[1]
    assert lp % KEY_TILE == 0 and l_valid <= lp and t % tq == 0
    top_k = min(TOPK_MAX, l_valid // 4)
    if q_pos0 == 0 and lp % DSA_LEVEL == 0 and tq % CHUNK == 0 and DSA_LEVEL >= top_k:
        levels = tuple(range(DSA_LEVEL, lp + 1, DSA_LEVEL))
    else:
        levels = (lp,)
    kern = functools.partial(_dsa_rows_kernel, tq=tq, l_valid=l_valid, q_pos0=q_pos0, top_k=top_k, levels=levels)
    qmap = lambda bi, i: (bi, i, 0)
    kmap = lambda bi, i: (bi, 0, 0)
    return pl.pallas_call(
        kern,
        grid=(b, t // tq),
        in_specs=[pl.BlockSpec((None, tq, c), qmap), pl.BlockSpec((None, tq, qi.shape[2]), qmap),
                  pl.BlockSpec((None, tq, LANES), qmap),
                  pl.BlockSpec((None, lp, LANES), kmap), pl.BlockSpec((None, lp, LANES), kmap),
                  pl.BlockSpec((None, lp, LANES), kmap)],
        out_specs=pl.BlockSpec((None, tq, c), qmap),
        out_shape=jax.ShapeDtypeStruct((b, t, c), F32),
        scratch_shapes=[pltpu.VMEM((tq, lp), jnp.int32), pltpu.VMEM((tq, lp), F32)],
        compiler_params=_cparams(("parallel", "arbitrary")),
        name="dsa_attn",
    )(q, qi, wi, k, v, ki)


FFN_CHUNK = 1024


def _ffn_tail(x1, gf_ref, w1_ref, w2_ref, o_ref):
    h = _rms(x1, gf_ref[...]).astype(BF16)
    acc = None
    for c in range(w1_ref.shape[1] // FFN_CHUNK):
        cols = slice(c * FFN_CHUNK, (c + 1) * FFN_CHUNK)
        a = jnp.square(jnp.maximum(_dot(h, w1_ref[:, cols]), 0.0))
        part = _dot(a.astype(BF16), w2_ref[cols, :])
        acc = part if acc is None else acc + part
    o_ref[...] = x1 + acc


def _even_out_kernel(x_ref, att_ref, u_ref, vb_ref, ws_ref, bst_ref, w_ref, gf_ref, w1_ref, w2_ref, o_ref,
                     *, p_len):
    tm = x_ref.shape[0]
    cg = vb_ref.shape[1] // G_B
    rr = lax.broadcasted_iota(jnp.int32, (p_len, p_len), 0)
    cc = lax.broadcasted_iota(jnp.int32, (p_len, p_len), 1)
    causal = jnp.right_shift(cc, CHUNK_SHIFT) <= jnp.right_shift(rr, CHUNK_SHIFT)
    u = u_ref[...]
    vb = vb_ref[...].astype(BF16)
    bst = bst_ref[...]
    mixed_rows = []
    for c in range(tm // p_len):
        cols = []
        for g in range(G_B):
            wg = jnp.where(causal, ws_ref[g], 0.0).astype(BF16)
            mg = _dot(wg, vb[c * p_len:(c + 1) * p_len, g * cg:(g + 1) * cg]) + bst[:, g:g + 1]
            cols.append(mg)
        mixed_rows.append(jnp.concatenate(cols, axis=1))
    gated = u * jnp.concatenate(mixed_rows, axis=0)
    na = att_ref.shape[1]
    y = _dot(att_ref[...].astype(BF16), w_ref[:na, :]) + _dot(gated.astype(BF16), w_ref[na:, :])
    _ffn_tail(x_ref[...] + y, gf_ref, w1_ref, w2_ref, o_ref)


def _even_out_ffn(x, att, u, vb, ws, bst, w, gf, w1, w2, tm, p_len):
    n_rows, d = x.shape
    n = att.shape[1]
    row = lambda i: (i, 0)
    kern = functools.partial(_even_out_kernel, p_len=p_len)
    return pl.pallas_call(
        kern,
        grid=(n_rows // tm,),
        in_specs=[pl.BlockSpec((tm, d), row), pl.BlockSpec((tm, n), row), pl.BlockSpec((tm, n), row),
                  pl.BlockSpec((tm, n), row), _resident(ws), _resident(bst), _resident(w),
                  _resident(gf), _resident(w1), _resident(w2)],
        out_specs=pl.BlockSpec((tm, d), row),
        out_shape=jax.ShapeDtypeStruct((n_rows, d), F32),
        compiler_params=_cparams(("parallel",)),
        name="even_out_ffn",
    )(x, att, u, vb, ws, bst, _operand(w), gf, _operand(w1), _operand(w2))


def _odd_out_kernel(x_ref, att_ref, gb_ref, cin_ref, prev_ref, cw_ref, w_ref, gf_ref, w1_ref, w2_ref, o_ref,
                    *, t_len, has_prev):
    tm = x_ref.shape[0]
    prev = prev_ref[...]
    if not has_prev:
        start = (pl.program_id(0) * tm) % t_len == 0
        prev = jnp.where(start, 0.0, prev)
    cw = cw_ref[...]
    seg = t_len if has_prev else tm
    convs = []
    for s in range(tm // seg):
        cin = cin_ref[s * seg:(s + 1) * seg, :]
        ext = jnp.concatenate([prev[8 * s:8 * (s + 1)], cin], axis=0)
        conv = cw[CONV_W - 1:CONV_W] * cin
        for j in range(1, CONV_W):
            conv = conv + cw[CONV_W - 1 - j:CONV_W - j] * pltpu.roll(ext, j, 0)[8:]
        convs.append(conv)
    gated = gb_ref[...] * (convs[0] if len(convs) == 1 else jnp.concatenate(convs, axis=0))
    na = att_ref.shape[1]
    y = _dot(att_ref[...].astype(BF16), w_ref[:na, :]) + _dot(gated.astype(BF16), w_ref[na:, :])
    _ffn_tail(x_ref[...] + y, gf_ref, w1_ref, w2_ref, o_ref)


def _odd_out_ffn(x, att, gb, cin, prev, cw, w, gf, w1, w2, tm, t_len):
    n_rows, d = x.shape
    n = att.shape[1]
    row = lambda i: (i, 0)
    has_prev = prev is not None
    if has_prev:
        assert tm % t_len == 0 and t_len % 8 == 0
        prev_arr, prev_map, prev_rows = prev, row, 8 * (tm // t_len)
    else:
        assert t_len % tm == 0 and tm % 8 == 0
        prev_arr, prev_map, prev_rows = cin, (lambda i: (jnp.maximum(i * (tm // 8) - 1, 0), 0)), 8
    kern = functools.partial(_odd_out_kernel, t_len=t_len, has_prev=has_prev)
    return pl.pallas_call(
        kern,
        grid=(n_rows // tm,),
        in_specs=[pl.BlockSpec((tm, d), row), pl.BlockSpec((tm, n), row), pl.BlockSpec((tm, n), row),
                  pl.BlockSpec((tm, n), row), pl.BlockSpec((prev_rows, n), prev_map),
                  _resident(cw), _resident(w), _resident(gf), _resident(w1), _resident(w2)],
        out_specs=pl.BlockSpec((tm, d), row),
        out_shape=jax.ShapeDtypeStruct((n_rows, d), F32),
        compiler_params=_cparams(("parallel",)),
        name="odd_out_ffn",
    )(x, att, gb, cin, prev_arr, cw, _operand(w), gf, _operand(w1), _operand(w2))


def _round_up(n, m):
    return (n + m - 1) // m * m


def _row_tile(n_rows, cap):
    t = min(cap, n_rows)
    while n_rows % t:
        t //= 2
    return t


def _odd_weight_layout(n_q):
    n_heads = n_q // D_HEAD
    rep = n_heads // 2
    perm = np.zeros((n_q,), np.int32)
    for r in range(rep):
        for g in range(2):
            for dd in range(D_HEAD):
                perm[r * LANES + g * D_HEAD + dd] = (g * rep + r) * D_HEAD + dd
    return perm


def kernel(x_prompt, x_sample, cache_a_k, cache_a_v, cache_c_k, cache_c_v, cache_c_kidx, state_d_conv,
           norm_mix_g, norm_ffn_g, w_in_even, gq_a, gk_a, g_b, ws_b, bs_b, w_out_even,
           w_in_odd, gq_c, gk_c, conv_w_d, w_out_odd, w_ffn1, w_ffn2):
    bp, tp, d = x_prompt.shape
    bs, ts, _ = x_sample.shape
    depth = norm_mix_g.shape[0]
    past = cache_a_k.shape[2]
    n_a = cache_a_k.shape[3] * cache_a_k.shape[4]
    n_kv = cache_c_k.shape[3] * cache_c_k.shape[4]
    n_b = g_b.shape[1]
    n_d = conv_w_d.shape[2]
    n_q = w_out_odd.shape[1] - n_d
    assert n_a == 512 and n_b == 512 and n_kv == 128 and n_q == 512 and n_d == 512 and d == 1024

    xp = x_prompt.reshape(bp * tp, d)
    xs = x_sample.reshape(bs * ts, d)
    tm_p = _row_tile(bp * tp, 512)
    tm_s = _row_tile(bs * ts, 512)
    ls = past + ts
    ls_pad = _round_up(ls, KEY_TILE)

    hmean = jnp.asarray(np.kron(np.eye(n_a // D_HEAD), np.full((D_HEAD, D_HEAD), 1.0 / D_HEAD)), BF16)
    perm = _odd_weight_layout(n_q)
    tile8 = lambda gvec: jnp.tile(gvec, n_a // D_HEAD).reshape(1, n_a)

    ak_p, av_p, ak_s, av_s, bv_s = [], [], [], [], []
    ck_p, cv_p, ci_p, ck_s, cv_s, ci_s, dc_p, dc_s = [], [], [], [], [], [], [], []

    tri = _stick_tri()

    def pad_keys(past_rows, new_rows, multiple=KEY_TILE):
        width = new_rows.shape[-1]
        zeros = jnp.zeros((bs, _round_up(ls, multiple) - ls, width), F32)
        return jnp.concatenate([past_rows.reshape(bs, past, width), new_rows.reshape(bs, ts, width), zeros], axis=1)

    w1_all, w2_all = w_ffn1.astype(BF16), w_ffn2.astype(BF16)
    w_in_even_all, w_out_even_all = w_in_even.astype(BF16), w_out_even.astype(BF16)
    offs = np.cumsum([0, n_q, n_kv, n_kv, H_IDX * D_IDX, D_IDX, H_IDX, n_d, n_d, n_d])
    col = lambda s: w_in_odd[:, :, int(offs[s]):int(offs[s + 1])]
    w_in_odd_all = jnp.concatenate(
        [col(0)[:, :, perm], col(1), col(2), col(3), col(4), col(4), col(5),
         jnp.zeros((w_in_odd.shape[0], d, LANES - H_IDX), F32), col(6), col(7), col(8)], axis=2).astype(BF16)
    assert w_in_odd_all.shape[2] == ODD_COLS
    w_out_odd_all = jnp.concatenate([w_out_odd[:, :n_q][:, perm], w_out_odd[:, n_q:]], axis=1).astype(BF16)

    for i in range(depth):
        j = i // 2
        g_mix = norm_mix_g[i].reshape(1, d)
        ffn = (norm_ffn_g[i].reshape(1, d), (w1_all, i), (w2_all, i))
        if i % 2 == 0:
            w_in = (w_in_even_all, j)
            w_out = (w_out_even_all, j)
            gq, gk, gb = tile8(gq_a[j]), tile8(gk_a[j]), g_b[j].reshape(1, n_b)
            q, k, v, u, vb, kb, vm = _even_in(xp, g_mix, w_in, hmean, gq, gk, gb, tm_p)
            att = _stick(q.reshape(bp, tp, n_a), kb.reshape(bp, tp, n_a), vm.reshape(bp, tp, 2 * n_a), tri,
                         min(tp, KEY_TILE), 0)
            xp = _even_out_ffn(xp, att.reshape(bp * tp, n_a), u, vb, ws_b[j], bs_b[j].T, w_out, *ffn, tm_p, B_CHUNK)
            ak_p.append(k.reshape(bp, tp, n_a // D_HEAD, D_HEAD))
            av_p.append(v.reshape(bp, tp, n_a // D_HEAD, D_HEAD))
            q, k, v, u, vb, _, _ = _even_in(xs, g_mix, w_in, hmean, gq, gk, gb, tm_s)
            att = _stick(q.reshape(bs, ts, n_a), pad_keys(cache_a_k[j], k, STICK_TK),
                         pad_keys(cache_a_v[j], v, STICK_TK), tri, ts, past)
            xs = _even_out_ffn(xs, att.reshape(bs * ts, n_a), u, vb, ws_b[j][:, :ts, :ts], bs_b[j][:, :ts].T,
                               w_out, *ffn, tm_s, ts)
            ak_s.append(k.reshape(bs, ts, n_a // D_HEAD, D_HEAD))
            av_s.append(v.reshape(bs, ts, n_a // D_HEAD, D_HEAD))
            bv_s.append(vb.reshape(bs, ts, n_b))
        else:
            w_in = (w_in_odd_all, j)
            w_out = (w_out_odd_all, j)
            gq, gk = tile8(gq_c[j]), jnp.tile(gk_c[j], n_kv // D_HEAD).reshape(1, n_kv)
            cw = conv_w_d[j]
            q, k, v, qi, ki, wi, gbt, cin = _odd_in(xp, g_mix, w_in, hmean, gq, gk, tm_p)
            r3 = lambda a: a.reshape(bp, tp, a.shape[-1])
            att = _dsa(r3(q), r3(qi), r3(wi), r3(k), r3(v), r3(ki), min(tp, 2 * LANES), 0, tp)
            xp = _odd_out_ffn(xp, att.reshape(bp * tp, n_q), gbt, cin, None, cw, w_out, *ffn, tm_p, tp)
            ck_p.append(k.reshape(bp, tp, n_kv // D_HEAD, D_HEAD))
            cv_p.append(v.reshape(bp, tp, n_kv // D_HEAD, D_HEAD))
            ci_p.append(ki[:, :D_IDX].reshape(bp, tp, D_IDX))
            dc_p.append(cin.reshape(bp, tp, n_d)[:, tp - (CONV_W - 1):])
            q, k, v, qi, ki, wi, gbt, cin = _odd_in(xs, g_mix, w_in, hmean, gq, gk, tm_s)
            r3 = lambda a: a.reshape(bs, ts, a.shape[-1])
            past_ki = jnp.concatenate([cache_c_kidx[j], cache_c_kidx[j]], axis=-1)
            pad_q = lambda a: jnp.pad(r3(a), ((0, 0), (0, LANES - ts), (0, 0)))
            att = _dsa(pad_q(q), pad_q(qi), pad_q(wi), pad_keys(cache_c_k[j], k, DSA_CK),
                       pad_keys(cache_c_v[j], v, DSA_CK), pad_keys(past_ki, ki, DSA_CK), LANES, past, ls)[:, :ts]
            prev = jnp.concatenate([jnp.zeros((bs, 8 - (CONV_W - 1), n_d), F32), state_d_conv[j]], axis=1)
            xs = _odd_out_ffn(xs, att.reshape(bs * ts, n_q), gbt, cin, prev.reshape(bs * 8, n_d), cw, w_out, *ffn,
                              tm_s, ts)
            ck_s.append(k.reshape(bs, ts, n_kv // D_HEAD, D_HEAD))
            cv_s.append(v.reshape(bs, ts, n_kv // D_HEAD, D_HEAD))
            ci_s.append(ki[:, :D_IDX].reshape(bs, ts, D_IDX))
            dc_s.append(cin.reshape(bs, ts, n_d)[:, ts - (CONV_W - 1):])

    return (xp.reshape(bp, tp, d), xs.reshape(bs, ts, d),
            jnp.stack(ak_p), jnp.stack(av_p), jnp.stack(ak_s), jnp.stack(av_s),
            jnp.stack(bv_s),
            jnp.stack(ck_p), jnp.stack(cv_p), jnp.stack(ci_p),
            jnp.stack(ck_s), jnp.stack(cv_s), jnp.stack(ci_s),
            jnp.stack(dc_p), jnp.stack(dc_s))
```

```python
import functools

import numpy as np
import jax
import jax.numpy as jnp
from jax import lax
from jax.experimental import pallas as pl
from jax.experimental.pallas import tpu as pltpu

F32 = jnp.float32
BF16 = jnp.bfloat16

EPS = 1e-6
NEG = -1e30
LOG2_E = 1.4426950408889634
CHUNK = 64
CHUNK_SHIFT = 6
D_HEAD = 64
G_B = 4
B_CHUNK = 128
H_IDX = 4
D_IDX = 64
TOPK_MAX = 256
CONV_W = 3
LANES = 128
KEY_TILE = 128
VMEM_LIMIT = 56 * 1024 * 1024


def _cparams(sem):
    return pltpu.CompilerParams(dimension_semantics=sem, vmem_limit_bytes=VMEM_LIMIT)


def _resident(a):
    if isinstance(a, tuple):
        arr, layer = a
        shape = arr.shape[1:]
        return pl.BlockSpec((None,) + shape, lambda *_: (layer,) + (0,) * len(shape),
                            pipeline_mode=pl.Buffered(1))
    return pl.BlockSpec(a.shape, lambda *_: (0,) * a.ndim, pipeline_mode=pl.Buffered(1))


def _operand(a):
    return a[0] if isinstance(a, tuple) else a


def _dot(a, b):
    return jnp.dot(a, b, preferred_element_type=F32)


def _dot_t(a, b):
    return lax.dot_general(a, b, (((1,), (1,)), ((), ())), preferred_element_type=F32)


def _split(x):
    hi = x.astype(BF16)
    lo = (x - hi.astype(F32)).astype(BF16)
    return hi, lo


def _rms(x, g):
    return x * lax.rsqrt(jnp.mean(x * x, axis=-1, keepdims=True) + EPS) * g


def _head_rms(x, hmean, g):
    ms = _dot((x * x).astype(BF16), hmean)
    return x * lax.rsqrt(ms + EPS) * g


def _gelu(x):
    return jax.nn.gelu(x)


def _even_in_kernel(x_ref, g_ref, w_ref, hm_ref, gq_ref, gk_ref, gb_ref,
                    q_ref, k_ref, v_ref, u_ref, vb_ref, kb_ref, vm_ref):
    h = _rms(x_ref[...], g_ref[...]).astype(BF16)
    n = q_ref.shape[-1]

    def seg(i):
        return _dot(h, w_ref[:, i * n:(i + 1) * n])

    hm = hm_ref[...]
    q_ref[...] = _head_rms(seg(0), hm, gq_ref[...])
    k = _head_rms(seg(1), hm, gk_ref[...])
    k_ref[...] = k
    kb_ref[...] = k.astype(BF16)
    v = seg(2)
    v_ref[...] = v
    first = lax.broadcasted_iota(jnp.int32, (1, LANES), 1) < D_HEAD
    for p in range(n // LANES):
        vp = v[:, p * LANES:(p + 1) * LANES]
        vm_ref[:, 2 * p * LANES:(2 * p + 1) * LANES] = jnp.where(first, vp, 0.0).astype(BF16)
        vm_ref[:, (2 * p + 1) * LANES:(2 * p + 2) * LANES] = jnp.where(first, 0.0, vp).astype(BF16)
    u_ref[...] = _gelu(seg(3))
    vb_ref[...] = _rms(_gelu(seg(4)), gb_ref[...])


def _even_in(x, g, w, hm, gq, gk, gb, tm):
    n_rows, d = x.shape
    n = _operand(w).shape[-1] // 5
    row = lambda i: (i, 0)
    out = jax.ShapeDtypeStruct((n_rows, n), F32)
    return pl.pallas_call(
        _even_in_kernel,
        grid=(n_rows // tm,),
        in_specs=[pl.BlockSpec((tm, d), row), _resident(g), _resident(w), _resident(hm),
                  _resident(gq), _resident(gk), _resident(gb)],
        out_specs=[pl.BlockSpec((tm, n), row)] * 6 + [pl.BlockSpec((tm, 2 * n), row)],
        out_shape=[out] * 5 + [jax.ShapeDtypeStruct((n_rows, n), BF16), jax.ShapeDtypeStruct((n_rows, 2 * n), BF16)],
        compiler_params=_cparams(("parallel",)),
        name="even_in",
    )(x, g, _operand(w), hm, gq, gk, gb)


ODD_Q, ODD_K, ODD_V, ODD_QI, ODD_KI, ODD_WI, ODD_GB, ODD_GC, ODD_HD = (
    0, 512, 640, 768, 1024, 1152, 1280, 1792, 2304)
ODD_COLS = 2816


def _odd_in_kernel(x_ref, g_ref, w_ref, hm_ref, gq_ref, gk_ref,
                   q_ref, k_ref, v_ref, qi_ref, ki_ref, wi_ref, gb_ref, cin_ref):
    h = _rms(x_ref[...], g_ref[...]).astype(BF16)

    def seg(lo, width):
        return _dot(h, w_ref[:, lo:lo + width])

    hm = hm_ref[...]
    q_ref[...] = _head_rms(seg(ODD_Q, 512), hm, gq_ref[...])
    k_ref[...] = _head_rms(seg(ODD_K, 128), hm[:128, :128], gk_ref[...])
    v_ref[...] = seg(ODD_V, 128)
    qi_ref[...] = seg(ODD_QI, 256)
    ki_ref[...] = seg(ODD_KI, 128)
    wi_ref[...] = seg(ODD_WI, 128)
    gb_ref[...] = seg(ODD_GB, 512)
    cin_ref[...] = seg(ODD_GC, 512) * seg(ODD_HD, 512)


def _odd_in(x, g, w, hm, gq, gk, tm):
    n_rows, d = x.shape
    row = lambda i: (i, 0)
    widths = (512, 128, 128, 256, 128, 128, 512, 512)
    return pl.pallas_call(
        _odd_in_kernel,
        grid=(n_rows // tm,),
        in_specs=[pl.BlockSpec((tm, d), row), _resident(g), _resident(w), _resident(hm),
                  _resident(gq), _resident(gk)],
        out_specs=[pl.BlockSpec((tm, n), row) for n in widths],
        out_shape=[jax.ShapeDtypeStruct((n_rows, n), F32) for n in widths],
        compiler_params=_cparams(("parallel",)),
        name="odd_in",
    )(x, g, _operand(w), hm, gq, gk)


STICK_TK = 256
F32_EXP2_ZERO = -150.0


def _stick_tri():
    half = STICK_TK // 2
    later = (np.arange(half)[:, None] > np.arange(half)[None, :]).astype(np.float32)
    blk = np.concatenate([later, np.ones((half, half), np.float32)], axis=1)
    return jnp.asarray(np.concatenate([blk, blk], axis=0), BF16)


def _stick_kernel(q_ref, k_ref, v_ref, *rest, tq, q_pos0, has_new):
    if has_new:
        kn_ref, vn_ref, tri_ref, o_ref, carry_sc, acc_sc = rest
    else:
        tri_ref, o_ref, carry_sc, acc_sc = rest
    tk = STICK_TK
    half = tk // 2
    n_pairs = q_ref.shape[1] // LANES
    v_split = v_ref.shape[1] == 2 * q_ref.shape[1]
    q_start = q_pos0 + pl.program_id(1) * tq
    n_full = q_start // tk
    lane = lax.broadcasted_iota(jnp.int32, (1, LANES), 1)
    first = lane < D_HEAD
    q = q_ref[...] * (D_HEAD ** -0.5 * LOG2_E)
    q2 = []
    for p in range(n_pairs):
        qp = q[:, p * LANES:(p + 1) * LANES]
        q2.append(jnp.concatenate([jnp.where(first, qp, 0.0), jnp.where(first, 0.0, qp)], axis=0).astype(BF16))
    tri = tri_ref[...]
    carry_sc[...] = jnp.zeros_like(carry_sc)
    acc_sc[...] = jnp.zeros_like(acc_sc)

    def block(kb, masked):
        ks = pl.multiple_of(kb * tk, tk)
        if masked:
            row = lax.broadcasted_iota(jnp.int32, (2 * tq, tk), 0)
            qpos = q_start + jnp.where(row >= tq, row - tq, row)
            kpos = ks + lax.broadcasted_iota(jnp.int32, (2 * tq, tk), 1)
            ok = kpos < qpos
        ksrc, vsrc, rows = (kn_ref, vn_ref, pl.ds(0, tk)) if masked and has_new else (k_ref, v_ref, pl.ds(ks, tk))
        for p in range(n_pairs):
            cols = slice(p * LANES, (p + 1) * LANES)
            z = _dot_t(q2[p], ksrc[rows, cols].astype(BF16))
            nz = -z
            stay = jnp.minimum(nz, 0.0) - jnp.log2(1.0 + jnp.exp2(jnp.minimum(z, nz)))
            if masked:
                stay = jnp.where(ok, stay, 0.0)
            carry = carry_sc[p]
            after = [None, None]
            for h in (1, 0):
                hi, lo = _split(stay[:, h * half:(h + 1) * half])
                r = _dot(jnp.concatenate([hi, lo], axis=1), tri)
                after[h] = r[:, :half] + carry
                carry = carry + r[:, half:]
            carry_sc[p] = carry
            w = jnp.exp2(z + stay + jnp.concatenate(after, axis=1))
            if masked:
                w = jnp.where(ok, w, 0.0)
            w = w.astype(BF16)
            if v_split:
                v0 = vsrc[rows, 2 * p * LANES:(2 * p + 1) * LANES]
                v1 = vsrc[rows, (2 * p + 1) * LANES:(2 * p + 2) * LANES]
            else:
                vblk = vsrc[rows, cols]
                v0 = jnp.where(first, vblk, 0.0).astype(BF16)
                v1 = jnp.where(first, 0.0, vblk).astype(BF16)
            acc_sc[p] += _dot(jnp.concatenate([w[:tq], w[tq:]], axis=1), jnp.concatenate([v0, v1], axis=0))

    block(n_full, True)

    def any_weight_left():
        m = jnp.max(jnp.max(carry_sc[...], axis=0), axis=0, keepdims=True)
        return jnp.max(m, axis=1, keepdims=True)[0, 0] > F32_EXP2_ZERO

    def body(s):
        block(s[0], False)
        return s[0] - 1, any_weight_left()

    lax.while_loop(lambda s: (s[0] >= 0) & s[1], body, (n_full - 1, any_weight_left()))
    for p in range(n_pairs):
        o_ref[:, p * LANES:(p + 1) * LANES] = acc_sc[p]


def _stick(q, k, v, tri, tq, q_pos0, new=None):
    b, t, c = q.shape
    tk = STICK_TK
    has_new = new is not None
    kmap = lambda bi, i: (bi, 0, 0)
    if has_new:
        (k_arr, layer), (v_arr, _) = k, v
        lp = k_arr.shape[2]
        assert lp == q_pos0 and t == tq <= tk and new[0].shape[1] == tk
        past_map = lambda bi, i: (layer, bi, 0, 0)
        kv_specs = [pl.BlockSpec((None, None, lp, c), past_map), pl.BlockSpec((None, None, lp, v_arr.shape[3]), past_map),
                    pl.BlockSpec((None, tk, c), kmap), pl.BlockSpec((None, tk, c), kmap)]
        kv_args = (k_arr, v_arr) + tuple(new)
    else:
        lp = k.shape[1]
        assert q_pos0 + t <= lp
        kv_specs = [pl.BlockSpec((None, lp, c), kmap), pl.BlockSpec((None, lp, v.shape[2]), kmap)]
        kv_args = (k, v)
    assert lp % tk == 0 and q_pos0 % tq == 0 and tk % tq == 0
    kern = functools.partial(_stick_kernel, tq=tq, q_pos0=q_pos0, has_new=has_new)
    return pl.pallas_call(
        kern,
        grid=(b, t // tq),
        in_specs=[pl.BlockSpec((None, tq, c), lambda bi, i: (bi, i, 0))] + kv_specs
                 + [pl.BlockSpec(tri.shape, lambda bi, i: (0, 0))],
        out_specs=pl.BlockSpec((None, tq, c), lambda bi, i: (bi, i, 0)),
        out_shape=jax.ShapeDtypeStruct((b, t, c), F32),
        scratch_shapes=[pltpu.VMEM((c // LANES, 2 * tq, LANES), F32), pltpu.VMEM((c // LANES, tq, LANES), F32)],
        compiler_params=_cparams(("parallel", "arbitrary")),
        name="stick_attn",
    )(q, *kv_args, tri)


DSA_LEVEL = 512


def _dsa_rows_kernel(q_ref, qi_ref, wi_ref, k_ref, v_ref, ki_ref, o_ref, key_sc, bias_sc,
                     *, tq, l_valid, q_pos0, top_k, levels):
    kt = KEY_TILE
    th = min(tq, 128)
    parts = [slice(i * th, (i + 1) * th) for i in range(tq // th)]
    n_rep = q_ref.shape[1] // LANES
    lane = lax.broadcasted_iota(jnp.int32, (1, LANES), 1)
    first = lane < D_HEAD
    q_start = q_pos0 + pl.program_id(1) * tq
    kf = jnp.float32(top_k)

    def body(lv):
        n_kt = lv // kt

        def admissible(rows, lo, width):
            qpos = q_start + rows.start + lax.broadcasted_iota(jnp.int32, (th, 1), 0)
            kpos = lo + lax.broadcasted_iota(jnp.int32, (1, width), 1)
            return (jnp.right_shift(kpos, CHUNK_SHIFT) <= jnp.right_shift(qpos, CHUNK_SHIFT)) & (kpos < l_valid)

        ki_hi, ki_lo = _split(ki_ref[:lv, :])
        for rows in parts:
            qi = qi_ref[rows, :]
            wi = wi_ref[rows, :]
            qm = []
            for h in range(H_IDX):
                grp = qi[:, (h // 2) * LANES:(h // 2 + 1) * LANES]
                qm.append(jnp.where(first, grp, 0.0) if h % 2 == 0 else jnp.where(first, 0.0, grp))
            q_hi, q_lo = _split(jnp.concatenate(qm, axis=0))
            s = _dot_t(q_hi, ki_hi) + (_dot_t(q_hi, ki_lo) + _dot_t(q_lo, ki_hi))
            score = jnp.zeros((th, lv), F32)
            for h in range(H_IDX):
                score = score + jnp.maximum(s[h * th:(h + 1) * th], 0.0) * wi[:, h:h + 1]
            score = score * ((D_IDX ** -0.5) * (H_IDX ** -0.5)) + 0.0
            score = jnp.where(admissible(rows, 0, lv), score, NEG)
            bits = lax.bitcast_convert_type(score, jnp.int32)
            key_sc[rows, :lv] = jnp.where(bits < 0, bits ^ jnp.int32(0x7FFFFFFF), bits)

        def count_ge(rows, trial):
            part = jnp.zeros((th, kt), F32)
            for c in range(n_kt):
                part = part + jnp.where(key_sc[rows, c * kt:(c + 1) * kt] >= trial, 1.0, 0.0)
            return jnp.sum(part, axis=1, keepdims=True)

        int_min = jnp.int32(-2 ** 31)
        zero = jnp.zeros((th, 1), jnp.int32)
        thr0 = tuple(jnp.where(count_ge(rows, zero) >= kf, jnp.int32(0), int_min) for rows in parts)

        def bit_step(i, thr):
            bit = jnp.left_shift(jnp.int32(1), jnp.int32(30) - i)
            return tuple(jnp.where(count_ge(rows, t + bit) >= kf, t + bit, t) for rows, t in zip(parts, thr))

        thrs = lax.fori_loop(0, 31, bit_step, thr0)

        rr = lax.broadcasted_iota(jnp.int32, (kt, kt), 0)
        cc = lax.broadcasted_iota(jnp.int32, (kt, kt), 1)
        before = jnp.where(rr < cc, 1.0, 0.0).astype(BF16)
        for rows, thr in zip(parts, thrs):
            n_gt = jnp.zeros((th, kt), F32)
            for c in range(n_kt):
                n_gt = n_gt + jnp.where(key_sc[rows, c * kt:(c + 1) * kt] > thr, 1.0, 0.0)
            room = kf - jnp.sum(n_gt, axis=1, keepdims=True)
            run = jnp.zeros((th, 1), F32)
            for c in range(n_kt):
                keyc = key_sc[rows, c * kt:(c + 1) * kt]
                eqf = jnp.where(keyc == thr, 1.0, 0.0)
                rank = _dot(eqf.astype(BF16), before) + run
                tie_ok = jnp.where(keyc == thr, jnp.where(rank < room, 0.0, NEG), NEG)
                sel = jnp.where(keyc > thr, 0.0, tie_ok)
                bias_sc[rows, c * kt:(c + 1) * kt] = jnp.where(admissible(rows, c * kt, kt), sel, NEG)
                run = run + jnp.sum(eqf, axis=1, keepdims=True)

        kb = k_ref[:lv, :].astype(BF16)
        vf = v_ref[:lv, :]
        vg = [jnp.where(first, vf, 0.0).astype(BF16), jnp.where(first, 0.0, vf).astype(BF16)]
        for rows in parts:
            q = q_ref[rows, :] * (D_HEAD ** -0.5)
            bias = bias_sc[rows, :lv]
            out = jnp.zeros((n_rep * th, LANES), F32)
            for g in range(2):
                keep = first if g == 0 else jnp.logical_not(first)
                qs = jnp.concatenate(
                    [jnp.where(keep, q[:, r * LANES:(r + 1) * LANES], 0.0) for r in range(n_rep)],
                    axis=0).astype(BF16)
                logits = _dot_t(qs, kb)
                logits = (logits.reshape(n_rep, th, lv) + bias[None]).reshape(n_rep * th, lv)
                m = jnp.max(logits, axis=1, keepdims=True)
                p = jnp.exp(logits - m)
                den = jnp.sum(p, axis=1, keepdims=True)
                out = out + _dot(p.astype(BF16), vg[g]) / den
            for r in range(n_rep):
                o_ref[rows, r * LANES:(r + 1) * LANES] = out[r * th:(r + 1) * th]

    if len(levels) == 1:
        body(levels[0])
    else:
        cls = (q_start + tq - 1) // DSA_LEVEL
        for c, lv in enumerate(levels):
            pl.when(cls == c)(functools.partial(body, lv))


DSA_CK = 256


def _dsa_kernel(*refs, tq, l_valid, q_pos0, top_k, levels, has_new):
    if len(levels) == 1:
        _dsa_tile(*refs, tq=tq, l_valid=l_valid, q_pos0=q_pos0, top_k=top_k, n_chunks=levels[0], has_new=has_new)
        return
    assert not has_new
    q_last = q_pos0 + pl.program_id(1) * tq + tq - 1
    vis_end = jnp.left_shift(jnp.right_shift(q_last, CHUNK_SHIFT) + 1, CHUNK_SHIFT)
    step = levels[0] * DSA_CK
    for i, n in enumerate(levels):
        pl.when((vis_end + step - 1) // step == i + 1)(functools.partial(
            _dsa_tile, *refs, tq=tq, l_valid=l_valid, q_pos0=q_pos0, top_k=top_k, n_chunks=n, has_new=False))


def _dsa_tile(q_ref, qi_ref, wi_ref, k_ref, v_ref, ki_ref, *rest, tq, l_valid, q_pos0, top_k, n_chunks, has_new):
    if has_new:
        kn_ref, vn_ref, kin_ref, o_ref, kic_sc, kb_sc, vt_sc, key_sc, bias_sc = rest
    else:
        o_ref, kic_sc, kb_sc, vt_sc, key_sc, bias_sc = rest
    ck = DSA_CK
    n_cached = k_ref.shape[0] // ck
    lp = k_ref.shape[0] + (ck if has_new else 0)
    n_rep = q_ref.shape[1] // LANES
    lane = lax.broadcasted_iota(jnp.int32, (1, LANES), 1)
    first = lane < D_HEAD
    q_start = q_pos0 + pl.program_id(1) * tq
    tl = LANES
    parts = range(tq // tl)
    lanes_of = lambda pt: slice(pt * tl, (pt + 1) * tl)
    qchunk = [jnp.right_shift(q_start + pt * tl + lax.broadcasted_iota(jnp.int32, (1, tl), 1), CHUNK_SHIFT)
              for pt in parts]
    kf = jnp.float32(top_k)
    top_rows = lax.broadcasted_iota(jnp.int32, (LANES, 1), 0) < D_HEAD

    def rows_of(c):
        return pl.ds(c * ck, ck)

    def admissible(c, pt):
        kpos = c * ck + lax.broadcasted_iota(jnp.int32, (ck, 1), 0)
        return (jnp.right_shift(kpos, CHUNK_SHIFT) <= qchunk[pt]) & (kpos < l_valid)

    def chunks(n, step, carry):
        for c in range(n):
            carry = step(c, carry)
        return carry

    @pl.when(pl.program_id(1) == 0)
    def _():
        def prep(c, _):
            rows = rows_of(c)
            ksrc, vsrc, kisrc, src = (kn_ref, vn_ref, kin_ref, rows_of(0)) if c >= n_cached else (
                k_ref, v_ref, ki_ref, rows)
            ki2 = kisrc[src, :]
            if ki2.shape[1] == D_IDX:
                ki2 = jnp.concatenate([ki2, ki2], axis=1)
            hi, lo = _split(ki2)
            kic_sc[rows, :LANES] = jnp.where(first, hi, lo)
            kic_sc[rows, LANES:] = jnp.where(first, hi, jnp.zeros_like(hi))
            kb_sc[rows, :] = ksrc[src, :].astype(BF16)
            vt = vsrc[src, :].T
            vt_sc[c, :, :ck] = jnp.where(top_rows, vt, 0.0).astype(BF16)
            vt_sc[c, :, ck:] = jnp.where(top_rows, 0.0, vt).astype(BF16)
            return 0
        chunks(lp // ck, prep, 0)

    qic_t, wit = [], []
    for pt in parts:
        qi = qi_ref[lanes_of(pt), :]
        blocks = []
        for h in range(H_IDX):
            grp = qi[:, (h // 2) * LANES:(h // 2 + 1) * LANES]
            swapped = pltpu.roll(grp, D_IDX, 1)
            twice = jnp.where(first, grp, swapped) if h % 2 == 0 else jnp.where(first, swapped, grp)
            x3 = jnp.concatenate([twice, jnp.where(first, twice, 0.0)], axis=1)
            hi = x3.astype(BF16).astype(F32)
            blocks.append(jnp.concatenate([hi[:, :LANES], (x3 - hi)[:, LANES:]], axis=1))
        qic_t.append(jnp.concatenate(blocks, axis=0).T.astype(BF16))
        wit.append(wi_ref[lanes_of(pt), :].T)

    def score_chunk(c, _):
        for pt in parts:
            s = _dot(kic_sc[rows_of(c), :], qic_t[pt])
            score = jnp.zeros((ck, tl), F32)
            for h in range(H_IDX):
                score = score + jnp.maximum(s[:, h * tl:(h + 1) * tl], 0.0) * wit[pt][h:h + 1, :]
            score = score * ((D_IDX ** -0.5) * (H_IDX ** -0.5)) + 0.0
            score = jnp.where(admissible(c, pt), score, NEG)
            bits = lax.bitcast_convert_type(score, jnp.int32)
            key_sc[rows_of(c), lanes_of(pt)] = jnp.where(bits < 0, bits ^ jnp.int32(0x7FFFFFFF), bits)
        return 0

    chunks(n_chunks, score_chunk, 0)

    def count(pred, trials):
        def body(c, accs):
            out = []
            for pt in parts:
                hit = jnp.where(pred(key_sc[rows_of(c), lanes_of(pt)], trials[pt]), 1.0, 0.0)
                out.append(accs[pt] + jnp.sum(hit.reshape(ck // 8, 8, tl), axis=0))
            return out
        accs = chunks(n_chunks, body, [jnp.zeros((8, tl), F32) for _ in parts])
        return [jnp.sum(a, axis=0, keepdims=True) for a in accs]

    ge = lambda a, b: a >= b
    above = count(ge, [jnp.zeros((1, tl), jnp.int32) for _ in parts])
    thr0 = tuple(jnp.where(n >= kf, jnp.int32(0), jnp.int32(-2 ** 31)) for n in above)

    def bit_step(i, thr):
        bit = jnp.left_shift(jnp.int32(1), jnp.int32(30) - i)
        trials = [t + bit for t in thr]
        return tuple(jnp.where(n >= kf, tr, t) for n, tr, t in zip(count(ge, trials), trials, thr))

    thr = lax.fori_loop(0, 31, bit_step, thr0)

    room = [kf - n for n in count(lambda a, b: a > b, thr)]
    rr = lax.broadcasted_iota(jnp.int32, (ck, ck), 0)
    cc = lax.broadcasted_iota(jnp.int32, (ck, ck), 1)
    earlier = jnp.where(cc < rr, 1.0, 0.0).astype(BF16)

    def tie_chunk(c, runs):
        out = []
        for pt in parts:
            keyc = key_sc[rows_of(c), lanes_of(pt)]
            eqf = jnp.where(keyc == thr[pt], 1.0, 0.0)
            rank = _dot(earlier, eqf.astype(BF16)) + runs[pt]
            tie_ok = jnp.where(keyc == thr[pt], jnp.where(rank < room[pt], 0.0, NEG), NEG)
            sel = jnp.where(keyc > thr[pt], 0.0, tie_ok)
            bias_sc[rows_of(c), lanes_of(pt)] = jnp.where(admissible(c, pt), sel, NEG)
            out.append(runs[pt] + jnp.sum(eqf, axis=0, keepdims=True))
        return out

    chunks(n_chunks, tie_chunk, [jnp.zeros((1, tl), F32) for _ in parts])

    qs_t = []
    for pt in parts:
        q = q_ref[lanes_of(pt), :] * (D_HEAD ** -0.5 * LOG2_E)
        per_group = []
        for g in range(2):
            keep = first if g == 0 else jnp.logical_not(first)
            per_group.append(jnp.concatenate(
                [jnp.where(keep, q[:, r * LANES:(r + 1) * LANES], 0.0) for r in range(n_rep)],
                axis=0).T.astype(BF16))
        qs_t.append(per_group)
    width = n_rep * tl

    def att_chunk(c, carries):
        kb = kb_sc[rows_of(c), :]
        out = []
        for pt in parts:
            m0, m1, d0, d1, acc = carries[pt]
            bias = bias_sc[rows_of(c), lanes_of(pt)]
            bias = jnp.concatenate([bias] * n_rep, axis=1)
            l0 = _dot(kb, qs_t[pt][0]) + bias
            l1 = _dot(kb, qs_t[pt][1]) + bias
            n0 = jnp.maximum(m0, jnp.max(l0, axis=0, keepdims=True))
            n1 = jnp.maximum(m1, jnp.max(l1, axis=0, keepdims=True))
            a0, a1 = jnp.exp2(m0 - n0), jnp.exp2(m1 - n1)
            p0, p1 = jnp.exp2(l0 - n0), jnp.exp2(l1 - n1)
            d0 = d0 * a0 + jnp.sum(p0, axis=0, keepdims=True)
            d1 = d1 * a1 + jnp.sum(p1, axis=0, keepdims=True)
            pv = _dot(vt_sc[c], jnp.concatenate([p0.astype(BF16), p1.astype(BF16)], axis=0))
            out.append((n0, n1, d0, d1, acc * jnp.where(top_rows, a0, a1) + pv))
        return out

    neg = jnp.full((1, width), NEG, F32)
    zero = jnp.zeros((1, width), F32)
    done = chunks(n_chunks, att_chunk, [(neg, neg, zero, zero, jnp.zeros((LANES, width), F32)) for _ in parts])
    for pt in parts:
        _, _, d0, d1, acc = done[pt]
        out_t = acc / jnp.where(top_rows, d0, d1)
        for r in range(n_rep):
            o_ref[lanes_of(pt), r * LANES:(r + 1) * LANES] = out_t[:, r * tl:(r + 1) * tl].T


def _dsa(q, qi, wi, k, v, ki, tq, q_pos0, l_valid, new=None):
    b, t, c = q.shape
    has_new = new is not None
    qmap = lambda bi, i: (bi, i, 0)
    kmap = lambda bi, i: (bi, 0, 0)
    if has_new:
        layer = k[1]
        past = k[0].shape[2]
        lp = past + DSA_CK
        assert past == q_pos0 and past % DSA_CK == 0
        past_map = lambda bi, i: (layer, bi, 0, 0)
        key_specs = [pl.BlockSpec((None, None, past, a[0].shape[3]), past_map) for a in (k, v, ki)]
        key_specs += [pl.BlockSpec((None, DSA_CK, LANES), kmap)] * 3
        key_args = (k[0], v[0], ki[0]) + tuple(new)
    else:
        lp = k.shape[1]
        key_specs = [pl.BlockSpec((None, lp, LANES), kmap)] * 3
        key_args = (k, v, ki)
    assert lp % DSA_CK == 0 and l_valid <= lp and t % tq == 0 and tq % LANES == 0
    top_k = min(TOPK_MAX, l_valid // 4)
    assert DSA_CK >= top_k
    n_all = lp // DSA_CK
    if q_pos0 == 0 and n_all % 4 == 0:
        levels = tuple(range(n_all // 4, n_all + 1, n_all // 4))
    else:
        levels = (n_all,)
    kern = functools.partial(_dsa_kernel, tq=tq, l_valid=l_valid, q_pos0=q_pos0, top_k=top_k, levels=levels,
                             has_new=has_new)
    return pl.pallas_call(
        kern,
        grid=(b, t // tq),
        in_specs=[pl.BlockSpec((None, tq, c), qmap), pl.BlockSpec((None, tq, qi.shape[2]), qmap),
                  pl.BlockSpec((None, tq, LANES), qmap)] + key_specs,
        out_specs=pl.BlockSpec((None, tq, c), qmap),
        out_shape=jax.ShapeDtypeStruct((b, t, c), F32),
        scratch_shapes=[pltpu.VMEM((lp, 2 * LANES), BF16), pltpu.VMEM((lp, LANES), BF16),
                        pltpu.VMEM((lp // DSA_CK, LANES, 2 * DSA_CK), BF16), pltpu.VMEM((lp, tq), jnp.int32),
                        pltpu.VMEM((lp, tq), F32)],
        compiler_params=_cparams(("parallel", "arbitrary")),
        name="dsa_attn",
    )(q, qi, wi, *key_args)


def _dsa_rows(q, qi, wi, k, v, ki, tq, q_pos0, l_valid):
    b, t, c = q.shape
    lp = k.shape[1]
    assert lp % KEY_TILE == 0 and l_valid <= lp and t % tq == 0
    top_k = min(TOPK_MAX, l_valid // 4)
    if q_pos0 == 0 and lp % DSA_LEVEL == 0 and tq % CHUNK == 0 and DSA_LEVEL >= top_k:
        levels = tuple(range(DSA_LEVEL, lp + 1, DSA_LEVEL))
    else:
        levels = (lp,)
    kern = functools.partial(_dsa_rows_kernel, tq=tq, l_valid=l_valid, q_pos0=q_pos0, top_k=top_k, levels=levels)
    qmap = lambda bi, i: (bi, i, 0)
    kmap = lambda bi, i: (bi, 0, 0)
    return pl.pallas_call(
        kern,
        grid=(b, t // tq),
        in_specs=[pl.BlockSpec((None, tq, c), qmap), pl.BlockSpec((None, tq, qi.shape[2]), qmap),
                  pl.BlockSpec((None, tq, LANES), qmap),
                  pl.BlockSpec((None, lp, LANES), kmap), pl.BlockSpec((None, lp, LANES), kmap),
                  pl.BlockSpec((None, lp, LANES), kmap)],
        out_specs=pl.BlockSpec((None, tq, c), qmap),
        out_shape=jax.ShapeDtypeStruct((b, t, c), F32),
        scratch_shapes=[pltpu.VMEM((tq, lp), jnp.int32), pltpu.VMEM((tq, lp), F32)],
        compiler_params=_cparams(("parallel", "arbitrary")),
        name="dsa_attn",
    )(q, qi, wi, k, v, ki)


FFN_CHUNK = 1024


def _ffn_tail(x1, gf_ref, w1_ref, w2_ref, o_ref):
    h = _rms(x1, gf_ref[...]).astype(BF16)
    acc = None
    for c in range(w1_ref.shape[1] // FFN_CHUNK):
        cols = slice(c * FFN_CHUNK, (c + 1) * FFN_CHUNK)
        a = jnp.square(jnp.maximum(_dot(h, w1_ref[:, cols]), 0.0))
        part = _dot(a.astype(BF16), w2_ref[cols, :])
        acc = part if acc is None else acc + part
    o_ref[...] = x1 + acc


def _even_out_kernel(x_ref, att_ref, u_ref, vb_ref, ws_ref, bst_ref, w_ref, gf_ref, w1_ref, w2_ref, o_ref,
                     *, p_len):
    tm = x_ref.shape[0]
    cg = vb_ref.shape[1] // G_B
    rr = lax.broadcasted_iota(jnp.int32, (p_len, p_len), 0)
    cc = lax.broadcasted_iota(jnp.int32, (p_len, p_len), 1)
    causal = jnp.right_shift(cc, CHUNK_SHIFT) <= jnp.right_shift(rr, CHUNK_SHIFT)
    u = u_ref[...]
    vb = vb_ref[...].astype(BF16)
    bst = bst_ref[...]
    mixed_rows = []
    for c in range(tm // p_len):
        cols = []
        for g in range(G_B):
            wg = jnp.where(causal, ws_ref[g], 0.0).astype(BF16)
            mg = _dot(wg, vb[c * p_len:(c + 1) * p_len, g * cg:(g + 1) * cg]) + bst[:, g:g + 1]
            cols.append(mg)
        mixed_rows.append(jnp.concatenate(cols, axis=1))
    gated = u * jnp.concatenate(mixed_rows, axis=0)
    na = att_ref.shape[1]
    y = _dot(att_ref[...].astype(BF16), w_ref[:na, :]) + _dot(gated.astype(BF16), w_ref[na:, :])
    _ffn_tail(x_ref[...] + y, gf_ref, w1_ref, w2_ref, o_ref)


def _even_out_ffn(x, att, u, vb, ws, bst, w, gf, w1, w2, tm, p_len):
    n_rows, d = x.shape
    n = att.shape[1]
    row = lambda i: (i, 0)
    kern = functools.partial(_even_out_kernel, p_len=p_len)
    return pl.pallas_call(
        kern,
        grid=(n_rows // tm,),
        in_specs=[pl.BlockSpec((tm, d), row), pl.BlockSpec((tm, n), row), pl.BlockSpec((tm, n), row),
                  pl.BlockSpec((tm, n), row), _resident(ws), _resident(bst), _resident(w),
                  _resident(gf), _resident(w1), _resident(w2)],
        out_specs=pl.BlockSpec((tm, d), row),
        out_shape=jax.ShapeDtypeStruct((n_rows, d), F32),
        compiler_params=_cparams(("parallel",)),
        name="even_out_ffn",
    )(x, att, u, vb, ws, bst, _operand(w), gf, _operand(w1), _operand(w2))


def _odd_out_kernel(x_ref, att_ref, gb_ref, cin_ref, prev_ref, cw_ref, w_ref, gf_ref, w1_ref, w2_ref, o_ref,
                    *, t_len, has_prev):
    tm = x_ref.shape[0]
    prev = prev_ref[...]
    if not has_prev:
        start = (pl.program_id(0) * tm) % t_len == 0
        prev = jnp.where(start, 0.0, prev)
    cw = cw_ref[...]
    seg = t_len if has_prev else tm
    convs = []
    for s in range(tm // seg):
        cin = cin_ref[s * seg:(s + 1) * seg, :]
        ext = jnp.concatenate([prev[8 * s:8 * (s + 1)], cin], axis=0)
        conv = cw[CONV_W - 1:CONV_W] * cin
        for j in range(1, CONV_W):
            conv = conv + cw[CONV_W - 1 - j:CONV_W - j] * pltpu.roll(ext, j, 0)[8:]
        convs.append(conv)
    gated = gb_ref[...] * (convs[0] if len(convs) == 1 else jnp.concatenate(convs, axis=0))
    na = att_ref.shape[1]
    y = _dot(att_ref[...].astype(BF16), w_ref[:na, :]) + _dot(gated.astype(BF16), w_ref[na:, :])
    _ffn_tail(x_ref[...] + y, gf_ref, w1_ref, w2_ref, o_ref)


def _odd_out_ffn(x, att, gb, cin, prev, cw, w, gf, w1, w2, tm, t_len):
    n_rows, d = x.shape
    n = att.shape[1]
    row = lambda i: (i, 0)
    has_prev = prev is not None
    if has_prev:
        assert tm % t_len == 0 and t_len % 8 == 0
        prev_arr, prev_map, prev_rows = prev, row, 8 * (tm // t_len)
    else:
        assert t_len % tm == 0 and tm % 8 == 0
        prev_arr, prev_map, prev_rows = cin, (lambda i: (jnp.maximum(i * (tm // 8) - 1, 0), 0)), 8
    kern = functools.partial(_odd_out_kernel, t_len=t_len, has_prev=has_prev)
    return pl.pallas_call(
        kern,
        grid=(n_rows // tm,),
        in_specs=[pl.BlockSpec((tm, d), row), pl.BlockSpec((tm, n), row), pl.BlockSpec((tm, n), row),
                  pl.BlockSpec((tm, n), row), pl.BlockSpec((prev_rows, n), prev_map),
                  _resident(cw), _resident(w), _resident(gf), _resident(w1), _resident(w2)],
        out_specs=pl.BlockSpec((tm, d), row),
        out_shape=jax.ShapeDtypeStruct((n_rows, d), F32),
        compiler_params=_cparams(("parallel",)),
        name="odd_out_ffn",
    )(x, att, gb, cin, prev_arr, cw, _operand(w), gf, _operand(w1), _operand(w2))


def _round_up(n, m):
    return (n + m - 1) // m * m


def _row_tile(n_rows, cap):
    t = min(cap, n_rows)
    while n_rows % t:
        t //= 2
    return t


def _odd_weight_layout(n_q):
    n_heads = n_q // D_HEAD
    rep = n_heads // 2
    perm = np.zeros((n_q,), np.int32)
    for r in range(rep):
        for g in range(2):
            for dd in range(D_HEAD):
                perm[r * LANES + g * D_HEAD + dd] = (g * rep + r) * D_HEAD + dd
    return perm


def kernel(x_prompt, x_sample, cache_a_k, cache_a_v, cache_c_k, cache_c_v, cache_c_kidx, state_d_conv,
           norm_mix_g, norm_ffn_g, w_in_even, gq_a, gk_a, g_b, ws_b, bs_b, w_out_even,
           w_in_odd, gq_c, gk_c, conv_w_d, w_out_odd, w_ffn1, w_ffn2):
    bp, tp, d = x_prompt.shape
    bs, ts, _ = x_sample.shape
    depth = norm_mix_g.shape[0]
    past = cache_a_k.shape[2]
    n_a = cache_a_k.shape[3] * cache_a_k.shape[4]
    n_kv = cache_c_k.shape[3] * cache_c_k.shape[4]
    n_b = g_b.shape[1]
    n_d = conv_w_d.shape[2]
    n_q = w_out_odd.shape[1] - n_d
    assert n_a == 512 and n_b == 512 and n_kv == 128 and n_q == 512 and n_d == 512 and d == 1024

    xp = x_prompt.reshape(bp * tp, d)
    xs = x_sample.reshape(bs * ts, d)
    tm_p = _row_tile(bp * tp, 512)
    tm_s = _row_tile(bs * ts, 512)
    ls = past + ts
    ls_pad = _round_up(ls, KEY_TILE)

    hmean = jnp.asarray(np.kron(np.eye(n_a // D_HEAD), np.full((D_HEAD, D_HEAD), 1.0 / D_HEAD)), BF16)
    perm = _odd_weight_layout(n_q)
    tile8 = lambda gvec: jnp.tile(gvec, n_a // D_HEAD).reshape(1, n_a)

    ak_p, av_p, ak_s, av_s, bv_s = [], [], [], [], []
    ck_p, cv_p, ci_p, ck_s, cv_s, ci_s, dc_p, dc_s = [], [], [], [], [], [], [], []

    tri = _stick_tri()

    def pad_keys(past_rows, new_rows, multiple=KEY_TILE):
        width = new_rows.shape[-1]
        zeros = jnp.zeros((bs, _round_up(ls, multiple) - ls, width), F32)
        return jnp.concatenate([past_rows.reshape(bs, past, width), new_rows.reshape(bs, ts, width), zeros], axis=1)

    past_a_k = cache_a_k.reshape(cache_a_k.shape[0], bs, past, n_a)
    past_a_v = cache_a_v.reshape(cache_a_v.shape[0], bs, past, n_a)
    past_c_k = cache_c_k.reshape(cache_c_k.shape[0], bs, past, n_kv)
    past_c_v = cache_c_v.reshape(cache_c_v.shape[0], bs, past, n_kv)

    w1_all, w2_all = w_ffn1.astype(BF16), w_ffn2.astype(BF16)
    w_in_even_all, w_out_even_all = w_in_even.astype(BF16), w_out_even.astype(BF16)
    offs = np.cumsum([0, n_q, n_kv, n_kv, H_IDX * D_IDX, D_IDX, H_IDX, n_d, n_d, n_d])
    col = lambda s: w_in_odd[:, :, int(offs[s]):int(offs[s + 1])]
    w_in_odd_all = jnp.concatenate(
        [col(0)[:, :, perm], col(1), col(2), col(3), col(4), col(4), col(5),
         jnp.zeros((w_in_odd.shape[0], d, LANES - H_IDX), F32), col(6), col(7), col(8)], axis=2).astype(BF16)
    assert w_in_odd_all.shape[2] == ODD_COLS
    w_out_odd_all = jnp.concatenate([w_out_odd[:, :n_q][:, perm], w_out_odd[:, n_q:]], axis=1).astype(BF16)

    for i in range(depth):
        j = i // 2
        g_mix = norm_mix_g[i].reshape(1, d)
        ffn = (norm_ffn_g[i].reshape(1, d), (w1_all, i), (w2_all, i))
        if i % 2 == 0:
            w_in = (w_in_even_all, j)
            w_out = (w_out_even_all, j)
            gq, gk, gb = tile8(gq_a[j]), tile8(gk_a[j]), g_b[j].reshape(1, n_b)
            q, k, v, u, vb, kb, vm = _even_in(xp, g_mix, w_in, hmean, gq, gk, gb, tm_p)
            att = _stick(q.reshape(bp, tp, n_a), kb.reshape(bp, tp, n_a), vm.reshape(bp, tp, 2 * n_a), tri,
                         min(tp, KEY_TILE), 0)
            xp = _even_out_ffn(xp, att.reshape(bp * tp, n_a), u, vb, ws_b[j], bs_b[j].T, w_out, *ffn, tm_p, B_CHUNK)
            ak_p.append(k.reshape(bp, tp, n_a // D_HEAD, D_HEAD))
            av_p.append(v.reshape(bp, tp, n_a // D_HEAD, D_HEAD))
            q, k, v, u, vb, _, _ = _even_in(xs, g_mix, w_in, hmean, gq, gk, gb, tm_s)
            new_rows = lambda a: jnp.pad(a.reshape(bs, ts, n_a), ((0, 0), (0, STICK_TK - ts), (0, 0)))
            att = _stick(q.reshape(bs, ts, n_a), (past_a_k, j), (past_a_v, j), tri, ts, past,
                         new=(new_rows(k), new_rows(v)))
            xs = _even_out_ffn(xs, att.reshape(bs * ts, n_a), u, vb, ws_b[j][:, :ts, :ts], bs_b[j][:, :ts].T,
                               w_out, *ffn, tm_s, ts)
            ak_s.append(k.reshape(bs, ts, n_a // D_HEAD, D_HEAD))
            av_s.append(v.reshape(bs, ts, n_a // D_HEAD, D_HEAD))
            bv_s.append(vb.reshape(bs, ts, n_b))
        else:
            w_in = (w_in_odd_all, j)
            w_out = (w_out_odd_all, j)
            gq, gk = tile8(gq_c[j]), jnp.tile(gk_c[j], n_kv // D_HEAD).reshape(1, n_kv)
            cw = conv_w_d[j]
            q, k, v, qi, ki, wi, gbt, cin = _odd_in(xp, g_mix, w_in, hmean, gq, gk, tm_p)
            r3 = lambda a: a.reshape(bp, tp, a.shape[-1])
            att = _dsa(r3(q), r3(qi), r3(wi), r3(k), r3(v), r3(ki), min(tp, 2 * LANES), 0, tp)
            xp = _odd_out_ffn(xp, att.reshape(bp * tp, n_q), gbt, cin, None, cw, w_out, *ffn, tm_p, tp)
            ck_p.append(k.reshape(bp, tp, n_kv // D_HEAD, D_HEAD))
            cv_p.append(v.reshape(bp, tp, n_kv // D_HEAD, D_HEAD))
            ci_p.append(ki[:, :D_IDX].reshape(bp, tp, D_IDX))
            dc_p.append(cin.reshape(bp, tp, n_d)[:, tp - (CONV_W - 1):])
            q, k, v, qi, ki, wi, gbt, cin = _odd_in(xs, g_mix, w_in, hmean, gq, gk, tm_s)
            r3 = lambda a: a.reshape(bs, ts, a.shape[-1])
            pad_q = lambda a: jnp.pad(r3(a), ((0, 0), (0, LANES - ts), (0, 0)))
            new_rows = lambda a: jnp.pad(r3(a), ((0, 0), (0, DSA_CK - ts), (0, 0)))
            att = _dsa(pad_q(q), pad_q(qi), pad_q(wi), (past_c_k, j), (past_c_v, j), (cache_c_kidx, j),
                       LANES, past, ls, new=(new_rows(k), new_rows(v), new_rows(ki)))[:, :ts]
            prev = jnp.concatenate([jnp.zeros((bs, 8 - (CONV_W - 1), n_d), F32), state_d_conv[j]], axis=1)
            xs = _odd_out_ffn(xs, att.reshape(bs * ts, n_q), gbt, cin, prev.reshape(bs * 8, n_d), cw, w_out, *ffn,
                              tm_s, ts)
            ck_s.append(k.reshape(bs, ts, n_kv // D_HEAD, D_HEAD))
            cv_s.append(v.reshape(bs, ts, n_kv // D_HEAD, D_HEAD))
            ci_s.append(ki[:, :D_IDX].reshape(bs, ts, D_IDX))
            dc_s.append(cin.reshape(bs, ts, n_d)[:, ts - (CONV_W - 1):])

    return (xp.reshape(bp, tp, d), xs.reshape(bs, ts, d),
            jnp.stack(ak_p), jnp.stack(av_p), jnp.stack(ak_s), jnp.stack(av_s),
            jnp.stack(bv_s),
            jnp.stack(ck_p), jnp.stack(cv_p), jnp.stack(ci_p),
            jnp.stack(ck_s), jnp.stack(cv_s), jnp.stack(ci_s),
            jnp.stack(dc_p), jnp.stack(dc_s))
```

```python
import functools

import numpy as np
import jax
import jax.numpy as jnp
from jax import lax
from jax.experimental import pallas as pl
from jax.experimental.pallas import tpu as pltpu

F32 = jnp.float32
BF16 = jnp.bfloat16

EPS = 1e-6
NEG = -1e30
LOG2_E = 1.4426950408889634
CHUNK = 64
CHUNK_SHIFT = 6
D_HEAD = 64
G_B = 4
B_CHUNK = 128
H_IDX = 4
D_IDX = 64
TOPK_MAX = 256
CONV_W = 3
LANES = 128
KEY_TILE = 128
VMEM_LIMIT = 56 * 1024 * 1024


def _cparams(sem):
    return pltpu.CompilerParams(dimension_semantics=sem, vmem_limit_bytes=VMEM_LIMIT)


def _resident(a):
    if isinstance(a, tuple):
        arr, layer = a
        shape = arr.shape[1:]
        return pl.BlockSpec((None,) + shape, lambda *_: (layer,) + (0,) * len(shape),
                            pipeline_mode=pl.Buffered(1))
    return pl.BlockSpec(a.shape, lambda *_: (0,) * a.ndim, pipeline_mode=pl.Buffered(1))


def _operand(a):
    return a[0] if isinstance(a, tuple) else a


def _dot(a, b):
    return jnp.dot(a, b, preferred_element_type=F32)


def _dot_t(a, b):
    return lax.dot_general(a, b, (((1,), (1,)), ((), ())), preferred_element_type=F32)


def _split(x):
    hi = x.astype(BF16)
    lo = (x - hi.astype(F32)).astype(BF16)
    return hi, lo


def _rms(x, g):
    return x * lax.rsqrt(jnp.mean(x * x, axis=-1, keepdims=True) + EPS) * g


def _head_rms(x, hmean, g):
    ms = _dot((x * x).astype(BF16), hmean)
    return x * lax.rsqrt(ms + EPS) * g


def _gelu(x):
    return jax.nn.gelu(x)


def _even_in_kernel(x_ref, g_ref, w_ref, hm_ref, gq_ref, gk_ref, gb_ref,
                    q_ref, k_ref, v_ref, u_ref, vb_ref, kb_ref, vm_ref):
    h = _rms(x_ref[...], g_ref[...]).astype(BF16)
    n = q_ref.shape[-1]

    def seg(i):
        return _dot(h, w_ref[:, i * n:(i + 1) * n])

    hm = hm_ref[...]
    q_ref[...] = _head_rms(seg(0), hm, gq_ref[...])
    k = _head_rms(seg(1), hm, gk_ref[...])
    k_ref[...] = k
    kb_ref[...] = k.astype(BF16)
    v = seg(2)
    v_ref[...] = v
    first = lax.broadcasted_iota(jnp.int32, (1, LANES), 1) < D_HEAD
    for p in range(n // LANES):
        vp = v[:, p * LANES:(p + 1) * LANES]
        vm_ref[:, 2 * p * LANES:(2 * p + 1) * LANES] = jnp.where(first, vp, 0.0).astype(BF16)
        vm_ref[:, (2 * p + 1) * LANES:(2 * p + 2) * LANES] = jnp.where(first, 0.0, vp).astype(BF16)
    u_ref[...] = _gelu(seg(3))
    vb_ref[...] = _rms(_gelu(seg(4)), gb_ref[...])


def _even_in(x, g, w, hm, gq, gk, gb, tm):
    n_rows, d = x.shape
    n = _operand(w).shape[-1] // 5
    row = lambda i: (i, 0)
    out = jax.ShapeDtypeStruct((n_rows, n), F32)
    return pl.pallas_call(
        _even_in_kernel,
        grid=(n_rows // tm,),
        in_specs=[pl.BlockSpec((tm, d), row), _resident(g), _resident(w), _resident(hm),
                  _resident(gq), _resident(gk), _resident(gb)],
        out_specs=[pl.BlockSpec((tm, n), row)] * 6 + [pl.BlockSpec((tm, 2 * n), row)],
        out_shape=[out] * 5 + [jax.ShapeDtypeStruct((n_rows, n), BF16), jax.ShapeDtypeStruct((n_rows, 2 * n), BF16)],
        compiler_params=_cparams(("parallel",)),
        name="even_in",
    )(x, g, _operand(w), hm, gq, gk, gb)


ODD_Q, ODD_K, ODD_V, ODD_QI, ODD_KI, ODD_WI, ODD_GB, ODD_GC, ODD_HD = (
    0, 512, 640, 768, 1024, 1152, 1280, 1792, 2304)
ODD_COLS = 2816


def _odd_in_kernel(x_ref, g_ref, w_ref, hm_ref, gq_ref, gk_ref,
                   q_ref, k_ref, v_ref, qi_ref, ki_ref, wi_ref, gb_ref, cin_ref):
    h = _rms(x_ref[...], g_ref[...]).astype(BF16)

    def seg(lo, width):
        return _dot(h, w_ref[:, lo:lo + width])

    hm = hm_ref[...]
    q_ref[...] = _head_rms(seg(ODD_Q, 512), hm, gq_ref[...])
    k_ref[...] = _head_rms(seg(ODD_K, 128), hm[:128, :128], gk_ref[...])
    v_ref[...] = seg(ODD_V, 128)
    qi_ref[...] = seg(ODD_QI, 256)
    ki_ref[...] = seg(ODD_KI, 128)
    wi_ref[...] = seg(ODD_WI, 128)
    gb_ref[...] = seg(ODD_GB, 512)
    cin_ref[...] = seg(ODD_GC, 512) * seg(ODD_HD, 512)


def _odd_in(x, g, w, hm, gq, gk, tm):
    n_rows, d = x.shape
    row = lambda i: (i, 0)
    widths = (512, 128, 128, 256, 128, 128, 512, 512)
    return pl.pallas_call(
        _odd_in_kernel,
        grid=(n_rows // tm,),
        in_specs=[pl.BlockSpec((tm, d), row), _resident(g), _resident(w), _resident(hm),
                  _resident(gq), _resident(gk)],
        out_specs=[pl.BlockSpec((tm, n), row) for n in widths],
        out_shape=[jax.ShapeDtypeStruct((n_rows, n), F32) for n in widths],
        compiler_params=_cparams(("parallel",)),
        name="odd_in",
    )(x, g, _operand(w), hm, gq, gk)


STICK_TK = 256
F32_EXP2_ZERO = -150.0


def _stick_tri():
    half = STICK_TK // 2
    later = (np.arange(half)[:, None] > np.arange(half)[None, :]).astype(np.float32)
    blk = np.concatenate([later, np.ones((half, half), np.float32)], axis=1)
    return jnp.asarray(np.concatenate([blk, blk], axis=0), BF16)


def _stick_kernel(q_ref, k_ref, v_ref, *rest, tq, q_pos0, has_new):
    if has_new:
        kn_ref, vn_ref, tri_ref, o_ref, carry_sc, acc_sc = rest
    else:
        tri_ref, o_ref, carry_sc, acc_sc = rest
    tk = STICK_TK
    half = tk // 2
    n_pairs = q_ref.shape[1] // LANES
    v_split = not has_new and v_ref.shape[1] == 2 * q_ref.shape[1]
    top_rows = lax.broadcasted_iota(jnp.int32, (LANES, 1), 0) < D_HEAD
    q_start = q_pos0 + pl.program_id(1) * tq
    n_full = q_start // tk
    lane = lax.broadcasted_iota(jnp.int32, (1, LANES), 1)
    first = lane < D_HEAD
    q = q_ref[...] * (D_HEAD ** -0.5 * LOG2_E)
    q2 = []
    for p in range(n_pairs):
        qp = q[:, p * LANES:(p + 1) * LANES]
        q2.append(jnp.concatenate([jnp.where(first, qp, 0.0), jnp.where(first, 0.0, qp)], axis=0).astype(BF16))
    tri = tri_ref[...]
    carry_sc[...] = jnp.zeros_like(carry_sc)
    acc_sc[...] = jnp.zeros_like(acc_sc)

    def block(kb, masked):
        channel_major = has_new and not masked
        ks = kb * tk if channel_major else pl.multiple_of(kb * tk, tk)
        if masked:
            row = lax.broadcasted_iota(jnp.int32, (2 * tq, tk), 0)
            qpos = q_start + jnp.where(row >= tq, row - tq, row)
            kpos = ks + lax.broadcasted_iota(jnp.int32, (2 * tq, tk), 1)
            ok = kpos < qpos
        ksrc, vsrc, rows = (kn_ref, vn_ref, pl.ds(0, tk)) if masked and has_new else (k_ref, v_ref, pl.ds(ks, tk))
        for p in range(n_pairs):
            cols = slice(p * LANES, (p + 1) * LANES)
            if channel_major:
                z = _dot(q2[p], k_ref[cols, ks:ks + tk].astype(BF16))
            else:
                z = _dot_t(q2[p], ksrc[rows, cols].astype(BF16))
            nz = -z
            stay = jnp.minimum(nz, 0.0) - jnp.log2(1.0 + jnp.exp2(jnp.minimum(z, nz)))
            if masked:
                stay = jnp.where(ok, stay, 0.0)
            carry = carry_sc[p]
            after = [None, None]
            for h in (1, 0):
                hi, lo = _split(stay[:, h * half:(h + 1) * half])
                r = _dot(jnp.concatenate([hi, lo], axis=1), tri)
                after[h] = r[:, :half] + carry
                carry = carry + r[:, half:]
            carry_sc[p] = carry
            w = jnp.exp2(z + stay + jnp.concatenate(after, axis=1))
            if masked:
                w = jnp.where(ok, w, 0.0)
            w = w.astype(BF16)
            w2 = jnp.concatenate([w[:tq], w[tq:]], axis=1)
            if channel_major:
                vt = v_ref[cols, ks:ks + tk]
                vt2 = jnp.concatenate([jnp.where(top_rows, vt, 0.0), jnp.where(top_rows, 0.0, vt)], axis=1)
                acc_sc[p] += _dot_t(w2, vt2.astype(BF16))
                continue
            if v_split:
                v0 = vsrc[rows, 2 * p * LANES:(2 * p + 1) * LANES]
                v1 = vsrc[rows, (2 * p + 1) * LANES:(2 * p + 2) * LANES]
            else:
                vblk = vsrc[rows, cols]
                v0 = jnp.where(first, vblk, 0.0).astype(BF16)
                v1 = jnp.where(first, 0.0, vblk).astype(BF16)
            acc_sc[p] += _dot(w2, jnp.concatenate([v0, v1], axis=0))

    block(n_full, True)

    def any_weight_left():
        m = jnp.max(jnp.max(carry_sc[...], axis=0), axis=0, keepdims=True)
        return jnp.max(m, axis=1, keepdims=True)[0, 0] > F32_EXP2_ZERO

    if has_new:
        go = any_weight_left()
        for kb in reversed(range(q_pos0 // tk)):
            pl.when(go)(functools.partial(block, kb, False))
            go = go & any_weight_left()
    else:
        def body(s):
            block(s[0], False)
            return s[0] - 1, any_weight_left()

        lax.while_loop(lambda s: (s[0] >= 0) & s[1], body, (n_full - 1, any_weight_left()))
    for p in range(n_pairs):
        o_ref[:, p * LANES:(p + 1) * LANES] = acc_sc[p]


def _stick(q, k, v, tri, tq, q_pos0, new=None):
    b, t, c = q.shape
    tk = STICK_TK
    has_new = new is not None
    kmap = lambda bi, i: (bi, 0, 0)
    if has_new:
        (k_arr, layer), (v_arr, _) = k, v
        lp = k_arr.shape[3]
        assert lp == q_pos0 and t == tq <= tk and new[0].shape[1] == tk and k_arr.shape[2] == c == v_arr.shape[2]
        past_map = lambda bi, i: (layer, bi, 0, 0)
        kv_specs = [pl.BlockSpec((None, None, c, lp), past_map), pl.BlockSpec((None, None, c, lp), past_map),
                    pl.BlockSpec((None, tk, c), kmap), pl.BlockSpec((None, tk, c), kmap)]
        kv_args = (k_arr, v_arr) + tuple(new)
    else:
        lp = k.shape[1]
        assert q_pos0 + t <= lp
        kv_specs = [pl.BlockSpec((None, lp, c), kmap), pl.BlockSpec((None, lp, v.shape[2]), kmap)]
        kv_args = (k, v)
    assert lp % tk == 0 and q_pos0 % tq == 0 and tk % tq == 0
    kern = functools.partial(_stick_kernel, tq=tq, q_pos0=q_pos0, has_new=has_new)
    return pl.pallas_call(
        kern,
        grid=(b, t // tq),
        in_specs=[pl.BlockSpec((None, tq, c), lambda bi, i: (bi, i, 0))] + kv_specs
                 + [pl.BlockSpec(tri.shape, lambda bi, i: (0, 0))],
        out_specs=pl.BlockSpec((None, tq, c), lambda bi, i: (bi, i, 0)),
        out_shape=jax.ShapeDtypeStruct((b, t, c), F32),
        scratch_shapes=[pltpu.VMEM((c // LANES, 2 * tq, LANES), F32), pltpu.VMEM((c // LANES, tq, LANES), F32)],
        compiler_params=_cparams(("parallel", "arbitrary")),
        name="stick_attn",
    )(q, *kv_args, tri)


DSA_LEVEL = 512


def _dsa_rows_kernel(q_ref, qi_ref, wi_ref, k_ref, v_ref, ki_ref, o_ref, key_sc, bias_sc,
                     *, tq, l_valid, q_pos0, top_k, levels):
    kt = KEY_TILE
    th = min(tq, 128)
    parts = [slice(i * th, (i + 1) * th) for i in range(tq // th)]
    n_rep = q_ref.shape[1] // LANES
    lane = lax.broadcasted_iota(jnp.int32, (1, LANES), 1)
    first = lane < D_HEAD
    q_start = q_pos0 + pl.program_id(1) * tq
    kf = jnp.float32(top_k)

    def body(lv):
        n_kt = lv // kt

        def admissible(rows, lo, width):
            qpos = q_start + rows.start + lax.broadcasted_iota(jnp.int32, (th, 1), 0)
            kpos = lo + lax.broadcasted_iota(jnp.int32, (1, width), 1)
            return (jnp.right_shift(kpos, CHUNK_SHIFT) <= jnp.right_shift(qpos, CHUNK_SHIFT)) & (kpos < l_valid)

        ki_hi, ki_lo = _split(ki_ref[:lv, :])
        for rows in parts:
            qi = qi_ref[rows, :]
            wi = wi_ref[rows, :]
            qm = []
            for h in range(H_IDX):
                grp = qi[:, (h // 2) * LANES:(h // 2 + 1) * LANES]
                qm.append(jnp.where(first, grp, 0.0) if h % 2 == 0 else jnp.where(first, 0.0, grp))
            q_hi, q_lo = _split(jnp.concatenate(qm, axis=0))
            s = _dot_t(q_hi, ki_hi) + (_dot_t(q_hi, ki_lo) + _dot_t(q_lo, ki_hi))
            score = jnp.zeros((th, lv), F32)
            for h in range(H_IDX):
                score = score + jnp.maximum(s[h * th:(h + 1) * th], 0.0) * wi[:, h:h + 1]
            score = score * ((D_IDX ** -0.5) * (H_IDX ** -0.5)) + 0.0
            score = jnp.where(admissible(rows, 0, lv), score, NEG)
            bits = lax.bitcast_convert_type(score, jnp.int32)
            key_sc[rows, :lv] = jnp.where(bits < 0, bits ^ jnp.int32(0x7FFFFFFF), bits)

        def count_ge(rows, trial):
            part = jnp.zeros((th, kt), F32)
            for c in range(n_kt):
                part = part + jnp.where(key_sc[rows, c * kt:(c + 1) * kt] >= trial, 1.0, 0.0)
            return jnp.sum(part, axis=1, keepdims=True)

        int_min = jnp.int32(-2 ** 31)
        zero = jnp.zeros((th, 1), jnp.int32)
        thr0 = tuple(jnp.where(count_ge(rows, zero) >= kf, jnp.int32(0), int_min) for rows in parts)

        def bit_step(i, thr):
            bit = jnp.left_shift(jnp.int32(1), jnp.int32(30) - i)
            return tuple(jnp.where(count_ge(rows, t + bit) >= kf, t + bit, t) for rows, t in zip(parts, thr))

        thrs = lax.fori_loop(0, 31, bit_step, thr0)

        rr = lax.broadcasted_iota(jnp.int32, (kt, kt), 0)
        cc = lax.broadcasted_iota(jnp.int32, (kt, kt), 1)
        before = jnp.where(rr < cc, 1.0, 0.0).astype(BF16)
        for rows, thr in zip(parts, thrs):
            n_gt = jnp.zeros((th, kt), F32)
            for c in range(n_kt):
                n_gt = n_gt + jnp.where(key_sc[rows, c * kt:(c + 1) * kt] > thr, 1.0, 0.0)
            room = kf - jnp.sum(n_gt, axis=1, keepdims=True)
            run = jnp.zeros((th, 1), F32)
            for c in range(n_kt):
                keyc = key_sc[rows, c * kt:(c + 1) * kt]
                eqf = jnp.where(keyc == thr, 1.0, 0.0)
                rank = _dot(eqf.astype(BF16), before) + run
                tie_ok = jnp.where(keyc == thr, jnp.where(rank < room, 0.0, NEG), NEG)
                sel = jnp.where(keyc > thr, 0.0, tie_ok)
                bias_sc[rows, c * kt:(c + 1) * kt] = jnp.where(admissible(rows, c * kt, kt), sel, NEG)
                run = run + jnp.sum(eqf, axis=1, keepdims=True)

        kb = k_ref[:lv, :].astype(BF16)
        vf = v_ref[:lv, :]
        vg = [jnp.where(first, vf, 0.0).astype(BF16), jnp.where(first, 0.0, vf).astype(BF16)]
        for rows in parts:
            q = q_ref[rows, :] * (D_HEAD ** -0.5)
            bias = bias_sc[rows, :lv]
            out = jnp.zeros((n_rep * th, LANES), F32)
            for g in range(2):
                keep = first if g == 0 else jnp.logical_not(first)
                qs = jnp.concatenate(
                    [jnp.where(keep, q[:, r * LANES:(r + 1) * LANES], 0.0) for r in range(n_rep)],
                    axis=0).astype(BF16)
                logits = _dot_t(qs, kb)
                logits = (logits.reshape(n_rep, th, lv) + bias[None]).reshape(n_rep * th, lv)
                m = jnp.max(logits, axis=1, keepdims=True)
                p = jnp.exp(logits - m)
                den = jnp.sum(p, axis=1, keepdims=True)
                out = out + _dot(p.astype(BF16), vg[g]) / den
            for r in range(n_rep):
                o_ref[rows, r * LANES:(r + 1) * LANES] = out[r * th:(r + 1) * th]

    if len(levels) == 1:
        body(levels[0])
    else:
        cls = (q_start + tq - 1) // DSA_LEVEL
        for c, lv in enumerate(levels):
            pl.when(cls == c)(functools.partial(body, lv))


DSA_CK = 256


def _dsa_kernel(*refs, tq, l_valid, q_pos0, top_k, levels, has_new):
    if len(levels) == 1:
        _dsa_tile(*refs, tq=tq, l_valid=l_valid, q_pos0=q_pos0, top_k=top_k, n_chunks=levels[0], has_new=has_new)
        return
    assert not has_new
    q_last = q_pos0 + pl.program_id(1) * tq + tq - 1
    vis_end = jnp.left_shift(jnp.right_shift(q_last, CHUNK_SHIFT) + 1, CHUNK_SHIFT)
    step = levels[0] * DSA_CK
    for i, n in enumerate(levels):
        pl.when((vis_end + step - 1) // step == i + 1)(functools.partial(
            _dsa_tile, *refs, tq=tq, l_valid=l_valid, q_pos0=q_pos0, top_k=top_k, n_chunks=n, has_new=False))


def _dsa_tile(q_ref, qi_ref, wi_ref, k_ref, v_ref, ki_ref, *rest, tq, l_valid, q_pos0, top_k, n_chunks, has_new):
    if has_new:
        kn_ref, vn_ref, kin_ref, o_ref, kic_sc, kb_sc, vt_sc, key_sc, bias_sc = rest
    else:
        o_ref, kic_sc, kb_sc, vt_sc, key_sc, bias_sc = rest
    ck = DSA_CK
    n_cached = k_ref.shape[1 if has_new else 0] // ck
    lp = (n_cached + 1) * ck if has_new else k_ref.shape[0]
    n_rep = q_ref.shape[1] // LANES
    lane = lax.broadcasted_iota(jnp.int32, (1, LANES), 1)
    first = lane < D_HEAD
    q_start = q_pos0 + pl.program_id(1) * tq
    tl = LANES
    parts = range(tq // tl)
    lanes_of = lambda pt: slice(pt * tl, (pt + 1) * tl)
    qchunk = [jnp.right_shift(q_start + pt * tl + lax.broadcasted_iota(jnp.int32, (1, tl), 1), CHUNK_SHIFT)
              for pt in parts]
    kf = jnp.float32(top_k)
    top_rows = lax.broadcasted_iota(jnp.int32, (LANES, 1), 0) < D_HEAD

    def rows_of(c):
        return pl.ds(c * ck, ck)

    def admissible(c, pt):
        kpos = c * ck + lax.broadcasted_iota(jnp.int32, (ck, 1), 0)
        return (jnp.right_shift(kpos, CHUNK_SHIFT) <= qchunk[pt]) & (kpos < l_valid)

    def chunks(n, step, carry):
        for c in range(n):
            carry = step(c, carry)
        return carry

    @pl.when(pl.program_id(1) == 0)
    def _():
        def prep(c, _):
            rows = rows_of(c)
            if has_new and c < n_cached:
                cols = slice(c * ck, (c + 1) * ck)
                kit = ki_ref[:, cols]
                ki2 = jnp.concatenate([kit, kit], axis=0).T
                kblk = k_ref[:, cols].T
                vt = v_ref[:, cols]
            else:
                ksrc, vsrc, kisrc, src = (kn_ref, vn_ref, kin_ref, rows_of(0)) if has_new else (
                    k_ref, v_ref, ki_ref, rows)
                ki2 = kisrc[src, :]
                kblk = ksrc[src, :]
                vt = vsrc[src, :].T
            hi, lo = _split(ki2)
            kic_sc[rows, :LANES] = jnp.where(first, hi, lo)
            kic_sc[rows, LANES:] = jnp.where(first, hi, jnp.zeros_like(hi))
            kb_sc[rows, :] = kblk.astype(BF16)
            vt_sc[c, :, :ck] = jnp.where(top_rows, vt, 0.0).astype(BF16)
            vt_sc[c, :, ck:] = jnp.where(top_rows, 0.0, vt).astype(BF16)
            return 0
        chunks(lp // ck, prep, 0)

    qic_t, wit = [], []
    for pt in parts:
        qi = qi_ref[lanes_of(pt), :]
        blocks = []
        for h in range(H_IDX):
            grp = qi[:, (h // 2) * LANES:(h // 2 + 1) * LANES]
            swapped = pltpu.roll(grp, D_IDX, 1)
            twice = jnp.where(first, grp, swapped) if h % 2 == 0 else jnp.where(first, swapped, grp)
            x3 = jnp.concatenate([twice, jnp.where(first, twice, 0.0)], axis=1)
            hi = x3.astype(BF16).astype(F32)
            blocks.append(jnp.concatenate([hi[:, :LANES], (x3 - hi)[:, LANES:]], axis=1))
        qic_t.append(jnp.concatenate(blocks, axis=0).T.astype(BF16))
        wit.append(wi_ref[lanes_of(pt), :].T)

    def score_chunk(c, _):
        for pt in parts:
            s = _dot(kic_sc[rows_of(c), :], qic_t[pt])
            score = jnp.zeros((ck, tl), F32)
            for h in range(H_IDX):
                score = score + jnp.maximum(s[:, h * tl:(h + 1) * tl], 0.0) * wit[pt][h:h + 1, :]
            score = score * ((D_IDX ** -0.5) * (H_IDX ** -0.5)) + 0.0
            score = jnp.where(admissible(c, pt), score, NEG)
            bits = lax.bitcast_convert_type(score, jnp.int32)
            key_sc[rows_of(c), lanes_of(pt)] = jnp.where(bits < 0, bits ^ jnp.int32(0x7FFFFFFF), bits)
        return 0

    chunks(n_chunks, score_chunk, 0)

    def count(pred, trials):
        def body(c, accs):
            out = []
            for pt in parts:
                hit = jnp.where(pred(key_sc[rows_of(c), lanes_of(pt)], trials[pt]), 1.0, 0.0)
                out.append(accs[pt] + jnp.sum(hit.reshape(ck // 8, 8, tl), axis=0))
            return out
        accs = chunks(n_chunks, body, [jnp.zeros((8, tl), F32) for _ in parts])
        return [jnp.sum(a, axis=0, keepdims=True) for a in accs]

    ge = lambda a, b: a >= b
    above = count(ge, [jnp.zeros((1, tl), jnp.int32) for _ in parts])
    thr0 = tuple(jnp.where(n >= kf, jnp.int32(0), jnp.int32(-2 ** 31)) for n in above)

    def bit_step(i, thr):
        bit = jnp.left_shift(jnp.int32(1), jnp.int32(30) - i)
        trials = [t + bit for t in thr]
        return tuple(jnp.where(n >= kf, tr, t) for n, tr, t in zip(count(ge, trials), trials, thr))

    thr = lax.fori_loop(0, 31, bit_step, thr0)

    room = [kf - n for n in count(lambda a, b: a > b, thr)]
    rr = lax.broadcasted_iota(jnp.int32, (ck, ck), 0)
    cc = lax.broadcasted_iota(jnp.int32, (ck, ck), 1)
    earlier = jnp.where(cc < rr, 1.0, 0.0).astype(BF16)

    def tie_chunk(c, runs):
        out = []
        for pt in parts:
            keyc = key_sc[rows_of(c), lanes_of(pt)]
            eqf = jnp.where(keyc == thr[pt], 1.0, 0.0)
            rank = _dot(earlier, eqf.astype(BF16)) + runs[pt]
            tie_ok = jnp.where(keyc == thr[pt], jnp.where(rank < room[pt], 0.0, NEG), NEG)
            sel = jnp.where(keyc > thr[pt], 0.0, tie_ok)
            bias_sc[rows_of(c), lanes_of(pt)] = jnp.where(admissible(c, pt), sel, NEG)
            out.append(runs[pt] + jnp.sum(eqf, axis=0, keepdims=True))
        return out

    chunks(n_chunks, tie_chunk, [jnp.zeros((1, tl), F32) for _ in parts])

    qs_t = []
    for pt in parts:
        q = q_ref[lanes_of(pt), :] * (D_HEAD ** -0.5 * LOG2_E)
        per_group = []
        for g in range(2):
            keep = first if g == 0 else jnp.logical_not(first)
            per_group.append(jnp.concatenate(
                [jnp.where(keep, q[:, r * LANES:(r + 1) * LANES], 0.0) for r in range(n_rep)],
                axis=0).T.astype(BF16))
        qs_t.append(per_group)
    width = n_rep * tl

    def att_chunk(c, carries):
        kb = kb_sc[rows_of(c), :]
        out = []
        for pt in parts:
            m0, m1, d0, d1, acc = carries[pt]
            bias = bias_sc[rows_of(c), lanes_of(pt)]
            bias = jnp.concatenate([bias] * n_rep, axis=1)
            l0 = _dot(kb, qs_t[pt][0]) + bias
            l1 = _dot(kb, qs_t[pt][1]) + bias
            n0 = jnp.maximum(m0, jnp.max(l0, axis=0, keepdims=True))
            n1 = jnp.maximum(m1, jnp.max(l1, axis=0, keepdims=True))
            a0, a1 = jnp.exp2(m0 - n0), jnp.exp2(m1 - n1)
            p0, p1 = jnp.exp2(l0 - n0), jnp.exp2(l1 - n1)
            d0 = d0 * a0 + jnp.sum(p0, axis=0, keepdims=True)
            d1 = d1 * a1 + jnp.sum(p1, axis=0, keepdims=True)
            pv = _dot(vt_sc[c], jnp.concatenate([p0.astype(BF16), p1.astype(BF16)], axis=0))
            out.append((n0, n1, d0, d1, acc * jnp.where(top_rows, a0, a1) + pv))
        return out

    neg = jnp.full((1, width), NEG, F32)
    zero = jnp.zeros((1, width), F32)
    done = chunks(n_chunks, att_chunk, [(neg, neg, zero, zero, jnp.zeros((LANES, width), F32)) for _ in parts])
    for pt in parts:
        _, _, d0, d1, acc = done[pt]
        out_t = acc / jnp.where(top_rows, d0, d1)
        for r in range(n_rep):
            o_ref[lanes_of(pt), r * LANES:(r + 1) * LANES] = out_t[:, r * tl:(r + 1) * tl].T


def _dsa(q, qi, wi, k, v, ki, tq, q_pos0, l_valid, new=None):
    b, t, c = q.shape
    has_new = new is not None
    qmap = lambda bi, i: (bi, i, 0)
    kmap = lambda bi, i: (bi, 0, 0)
    if has_new:
        layer = k[1]
        past = k[0].shape[3]
        lp = past + DSA_CK
        assert past == q_pos0 and past % DSA_CK == 0
        past_map = lambda bi, i: (layer, bi, 0, 0)
        key_specs = [pl.BlockSpec((None, None, a[0].shape[2], past), past_map) for a in (k, v, ki)]
        key_specs += [pl.BlockSpec((None, DSA_CK, LANES), kmap)] * 3
        key_args = (k[0], v[0], ki[0]) + tuple(new)
    else:
        lp = k.shape[1]
        key_specs = [pl.BlockSpec((None, lp, LANES), kmap)] * 3
        key_args = (k, v, ki)
    assert lp % DSA_CK == 0 and l_valid <= lp and t % tq == 0 and tq % LANES == 0
    top_k = min(TOPK_MAX, l_valid // 4)
    assert DSA_CK >= top_k
    n_all = lp // DSA_CK
    if q_pos0 == 0 and n_all % 4 == 0:
        levels = tuple(range(n_all // 4, n_all + 1, n_all // 4))
    else:
        levels = (n_all,)
    kern = functools.partial(_dsa_kernel, tq=tq, l_valid=l_valid, q_pos0=q_pos0, top_k=top_k, levels=levels,
                             has_new=has_new)
    return pl.pallas_call(
        kern,
        grid=(b, t // tq),
        in_specs=[pl.BlockSpec((None, tq, c), qmap), pl.BlockSpec((None, tq, qi.shape[2]), qmap),
                  pl.BlockSpec((None, tq, LANES), qmap)] + key_specs,
        out_specs=pl.BlockSpec((None, tq, c), qmap),
        out_shape=jax.ShapeDtypeStruct((b, t, c), F32),
        scratch_shapes=[pltpu.VMEM((lp, 2 * LANES), BF16), pltpu.VMEM((lp, LANES), BF16),
                        pltpu.VMEM((lp // DSA_CK, LANES, 2 * DSA_CK), BF16), pltpu.VMEM((lp, tq), jnp.int32),
                        pltpu.VMEM((lp, tq), F32)],
        compiler_params=_cparams(("parallel", "arbitrary")),
        name="dsa_attn",
    )(q, qi, wi, *key_args)


def _dsa_rows(q, qi, wi, k, v, ki, tq, q_pos0, l_valid):
    b, t, c = q.shape
    lp = k.shape[1]
    assert lp % KEY_TILE == 0 and l_valid <= lp and t % tq == 0
    top_k = min(TOPK_MAX, l_valid // 4)
    if q_pos0 == 0 and lp % DSA_LEVEL == 0 and tq % CHUNK == 0 and DSA_LEVEL >= top_k:
        levels = tuple(range(DSA_LEVEL, lp + 1, DSA_LEVEL))
    else:
        levels = (lp,)
    kern = functools.partial(_dsa_rows_kernel, tq=tq, l_valid=l_valid, q_pos0=q_pos0, top_k=top_k, levels=levels)
    qmap = lambda bi, i: (bi, i, 0)
    kmap = lambda bi, i: (bi, 0, 0)
    return pl.pallas_call(
        kern,
        grid=(b, t // tq),
        in_specs=[pl.BlockSpec((None, tq, c), qmap), pl.BlockSpec((None, tq, qi.shape[2]), qmap),
                  pl.BlockSpec((None, tq, LANES), qmap),
                  pl.BlockSpec((None, lp, LANES), kmap), pl.BlockSpec((None, lp, LANES), kmap),
                  pl.BlockSpec((None, lp, LANES), kmap)],
        out_specs=pl.BlockSpec((None, tq, c), qmap),
        out_shape=jax.ShapeDtypeStruct((b, t, c), F32),
        scratch_shapes=[pltpu.VMEM((tq, lp), jnp.int32), pltpu.VMEM((tq, lp), F32)],
        compiler_params=_cparams(("parallel", "arbitrary")),
        name="dsa_attn",
    )(q, qi, wi, k, v, ki)


FFN_CHUNK = 1024


def _ffn_tail(x1, gf_ref, w1_ref, w2_ref, o_ref):
    h = _rms(x1, gf_ref[...]).astype(BF16)
    acc = None
    for c in range(w1_ref.shape[1] // FFN_CHUNK):
        cols = slice(c * FFN_CHUNK, (c + 1) * FFN_CHUNK)
        a = jnp.square(jnp.maximum(_dot(h, w1_ref[:, cols]), 0.0))
        part = _dot(a.astype(BF16), w2_ref[cols, :])
        acc = part if acc is None else acc + part
    o_ref[...] = x1 + acc


def _even_out_kernel(x_ref, att_ref, u_ref, vb_ref, ws_ref, bst_ref, w_ref, gf_ref, w1_ref, w2_ref, o_ref,
                     *, p_len):
    tm = x_ref.shape[0]
    cg = vb_ref.shape[1] // G_B
    rr = lax.broadcasted_iota(jnp.int32, (p_len, p_len), 0)
    cc = lax.broadcasted_iota(jnp.int32, (p_len, p_len), 1)
    causal = jnp.right_shift(cc, CHUNK_SHIFT) <= jnp.right_shift(rr, CHUNK_SHIFT)
    u = u_ref[...]
    vb = vb_ref[...].astype(BF16)
    bst = bst_ref[...]
    mixed_rows = []
    for c in range(tm // p_len):
        cols = []
        for g in range(G_B):
            wg = jnp.where(causal, ws_ref[g], 0.0).astype(BF16)
            mg = _dot(wg, vb[c * p_len:(c + 1) * p_len, g * cg:(g + 1) * cg]) + bst[:, g:g + 1]
            cols.append(mg)
        mixed_rows.append(jnp.concatenate(cols, axis=1))
    gated = u * jnp.concatenate(mixed_rows, axis=0)
    na = att_ref.shape[1]
    y = _dot(att_ref[...].astype(BF16), w_ref[:na, :]) + _dot(gated.astype(BF16), w_ref[na:, :])
    _ffn_tail(x_ref[...] + y, gf_ref, w1_ref, w2_ref, o_ref)


def _even_out_ffn(x, att, u, vb, ws, bst, w, gf, w1, w2, tm, p_len):
    n_rows, d = x.shape
    n = att.shape[1]
    row = lambda i: (i, 0)
    kern = functools.partial(_even_out_kernel, p_len=p_len)
    return pl.pallas_call(
        kern,
        grid=(n_rows // tm,),
        in_specs=[pl.BlockSpec((tm, d), row), pl.BlockSpec((tm, n), row), pl.BlockSpec((tm, n), row),
                  pl.BlockSpec((tm, n), row), _resident(ws), _resident(bst), _resident(w),
                  _resident(gf), _resident(w1), _resident(w2)],
        out_specs=pl.BlockSpec((tm, d), row),
        out_shape=jax.ShapeDtypeStruct((n_rows, d), F32),
        compiler_params=_cparams(("parallel",)),
        name="even_out_ffn",
    )(x, att, u, vb, ws, bst, _operand(w), gf, _operand(w1), _operand(w2))


def _odd_out_kernel(x_ref, att_ref, gb_ref, cin_ref, prev_ref, cw_ref, w_ref, gf_ref, w1_ref, w2_ref, o_ref,
                    *, t_len, has_prev):
    tm = x_ref.shape[0]
    prev = prev_ref[...]
    if not has_prev:
        start = (pl.program_id(0) * tm) % t_len == 0
        prev = jnp.where(start, 0.0, prev)
    cw = cw_ref[...]
    seg = t_len if has_prev else tm
    convs = []
    for s in range(tm // seg):
        cin = cin_ref[s * seg:(s + 1) * seg, :]
        ext = jnp.concatenate([prev[8 * s:8 * (s + 1)], cin], axis=0)
        conv = cw[CONV_W - 1:CONV_W] * cin
        for j in range(1, CONV_W):
            conv = conv + cw[CONV_W - 1 - j:CONV_W - j] * pltpu.roll(ext, j, 0)[8:]
        convs.append(conv)
    gated = gb_ref[...] * (convs[0] if len(convs) == 1 else jnp.concatenate(convs, axis=0))
    na = att_ref.shape[1]
    y = _dot(att_ref[...].astype(BF16), w_ref[:na, :]) + _dot(gated.astype(BF16), w_ref[na:, :])
    _ffn_tail(x_ref[...] + y, gf_ref, w1_ref, w2_ref, o_ref)


def _odd_out_ffn(x, att, gb, cin, prev, cw, w, gf, w1, w2, tm, t_len):
    n_rows, d = x.shape
    n = att.shape[1]
    row = lambda i: (i, 0)
    has_prev = prev is not None
    if has_prev:
        assert tm % t_len == 0 and t_len % 8 == 0
        prev_arr, prev_map, prev_rows = prev, row, 8 * (tm // t_len)
    else:
        assert t_len % tm == 0 and tm % 8 == 0
        prev_arr, prev_map, prev_rows = cin, (lambda i: (jnp.maximum(i * (tm // 8) - 1, 0), 0)), 8
    kern = functools.partial(_odd_out_kernel, t_len=t_len, has_prev=has_prev)
    return pl.pallas_call(
        kern,
        grid=(n_rows // tm,),
        in_specs=[pl.BlockSpec((tm, d), row), pl.BlockSpec((tm, n), row), pl.BlockSpec((tm, n), row),
                  pl.BlockSpec((tm, n), row), pl.BlockSpec((prev_rows, n), prev_map),
                  _resident(cw), _resident(w), _resident(gf), _resident(w1), _resident(w2)],
        out_specs=pl.BlockSpec((tm, d), row),
        out_shape=jax.ShapeDtypeStruct((n_rows, d), F32),
        compiler_params=_cparams(("parallel",)),
        name="odd_out_ffn",
    )(x, att, gb, cin, prev_arr, cw, _operand(w), gf, _operand(w1), _operand(w2))


def _round_up(n, m):
    return (n + m - 1) // m * m


def _row_tile(n_rows, cap):
    t = min(cap, n_rows)
    while n_rows % t:
        t //= 2
    return t


def _odd_weight_layout(n_q):
    n_heads = n_q // D_HEAD
    rep = n_heads // 2
    perm = np.zeros((n_q,), np.int32)
    for r in range(rep):
        for g in range(2):
            for dd in range(D_HEAD):
                perm[r * LANES + g * D_HEAD + dd] = (g * rep + r) * D_HEAD + dd
    return perm


def kernel(x_prompt, x_sample, cache_a_k, cache_a_v, cache_c_k, cache_c_v, cache_c_kidx, state_d_conv,
           norm_mix_g, norm_ffn_g, w_in_even, gq_a, gk_a, g_b, ws_b, bs_b, w_out_even,
           w_in_odd, gq_c, gk_c, conv_w_d, w_out_odd, w_ffn1, w_ffn2):
    bp, tp, d = x_prompt.shape
    bs, ts, _ = x_sample.shape
    depth = norm_mix_g.shape[0]
    past = cache_a_k.shape[2]
    n_a = cache_a_k.shape[3] * cache_a_k.shape[4]
    n_kv = cache_c_k.shape[3] * cache_c_k.shape[4]
    n_b = g_b.shape[1]
    n_d = conv_w_d.shape[2]
    n_q = w_out_odd.shape[1] - n_d
    assert n_a == 512 and n_b == 512 and n_kv == 128 and n_q == 512 and n_d == 512 and d == 1024

    xp = x_prompt.reshape(bp * tp, d)
    xs = x_sample.reshape(bs * ts, d)
    tm_p = _row_tile(bp * tp, 512)
    tm_s = _row_tile(bs * ts, 512)
    ls = past + ts
    ls_pad = _round_up(ls, KEY_TILE)

    hmean = jnp.asarray(np.kron(np.eye(n_a // D_HEAD), np.full((D_HEAD, D_HEAD), 1.0 / D_HEAD)), BF16)
    perm = _odd_weight_layout(n_q)
    tile8 = lambda gvec: jnp.tile(gvec, n_a // D_HEAD).reshape(1, n_a)

    ak_p, av_p, ak_s, av_s, bv_s = [], [], [], [], []
    ck_p, cv_p, ci_p, ck_s, cv_s, ci_s, dc_p, dc_s = [], [], [], [], [], [], [], []

    tri = _stick_tri()

    def pad_keys(past_rows, new_rows, multiple=KEY_TILE):
        width = new_rows.shape[-1]
        zeros = jnp.zeros((bs, _round_up(ls, multiple) - ls, width), F32)
        return jnp.concatenate([past_rows.reshape(bs, past, width), new_rows.reshape(bs, ts, width), zeros], axis=1)

    channel_major = lambda a: jnp.transpose(a, (0, 1, 3, 4, 2)).reshape(a.shape[0], a.shape[1], -1, a.shape[2])
    past_a_k = channel_major(cache_a_k)
    past_a_v = channel_major(cache_a_v)
    past_c_k = channel_major(cache_c_k)
    past_c_v = channel_major(cache_c_v)
    past_c_ki = jnp.transpose(cache_c_kidx, (0, 1, 3, 2))

    w1_all, w2_all = w_ffn1.astype(BF16), w_ffn2.astype(BF16)
    w_in_even_all, w_out_even_all = w_in_even.astype(BF16), w_out_even.astype(BF16)
    offs = np.cumsum([0, n_q, n_kv, n_kv, H_IDX * D_IDX, D_IDX, H_IDX, n_d, n_d, n_d])
    col = lambda s: w_in_odd[:, :, int(offs[s]):int(offs[s + 1])]
    w_in_odd_all = jnp.concatenate(
        [col(0)[:, :, perm], col(1), col(2), col(3), col(4), col(4), col(5),
         jnp.zeros((w_in_odd.shape[0], d, LANES - H_IDX), F32), col(6), col(7), col(8)], axis=2).astype(BF16)
    assert w_in_odd_all.shape[2] == ODD_COLS
    w_out_odd_all = jnp.concatenate([w_out_odd[:, :n_q][:, perm], w_out_odd[:, n_q:]], axis=1).astype(BF16)

    for i in range(depth):
        j = i // 2
        g_mix = norm_mix_g[i].reshape(1, d)
        ffn = (norm_ffn_g[i].reshape(1, d), (w1_all, i), (w2_all, i))
        if i % 2 == 0:
            w_in = (w_in_even_all, j)
            w_out = (w_out_even_all, j)
            gq, gk, gb = tile8(gq_a[j]), tile8(gk_a[j]), g_b[j].reshape(1, n_b)
            q, k, v, u, vb, kb, vm = _even_in(xp, g_mix, w_in, hmean, gq, gk, gb, tm_p)
            att = _stick(q.reshape(bp, tp, n_a), kb.reshape(bp, tp, n_a), vm.reshape(bp, tp, 2 * n_a), tri,
                         min(tp, KEY_TILE), 0)
            xp = _even_out_ffn(xp, att.reshape(bp * tp, n_a), u, vb, ws_b[j], bs_b[j].T, w_out, *ffn, tm_p, B_CHUNK)
            ak_p.append(k.reshape(bp, tp, n_a // D_HEAD, D_HEAD))
            av_p.append(v.reshape(bp, tp, n_a // D_HEAD, D_HEAD))
            q, k, v, u, vb, _, _ = _even_in(xs, g_mix, w_in, hmean, gq, gk, gb, tm_s)
            new_rows = lambda a: jnp.pad(a.reshape(bs, ts, n_a), ((0, 0), (0, STICK_TK - ts), (0, 0)))
            att = _stick(q.reshape(bs, ts, n_a), (past_a_k, j), (past_a_v, j), tri, ts, past,
                         new=(new_rows(k), new_rows(v)))
            xs = _even_out_ffn(xs, att.reshape(bs * ts, n_a), u, vb, ws_b[j][:, :ts, :ts], bs_b[j][:, :ts].T,
                               w_out, *ffn, tm_s, ts)
            ak_s.append(k.reshape(bs, ts, n_a // D_HEAD, D_HEAD))
            av_s.append(v.reshape(bs, ts, n_a // D_HEAD, D_HEAD))
            bv_s.append(vb.reshape(bs, ts, n_b))
        else:
            w_in = (w_in_odd_all, j)
            w_out = (w_out_odd_all, j)
            gq, gk = tile8(gq_c[j]), jnp.tile(gk_c[j], n_kv // D_HEAD).reshape(1, n_kv)
            cw = conv_w_d[j]
            q, k, v, qi, ki, wi, gbt, cin = _odd_in(xp, g_mix, w_in, hmean, gq, gk, tm_p)
            r3 = lambda a: a.reshape(bp, tp, a.shape[-1])
            att = _dsa(r3(q), r3(qi), r3(wi), r3(k), r3(v), r3(ki), min(tp, 2 * LANES), 0, tp)
            xp = _odd_out_ffn(xp, att.reshape(bp * tp, n_q), gbt, cin, None, cw, w_out, *ffn, tm_p, tp)
            ck_p.append(k.reshape(bp, tp, n_kv // D_HEAD, D_HEAD))
            cv_p.append(v.reshape(bp, tp, n_kv // D_HEAD, D_HEAD))
            ci_p.append(ki[:, :D_IDX].reshape(bp, tp, D_IDX))
            dc_p.append(cin.reshape(bp, tp, n_d)[:, tp - (CONV_W - 1):])
            q, k, v, qi, ki, wi, gbt, cin = _odd_in(xs, g_mix, w_in, hmean, gq, gk, tm_s)
            r3 = lambda a: a.reshape(bs, ts, a.shape[-1])
            pad_q = lambda a: jnp.pad(r3(a), ((0, 0), (0, LANES - ts), (0, 0)))
            new_rows = lambda a: jnp.pad(r3(a), ((0, 0), (0, DSA_CK - ts), (0, 0)))
            att = _dsa(pad_q(q), pad_q(qi), pad_q(wi), (past_c_k, j), (past_c_v, j), (past_c_ki, j),
                       LANES, past, ls, new=(new_rows(k), new_rows(v), new_rows(ki)))[:, :ts]
            prev = jnp.concatenate([jnp.zeros((bs, 8 - (CONV_W - 1), n_d), F32), state_d_conv[j]], axis=1)
            xs = _odd_out_ffn(xs, att.reshape(bs * ts, n_q), gbt, cin, prev.reshape(bs * 8, n_d), cw, w_out, *ffn,
                              tm_s, ts)
            ck_s.append(k.reshape(bs, ts, n_kv // D_HEAD, D_HEAD))
            cv_s.append(v.reshape(bs, ts, n_kv // D_HEAD, D_HEAD))
            ci_s.append(ki[:, :D_IDX].reshape(bs, ts, D_IDX))
            dc_s.append(cin.reshape(bs, ts, n_d)[:, ts - (CONV_W - 1):])

    return (xp.reshape(bp, tp, d), xs.reshape(bs, ts, d),
            jnp.stack(ak_p), jnp.stack(av_p), jnp.stack(ak_s), jnp.stack(av_s),
            jnp.stack(bv_s),
            jnp.stack(ck_p), jnp.stack(cv_p), jnp.stack(ci_p),
            jnp.stack(ck_s), jnp.stack(cv_s), jnp.stack(ci_s),
            jnp.stack(dc_p), jnp.stack(dc_s))
```

```python
import functools

import numpy as np
import jax
import jax.numpy as jnp
from jax import lax
from jax.experimental import pallas as pl
from jax.experimental.pallas import tpu as pltpu

F32 = jnp.float32
BF16 = jnp.bfloat16

EPS = 1e-6
NEG = -1e30
LOG2_E = 1.4426950408889634
CHUNK = 64
CHUNK_SHIFT = CHUNK.bit_length() - 1
D_HEAD = 64
G_B = 4
B_CHUNK = 128
H_IDX = 4
D_IDX = 64
TOPK_MAX = 256
CONV_W = 3
LANES = 128
VMEM_LIMIT = 56 * 1024 * 1024


def _cparams(sem):
    return pltpu.CompilerParams(dimension_semantics=sem, vmem_limit_bytes=VMEM_LIMIT)


def _resident(a):
    if isinstance(a, tuple):
        arr, layer = a
        shape = arr.shape[1:]
        return pl.BlockSpec((None,) + shape, lambda *_: (layer,) + (0,) * len(shape),
                            pipeline_mode=pl.Buffered(1))
    return pl.BlockSpec(a.shape, lambda *_: (0,) * a.ndim, pipeline_mode=pl.Buffered(1))


def _operand(a):
    return a[0] if isinstance(a, tuple) else a


def _dot(a, b):
    return jnp.dot(a, b, preferred_element_type=F32)


def _dot_t(a, b):
    return lax.dot_general(a, b, (((1,), (1,)), ((), ())), preferred_element_type=F32)


def _split(x):
    hi = x.astype(BF16)
    lo = (x - hi.astype(F32)).astype(BF16)
    return hi, lo


def _rms(x, g):
    return x * lax.rsqrt(jnp.mean(x * x, axis=-1, keepdims=True) + EPS) * g


def _head_rms(x, hmean, g):
    ms = _dot((x * x).astype(BF16), hmean)
    return x * lax.rsqrt(ms + EPS) * g


def _gelu(x):
    return jax.nn.gelu(x)


def _even_in_kernel(x_ref, g_ref, w_ref, hm_ref, gq_ref, gk_ref, gb_ref,
                    q_ref, k_ref, v_ref, u_ref, vb_ref, kb_ref, vm_ref, *, channel_major):
    h = _rms(x_ref[...], g_ref[...]).astype(BF16)
    n = q_ref.shape[-1]

    def seg(i):
        return _dot(h, w_ref[:, i * n:(i + 1) * n])

    hm = hm_ref[...]
    q_ref[...] = _head_rms(seg(0), hm, gq_ref[...])
    k = _head_rms(seg(1), hm, gk_ref[...])
    k_ref[...] = k.T if channel_major else k
    kb_ref[...] = k.astype(BF16)
    v = seg(2)
    v_ref[...] = v.T if channel_major else v
    first = lax.broadcasted_iota(jnp.int32, (1, LANES), 1) < D_HEAD
    for p in range(n // LANES):
        vp = v[:, p * LANES:(p + 1) * LANES]
        vm_ref[:, 2 * p * LANES:(2 * p + 1) * LANES] = jnp.where(first, vp, 0.0).astype(BF16)
        vm_ref[:, (2 * p + 1) * LANES:(2 * p + 2) * LANES] = jnp.where(first, 0.0, vp).astype(BF16)
    u_ref[...] = _gelu(seg(3))
    vb_ref[...] = _rms(_gelu(seg(4)), gb_ref[...])


def _channel_major_out(n_rows, tm, t_len, channels):
    per_stream = t_len // tm
    spec = pl.BlockSpec((None, channels, tm), lambda i: (i // per_stream, 0, i % per_stream))
    return spec, jax.ShapeDtypeStruct((n_rows // t_len, channels, t_len), F32)


def _even_in(x, g, w, hm, gq, gk, gb, tm, t_len=None):
    n_rows, d = x.shape
    n = _operand(w).shape[-1] // 5
    row = lambda i: (i, 0)
    nat = (pl.BlockSpec((tm, n), row), jax.ShapeDtypeStruct((n_rows, n), F32))
    kv = nat if t_len is None else _channel_major_out(n_rows, tm, t_len, n)
    outs = [nat, kv, kv, nat, nat,
            (pl.BlockSpec((tm, n), row), jax.ShapeDtypeStruct((n_rows, n), BF16)),
            (pl.BlockSpec((tm, 2 * n), row), jax.ShapeDtypeStruct((n_rows, 2 * n), BF16))]
    return pl.pallas_call(
        functools.partial(_even_in_kernel, channel_major=t_len is not None),
        grid=(n_rows // tm,),
        in_specs=[pl.BlockSpec((tm, d), row), _resident(g), _resident(w), _resident(hm),
                  _resident(gq), _resident(gk), _resident(gb)],
        out_specs=[o[0] for o in outs],
        out_shape=[o[1] for o in outs],
        compiler_params=_cparams(("parallel",)),
        name="even_in",
    )(x, g, _operand(w), hm, gq, gk, gb)


ODD_Q, ODD_K, ODD_V, ODD_QI, ODD_KI, ODD_WI, ODD_GB, ODD_GC, ODD_HD = (
    0, 512, 640, 768, 1024, 1152, 1280, 1792, 2304)
ODD_COLS = 2816


def _odd_in_kernel(x_ref, g_ref, w_ref, hm_ref, gq_ref, gk_ref,
                   q_ref, k_ref, v_ref, qi_ref, ki_ref, wi_ref, gb_ref, cin_ref, *cm_refs):
    h = _rms(x_ref[...], g_ref[...]).astype(BF16)

    def seg(lo, width):
        return _dot(h, w_ref[:, lo:lo + width])

    hm = hm_ref[...]
    q_ref[...] = _head_rms(seg(ODD_Q, 512), hm, gq_ref[...])
    k = _head_rms(seg(ODD_K, 128), hm[:128, :128], gk_ref[...])
    k_ref[...] = k
    v = seg(ODD_V, 128)
    v_ref[...] = v
    qi_ref[...] = seg(ODD_QI, 256)
    ki = seg(ODD_KI, 128)
    ki_ref[...] = ki
    wi_ref[...] = seg(ODD_WI, 128)
    gb_ref[...] = seg(ODD_GB, 512)
    cin_ref[...] = seg(ODD_GC, 512) * seg(ODD_HD, 512)
    if cm_refs:
        kt_ref, vt_ref, kit_ref = cm_refs
        kt_ref[...] = k.T
        vt_ref[...] = v.T
        kit_ref[...] = ki.T[:D_IDX]


def _odd_in(x, g, w, hm, gq, gk, tm, t_len=None):
    n_rows, d = x.shape
    row = lambda i: (i, 0)
    widths = (512, 128, 128, 256, 128, 128, 512, 512)
    outs = [(pl.BlockSpec((tm, n), row), jax.ShapeDtypeStruct((n_rows, n), F32)) for n in widths]
    if t_len is not None:
        outs += [_channel_major_out(n_rows, tm, t_len, n) for n in (128, 128, D_IDX)]
    return pl.pallas_call(
        _odd_in_kernel,
        grid=(n_rows // tm,),
        in_specs=[pl.BlockSpec((tm, d), row), _resident(g), _resident(w), _resident(hm),
                  _resident(gq), _resident(gk)],
        out_specs=[o[0] for o in outs],
        out_shape=[o[1] for o in outs],
        compiler_params=_cparams(("parallel",)),
        name="odd_in",
    )(x, g, _operand(w), hm, gq, gk)


STICK_TK = 256
F32_EXP2_ZERO = -150.0


def _stick_tri():
    half = STICK_TK // 2
    later = (np.arange(half)[:, None] > np.arange(half)[None, :]).astype(np.float32)
    blk = np.concatenate([later, np.ones((half, half), np.float32)], axis=1)
    return jnp.asarray(np.concatenate([blk, blk], axis=0), BF16)


def _stick_kernel(q_ref, k_ref, v_ref, *rest, tq, q_pos0, has_new):
    if has_new:
        kn_ref, vn_ref, tri_ref, o_ref, carry_sc, acc_sc = rest
    else:
        tri_ref, o_ref, carry_sc, acc_sc = rest
    tk = STICK_TK
    half = tk // 2
    n_pairs = q_ref.shape[1] // LANES
    v_split = not has_new and v_ref.shape[1] == 2 * q_ref.shape[1]
    top_rows = lax.broadcasted_iota(jnp.int32, (LANES, 1), 0) < D_HEAD
    q_start = q_pos0 + pl.program_id(1) * tq
    n_full = q_start // tk
    lane = lax.broadcasted_iota(jnp.int32, (1, LANES), 1)
    first = lane < D_HEAD
    q = q_ref[...] * (D_HEAD ** -0.5 * LOG2_E)
    q2 = []
    for p in range(n_pairs):
        qp = q[:, p * LANES:(p + 1) * LANES]
        q2.append(jnp.concatenate([jnp.where(first, qp, 0.0), jnp.where(first, 0.0, qp)], axis=0).astype(BF16))
    tri = tri_ref[...]
    carry_sc[...] = jnp.zeros_like(carry_sc)
    acc_sc[...] = jnp.zeros_like(acc_sc)

    def block(kb, masked):
        channel_major = has_new and not masked
        ks = kb * tk if channel_major else pl.multiple_of(kb * tk, tk)
        if masked:
            row = lax.broadcasted_iota(jnp.int32, (2 * tq, tk), 0)
            qpos = q_start + jnp.where(row >= tq, row - tq, row)
            kpos = ks + lax.broadcasted_iota(jnp.int32, (2 * tq, tk), 1)
            ok = kpos < qpos
        ksrc, vsrc, rows = (kn_ref, vn_ref, pl.ds(0, tk)) if masked and has_new else (k_ref, v_ref, pl.ds(ks, tk))
        for p in range(n_pairs):
            cols = slice(p * LANES, (p + 1) * LANES)
            if channel_major:
                z = _dot(q2[p], k_ref[cols, ks:ks + tk].astype(BF16))
            else:
                z = _dot_t(q2[p], ksrc[rows, cols].astype(BF16))
            nz = -z
            stay = jnp.minimum(nz, 0.0) - jnp.log2(1.0 + jnp.exp2(jnp.minimum(z, nz)))
            if masked:
                stay = jnp.where(ok, stay, 0.0)
            carry = carry_sc[p]
            after = [None, None]
            for h in (1, 0):
                hi, lo = _split(stay[:, h * half:(h + 1) * half])
                r = _dot(jnp.concatenate([hi, lo], axis=1), tri)
                after[h] = r[:, :half] + carry
                carry = carry + r[:, half:]
            carry_sc[p] = carry
            w = jnp.exp2(z + stay + jnp.concatenate(after, axis=1))
            if masked:
                w = jnp.where(ok, w, 0.0)
            w = w.astype(BF16)
            w2 = jnp.concatenate([w[:tq], w[tq:]], axis=1)
            if channel_major:
                vt = v_ref[cols, ks:ks + tk]
                vt2 = jnp.concatenate([jnp.where(top_rows, vt, 0.0), jnp.where(top_rows, 0.0, vt)], axis=1)
                acc_sc[p] += _dot_t(w2, vt2.astype(BF16))
                continue
            if v_split:
                v0 = vsrc[rows, 2 * p * LANES:(2 * p + 1) * LANES]
                v1 = vsrc[rows, (2 * p + 1) * LANES:(2 * p + 2) * LANES]
            else:
                vblk = vsrc[rows, cols]
                v0 = jnp.where(first, vblk, 0.0).astype(BF16)
                v1 = jnp.where(first, 0.0, vblk).astype(BF16)
            acc_sc[p] += _dot(w2, jnp.concatenate([v0, v1], axis=0))

    block(n_full, True)

    def any_weight_left():
        m = jnp.max(jnp.max(carry_sc[...], axis=0), axis=0, keepdims=True)
        return jnp.max(m, axis=1, keepdims=True)[0, 0] > F32_EXP2_ZERO

    if has_new:
        go = any_weight_left()
        for kb in reversed(range(q_pos0 // tk)):
            pl.when(go)(functools.partial(block, kb, False))
            go = go & any_weight_left()
    else:
        def body(s):
            block(s[0], False)
            return s[0] - 1, any_weight_left()

        lax.while_loop(lambda s: (s[0] >= 0) & s[1], body, (n_full - 1, any_weight_left()))
    for p in range(n_pairs):
        o_ref[:, p * LANES:(p + 1) * LANES] = acc_sc[p]


def _stick(q, k, v, tri, tq, q_pos0, new=None):
    b, t, c = q.shape
    tk = STICK_TK
    has_new = new is not None
    kmap = lambda bi, i: (bi, 0, 0)
    if has_new:
        (k_arr, layer), (v_arr, _) = k, v
        lp = k_arr.shape[3]
        assert lp == q_pos0 and t == tq <= tk and new[0].shape[1] == tk and k_arr.shape[2] == c == v_arr.shape[2]
        past_map = lambda bi, i: (layer, bi, 0, 0)
        kv_specs = [pl.BlockSpec((None, None, c, lp), past_map), pl.BlockSpec((None, None, c, lp), past_map),
                    pl.BlockSpec((None, tk, c), kmap), pl.BlockSpec((None, tk, c), kmap)]
        kv_args = (k_arr, v_arr) + tuple(new)
    else:
        lp = k.shape[1]
        assert q_pos0 + t <= lp
        kv_specs = [pl.BlockSpec((None, lp, c), kmap), pl.BlockSpec((None, lp, v.shape[2]), kmap)]
        kv_args = (k, v)
    assert lp % tk == 0 and q_pos0 % tq == 0 and tk % tq == 0
    kern = functools.partial(_stick_kernel, tq=tq, q_pos0=q_pos0, has_new=has_new)
    return pl.pallas_call(
        kern,
        grid=(b, t // tq),
        in_specs=[pl.BlockSpec((None, tq, c), lambda bi, i: (bi, i, 0))] + kv_specs
                 + [pl.BlockSpec(tri.shape, lambda bi, i: (0, 0))],
        out_specs=pl.BlockSpec((None, tq, c), lambda bi, i: (bi, i, 0)),
        out_shape=jax.ShapeDtypeStruct((b, t, c), F32),
        scratch_shapes=[pltpu.VMEM((c // LANES, 2 * tq, LANES), F32), pltpu.VMEM((c // LANES, tq, LANES), F32)],
        compiler_params=_cparams(("parallel", "arbitrary")),
        name="stick_attn",
    )(q, *kv_args, tri)


DSA_CK = 256


def _dsa_kernel(*refs, tq, l_valid, q_pos0, top_k, levels, has_new):
    if len(levels) == 1:
        _dsa_tile(*refs, tq=tq, l_valid=l_valid, q_pos0=q_pos0, top_k=top_k, n_chunks=levels[0], has_new=has_new)
        return
    assert not has_new
    q_last = q_pos0 + pl.program_id(1) * tq + tq - 1
    vis_end = jnp.left_shift(jnp.right_shift(q_last, CHUNK_SHIFT) + 1, CHUNK_SHIFT)
    step = levels[0] * DSA_CK
    for i, n in enumerate(levels):
        pl.when((vis_end + step - 1) // step == i + 1)(functools.partial(
            _dsa_tile, *refs, tq=tq, l_valid=l_valid, q_pos0=q_pos0, top_k=top_k, n_chunks=n, has_new=False))


def _dsa_tile(q_ref, qi_ref, wi_ref, k_ref, v_ref, ki_ref, *rest, tq, l_valid, q_pos0, top_k, n_chunks, has_new):
    if has_new:
        kn_ref, vn_ref, kin_ref, o_ref, kic_sc, kb_sc, vt_sc, key_sc, bias_sc = rest
    else:
        o_ref, kic_sc, kb_sc, vt_sc, key_sc, bias_sc = rest
    ck = DSA_CK
    n_cached = k_ref.shape[1 if has_new else 0] // ck
    lp = (n_cached + 1) * ck if has_new else k_ref.shape[0]
    n_rep = q_ref.shape[1] // LANES
    lane = lax.broadcasted_iota(jnp.int32, (1, LANES), 1)
    first = lane < D_HEAD
    q_start = q_pos0 + pl.program_id(1) * tq
    tl = LANES
    parts = range(tq // tl)
    lanes_of = lambda pt: slice(pt * tl, (pt + 1) * tl)
    qchunk = [jnp.right_shift(q_start + pt * tl + lax.broadcasted_iota(jnp.int32, (1, tl), 1), CHUNK_SHIFT)
              for pt in parts]
    kf = jnp.float32(top_k)
    top_rows = lax.broadcasted_iota(jnp.int32, (LANES, 1), 0) < D_HEAD

    def rows_of(c):
        return pl.ds(c * ck, ck)

    def admissible(c, pt):
        kpos = c * ck + lax.broadcasted_iota(jnp.int32, (ck, 1), 0)
        return (jnp.right_shift(kpos, CHUNK_SHIFT) <= qchunk[pt]) & (kpos < l_valid)

    def chunks(n, step, carry):
        for c in range(n):
            carry = step(c, carry)
        return carry

    @pl.when(pl.program_id(1) == 0)
    def _():
        def prep(c, _):
            rows = rows_of(c)
            if has_new and c < n_cached:
                cols = slice(c * ck, (c + 1) * ck)
                kit = ki_ref[:, cols]
                ki2 = jnp.concatenate([kit, kit], axis=0).T
                kblk = k_ref[:, cols].T
                vt = v_ref[:, cols]
            else:
                ksrc, vsrc, kisrc, src = (kn_ref, vn_ref, kin_ref, rows_of(0)) if has_new else (
                    k_ref, v_ref, ki_ref, rows)
                ki2 = kisrc[src, :]
                kblk = ksrc[src, :]
                vt = vsrc[src, :].T
            hi, lo = _split(ki2)
            kic_sc[rows, :LANES] = jnp.where(first, hi, lo)
            kic_sc[rows, LANES:] = jnp.where(first, hi, jnp.zeros_like(hi))
            kb_sc[rows, :] = kblk.astype(BF16)
            vt_sc[c, :, :ck] = jnp.where(top_rows, vt, 0.0).astype(BF16)
            vt_sc[c, :, ck:] = jnp.where(top_rows, 0.0, vt).astype(BF16)
            return 0
        chunks(lp // ck, prep, 0)

    qic_t, wit = [], []
    for pt in parts:
        qi = qi_ref[lanes_of(pt), :]
        blocks = []
        for h in range(H_IDX):
            grp = qi[:, (h // 2) * LANES:(h // 2 + 1) * LANES]
            swapped = pltpu.roll(grp, D_IDX, 1)
            twice = jnp.where(first, grp, swapped) if h % 2 == 0 else jnp.where(first, swapped, grp)
            x3 = jnp.concatenate([twice, jnp.where(first, twice, 0.0)], axis=1)
            hi = x3.astype(BF16).astype(F32)
            blocks.append(jnp.concatenate([hi[:, :LANES], (x3 - hi)[:, LANES:]], axis=1))
        qic_t.append(jnp.concatenate(blocks, axis=0).T.astype(BF16))
        wit.append(wi_ref[lanes_of(pt), :].T)

    def score_chunk(c, _):
        for pt in parts:
            s = _dot(kic_sc[rows_of(c), :], qic_t[pt])
            score = jnp.zeros((ck, tl), F32)
            for h in range(H_IDX):
                score = score + jnp.maximum(s[:, h * tl:(h + 1) * tl], 0.0) * wit[pt][h:h + 1, :]
            score = score * ((D_IDX ** -0.5) * (H_IDX ** -0.5)) + 0.0
            score = jnp.where(admissible(c, pt), score, NEG)
            bits = lax.bitcast_convert_type(score, jnp.int32)
            key_sc[rows_of(c), lanes_of(pt)] = jnp.where(bits < 0, bits ^ jnp.int32(0x7FFFFFFF), bits)
        return 0

    chunks(n_chunks, score_chunk, 0)

    def count(pred, trials):
        def body(c, accs):
            out = []
            for pt in parts:
                hit = jnp.where(pred(key_sc[rows_of(c), lanes_of(pt)], trials[pt]), 1.0, 0.0)
                out.append(accs[pt] + jnp.sum(hit.reshape(ck // 8, 8, tl), axis=0))
            return out
        accs = chunks(n_chunks, body, [jnp.zeros((8, tl), F32) for _ in parts])
        return [jnp.sum(a, axis=0, keepdims=True) for a in accs]

    ge = lambda a, b: a >= b
    above = count(ge, [jnp.zeros((1, tl), jnp.int32) for _ in parts])
    thr0 = tuple(jnp.where(n >= kf, jnp.int32(0), jnp.int32(-2 ** 31)) for n in above)

    def bit_step(i, thr):
        bit = jnp.left_shift(jnp.int32(1), jnp.int32(30) - i)
        trials = [t + bit for t in thr]
        return tuple(jnp.where(n >= kf, tr, t) for n, tr, t in zip(count(ge, trials), trials, thr))

    thr = lax.fori_loop(0, 31, bit_step, thr0)

    room = [kf - n for n in count(lambda a, b: a > b, thr)]
    rr = lax.broadcasted_iota(jnp.int32, (ck, ck), 0)
    cc = lax.broadcasted_iota(jnp.int32, (ck, ck), 1)
    earlier = jnp.where(cc < rr, 1.0, 0.0).astype(BF16)

    def tie_chunk(c, runs):
        out = []
        for pt in parts:
            keyc = key_sc[rows_of(c), lanes_of(pt)]
            eqf = jnp.where(keyc == thr[pt], 1.0, 0.0)
            rank = _dot(earlier, eqf.astype(BF16)) + runs[pt]
            tie_ok = jnp.where(keyc == thr[pt], jnp.where(rank < room[pt], 0.0, NEG), NEG)
            sel = jnp.where(keyc > thr[pt], 0.0, tie_ok)
            bias_sc[rows_of(c), lanes_of(pt)] = jnp.where(admissible(c, pt), sel, NEG)
            out.append(runs[pt] + jnp.sum(eqf, axis=0, keepdims=True))
        return out

    chunks(n_chunks, tie_chunk, [jnp.zeros((1, tl), F32) for _ in parts])

    qs_t = []
    for pt in parts:
        q = q_ref[lanes_of(pt), :] * (D_HEAD ** -0.5 * LOG2_E)
        per_group = []
        for g in range(2):
            keep = first if g == 0 else jnp.logical_not(first)
            per_group.append(jnp.concatenate(
                [jnp.where(keep, q[:, r * LANES:(r + 1) * LANES], 0.0) for r in range(n_rep)],
                axis=0).T.astype(BF16))
        qs_t.append(per_group)
    width = n_rep * tl

    def att_chunk(c, carries):
        kb = kb_sc[rows_of(c), :]
        out = []
        for pt in parts:
            m0, m1, d0, d1, acc = carries[pt]
            bias = bias_sc[rows_of(c), lanes_of(pt)]
            bias = jnp.concatenate([bias] * n_rep, axis=1)
            l0 = _dot(kb, qs_t[pt][0]) + bias
            l1 = _dot(kb, qs_t[pt][1]) + bias
            n0 = jnp.maximum(m0, jnp.max(l0, axis=0, keepdims=True))
            n1 = jnp.maximum(m1, jnp.max(l1, axis=0, keepdims=True))
            a0, a1 = jnp.exp2(m0 - n0), jnp.exp2(m1 - n1)
            p0, p1 = jnp.exp2(l0 - n0), jnp.exp2(l1 - n1)
            d0 = d0 * a0 + jnp.sum(p0, axis=0, keepdims=True)
            d1 = d1 * a1 + jnp.sum(p1, axis=0, keepdims=True)
            pv = _dot(vt_sc[c], jnp.concatenate([p0.astype(BF16), p1.astype(BF16)], axis=0))
            out.append((n0, n1, d0, d1, acc * jnp.where(top_rows, a0, a1) + pv))
        return out

    neg = jnp.full((1, width), NEG, F32)
    zero = jnp.zeros((1, width), F32)
    done = chunks(n_chunks, att_chunk, [(neg, neg, zero, zero, jnp.zeros((LANES, width), F32)) for _ in parts])
    for pt in parts:
        _, _, d0, d1, acc = done[pt]
        out_t = acc / jnp.where(top_rows, d0, d1)
        for r in range(n_rep):
            o_ref[lanes_of(pt), r * LANES:(r + 1) * LANES] = out_t[:, r * tl:(r + 1) * tl].T


def _dsa(q, qi, wi, k, v, ki, tq, q_pos0, l_valid, new=None):
    b, t, c = q.shape
    has_new = new is not None
    qmap = lambda bi, i: (bi, i, 0)
    kmap = lambda bi, i: (bi, 0, 0)
    if has_new:
        layer = k[1]
        past = k[0].shape[3]
        lp = past + DSA_CK
        assert past == q_pos0 and past % DSA_CK == 0
        past_map = lambda bi, i: (layer, bi, 0, 0)
        key_specs = [pl.BlockSpec((None, None, a[0].shape[2], past), past_map) for a in (k, v, ki)]
        key_specs += [pl.BlockSpec((None, DSA_CK, LANES), kmap)] * 3
        key_args = (k[0], v[0], ki[0]) + tuple(new)
    else:
        lp = k.shape[1]
        key_specs = [pl.BlockSpec((None, lp, LANES), kmap)] * 3
        key_args = (k, v, ki)
    assert lp % DSA_CK == 0 and l_valid <= lp and t % tq == 0 and tq % LANES == 0
    top_k = min(TOPK_MAX, l_valid // 4)
    assert DSA_CK >= top_k
    n_all = lp // DSA_CK
    if q_pos0 == 0 and n_all % 4 == 0:
        levels = tuple(range(n_all // 4, n_all + 1, n_all // 4))
    else:
        levels = (n_all,)
    kern = functools.partial(_dsa_kernel, tq=tq, l_valid=l_valid, q_pos0=q_pos0, top_k=top_k, levels=levels,
                             has_new=has_new)
    return pl.pallas_call(
        kern,
        grid=(b, t // tq),
        in_specs=[pl.BlockSpec((None, tq, c), qmap), pl.BlockSpec((None, tq, qi.shape[2]), qmap),
                  pl.BlockSpec((None, tq, LANES), qmap)] + key_specs,
        out_specs=pl.BlockSpec((None, tq, c), qmap),
        out_shape=jax.ShapeDtypeStruct((b, t, c), F32),
        scratch_shapes=[pltpu.VMEM((lp, 2 * LANES), BF16), pltpu.VMEM((lp, LANES), BF16),
                        pltpu.VMEM((lp // DSA_CK, LANES, 2 * DSA_CK), BF16), pltpu.VMEM((lp, tq), jnp.int32),
                        pltpu.VMEM((lp, tq), F32)],
        compiler_params=_cparams(("parallel", "arbitrary")),
        name="dsa_attn",
    )(q, qi, wi, *key_args)


FFN_CHUNK = 1024


def _ffn_tail(x1, gf_ref, w1_ref, w2_ref, o_ref):
    h = _rms(x1, gf_ref[...]).astype(BF16)
    acc = None
    for c in range(w1_ref.shape[1] // FFN_CHUNK):
        cols = slice(c * FFN_CHUNK, (c + 1) * FFN_CHUNK)
        a = jnp.square(jnp.maximum(_dot(h, w1_ref[:, cols]), 0.0))
        part = _dot(a.astype(BF16), w2_ref[cols, :])
        acc = part if acc is None else acc + part
    o_ref[...] = x1 + acc


def _even_out_kernel(x_ref, att_ref, u_ref, vb_ref, ws_ref, bst_ref, w_ref, gf_ref, w1_ref, w2_ref, o_ref,
                     *, p_len):
    tm = x_ref.shape[0]
    cg = vb_ref.shape[1] // G_B
    rr = lax.broadcasted_iota(jnp.int32, (p_len, p_len), 0)
    cc = lax.broadcasted_iota(jnp.int32, (p_len, p_len), 1)
    causal = jnp.right_shift(cc, CHUNK_SHIFT) <= jnp.right_shift(rr, CHUNK_SHIFT)
    u = u_ref[...]
    vb = vb_ref[...].astype(BF16)
    bst = bst_ref[...]
    mixed_rows = []
    for c in range(tm // p_len):
        cols = []
        for g in range(G_B):
            wg = jnp.where(causal, ws_ref[g], 0.0).astype(BF16)
            mg = _dot(wg, vb[c * p_len:(c + 1) * p_len, g * cg:(g + 1) * cg]) + bst[:, g:g + 1]
            cols.append(mg)
        mixed_rows.append(jnp.concatenate(cols, axis=1))
    gated = u * jnp.concatenate(mixed_rows, axis=0)
    na = att_ref.shape[1]
    y = _dot(att_ref[...].astype(BF16), w_ref[:na, :]) + _dot(gated.astype(BF16), w_ref[na:, :])
    _ffn_tail(x_ref[...] + y, gf_ref, w1_ref, w2_ref, o_ref)


def _even_out_ffn(x, att, u, vb, ws, bst, w, gf, w1, w2, tm, p_len):
    n_rows, d = x.shape
    n = att.shape[1]
    row = lambda i: (i, 0)
    kern = functools.partial(_even_out_kernel, p_len=p_len)
    return pl.pallas_call(
        kern,
        grid=(n_rows // tm,),
        in_specs=[pl.BlockSpec((tm, d), row), pl.BlockSpec((tm, n), row), pl.BlockSpec((tm, n), row),
                  pl.BlockSpec((tm, n), row), _resident(ws), _resident(bst), _resident(w),
                  _resident(gf), _resident(w1), _resident(w2)],
        out_specs=pl.BlockSpec((tm, d), row),
        out_shape=jax.ShapeDtypeStruct((n_rows, d), F32),
        compiler_params=_cparams(("parallel",)),
        name="even_out_ffn",
    )(x, att, u, vb, ws, bst, _operand(w), gf, _operand(w1), _operand(w2))


def _odd_out_kernel(x_ref, att_ref, gb_ref, cin_ref, prev_ref, cw_ref, w_ref, gf_ref, w1_ref, w2_ref, o_ref,
                    *, t_len, has_prev):
    tm = x_ref.shape[0]
    prev = prev_ref[...]
    if not has_prev:
        start = (pl.program_id(0) * tm) % t_len == 0
        prev = jnp.where(start, 0.0, prev)
    cw = cw_ref[...]
    seg = t_len if has_prev else tm
    convs = []
    for s in range(tm // seg):
        cin = cin_ref[s * seg:(s + 1) * seg, :]
        ext = jnp.concatenate([prev[8 * s:8 * (s + 1)], cin], axis=0)
        conv = cw[CONV_W - 1:CONV_W] * cin
        for j in range(1, CONV_W):
            conv = conv + cw[CONV_W - 1 - j:CONV_W - j] * pltpu.roll(ext, j, 0)[8:]
        convs.append(conv)
    gated = gb_ref[...] * (convs[0] if len(convs) == 1 else jnp.concatenate(convs, axis=0))
    na = att_ref.shape[1]
    y = _dot(att_ref[...].astype(BF16), w_ref[:na, :]) + _dot(gated.astype(BF16), w_ref[na:, :])
    _ffn_tail(x_ref[...] + y, gf_ref, w1_ref, w2_ref, o_ref)


def _odd_out_ffn(x, att, gb, cin, prev, cw, w, gf, w1, w2, tm, t_len):
    n_rows, d = x.shape
    n = att.shape[1]
    row = lambda i: (i, 0)
    has_prev = prev is not None
    if has_prev:
        assert tm % t_len == 0 and t_len % 8 == 0
        prev_arr, prev_map, prev_rows = prev, row, 8 * (tm // t_len)
    else:
        assert t_len % tm == 0 and tm % 8 == 0
        prev_arr, prev_map, prev_rows = cin, (lambda i: (jnp.maximum(i * (tm // 8) - 1, 0), 0)), 8
    kern = functools.partial(_odd_out_kernel, t_len=t_len, has_prev=has_prev)
    return pl.pallas_call(
        kern,
        grid=(n_rows // tm,),
        in_specs=[pl.BlockSpec((tm, d), row), pl.BlockSpec((tm, n), row), pl.BlockSpec((tm, n), row),
                  pl.BlockSpec((tm, n), row), pl.BlockSpec((prev_rows, n), prev_map),
                  _resident(cw), _resident(w), _resident(gf), _resident(w1), _resident(w2)],
        out_specs=pl.BlockSpec((tm, d), row),
        out_shape=jax.ShapeDtypeStruct((n_rows, d), F32),
        compiler_params=_cparams(("parallel",)),
        name="odd_out_ffn",
    )(x, att, gb, cin, prev_arr, cw, _operand(w), gf, _operand(w1), _operand(w2))


def _row_tile(n_rows, cap):
    t = min(cap, n_rows)
    while n_rows % t:
        t //= 2
    return t


def _odd_weight_layout(n_q):
    n_heads = n_q // D_HEAD
    rep = n_heads // 2
    perm = np.zeros((n_q,), np.int32)
    for r in range(rep):
        for g in range(2):
            for dd in range(D_HEAD):
                perm[r * LANES + g * D_HEAD + dd] = (g * rep + r) * D_HEAD + dd
    return perm


def kernel(x_prompt, x_sample, cache_a_k, cache_a_v, cache_c_k, cache_c_v, cache_c_kidx, state_d_conv,
           norm_mix_g, norm_ffn_g, w_in_even, gq_a, gk_a, g_b, ws_b, bs_b, w_out_even,
           w_in_odd, gq_c, gk_c, conv_w_d, w_out_odd, w_ffn1, w_ffn2):
    bp, tp, d = x_prompt.shape
    bs, ts, _ = x_sample.shape
    depth = norm_mix_g.shape[0]
    past = cache_a_k.shape[2]
    n_a = cache_a_k.shape[3] * cache_a_k.shape[4]
    n_kv = cache_c_k.shape[3] * cache_c_k.shape[4]
    n_b = g_b.shape[1]
    n_d = conv_w_d.shape[2]
    n_q = w_out_odd.shape[1] - n_d
    assert n_a == 512 and n_b == 512 and n_kv == 128 and n_q == 512 and n_d == 512 and d == 1024

    xp = x_prompt.reshape(bp * tp, d)
    xs = x_sample.reshape(bs * ts, d)
    tm_p = _row_tile(bp * tp, 512)
    tm_s = _row_tile(bs * ts, 512)
    ls = past + ts

    hmean = jnp.asarray(np.kron(np.eye(n_a // D_HEAD), np.full((D_HEAD, D_HEAD), 1.0 / D_HEAD)), BF16)
    perm = _odd_weight_layout(n_q)
    tile8 = lambda gvec: jnp.tile(gvec, n_a // D_HEAD).reshape(1, n_a)

    ak_p, av_p, ak_s, av_s, bv_s = [], [], [], [], []
    ck_p, cv_p, ci_p, ck_s, cv_s, ci_s, dc_p, dc_s = [], [], [], [], [], [], [], []

    tri = _stick_tri()

    channel_major = lambda a: jnp.transpose(a, (0, 1, 3, 4, 2)).reshape(a.shape[0], a.shape[1], -1, a.shape[2])
    past_a_k = channel_major(cache_a_k)
    past_a_v = channel_major(cache_a_v)
    past_c_k = channel_major(cache_c_k)
    past_c_v = channel_major(cache_c_v)
    past_c_ki = jnp.transpose(cache_c_kidx, (0, 1, 3, 2))

    w1_all, w2_all = w_ffn1.astype(BF16), w_ffn2.astype(BF16)
    w_in_even_all, w_out_even_all = w_in_even.astype(BF16), w_out_even.astype(BF16)
    offs = np.cumsum([0, n_q, n_kv, n_kv, H_IDX * D_IDX, D_IDX, H_IDX, n_d, n_d, n_d])
    col = lambda s: w_in_odd[:, :, int(offs[s]):int(offs[s + 1])]
    w_in_odd_all = jnp.concatenate(
        [col(0)[:, :, perm], col(1), col(2), col(3), col(4), col(4), col(5),
         jnp.zeros((w_in_odd.shape[0], d, LANES - H_IDX), F32), col(6), col(7), col(8)], axis=2).astype(BF16)
    assert w_in_odd_all.shape[2] == ODD_COLS
    w_out_odd_all = jnp.concatenate([w_out_odd[:, :n_q][:, perm], w_out_odd[:, n_q:]], axis=1).astype(BF16)

    for i in range(depth):
        j = i // 2
        g_mix = norm_mix_g[i].reshape(1, d)
        ffn = (norm_ffn_g[i].reshape(1, d), (w1_all, i), (w2_all, i))
        if i % 2 == 0:
            w_in = (w_in_even_all, j)
            w_out = (w_out_even_all, j)
            gq, gk, gb = tile8(gq_a[j]), tile8(gk_a[j]), g_b[j].reshape(1, n_b)
            q, kt, vt, u, vb, kb, vm = _even_in(xp, g_mix, w_in, hmean, gq, gk, gb, tm_p, t_len=tp)
            att = _stick(q.reshape(bp, tp, n_a), kb.reshape(bp, tp, n_a), vm.reshape(bp, tp, 2 * n_a), tri,
                         min(tp, LANES), 0)
            xp = _even_out_ffn(xp, att.reshape(bp * tp, n_a), u, vb, ws_b[j], bs_b[j].T, w_out, *ffn, tm_p, B_CHUNK)
            ak_p.append(kt)
            av_p.append(vt)
            q, k, v, u, vb, _, _ = _even_in(xs, g_mix, w_in, hmean, gq, gk, gb, tm_s)
            new_rows = lambda a: jnp.pad(a.reshape(bs, ts, n_a), ((0, 0), (0, STICK_TK - ts), (0, 0)))
            att = _stick(q.reshape(bs, ts, n_a), (past_a_k, j), (past_a_v, j), tri, ts, past,
                         new=(new_rows(k), new_rows(v)))
            xs = _even_out_ffn(xs, att.reshape(bs * ts, n_a), u, vb, ws_b[j][:, :ts, :ts], bs_b[j][:, :ts].T,
                               w_out, *ffn, tm_s, ts)
            ak_s.append(k.reshape(bs, ts, n_a // D_HEAD, D_HEAD))
            av_s.append(v.reshape(bs, ts, n_a // D_HEAD, D_HEAD))
            bv_s.append(vb.reshape(bs, ts, n_b))
        else:
            w_in = (w_in_odd_all, j)
            w_out = (w_out_odd_all, j)
            gq, gk = tile8(gq_c[j]), jnp.tile(gk_c[j], n_kv // D_HEAD).reshape(1, n_kv)
            cw = conv_w_d[j]
            q, k, v, qi, ki, wi, gbt, cin, kt, vt, kit = _odd_in(xp, g_mix, w_in, hmean, gq, gk, tm_p, t_len=tp)
            r3 = lambda a: a.reshape(bp, tp, a.shape[-1])
            att = _dsa(r3(q), r3(qi), r3(wi), r3(k), r3(v), r3(ki), min(tp, 2 * LANES), 0, tp)
            xp = _odd_out_ffn(xp, att.reshape(bp * tp, n_q), gbt, cin, None, cw, w_out, *ffn, tm_p, tp)
            ck_p.append(kt)
            cv_p.append(vt)
            ci_p.append(kit)
            dc_p.append(cin.reshape(bp, tp, n_d)[:, tp - (CONV_W - 1):])
            q, k, v, qi, ki, wi, gbt, cin = _odd_in(xs, g_mix, w_in, hmean, gq, gk, tm_s)
            r3 = lambda a: a.reshape(bs, ts, a.shape[-1])
            pad_q = lambda a: jnp.pad(r3(a), ((0, 0), (0, LANES - ts), (0, 0)))
            new_rows = lambda a: jnp.pad(r3(a), ((0, 0), (0, DSA_CK - ts), (0, 0)))
            att = _dsa(pad_q(q), pad_q(qi), pad_q(wi), (past_c_k, j), (past_c_v, j), (past_c_ki, j),
                       LANES, past, ls, new=(new_rows(k), new_rows(v), new_rows(ki)))[:, :ts]
            prev = jnp.concatenate([jnp.zeros((bs, 8 - (CONV_W - 1), n_d), F32), state_d_conv[j]], axis=1)
            xs = _odd_out_ffn(xs, att.reshape(bs * ts, n_q), gbt, cin, prev.reshape(bs * 8, n_d), cw, w_out, *ffn,
                              tm_s, ts)
            ck_s.append(k.reshape(bs, ts, n_kv // D_HEAD, D_HEAD))
            cv_s.append(v.reshape(bs, ts, n_kv // D_HEAD, D_HEAD))
            ci_s.append(ki[:, :D_IDX].reshape(bs, ts, D_IDX))
            dc_s.append(cin.reshape(bs, ts, n_d)[:, ts - (CONV_W - 1):])

    def heads_last(parts):
        a = jnp.stack(parts)
        return jnp.transpose(a.reshape(a.shape[0], a.shape[1], -1, D_HEAD, a.shape[3]), (0, 1, 4, 2, 3))

    return (xp.reshape(bp, tp, d), xs.reshape(bs, ts, d),
            heads_last(ak_p), heads_last(av_p), jnp.stack(ak_s), jnp.stack(av_s),
            jnp.stack(bv_s),
            heads_last(ck_p), heads_last(cv_p), jnp.transpose(jnp.stack(ci_p), (0, 1, 3, 2)),
            jnp.stack(ck_s), jnp.stack(cv_s), jnp.stack(ci_s),
            jnp.stack(dc_p), jnp.stack(dc_s))
```

```python
import functools

import numpy as np
import jax
import jax.numpy as jnp
from jax import lax
from jax.experimental import pallas as pl
from jax.experimental.pallas import tpu as pltpu

F32 = jnp.float32
BF16 = jnp.bfloat16

EPS = 1e-6
NEG = -1e30
LOG2_E = 1.4426950408889634
CHUNK = 64
CHUNK_SHIFT = CHUNK.bit_length() - 1
D_HEAD = 64
G_B = 4
B_CHUNK = 128
H_IDX = 4
D_IDX = 64
TOPK_MAX = 256
CONV_W = 3
LANES = 128
VMEM_LIMIT = 56 * 1024 * 1024


def _cparams(sem):
    return pltpu.CompilerParams(dimension_semantics=sem, vmem_limit_bytes=VMEM_LIMIT)


def _resident(a):
    if isinstance(a, tuple):
        arr, layer = a
        shape = arr.shape[1:]
        return pl.BlockSpec((None,) + shape, lambda *_: (layer,) + (0,) * len(shape),
                            pipeline_mode=pl.Buffered(1))
    return pl.BlockSpec(a.shape, lambda *_: (0,) * a.ndim, pipeline_mode=pl.Buffered(1))


def _operand(a):
    return a[0] if isinstance(a, tuple) else a


def _dot(a, b):
    return jnp.dot(a, b, preferred_element_type=F32)


def _dot_t(a, b):
    return lax.dot_general(a, b, (((1,), (1,)), ((), ())), preferred_element_type=F32)


def _split(x):
    hi = x.astype(BF16)
    lo = (x - hi.astype(F32)).astype(BF16)
    return hi, lo


def _rms(x, g):
    return x * lax.rsqrt(jnp.mean(x * x, axis=-1, keepdims=True) + EPS) * g


def _head_rms(x, hmean, g):
    ms = _dot((x * x).astype(BF16), hmean)
    return x * lax.rsqrt(ms + EPS) * g


def _gelu(x):
    return jax.nn.gelu(x)


def _even_in_kernel(x_ref, g_ref, w_ref, hm_ref, gq_ref, gk_ref, gb_ref,
                    q_ref, k_ref, v_ref, u_ref, vb_ref, kb_ref, vm_ref, *, channel_major):
    h = _rms(x_ref[...], g_ref[...]).astype(BF16)
    n = q_ref.shape[-1]

    def seg(i):
        return _dot(h, w_ref[:, i * n:(i + 1) * n])

    hm = hm_ref[...]
    q_ref[...] = _head_rms(seg(0), hm, gq_ref[...])
    k = _head_rms(seg(1), hm, gk_ref[...])
    k_ref[...] = k.T if channel_major else k
    kb_ref[...] = k.astype(BF16)
    v = seg(2)
    v_ref[...] = v.T if channel_major else v
    first = lax.broadcasted_iota(jnp.int32, (1, LANES), 1) < D_HEAD
    for p in range(n // LANES):
        vp = v[:, p * LANES:(p + 1) * LANES]
        vm_ref[:, 2 * p * LANES:(2 * p + 1) * LANES] = jnp.where(first, vp, 0.0).astype(BF16)
        vm_ref[:, (2 * p + 1) * LANES:(2 * p + 2) * LANES] = jnp.where(first, 0.0, vp).astype(BF16)
    u_ref[...] = _gelu(seg(3))
    vb_ref[...] = _rms(_gelu(seg(4)), gb_ref[...])


def _channel_major_out(n_rows, tm, t_len, channels):
    per_stream = t_len // tm
    spec = pl.BlockSpec((None, channels, tm), lambda i: (i // per_stream, 0, i % per_stream))
    return spec, jax.ShapeDtypeStruct((n_rows // t_len, channels, t_len), F32)


def _even_in(x, g, w, hm, gq, gk, gb, tm, t_len=None):
    n_rows, d = x.shape
    n = _operand(w).shape[-1] // 5
    row = lambda i: (i, 0)
    nat = (pl.BlockSpec((tm, n), row), jax.ShapeDtypeStruct((n_rows, n), F32))
    kv = nat if t_len is None else _channel_major_out(n_rows, tm, t_len, n)
    outs = [nat, kv, kv, nat, nat,
            (pl.BlockSpec((tm, n), row), jax.ShapeDtypeStruct((n_rows, n), BF16)),
            (pl.BlockSpec((tm, 2 * n), row), jax.ShapeDtypeStruct((n_rows, 2 * n), BF16))]
    return pl.pallas_call(
        functools.partial(_even_in_kernel, channel_major=t_len is not None),
        grid=(n_rows // tm,),
        in_specs=[pl.BlockSpec((tm, d), row), _resident(g), _resident(w), _resident(hm),
                  _resident(gq), _resident(gk), _resident(gb)],
        out_specs=[o[0] for o in outs],
        out_shape=[o[1] for o in outs],
        compiler_params=_cparams(("parallel",)),
        name="even_in",
    )(x, g, _operand(w), hm, gq, gk, gb)


ODD_Q, ODD_K, ODD_V, ODD_QI, ODD_KI, ODD_WI, ODD_GB, ODD_GC, ODD_HD = (
    0, 512, 640, 768, 1024, 1152, 1280, 1792, 2304)
ODD_COLS = 2816


def _odd_in_kernel(x_ref, g_ref, w_ref, hm_ref, gq_ref, gk_ref,
                   q_ref, k_ref, v_ref, qi_ref, ki_ref, wi_ref, gb_ref, cin_ref, *cm_refs):
    h = _rms(x_ref[...], g_ref[...]).astype(BF16)

    def seg(lo, width):
        return _dot(h, w_ref[:, lo:lo + width])

    hm = hm_ref[...]
    q_ref[...] = _head_rms(seg(ODD_Q, 512), hm, gq_ref[...])
    k = _head_rms(seg(ODD_K, 128), hm[:128, :128], gk_ref[...])
    k_ref[...] = k
    v = seg(ODD_V, 128)
    v_ref[...] = v
    qi_ref[...] = seg(ODD_QI, 256)
    ki = seg(ODD_KI, 128)
    ki_ref[...] = ki
    wi_ref[...] = seg(ODD_WI, 128)
    gb_ref[...] = seg(ODD_GB, 512)
    cin_ref[...] = seg(ODD_GC, 512) * seg(ODD_HD, 512)
    if cm_refs:
        kt_ref, vt_ref, kit_ref = cm_refs
        kt_ref[...] = k.T
        vt_ref[...] = v.T
        kit_ref[...] = ki.T[:D_IDX]


def _odd_in(x, g, w, hm, gq, gk, tm, t_len=None):
    n_rows, d = x.shape
    row = lambda i: (i, 0)
    widths = (512, 128, 128, 256, 128, 128, 512, 512)
    outs = [(pl.BlockSpec((tm, n), row), jax.ShapeDtypeStruct((n_rows, n), F32)) for n in widths]
    if t_len is not None:
        outs += [_channel_major_out(n_rows, tm, t_len, n) for n in (128, 128, D_IDX)]
    return pl.pallas_call(
        _odd_in_kernel,
        grid=(n_rows // tm,),
        in_specs=[pl.BlockSpec((tm, d), row), _resident(g), _resident(w), _resident(hm),
                  _resident(gq), _resident(gk)],
        out_specs=[o[0] for o in outs],
        out_shape=[o[1] for o in outs],
        compiler_params=_cparams(("parallel",)),
        name="odd_in",
    )(x, g, _operand(w), hm, gq, gk)


STICK_TK = 256
F32_EXP2_ZERO = -150.0


def _stick_tri():
    half = STICK_TK // 2
    later = (np.arange(half)[:, None] > np.arange(half)[None, :]).astype(np.float32)
    blk = np.concatenate([later, np.ones((half, half), np.float32)], axis=1)
    return jnp.asarray(np.concatenate([blk, blk], axis=0), BF16)


def _stick_kernel(q_ref, k_ref, v_ref, *rest, tq, q_pos0, has_new):
    if has_new:
        kn_ref, vn_ref, tri_ref, o_ref, carry_sc, acc_sc = rest
    else:
        tri_ref, o_ref, carry_sc, acc_sc = rest
    tk = STICK_TK
    half = tk // 2
    n_pairs = q_ref.shape[1] // LANES
    v_split = not has_new and v_ref.shape[1] == 2 * q_ref.shape[1]
    top_rows = lax.broadcasted_iota(jnp.int32, (LANES, 1), 0) < D_HEAD
    q_start = q_pos0 + pl.program_id(1) * tq
    n_full = q_start // tk
    lane = lax.broadcasted_iota(jnp.int32, (1, LANES), 1)
    first = lane < D_HEAD
    q = q_ref[...] * (D_HEAD ** -0.5 * LOG2_E)
    q2 = []
    for p in range(n_pairs):
        qp = q[:, p * LANES:(p + 1) * LANES]
        q2.append(jnp.concatenate([jnp.where(first, qp, 0.0), jnp.where(first, 0.0, qp)], axis=0).astype(BF16))
    tri = tri_ref[...]
    carry_sc[...] = jnp.zeros_like(carry_sc)
    acc_sc[...] = jnp.zeros_like(acc_sc)

    def block(kb, masked, width=tk):
        channel_major = has_new and not masked
        ks = kb * tk if channel_major else pl.multiple_of(kb * tk, tk)
        if masked:
            row = lax.broadcasted_iota(jnp.int32, (2 * tq, width), 0)
            qpos = q_start + jnp.where(row >= tq, row - tq, row)
            kpos = ks + lax.broadcasted_iota(jnp.int32, (2 * tq, width), 1)
            ok = kpos < qpos
        ksrc, vsrc, rows = (kn_ref, vn_ref, pl.ds(0, width)) if masked and has_new else (
            k_ref, v_ref, pl.ds(ks, width))
        for p in range(n_pairs):
            cols = slice(p * LANES, (p + 1) * LANES)
            if channel_major:
                z = _dot(q2[p], k_ref[cols, ks:ks + tk].astype(BF16))
            else:
                z = _dot_t(q2[p], ksrc[rows, cols].astype(BF16))
            nz = -z
            stay = jnp.minimum(nz, 0.0) - jnp.log2(1.0 + jnp.exp2(jnp.minimum(z, nz)))
            if masked:
                stay = jnp.where(ok, stay, 0.0)
            carry = carry_sc[p]
            after = [None] * (width // half)
            for h in reversed(range(width // half)):
                hi, lo = _split(stay[:, h * half:(h + 1) * half])
                r = _dot(jnp.concatenate([hi, lo], axis=1), tri)
                after[h] = r[:, :half] + carry
                carry = carry + r[:, half:]
            carry_sc[p] = carry
            after = after[0] if len(after) == 1 else jnp.concatenate(after, axis=1)
            w = jnp.exp2(z + stay + after)
            if masked:
                w = jnp.where(ok, w, 0.0)
            w = w.astype(BF16)
            w2 = jnp.concatenate([w[:tq], w[tq:]], axis=1)
            if channel_major:
                vt = v_ref[cols, ks:ks + tk]
                vt2 = jnp.concatenate([jnp.where(top_rows, vt, 0.0), jnp.where(top_rows, 0.0, vt)], axis=1)
                acc_sc[p] += _dot_t(w2, vt2.astype(BF16))
                continue
            if v_split:
                v0 = vsrc[rows, 2 * p * LANES:(2 * p + 1) * LANES]
                v1 = vsrc[rows, (2 * p + 1) * LANES:(2 * p + 2) * LANES]
            else:
                vblk = vsrc[rows, cols]
                v0 = jnp.where(first, vblk, 0.0).astype(BF16)
                v1 = jnp.where(first, 0.0, vblk).astype(BF16)
            acc_sc[p] += _dot(w2, jnp.concatenate([v0, v1], axis=0))

    if has_new:
        block(n_full, True, half if tq <= half else tk)
    else:
        reach = q_start - n_full * tk + tq
        pl.when(reach <= half)(functools.partial(block, n_full, True, half))
        pl.when(reach > half)(functools.partial(block, n_full, True, tk))

    def any_weight_left():
        m = jnp.max(jnp.max(carry_sc[...], axis=0), axis=0, keepdims=True)
        return jnp.max(m, axis=1, keepdims=True)[0, 0] > F32_EXP2_ZERO

    if has_new:
        go = any_weight_left()
        for kb in reversed(range(q_pos0 // tk)):
            pl.when(go)(functools.partial(block, kb, False))
            go = go & any_weight_left()
    else:
        def body(s):
            block(s[0], False)
            return s[0] - 1, any_weight_left()

        lax.while_loop(lambda s: (s[0] >= 0) & s[1], body, (n_full - 1, any_weight_left()))
    for p in range(n_pairs):
        o_ref[:, p * LANES:(p + 1) * LANES] = acc_sc[p]


def _stick(q, k, v, tri, tq, q_pos0, new=None):
    b, t, c = q.shape
    tk = STICK_TK
    has_new = new is not None
    kmap = lambda bi, i: (bi, 0, 0)
    if has_new:
        (k_arr, layer), (v_arr, _) = k, v
        lp = k_arr.shape[3]
        assert lp == q_pos0 and t == tq <= tk and new[0].shape[1] == tk and k_arr.shape[2] == c == v_arr.shape[2]
        past_map = lambda bi, i: (layer, bi, 0, 0)
        kv_specs = [pl.BlockSpec((None, None, c, lp), past_map), pl.BlockSpec((None, None, c, lp), past_map),
                    pl.BlockSpec((None, tk, c), kmap), pl.BlockSpec((None, tk, c), kmap)]
        kv_args = (k_arr, v_arr) + tuple(new)
    else:
        lp = k.shape[1]
        assert q_pos0 + t <= lp
        kv_specs = [pl.BlockSpec((None, lp, c), kmap), pl.BlockSpec((None, lp, v.shape[2]), kmap)]
        kv_args = (k, v)
    assert lp % tk == 0 and q_pos0 % tq == 0 and tk % tq == 0
    kern = functools.partial(_stick_kernel, tq=tq, q_pos0=q_pos0, has_new=has_new)
    return pl.pallas_call(
        kern,
        grid=(b, t // tq),
        in_specs=[pl.BlockSpec((None, tq, c), lambda bi, i: (bi, i, 0))] + kv_specs
                 + [pl.BlockSpec(tri.shape, lambda bi, i: (0, 0))],
        out_specs=pl.BlockSpec((None, tq, c), lambda bi, i: (bi, i, 0)),
        out_shape=jax.ShapeDtypeStruct((b, t, c), F32),
        scratch_shapes=[pltpu.VMEM((c // LANES, 2 * tq, LANES), F32), pltpu.VMEM((c // LANES, tq, LANES), F32)],
        compiler_params=_cparams(("parallel", "arbitrary")),
        name="stick_attn",
    )(q, *kv_args, tri)


DSA_CK = 256


def _dsa_kernel(*refs, tq, l_valid, q_pos0, top_k, levels, has_new):
    if len(levels) == 1:
        _dsa_tile(*refs, tq=tq, l_valid=l_valid, q_pos0=q_pos0, top_k=top_k, n_chunks=levels[0], has_new=has_new)
        return
    assert not has_new
    q_last = q_pos0 + pl.program_id(1) * tq + tq - 1
    vis_end = jnp.left_shift(jnp.right_shift(q_last, CHUNK_SHIFT) + 1, CHUNK_SHIFT)
    step = levels[0] * DSA_CK
    for i, n in enumerate(levels):
        pl.when((vis_end + step - 1) // step == i + 1)(functools.partial(
            _dsa_tile, *refs, tq=tq, l_valid=l_valid, q_pos0=q_pos0, top_k=top_k, n_chunks=n, has_new=False))


def _dsa_tile(q_ref, qi_ref, wi_ref, k_ref, v_ref, ki_ref, *rest, tq, l_valid, q_pos0, top_k, n_chunks, has_new):
    if has_new:
        kn_ref, vn_ref, kin_ref, o_ref, kic_sc, kb_sc, vt_sc, key_sc, bias_sc = rest
    else:
        o_ref, kic_sc, kb_sc, vt_sc, key_sc, bias_sc = rest
    ck = DSA_CK
    n_cached = k_ref.shape[1 if has_new else 0] // ck
    lp = (n_cached + 1) * ck if has_new else k_ref.shape[0]
    n_rep = q_ref.shape[1] // LANES
    lane = lax.broadcasted_iota(jnp.int32, (1, LANES), 1)
    first = lane < D_HEAD
    q_start = q_pos0 + pl.program_id(1) * tq
    tl = LANES
    parts = range(tq // tl)
    lanes_of = lambda pt: slice(pt * tl, (pt + 1) * tl)
    qchunk = [jnp.right_shift(q_start + pt * tl + lax.broadcasted_iota(jnp.int32, (1, tl), 1), CHUNK_SHIFT)
              for pt in parts]
    kf = jnp.float32(top_k)
    top_rows = lax.broadcasted_iota(jnp.int32, (LANES, 1), 0) < D_HEAD

    def rows_of(c):
        return pl.ds(c * ck, ck)

    def admissible(c, pt):
        kpos = c * ck + lax.broadcasted_iota(jnp.int32, (ck, 1), 0)
        return (jnp.right_shift(kpos, CHUNK_SHIFT) <= qchunk[pt]) & (kpos < l_valid)

    def chunks(n, step, carry):
        for c in range(n):
            carry = step(c, carry)
        return carry

    @pl.when(pl.program_id(1) == 0)
    def _():
        def prep(c, _):
            rows = rows_of(c)
            if has_new and c < n_cached:
                cols = slice(c * ck, (c + 1) * ck)
                kit = ki_ref[:, cols]
                ki2 = jnp.concatenate([kit, kit], axis=0).T
                kblk = k_ref[:, cols].T
                vt = v_ref[:, cols]
            else:
                ksrc, vsrc, kisrc, src = (kn_ref, vn_ref, kin_ref, rows_of(0)) if has_new else (
                    k_ref, v_ref, ki_ref, rows)
                ki2 = kisrc[src, :]
                kblk = ksrc[src, :]
                vt = vsrc[src, :].T
            hi, lo = _split(ki2)
            kic_sc[rows, :LANES] = jnp.where(first, hi, lo)
            kic_sc[rows, LANES:] = jnp.where(first, hi, jnp.zeros_like(hi))
            kb_sc[rows, :] = kblk.astype(BF16)
            vt_sc[c, :, :ck] = jnp.where(top_rows, vt, 0.0).astype(BF16)
            vt_sc[c, :, ck:] = jnp.where(top_rows, 0.0, vt).astype(BF16)
            return 0
        chunks(lp // ck, prep, 0)

    qic_t, wit = [], []
    for pt in parts:
        qi = qi_ref[lanes_of(pt), :]
        blocks = []
        for h in range(H_IDX):
            grp = qi[:, (h // 2) * LANES:(h // 2 + 1) * LANES]
            swapped = pltpu.roll(grp, D_IDX, 1)
            twice = jnp.where(first, grp, swapped) if h % 2 == 0 else jnp.where(first, swapped, grp)
            x3 = jnp.concatenate([twice, jnp.where(first, twice, 0.0)], axis=1)
            hi = x3.astype(BF16).astype(F32)
            blocks.append(jnp.concatenate([hi[:, :LANES], (x3 - hi)[:, LANES:]], axis=1))
        qic_t.append(jnp.concatenate(blocks, axis=0).T.astype(BF16))
        wit.append(wi_ref[lanes_of(pt), :].T)

    def score_chunk(c, _):
        for pt in parts:
            s = _dot(kic_sc[rows_of(c), :], qic_t[pt])
            score = jnp.zeros((ck, tl), F32)
            for h in range(H_IDX):
                score = score + jnp.maximum(s[:, h * tl:(h + 1) * tl], 0.0) * wit[pt][h:h + 1, :]
            score = score * ((D_IDX ** -0.5) * (H_IDX ** -0.5)) + 0.0
            score = jnp.where(admissible(c, pt), score, NEG)
            bits = lax.bitcast_convert_type(score, jnp.int32)
            key_sc[rows_of(c), lanes_of(pt)] = jnp.where(bits < 0, bits ^ jnp.int32(0x7FFFFFFF), bits)
        return 0

    chunks(n_chunks, score_chunk, 0)

    def count(pred, trials):
        def body(c, accs):
            out = []
            for pt in parts:
                hit = jnp.where(pred(key_sc[rows_of(c), lanes_of(pt)], trials[pt]), 1.0, 0.0)
                out.append(accs[pt] + jnp.sum(hit.reshape(ck // 8, 8, tl), axis=0))
            return out
        accs = chunks(n_chunks, body, [jnp.zeros((8, tl), F32) for _ in parts])
        return [jnp.sum(a, axis=0, keepdims=True) for a in accs]

    ge = lambda a, b: a >= b
    above = count(ge, [jnp.zeros((1, tl), jnp.int32) for _ in parts])
    thr0 = tuple(jnp.where(n >= kf, jnp.int32(0), jnp.int32(-2 ** 31)) for n in above)

    def bit_step(i, thr):
        bit = jnp.left_shift(jnp.int32(1), jnp.int32(30) - i)
        trials = [t + bit for t in thr]
        return tuple(jnp.where(n >= kf, tr, t) for n, tr, t in zip(count(ge, trials), trials, thr))

    thr = lax.fori_loop(0, 31, bit_step, thr0)

    room = [kf - n for n in count(lambda a, b: a > b, thr)]
    rr = lax.broadcasted_iota(jnp.int32, (ck, ck), 0)
    cc = lax.broadcasted_iota(jnp.int32, (ck, ck), 1)
    earlier = jnp.where(cc < rr, 1.0, 0.0).astype(BF16)

    def tie_chunk(c, runs):
        out = []
        for pt in parts:
            keyc = key_sc[rows_of(c), lanes_of(pt)]
            eqf = jnp.where(keyc == thr[pt], 1.0, 0.0)
            rank = _dot(earlier, eqf.astype(BF16)) + runs[pt]
            tie_ok = jnp.where(keyc == thr[pt], jnp.where(rank < room[pt], 0.0, NEG), NEG)
            sel = jnp.where(keyc > thr[pt], 0.0, tie_ok)
            bias_sc[rows_of(c), lanes_of(pt)] = jnp.where(admissible(c, pt), sel, NEG)
            out.append(runs[pt] + jnp.sum(eqf, axis=0, keepdims=True))
        return out

    chunks(n_chunks, tie_chunk, [jnp.zeros((1, tl), F32) for _ in parts])

    qs_t = []
    for pt in parts:
        q = q_ref[lanes_of(pt), :] * (D_HEAD ** -0.5 * LOG2_E)
        per_group = []
        for g in range(2):
            keep = first if g == 0 else jnp.logical_not(first)
            per_group.append(jnp.concatenate(
                [jnp.where(keep, q[:, r * LANES:(r + 1) * LANES], 0.0) for r in range(n_rep)],
                axis=0).T.astype(BF16))
        qs_t.append(per_group)
    width = n_rep * tl

    def att_chunk(c, carries):
        kb = kb_sc[rows_of(c), :]
        out = []
        for pt in parts:
            m0, m1, d0, d1, acc = carries[pt]
            bias = bias_sc[rows_of(c), lanes_of(pt)]
            bias = jnp.concatenate([bias] * n_rep, axis=1)
            l0 = _dot(kb, qs_t[pt][0]) + bias
            l1 = _dot(kb, qs_t[pt][1]) + bias
            n0 = jnp.maximum(m0, jnp.max(l0, axis=0, keepdims=True))
            n1 = jnp.maximum(m1, jnp.max(l1, axis=0, keepdims=True))
            a0, a1 = jnp.exp2(m0 - n0), jnp.exp2(m1 - n1)
            p0, p1 = jnp.exp2(l0 - n0), jnp.exp2(l1 - n1)
            d0 = d0 * a0 + jnp.sum(p0, axis=0, keepdims=True)
            d1 = d1 * a1 + jnp.sum(p1, axis=0, keepdims=True)
            pv = _dot(vt_sc[c], jnp.concatenate([p0.astype(BF16), p1.astype(BF16)], axis=0))
            out.append((n0, n1, d0, d1, acc * jnp.where(top_rows, a0, a1) + pv))
        return out

    neg = jnp.full((1, width), NEG, F32)
    zero = jnp.zeros((1, width), F32)
    done = chunks(n_chunks, att_chunk, [(neg, neg, zero, zero, jnp.zeros((LANES, width), F32)) for _ in parts])
    for pt in parts:
        _, _, d0, d1, acc = done[pt]
        out_t = acc / jnp.where(top_rows, d0, d1)
        for r in range(n_rep):
            o_ref[lanes_of(pt), r * LANES:(r + 1) * LANES] = out_t[:, r * tl:(r + 1) * tl].T


def _dsa(q, qi, wi, k, v, ki, tq, q_pos0, l_valid, new=None):
    b, t, c = q.shape
    has_new = new is not None
    qmap = lambda bi, i: (bi, i, 0)
    kmap = lambda bi, i: (bi, 0, 0)
    if has_new:
        layer = k[1]
        past = k[0].shape[3]
        lp = past + DSA_CK
        assert past == q_pos0 and past % DSA_CK == 0
        past_map = lambda bi, i: (layer, bi, 0, 0)
        key_specs = [pl.BlockSpec((None, None, a[0].shape[2], past), past_map) for a in (k, v, ki)]
        key_specs += [pl.BlockSpec((None, DSA_CK, LANES), kmap)] * 3
        key_args = (k[0], v[0], ki[0]) + tuple(new)
    else:
        lp = k.shape[1]
        key_specs = [pl.BlockSpec((None, lp, LANES), kmap)] * 3
        key_args = (k, v, ki)
    assert lp % DSA_CK == 0 and l_valid <= lp and t % tq == 0 and tq % LANES == 0
    top_k = min(TOPK_MAX, l_valid // 4)
    assert DSA_CK >= top_k
    n_all = lp // DSA_CK
    if q_pos0 == 0 and n_all % 4 == 0:
        levels = tuple(range(n_all // 4, n_all + 1, n_all // 4))
    else:
        levels = (n_all,)
    kern = functools.partial(_dsa_kernel, tq=tq, l_valid=l_valid, q_pos0=q_pos0, top_k=top_k, levels=levels,
                             has_new=has_new)
    return pl.pallas_call(
        kern,
        grid=(b, t // tq),
        in_specs=[pl.BlockSpec((None, tq, c), qmap), pl.BlockSpec((None, tq, qi.shape[2]), qmap),
                  pl.BlockSpec((None, tq, LANES), qmap)] + key_specs,
        out_specs=pl.BlockSpec((None, tq, c), qmap),
        out_shape=jax.ShapeDtypeStruct((b, t, c), F32),
        scratch_shapes=[pltpu.VMEM((lp, 2 * LANES), BF16), pltpu.VMEM((lp, LANES), BF16),
                        pltpu.VMEM((lp // DSA_CK, LANES, 2 * DSA_CK), BF16), pltpu.VMEM((lp, tq), jnp.int32),
                        pltpu.VMEM((lp, tq), F32)],
        compiler_params=_cparams(("parallel", "arbitrary")),
        name="dsa_attn",
    )(q, qi, wi, *key_args)


FFN_CHUNK = 1024


def _ffn_tail(x1, gf_ref, w1_ref, w2_ref, o_ref):
    h = _rms(x1, gf_ref[...]).astype(BF16)
    acc = None
    for c in range(w1_ref.shape[1] // FFN_CHUNK):
        cols = slice(c * FFN_CHUNK, (c + 1) * FFN_CHUNK)
        a = jnp.square(jnp.maximum(_dot(h, w1_ref[:, cols]), 0.0))
        part = _dot(a.astype(BF16), w2_ref[cols, :])
        acc = part if acc is None else acc + part
    o_ref[...] = x1 + acc


def _even_out_kernel(x_ref, att_ref, u_ref, vb_ref, ws_ref, bst_ref, w_ref, gf_ref, w1_ref, w2_ref, o_ref,
                     *, p_len):
    tm = x_ref.shape[0]
    cg = vb_ref.shape[1] // G_B
    rr = lax.broadcasted_iota(jnp.int32, (p_len, p_len), 0)
    cc = lax.broadcasted_iota(jnp.int32, (p_len, p_len), 1)
    causal = jnp.right_shift(cc, CHUNK_SHIFT) <= jnp.right_shift(rr, CHUNK_SHIFT)
    u = u_ref[...]
    vb = vb_ref[...].astype(BF16)
    bst = bst_ref[...]
    mixed_rows = []
    for c in range(tm // p_len):
        cols = []
        for g in range(G_B):
            wg = jnp.where(causal, ws_ref[g], 0.0).astype(BF16)
            mg = _dot(wg, vb[c * p_len:(c + 1) * p_len, g * cg:(g + 1) * cg]) + bst[:, g:g + 1]
            cols.append(mg)
        mixed_rows.append(jnp.concatenate(cols, axis=1))
    gated = u * jnp.concatenate(mixed_rows, axis=0)
    na = att_ref.shape[1]
    y = _dot(att_ref[...].astype(BF16), w_ref[:na, :]) + _dot(gated.astype(BF16), w_ref[na:, :])
    _ffn_tail(x_ref[...] + y, gf_ref, w1_ref, w2_ref, o_ref)


def _even_out_ffn(x, att, u, vb, ws, bst, w, gf, w1, w2, tm, p_len):
    n_rows, d = x.shape
    n = att.shape[1]
    row = lambda i: (i, 0)
    kern = functools.partial(_even_out_kernel, p_len=p_len)
    return pl.pallas_call(
        kern,
        grid=(n_rows // tm,),
        in_specs=[pl.BlockSpec((tm, d), row), pl.BlockSpec((tm, n), row), pl.BlockSpec((tm, n), row),
                  pl.BlockSpec((tm, n), row), _resident(ws), _resident(bst), _resident(w),
                  _resident(gf), _resident(w1), _resident(w2)],
        out_specs=pl.BlockSpec((tm, d), row),
        out_shape=jax.ShapeDtypeStruct((n_rows, d), F32),
        compiler_params=_cparams(("parallel",)),
        name="even_out_ffn",
    )(x, att, u, vb, ws, bst, _operand(w), gf, _operand(w1), _operand(w2))


def _odd_out_kernel(x_ref, att_ref, gb_ref, cin_ref, prev_ref, cw_ref, w_ref, gf_ref, w1_ref, w2_ref, o_ref,
                    *, t_len, has_prev):
    tm = x_ref.shape[0]
    prev = prev_ref[...]
    if not has_prev:
        start = (pl.program_id(0) * tm) % t_len == 0
        prev = jnp.where(start, 0.0, prev)
    cw = cw_ref[...]
    seg = t_len if has_prev else tm
    convs = []
    for s in range(tm // seg):
        cin = cin_ref[s * seg:(s + 1) * seg, :]
        ext = jnp.concatenate([prev[8 * s:8 * (s + 1)], cin], axis=0)
        conv = cw[CONV_W - 1:CONV_W] * cin
        for j in range(1, CONV_W):
            conv = conv + cw[CONV_W - 1 - j:CONV_W - j] * pltpu.roll(ext, j, 0)[8:]
        convs.append(conv)
    gated = gb_ref[...] * (convs[0] if len(convs) == 1 else jnp.concatenate(convs, axis=0))
    na = att_ref.shape[1]
    y = _dot(att_ref[...].astype(BF16), w_ref[:na, :]) + _dot(gated.astype(BF16), w_ref[na:, :])
    _ffn_tail(x_ref[...] + y, gf_ref, w1_ref, w2_ref, o_ref)


def _odd_out_ffn(x, att, gb, cin, prev, cw, w, gf, w1, w2, tm, t_len):
    n_rows, d = x.shape
    n = att.shape[1]
    row = lambda i: (i, 0)
    has_prev = prev is not None
    if has_prev:
        assert tm % t_len == 0 and t_len % 8 == 0
        prev_arr, prev_map, prev_rows = prev, row, 8 * (tm // t_len)
    else:
        assert t_len % tm == 0 and tm % 8 == 0
        prev_arr, prev_map, prev_rows = cin, (lambda i: (jnp.maximum(i * (tm // 8) - 1, 0), 0)), 8
    kern = functools.partial(_odd_out_kernel, t_len=t_len, has_prev=has_prev)
    return pl.pallas_call(
        kern,
        grid=(n_rows // tm,),
        in_specs=[pl.BlockSpec((tm, d), row), pl.BlockSpec((tm, n), row), pl.BlockSpec((tm, n), row),
                  pl.BlockSpec((tm, n), row), pl.BlockSpec((prev_rows, n), prev_map),
                  _resident(cw), _resident(w), _resident(gf), _resident(w1), _resident(w2)],
        out_specs=pl.BlockSpec((tm, d), row),
        out_shape=jax.ShapeDtypeStruct((n_rows, d), F32),
        compiler_params=_cparams(("parallel",)),
        name="odd_out_ffn",
    )(x, att, gb, cin, prev_arr, cw, _operand(w), gf, _operand(w1), _operand(w2))


def _row_tile(n_rows, cap):
    t = min(cap, n_rows)
    while n_rows % t:
        t //= 2
    return t


def _odd_weight_layout(n_q):
    n_heads = n_q // D_HEAD
    rep = n_heads // 2
    perm = np.zeros((n_q,), np.int32)
    for r in range(rep):
        for g in range(2):
            for dd in range(D_HEAD):
                perm[r * LANES + g * D_HEAD + dd] = (g * rep + r) * D_HEAD + dd
    return perm


def kernel(x_prompt, x_sample, cache_a_k, cache_a_v, cache_c_k, cache_c_v, cache_c_kidx, state_d_conv,
           norm_mix_g, norm_ffn_g, w_in_even, gq_a, gk_a, g_b, ws_b, bs_b, w_out_even,
           w_in_odd, gq_c, gk_c, conv_w_d, w_out_odd, w_ffn1, w_ffn2):
    bp, tp, d = x_prompt.shape
    bs, ts, _ = x_sample.shape
    depth = norm_mix_g.shape[0]
    past = cache_a_k.shape[2]
    n_a = cache_a_k.shape[3] * cache_a_k.shape[4]
    n_kv = cache_c_k.shape[3] * cache_c_k.shape[4]
    n_b = g_b.shape[1]
    n_d = conv_w_d.shape[2]
    n_q = w_out_odd.shape[1] - n_d
    assert n_a == 512 and n_b == 512 and n_kv == 128 and n_q == 512 and n_d == 512 and d == 1024

    xp = x_prompt.reshape(bp * tp, d)
    xs = x_sample.reshape(bs * ts, d)
    tm_p = _row_tile(bp * tp, 512)
    tm_s = _row_tile(bs * ts, 512)
    ls = past + ts

    hmean = jnp.asarray(np.kron(np.eye(n_a // D_HEAD), np.full((D_HEAD, D_HEAD), 1.0 / D_HEAD)), BF16)
    perm = _odd_weight_layout(n_q)
    tile8 = lambda gvec: jnp.tile(gvec, n_a // D_HEAD).reshape(1, n_a)

    ak_p, av_p, ak_s, av_s, bv_s = [], [], [], [], []
    ck_p, cv_p, ci_p, ck_s, cv_s, ci_s, dc_p, dc_s = [], [], [], [], [], [], [], []

    tri = _stick_tri()

    channel_major = lambda a: jnp.transpose(a, (0, 1, 3, 4, 2)).reshape(a.shape[0], a.shape[1], -1, a.shape[2])
    past_a_k = channel_major(cache_a_k)
    past_a_v = channel_major(cache_a_v)
    past_c_k = channel_major(cache_c_k)
    past_c_v = channel_major(cache_c_v)
    past_c_ki = jnp.transpose(cache_c_kidx, (0, 1, 3, 2))

    w1_all, w2_all = w_ffn1.astype(BF16), w_ffn2.astype(BF16)
    w_in_even_all, w_out_even_all = w_in_even.astype(BF16), w_out_even.astype(BF16)
    offs = np.cumsum([0, n_q, n_kv, n_kv, H_IDX * D_IDX, D_IDX, H_IDX, n_d, n_d, n_d])
    col = lambda s: w_in_odd[:, :, int(offs[s]):int(offs[s + 1])]
    w_in_odd_all = jnp.concatenate(
        [col(0)[:, :, perm], col(1), col(2), col(3), col(4), col(4), col(5),
         jnp.zeros((w_in_odd.shape[0], d, LANES - H_IDX), F32), col(6), col(7), col(8)], axis=2).astype(BF16)
    assert w_in_odd_all.shape[2] == ODD_COLS
    w_out_odd_all = jnp.concatenate([w_out_odd[:, :n_q][:, perm], w_out_odd[:, n_q:]], axis=1).astype(BF16)

    for i in range(depth):
        j = i // 2
        g_mix = norm_mix_g[i].reshape(1, d)
        ffn = (norm_ffn_g[i].reshape(1, d), (w1_all, i), (w2_all, i))
        if i % 2 == 0:
            w_in = (w_in_even_all, j)
            w_out = (w_out_even_all, j)
            gq, gk, gb = tile8(gq_a[j]), tile8(gk_a[j]), g_b[j].reshape(1, n_b)
            q, kt, vt, u, vb, kb, vm = _even_in(xp, g_mix, w_in, hmean, gq, gk, gb, tm_p, t_len=tp)
            att = _stick(q.reshape(bp, tp, n_a), kb.reshape(bp, tp, n_a), vm.reshape(bp, tp, 2 * n_a), tri,
                         min(tp, LANES), 0)
            xp = _even_out_ffn(xp, att.reshape(bp * tp, n_a), u, vb, ws_b[j], bs_b[j].T, w_out, *ffn, tm_p, B_CHUNK)
            ak_p.append(kt)
            av_p.append(vt)
            q, k, v, u, vb, _, _ = _even_in(xs, g_mix, w_in, hmean, gq, gk, gb, tm_s)
            new_rows = lambda a: jnp.pad(a.reshape(bs, ts, n_a), ((0, 0), (0, STICK_TK - ts), (0, 0)))
            att = _stick(q.reshape(bs, ts, n_a), (past_a_k, j), (past_a_v, j), tri, ts, past,
                         new=(new_rows(k), new_rows(v)))
            xs = _even_out_ffn(xs, att.reshape(bs * ts, n_a), u, vb, ws_b[j][:, :ts, :ts], bs_b[j][:, :ts].T,
                               w_out, *ffn, tm_s, ts)
            ak_s.append(k.reshape(bs, ts, n_a // D_HEAD, D_HEAD))
            av_s.append(v.reshape(bs, ts, n_a // D_HEAD, D_HEAD))
            bv_s.append(vb.reshape(bs, ts, n_b))
        else:
            w_in = (w_in_odd_all, j)
            w_out = (w_out_odd_all, j)
            gq, gk = tile8(gq_c[j]), jnp.tile(gk_c[j], n_kv // D_HEAD).reshape(1, n_kv)
            cw = conv_w_d[j]
            q, k, v, qi, ki, wi, gbt, cin, kt, vt, kit = _odd_in(xp, g_mix, w_in, hmean, gq, gk, tm_p, t_len=tp)
            r3 = lambda a: a.reshape(bp, tp, a.shape[-1])
            att = _dsa(r3(q), r3(qi), r3(wi), r3(k), r3(v), r3(ki), min(tp, 2 * LANES), 0, tp)
            xp = _odd_out_ffn(xp, att.reshape(bp * tp, n_q), gbt, cin, None, cw, w_out, *ffn, tm_p, tp)
            ck_p.append(kt)
            cv_p.append(vt)
            ci_p.append(kit)
            dc_p.append(cin.reshape(bp, tp, n_d)[:, tp - (CONV_W - 1):])
            q, k, v, qi, ki, wi, gbt, cin = _odd_in(xs, g_mix, w_in, hmean, gq, gk, tm_s)
            r3 = lambda a: a.reshape(bs, ts, a.shape[-1])
            pad_q = lambda a: jnp.pad(r3(a), ((0, 0), (0, LANES - ts), (0, 0)))
            new_rows = lambda a: jnp.pad(r3(a), ((0, 0), (0, DSA_CK - ts), (0, 0)))
            att = _dsa(pad_q(q), pad_q(qi), pad_q(wi), (past_c_k, j), (past_c_v, j), (past_c_ki, j),
                       LANES, past, ls, new=(new_rows(k), new_rows(v), new_rows(ki)))[:, :ts]
            prev = jnp.concatenate([jnp.zeros((bs, 8 - (CONV_W - 1), n_d), F32), state_d_conv[j]], axis=1)
            xs = _odd_out_ffn(xs, att.reshape(bs * ts, n_q), gbt, cin, prev.reshape(bs * 8, n_d), cw, w_out, *ffn,
                              tm_s, ts)
            ck_s.append(k.reshape(bs, ts, n_kv // D_HEAD, D_HEAD))
            cv_s.append(v.reshape(bs, ts, n_kv // D_HEAD, D_HEAD))
            ci_s.append(ki[:, :D_IDX].reshape(bs, ts, D_IDX))
            dc_s.append(cin.reshape(bs, ts, n_d)[:, ts - (CONV_W - 1):])

    def heads_last(parts):
        a = jnp.stack(parts)
        return jnp.transpose(a.reshape(a.shape[0], a.shape[1], -1, D_HEAD, a.shape[3]), (0, 1, 4, 2, 3))

    return (xp.reshape(bp, tp, d), xs.reshape(bs, ts, d),
            heads_last(ak_p), heads_last(av_p), jnp.stack(ak_s), jnp.stack(av_s),
            jnp.stack(bv_s),
            heads_last(ck_p), heads_last(cv_p), jnp.transpose(jnp.stack(ci_p), (0, 1, 3, 2)),
            jnp.stack(ck_s), jnp.stack(cv_s), jnp.stack(ci_s),
            jnp.stack(dc_p), jnp.stack(dc_s))
```

```python
import functools

import numpy as np
import jax
import jax.numpy as jnp
from jax import lax
from jax.experimental import pallas as pl
from jax.experimental.pallas import tpu as pltpu

F32 = jnp.float32
BF16 = jnp.bfloat16

EPS = 1e-6
NEG = -1e30
LOG2_E = 1.4426950408889634
CHUNK = 64
CHUNK_SHIFT = CHUNK.bit_length() - 1
D_HEAD = 64
G_B = 4
B_CHUNK = 128
H_IDX = 4
D_IDX = 64
TOPK_MAX = 256
CONV_W = 3
LANES = 128
VMEM_LIMIT = 56 * 1024 * 1024


def _cparams(sem):
    return pltpu.CompilerParams(dimension_semantics=sem, vmem_limit_bytes=VMEM_LIMIT)


def _resident(a):
    if isinstance(a, tuple):
        arr, layer = a
        shape = arr.shape[1:]
        return pl.BlockSpec((None,) + shape, lambda *_: (layer,) + (0,) * len(shape),
                            pipeline_mode=pl.Buffered(1))
    return pl.BlockSpec(a.shape, lambda *_: (0,) * a.ndim, pipeline_mode=pl.Buffered(1))


def _operand(a):
    return a[0] if isinstance(a, tuple) else a


def _dot(a, b):
    return jnp.dot(a, b, preferred_element_type=F32)


def _dot_t(a, b):
    return lax.dot_general(a, b, (((1,), (1,)), ((), ())), preferred_element_type=F32)


def _split(x):
    hi = x.astype(BF16)
    lo = (x - hi.astype(F32)).astype(BF16)
    return hi, lo


def _rms(x, g):
    return x * lax.rsqrt(jnp.mean(x * x, axis=-1, keepdims=True) + EPS) * g


def _head_rms(x, hmean, g):
    ms = _dot((x * x).astype(BF16), hmean)
    return x * lax.rsqrt(ms + EPS) * g


def _gelu(x):
    return jax.nn.gelu(x)


def _even_in_kernel(x_ref, g_ref, w_ref, hm_ref, gq_ref, gk_ref, gb_ref,
                    q_ref, k_ref, v_ref, u_ref, vb_ref, kb_ref, vm_ref, *, channel_major):
    h = _rms(x_ref[...], g_ref[...]).astype(BF16)
    n = q_ref.shape[-1]

    def seg(i):
        return _dot(h, w_ref[:, i * n:(i + 1) * n])

    hm = hm_ref[...]
    q_ref[...] = _head_rms(seg(0), hm, gq_ref[...])
    k = _head_rms(seg(1), hm, gk_ref[...])
    k_ref[...] = k.T if channel_major else k
    kb_ref[...] = k.astype(BF16)
    v = seg(2)
    v_ref[...] = v.T if channel_major else v
    first = lax.broadcasted_iota(jnp.int32, (1, LANES), 1) < D_HEAD
    for p in range(n // LANES):
        vp = v[:, p * LANES:(p + 1) * LANES]
        vm_ref[:, 2 * p * LANES:(2 * p + 1) * LANES] = jnp.where(first, vp, 0.0).astype(BF16)
        vm_ref[:, (2 * p + 1) * LANES:(2 * p + 2) * LANES] = jnp.where(first, 0.0, vp).astype(BF16)
    u_ref[...] = _gelu(seg(3))
    vb_ref[...] = _rms(_gelu(seg(4)), gb_ref[...])


def _channel_major_out(n_rows, tm, t_len, channels):
    per_stream = t_len // tm
    spec = pl.BlockSpec((None, channels, tm), lambda i: (i // per_stream, 0, i % per_stream))
    return spec, jax.ShapeDtypeStruct((n_rows // t_len, channels, t_len), F32)


def _even_in(x, g, w, hm, gq, gk, gb, tm, t_len=None):
    n_rows, d = x.shape
    n = _operand(w).shape[-1] // 5
    row = lambda i: (i, 0)
    nat = (pl.BlockSpec((tm, n), row), jax.ShapeDtypeStruct((n_rows, n), F32))
    kv = nat if t_len is None else _channel_major_out(n_rows, tm, t_len, n)
    outs = [nat, kv, kv, nat, nat,
            (pl.BlockSpec((tm, n), row), jax.ShapeDtypeStruct((n_rows, n), BF16)),
            (pl.BlockSpec((tm, 2 * n), row), jax.ShapeDtypeStruct((n_rows, 2 * n), BF16))]
    return pl.pallas_call(
        functools.partial(_even_in_kernel, channel_major=t_len is not None),
        grid=(n_rows // tm,),
        in_specs=[pl.BlockSpec((tm, d), row), _resident(g), _resident(w), _resident(hm),
                  _resident(gq), _resident(gk), _resident(gb)],
        out_specs=[o[0] for o in outs],
        out_shape=[o[1] for o in outs],
        compiler_params=_cparams(("parallel",)),
        name="even_in",
    )(x, g, _operand(w), hm, gq, gk, gb)


ODD_Q, ODD_K, ODD_V, ODD_QI, ODD_KI, ODD_WI, ODD_GB, ODD_GC, ODD_HD = (
    0, 512, 640, 768, 1024, 1152, 1280, 1792, 2304)
ODD_COLS = 2816


def _odd_in_kernel(x_ref, g_ref, w_ref, hm_ref, gq_ref, gk_ref,
                   q_ref, k_ref, v_ref, qi_ref, ki_ref, wi_ref, gb_ref, cin_ref, *cm_refs):
    h = _rms(x_ref[...], g_ref[...]).astype(BF16)

    def seg(lo, width):
        return _dot(h, w_ref[:, lo:lo + width])

    hm = hm_ref[...]
    q_ref[...] = _head_rms(seg(ODD_Q, 512), hm, gq_ref[...])
    k = _head_rms(seg(ODD_K, 128), hm[:128, :128], gk_ref[...])
    k_ref[...] = k
    v = seg(ODD_V, 128)
    v_ref[...] = v
    qi_ref[...] = seg(ODD_QI, 256)
    ki = seg(ODD_KI, 128)
    ki_ref[...] = ki
    wi_ref[...] = seg(ODD_WI, 128)
    gb_ref[...] = seg(ODD_GB, 512)
    cin_ref[...] = seg(ODD_GC, 512) * seg(ODD_HD, 512)
    if cm_refs:
        kt_ref, vt_ref, kit_ref = cm_refs
        kt_ref[...] = k.T
        vt_ref[...] = v.T
        kit_ref[...] = ki.T[:D_IDX]


def _odd_in(x, g, w, hm, gq, gk, tm, t_len=None):
    n_rows, d = x.shape
    row = lambda i: (i, 0)
    widths = (512, 128, 128, 256, 128, 128, 512, 512)
    outs = [(pl.BlockSpec((tm, n), row), jax.ShapeDtypeStruct((n_rows, n), F32)) for n in widths]
    if t_len is not None:
        outs += [_channel_major_out(n_rows, tm, t_len, n) for n in (128, 128, D_IDX)]
    return pl.pallas_call(
        _odd_in_kernel,
        grid=(n_rows // tm,),
        in_specs=[pl.BlockSpec((tm, d), row), _resident(g), _resident(w), _resident(hm),
                  _resident(gq), _resident(gk)],
        out_specs=[o[0] for o in outs],
        out_shape=[o[1] for o in outs],
        compiler_params=_cparams(("parallel",)),
        name="odd_in",
    )(x, g, _operand(w), hm, gq, gk)


STICK_TK = 256
F32_EXP2_ZERO = -150.0


def _stick_tri():
    half = STICK_TK // 2
    later = (np.arange(half)[:, None] > np.arange(half)[None, :]).astype(np.float32)
    blk = np.concatenate([later, np.ones((half, half), np.float32)], axis=1)
    return jnp.asarray(np.concatenate([blk, blk], axis=0), BF16)


def _stick_kernel(q_ref, k_ref, v_ref, *rest, tq, q_pos0, has_new):
    if has_new:
        kn_ref, vn_ref, tri_ref, o_ref, carry_sc, acc_sc = rest
    else:
        tri_ref, o_ref, carry_sc, acc_sc = rest
    tk = STICK_TK
    half = tk // 2
    n_pairs = q_ref.shape[1] // LANES
    v_split = not has_new and v_ref.shape[1] == 2 * q_ref.shape[1]
    top_rows = lax.broadcasted_iota(jnp.int32, (LANES, 1), 0) < D_HEAD
    q_start = q_pos0 + pl.program_id(1) * tq
    n_full = q_start // tk
    lane = lax.broadcasted_iota(jnp.int32, (1, LANES), 1)
    first = lane < D_HEAD
    q = q_ref[...] * (D_HEAD ** -0.5 * LOG2_E)
    q2 = []
    for p in range(n_pairs):
        qp = q[:, p * LANES:(p + 1) * LANES]
        q2.append(jnp.concatenate([jnp.where(first, qp, 0.0), jnp.where(first, 0.0, qp)], axis=0).astype(BF16))
    tri = tri_ref[...]
    carry_sc[...] = jnp.zeros_like(carry_sc)
    acc_sc[...] = jnp.zeros_like(acc_sc)

    def block(kb, masked):
        channel_major = has_new and not masked
        ks = kb * tk if channel_major else pl.multiple_of(kb * tk, tk)
        if masked:
            row = lax.broadcasted_iota(jnp.int32, (2 * tq, tk), 0)
            qpos = q_start + jnp.where(row >= tq, row - tq, row)
            kpos = ks + lax.broadcasted_iota(jnp.int32, (2 * tq, tk), 1)
            ok = kpos < qpos
        ksrc, vsrc, rows = (kn_ref, vn_ref, pl.ds(0, tk)) if masked and has_new else (k_ref, v_ref, pl.ds(ks, tk))
        for p in range(n_pairs):
            cols = slice(p * LANES, (p + 1) * LANES)
            if channel_major:
                z = _dot(q2[p], k_ref[cols, ks:ks + tk].astype(BF16))
            else:
                z = _dot_t(q2[p], ksrc[rows, cols].astype(BF16))
            nz = -z
            stay = jnp.minimum(nz, 0.0) - jnp.log2(1.0 + jnp.exp2(jnp.minimum(z, nz)))
            if masked:
                stay = jnp.where(ok, stay, 0.0)
            carry = carry_sc[p]
            after = [None, None]
            for h in (1, 0):
                hi, lo = _split(stay[:, h * half:(h + 1) * half])
                r = _dot(jnp.concatenate([hi, lo], axis=1), tri)
                after[h] = r[:, :half] + carry
                carry = carry + r[:, half:]
            carry_sc[p] = carry
            w = jnp.exp2(z + stay + jnp.concatenate(after, axis=1))
            if masked:
                w = jnp.where(ok, w, 0.0)
            w = w.astype(BF16)
            w2 = jnp.concatenate([w[:tq], w[tq:]], axis=1)
            if channel_major:
                vt = v_ref[cols, ks:ks + tk]
                vt2 = jnp.concatenate([jnp.where(top_rows, vt, 0.0), jnp.where(top_rows, 0.0, vt)], axis=1)
                acc_sc[p] += _dot_t(w2, vt2.astype(BF16))
                continue
            if v_split:
                v0 = vsrc[rows, 2 * p * LANES:(2 * p + 1) * LANES]
                v1 = vsrc[rows, (2 * p + 1) * LANES:(2 * p + 2) * LANES]
            else:
                vblk = vsrc[rows, cols]
                v0 = jnp.where(first, vblk, 0.0).astype(BF16)
                v1 = jnp.where(first, 0.0, vblk).astype(BF16)
            acc_sc[p] += _dot(w2, jnp.concatenate([v0, v1], axis=0))

    block(n_full, True)

    def any_weight_left():
        m = jnp.max(jnp.max(carry_sc[...], axis=0), axis=0, keepdims=True)
        return jnp.max(m, axis=1, keepdims=True)[0, 0] > F32_EXP2_ZERO

    if has_new:
        go = any_weight_left()
        for kb in reversed(range(q_pos0 // tk)):
            pl.when(go)(functools.partial(block, kb, False))
            go = go & any_weight_left()
    else:
        def body(s):
            block(s[0], False)
            return s[0] - 1, any_weight_left()

        lax.while_loop(lambda s: (s[0] >= 0) & s[1], body, (n_full - 1, any_weight_left()))
    for p in range(n_pairs):
        o_ref[:, p * LANES:(p + 1) * LANES] = acc_sc[p]


def _stick(q, k, v, tri, tq, q_pos0, new=None):
    b, t, c = q.shape
    tk = STICK_TK
    has_new = new is not None
    kmap = lambda bi, i: (bi, 0, 0)
    if has_new:
        (k_arr, layer), (v_arr, _) = k, v
        lp = k_arr.shape[3]
        assert lp == q_pos0 and t == tq <= tk and new[0].shape[1] == tk and k_arr.shape[2] == c == v_arr.shape[2]
        past_map = lambda bi, i: (layer, bi, 0, 0)
        kv_specs = [pl.BlockSpec((None, None, c, lp), past_map), pl.BlockSpec((None, None, c, lp), past_map),
                    pl.BlockSpec((None, tk, c), kmap), pl.BlockSpec((None, tk, c), kmap)]
        kv_args = (k_arr, v_arr) + tuple(new)
    else:
        lp = k.shape[1]
        assert q_pos0 + t <= lp
        kv_specs = [pl.BlockSpec((None, lp, c), kmap), pl.BlockSpec((None, lp, v.shape[2]), kmap)]
        kv_args = (k, v)
    assert lp % tk == 0 and q_pos0 % tq == 0 and tk % tq == 0
    kern = functools.partial(_stick_kernel, tq=tq, q_pos0=q_pos0, has_new=has_new)
    return pl.pallas_call(
        kern,
        grid=(b, t // tq),
        in_specs=[pl.BlockSpec((None, tq, c), lambda bi, i: (bi, i, 0))] + kv_specs
                 + [pl.BlockSpec(tri.shape, lambda bi, i: (0, 0))],
        out_specs=pl.BlockSpec((None, tq, c), lambda bi, i: (bi, i, 0)),
        out_shape=jax.ShapeDtypeStruct((b, t, c), F32),
        scratch_shapes=[pltpu.VMEM((c // LANES, 2 * tq, LANES), F32), pltpu.VMEM((c // LANES, tq, LANES), F32)],
        compiler_params=_cparams(("parallel", "arbitrary")),
        name="stick_attn",
    )(q, *kv_args, tri)


DSA_CK = 256


def _dsa_kernel(*refs, tq, l_valid, q_pos0, top_k, levels, has_new):
    if len(levels) == 1:
        _dsa_tile(*refs, tq=tq, l_valid=l_valid, q_pos0=q_pos0, top_k=top_k, n_chunks=levels[0], has_new=has_new)
        return
    assert not has_new
    q_last = q_pos0 + pl.program_id(1) * tq + tq - 1
    vis_end = jnp.left_shift(jnp.right_shift(q_last, CHUNK_SHIFT) + 1, CHUNK_SHIFT)
    step = levels[0] * DSA_CK
    for i, n in enumerate(levels):
        pl.when((vis_end + step - 1) // step == i + 1)(functools.partial(
            _dsa_tile, *refs, tq=tq, l_valid=l_valid, q_pos0=q_pos0, top_k=top_k, n_chunks=n, has_new=False))


def _dsa_tile(q_ref, qi_ref, wi_ref, k_ref, v_ref, ki_ref, *rest, tq, l_valid, q_pos0, top_k, n_chunks, has_new):
    if has_new:
        kn_ref, vn_ref, kin_ref, o_ref, kic_sc, kb_sc, vt_sc, key_sc, bias_sc = rest
    else:
        o_ref, kic_sc, kb_sc, vt_sc, key_sc, bias_sc = rest
    ck = DSA_CK
    n_cached = k_ref.shape[1 if has_new else 0] // ck
    lp = (n_cached + 1) * ck if has_new else k_ref.shape[0]
    n_rep = q_ref.shape[1] // LANES
    lane = lax.broadcasted_iota(jnp.int32, (1, LANES), 1)
    first = lane < D_HEAD
    q_start = q_pos0 + pl.program_id(1) * tq
    tl = LANES
    parts = range(tq // tl)
    lanes_of = lambda pt: slice(pt * tl, (pt + 1) * tl)
    qchunk = [jnp.right_shift(q_start + pt * tl + lax.broadcasted_iota(jnp.int32, (1, tl), 1), CHUNK_SHIFT)
              for pt in parts]
    kf = jnp.float32(top_k)
    top_rows = lax.broadcasted_iota(jnp.int32, (LANES, 1), 0) < D_HEAD

    def rows_of(c):
        return pl.ds(c * ck, ck)

    def admissible(c, pt):
        kpos = c * ck + lax.broadcasted_iota(jnp.int32, (ck, 1), 0)
        return (jnp.right_shift(kpos, CHUNK_SHIFT) <= qchunk[pt]) & (kpos < l_valid)

    def chunks(n, step, carry):
        for c in range(n):
            carry = step(c, carry)
        return carry

    @pl.when(pl.program_id(1) == 0)
    def _():
        def prep(c, _):
            rows = rows_of(c)
            if has_new and c < n_cached:
                cols = slice(c * ck, (c + 1) * ck)
                kit = ki_ref[:, cols]
                ki2 = jnp.concatenate([kit, kit], axis=0).T
                kblk = k_ref[:, cols].T
                vt = v_ref[:, cols]
            else:
                ksrc, vsrc, kisrc, src = (kn_ref, vn_ref, kin_ref, rows_of(0)) if has_new else (
                    k_ref, v_ref, ki_ref, rows)
                ki2 = kisrc[src, :]
                kblk = ksrc[src, :]
                vt = vsrc[src, :].T
            hi, lo = _split(ki2)
            kic_sc[rows, :LANES] = jnp.where(first, hi, lo)
            kic_sc[rows, LANES:] = jnp.where(first, hi, jnp.zeros_like(hi))
            kb_sc[rows, :] = kblk.astype(BF16)
            vt_sc[c, :, :ck] = jnp.where(top_rows, vt, 0.0).astype(BF16)
            vt_sc[c, :, ck:] = jnp.where(top_rows, 0.0, vt).astype(BF16)
            return 0
        chunks(lp // ck, prep, 0)

    qic_t, wit = [], []
    for pt in parts:
        qi = qi_ref[lanes_of(pt), :]
        blocks = []
        for h in range(H_IDX):
            grp = qi[:, (h // 2) * LANES:(h // 2 + 1) * LANES]
            swapped = pltpu.roll(grp, D_IDX, 1)
            twice = jnp.where(first, grp, swapped) if h % 2 == 0 else jnp.where(first, swapped, grp)
            x3 = jnp.concatenate([twice, jnp.where(first, twice, 0.0)], axis=1)
            hi = x3.astype(BF16).astype(F32)
            blocks.append(jnp.concatenate([hi[:, :LANES], (x3 - hi)[:, LANES:]], axis=1))
        qic_t.append(jnp.concatenate(blocks, axis=0).T.astype(BF16))
        wit.append(wi_ref[lanes_of(pt), :].T)

    def score_chunk(c, _):
        for pt in parts:
            s = _dot(kic_sc[rows_of(c), :], qic_t[pt])
            score = jnp.zeros((ck, tl), F32)
            for h in range(H_IDX):
                score = score + jnp.maximum(s[:, h * tl:(h + 1) * tl], 0.0) * wit[pt][h:h + 1, :]
            score = score * ((D_IDX ** -0.5) * (H_IDX ** -0.5)) + 0.0
            score = jnp.where(admissible(c, pt), score, NEG)
            bits = lax.bitcast_convert_type(score, jnp.int32)
            key_sc[rows_of(c), lanes_of(pt)] = jnp.where(bits < 0, bits ^ jnp.int32(0x7FFFFFFF), bits)
        return 0

    chunks(n_chunks, score_chunk, 0)

    def count(pred, trials):
        def body(c, accs):
            out = []
            for pt in parts:
                hit = jnp.where(pred(key_sc[rows_of(c), lanes_of(pt)], trials[pt]), 1.0, 0.0)
                out.append(accs[pt] + jnp.sum(hit.reshape(ck // 8, 8, tl), axis=0))
            return out
        accs = chunks(n_chunks, body, [jnp.zeros((8, tl), F32) for _ in parts])
        return [jnp.sum(a, axis=0, keepdims=True) for a in accs]

    ge = lambda a, b: a >= b
    above = count(ge, [jnp.zeros((1, tl), jnp.int32) for _ in parts])
    thr0 = tuple(jnp.where(n >= kf, jnp.int32(0), jnp.int32(-2 ** 31)) for n in above)

    def bit_step(i, thr):
        bit = jnp.left_shift(jnp.int32(1), jnp.int32(30) - i)
        trials = [t + bit for t in thr]
        return tuple(jnp.where(n >= kf, tr, t) for n, tr, t in zip(count(ge, trials), trials, thr))

    thr = lax.fori_loop(0, 31, bit_step, thr0)

    room = [kf - n for n in count(lambda a, b: a > b, thr)]
    rr = lax.broadcasted_iota(jnp.int32, (ck, ck), 0)
    cc = lax.broadcasted_iota(jnp.int32, (ck, ck), 1)
    earlier = jnp.where(cc < rr, 1.0, 0.0).astype(BF16)

    def tie_chunk(c, runs):
        out = []
        for pt in parts:
            keyc = key_sc[rows_of(c), lanes_of(pt)]
            eqf = jnp.where(keyc == thr[pt], 1.0, 0.0)
            rank = _dot(earlier, eqf.astype(BF16)) + runs[pt]
            tie_ok = jnp.where(keyc == thr[pt], jnp.where(rank < room[pt], 0.0, NEG), NEG)
            sel = jnp.where(keyc > thr[pt], 0.0, tie_ok)
            bias_sc[rows_of(c), lanes_of(pt)] = jnp.where(admissible(c, pt), sel, NEG)
            out.append(runs[pt] + jnp.sum(eqf, axis=0, keepdims=True))
        return out

    chunks(n_chunks, tie_chunk, [jnp.zeros((1, tl), F32) for _ in parts])

    qs_t = []
    for pt in parts:
        q = q_ref[lanes_of(pt), :] * (D_HEAD ** -0.5 * LOG2_E)
        per_group = []
        for g in range(2):
            keep = first if g == 0 else jnp.logical_not(first)
            per_group.append(jnp.concatenate(
                [jnp.where(keep, q[:, r * LANES:(r + 1) * LANES], 0.0) for r in range(n_rep)],
                axis=0).T.astype(BF16))
        qs_t.append(per_group)
    width = n_rep * tl

    def att_chunk(c, carries):
        kb = kb_sc[rows_of(c), :]
        out = []
        for pt in parts:
            m0, m1, d0, d1, acc = carries[pt]
            bias = bias_sc[rows_of(c), lanes_of(pt)]
            bias = jnp.concatenate([bias] * n_rep, axis=1)
            l0 = _dot(kb, qs_t[pt][0]) + bias
            l1 = _dot(kb, qs_t[pt][1]) + bias
            n0 = jnp.maximum(m0, jnp.max(l0, axis=0, keepdims=True))
            n1 = jnp.maximum(m1, jnp.max(l1, axis=0, keepdims=True))
            a0, a1 = jnp.exp2(m0 - n0), jnp.exp2(m1 - n1)
            p0, p1 = jnp.exp2(l0 - n0), jnp.exp2(l1 - n1)
            d0 = d0 * a0 + jnp.sum(p0, axis=0, keepdims=True)
            d1 = d1 * a1 + jnp.sum(p1, axis=0, keepdims=True)
            pv = _dot(vt_sc[c], jnp.concatenate([p0.astype(BF16), p1.astype(BF16)], axis=0))
            out.append((n0, n1, d0, d1, acc * jnp.where(top_rows, a0, a1) + pv))
        return out

    neg = jnp.full((1, width), NEG, F32)
    zero = jnp.zeros((1, width), F32)
    done = chunks(n_chunks, att_chunk, [(neg, neg, zero, zero, jnp.zeros((LANES, width), F32)) for _ in parts])
    for pt in parts:
        _, _, d0, d1, acc = done[pt]
        out_t = acc / jnp.where(top_rows, d0, d1)
        for r in range(n_rep):
            o_ref[lanes_of(pt), r * LANES:(r + 1) * LANES] = out_t[:, r * tl:(r + 1) * tl].T


def _dsa(q, qi, wi, k, v, ki, tq, q_pos0, l_valid, new=None):
    b, t, c = q.shape
    has_new = new is not None
    qmap = lambda bi, i: (bi, i, 0)
    kmap = lambda bi, i: (bi, 0, 0)
    if has_new:
        layer = k[1]
        past = k[0].shape[3]
        lp = past + DSA_CK
        assert past == q_pos0 and past % DSA_CK == 0
        past_map = lambda bi, i: (layer, bi, 0, 0)
        key_specs = [pl.BlockSpec((None, None, a[0].shape[2], past), past_map) for a in (k, v, ki)]
        key_specs += [pl.BlockSpec((None, DSA_CK, LANES), kmap)] * 3
        key_args = (k[0], v[0], ki[0]) + tuple(new)
    else:
        lp = k.shape[1]
        key_specs = [pl.BlockSpec((None, lp, LANES), kmap)] * 3
        key_args = (k, v, ki)
    assert lp % DSA_CK == 0 and l_valid <= lp and t % tq == 0 and tq % LANES == 0
    top_k = min(TOPK_MAX, l_valid // 4)
    assert DSA_CK >= top_k
    n_all = lp // DSA_CK
    if q_pos0 == 0 and n_all % 4 == 0:
        levels = tuple(range(n_all // 4, n_all + 1, n_all // 4))
    else:
        levels = (n_all,)
    kern = functools.partial(_dsa_kernel, tq=tq, l_valid=l_valid, q_pos0=q_pos0, top_k=top_k, levels=levels,
                             has_new=has_new)
    return pl.pallas_call(
        kern,
        grid=(b, t // tq),
        in_specs=[pl.BlockSpec((None, tq, c), qmap), pl.BlockSpec((None, tq, qi.shape[2]), qmap),
                  pl.BlockSpec((None, tq, LANES), qmap)] + key_specs,
        out_specs=pl.BlockSpec((None, tq, c), qmap),
        out_shape=jax.ShapeDtypeStruct((b, t, c), F32),
        scratch_shapes=[pltpu.VMEM((lp, 2 * LANES), BF16), pltpu.VMEM((lp, LANES), BF16),
                        pltpu.VMEM((lp // DSA_CK, LANES, 2 * DSA_CK), BF16), pltpu.VMEM((lp, tq), jnp.int32),
                        pltpu.VMEM((lp, tq), F32)],
        compiler_params=_cparams(("parallel", "arbitrary")),
        name="dsa_attn",
    )(q, qi, wi, *key_args)


FFN_CHUNK = 1024


def _ffn_tail(x1, gf_ref, w1_ref, w2_ref, o_ref):
    h = _rms(x1, gf_ref[...]).astype(BF16)
    acc = None
    for c in range(w1_ref.shape[1] // FFN_CHUNK):
        cols = slice(c * FFN_CHUNK, (c + 1) * FFN_CHUNK)
        a = jnp.square(jnp.maximum(_dot(h, w1_ref[:, cols]), 0.0))
        part = _dot(a.astype(BF16), w2_ref[cols, :])
        acc = part if acc is None else acc + part
    o_ref[...] = x1 + acc


def _even_out_kernel(x_ref, att_ref, u_ref, vb_ref, ws_ref, bst_ref, w_ref, gf_ref, w1_ref, w2_ref, o_ref,
                     *, p_len):
    tm = x_ref.shape[0]
    cg = vb_ref.shape[1] // G_B
    rr = lax.broadcasted_iota(jnp.int32, (p_len, p_len), 0)
    cc = lax.broadcasted_iota(jnp.int32, (p_len, p_len), 1)
    causal = jnp.right_shift(cc, CHUNK_SHIFT) <= jnp.right_shift(rr, CHUNK_SHIFT)
    u = u_ref[...]
    vb = vb_ref[...].astype(BF16)
    bst = bst_ref[...]
    mixed_rows = []
    for c in range(tm // p_len):
        cols = []
        for g in range(G_B):
            wg = jnp.where(causal, ws_ref[g], 0.0).astype(BF16)
            mg = _dot(wg, vb[c * p_len:(c + 1) * p_len, g * cg:(g + 1) * cg]) + bst[:, g:g + 1]
            cols.append(mg)
        mixed_rows.append(jnp.concatenate(cols, axis=1))
    gated = u * jnp.concatenate(mixed_rows, axis=0)
    na = att_ref.shape[1]
    y = _dot(att_ref[...].astype(BF16), w_ref[:na, :]) + _dot(gated.astype(BF16), w_ref[na:, :])
    _ffn_tail(x_ref[...] + y, gf_ref, w1_ref, w2_ref, o_ref)


def _even_out_ffn(x, att, u, vb, ws, bst, w, gf, w1, w2, tm, p_len):
    n_rows, d = x.shape
    n = att.shape[1]
    row = lambda i: (i, 0)
    kern = functools.partial(_even_out_kernel, p_len=p_len)
    return pl.pallas_call(
        kern,
        grid=(n_rows // tm,),
        in_specs=[pl.BlockSpec((tm, d), row), pl.BlockSpec((tm, n), row), pl.BlockSpec((tm, n), row),
                  pl.BlockSpec((tm, n), row), _resident(ws), _resident(bst), _resident(w),
                  _resident(gf), _resident(w1), _resident(w2)],
        out_specs=pl.BlockSpec((tm, d), row),
        out_shape=jax.ShapeDtypeStruct((n_rows, d), F32),
        compiler_params=_cparams(("parallel",)),
        name="even_out_ffn",
    )(x, att, u, vb, ws, bst, _operand(w), gf, _operand(w1), _operand(w2))


def _odd_out_kernel(x_ref, att_ref, gb_ref, cin_ref, prev_ref, cw_ref, w_ref, gf_ref, w1_ref, w2_ref, o_ref,
                    *, t_len, has_prev):
    tm = x_ref.shape[0]
    prev = prev_ref[...]
    if not has_prev:
        start = (pl.program_id(0) * tm) % t_len == 0
        prev = jnp.where(start, 0.0, prev)
    cw = cw_ref[...]
    seg = t_len if has_prev else tm
    convs = []
    for s in range(tm // seg):
        cin = cin_ref[s * seg:(s + 1) * seg, :]
        ext = jnp.concatenate([prev[8 * s:8 * (s + 1)], cin], axis=0)
        conv = cw[CONV_W - 1:CONV_W] * cin
        for j in range(1, CONV_W):
            conv = conv + cw[CONV_W - 1 - j:CONV_W - j] * pltpu.roll(ext, j, 0)[8:]
        convs.append(conv)
    gated = gb_ref[...] * (convs[0] if len(convs) == 1 else jnp.concatenate(convs, axis=0))
    na = att_ref.shape[1]
    y = _dot(att_ref[...].astype(BF16), w_ref[:na, :]) + _dot(gated.astype(BF16), w_ref[na:, :])
    _ffn_tail(x_ref[...] + y, gf_ref, w1_ref, w2_ref, o_ref)


def _odd_out_ffn(x, att, gb, cin, prev, cw, w, gf, w1, w2, tm, t_len):
    n_rows, d = x.shape
    n = att.shape[1]
    row = lambda i: (i, 0)
    has_prev = prev is not None
    if has_prev:
        assert tm % t_len == 0 and t_len % 8 == 0
        prev_arr, prev_map, prev_rows = prev, row, 8 * (tm // t_len)
    else:
        assert t_len % tm == 0 and tm % 8 == 0
        prev_arr, prev_map, prev_rows = cin, (lambda i: (jnp.maximum(i * (tm // 8) - 1, 0), 0)), 8
    kern = functools.partial(_odd_out_kernel, t_len=t_len, has_prev=has_prev)
    return pl.pallas_call(
        kern,
        grid=(n_rows // tm,),
        in_specs=[pl.BlockSpec((tm, d), row), pl.BlockSpec((tm, n), row), pl.BlockSpec((tm, n), row),
                  pl.BlockSpec((tm, n), row), pl.BlockSpec((prev_rows, n), prev_map),
                  _resident(cw), _resident(w), _resident(gf), _resident(w1), _resident(w2)],
        out_specs=pl.BlockSpec((tm, d), row),
        out_shape=jax.ShapeDtypeStruct((n_rows, d), F32),
        compiler_params=_cparams(("parallel",)),
        name="odd_out_ffn",
    )(x, att, gb, cin, prev_arr, cw, _operand(w), gf, _operand(w1), _operand(w2))


def _row_tile(n_rows, cap):
    t = min(cap, n_rows)
    while n_rows % t:
        t //= 2
    return t


def _pair_heads(a, axis):
    shape = a.shape
    rep = shape[axis] // (2 * D_HEAD)
    a = a.reshape(shape[:axis] + (2, rep, D_HEAD) + shape[axis + 1:])
    return jnp.swapaxes(a, axis, axis + 1).reshape(shape)


def kernel(x_prompt, x_sample, cache_a_k, cache_a_v, cache_c_k, cache_c_v, cache_c_kidx, state_d_conv,
           norm_mix_g, norm_ffn_g, w_in_even, gq_a, gk_a, g_b, ws_b, bs_b, w_out_even,
           w_in_odd, gq_c, gk_c, conv_w_d, w_out_odd, w_ffn1, w_ffn2):
    bp, tp, d = x_prompt.shape
    bs, ts, _ = x_sample.shape
    depth = norm_mix_g.shape[0]
    past = cache_a_k.shape[2]
    n_a = cache_a_k.shape[3] * cache_a_k.shape[4]
    n_kv = cache_c_k.shape[3] * cache_c_k.shape[4]
    n_b = g_b.shape[1]
    n_d = conv_w_d.shape[2]
    n_q = w_out_odd.shape[1] - n_d
    assert n_a == 512 and n_b == 512 and n_kv == 128 and n_q == 512 and n_d == 512 and d == 1024

    xp = x_prompt.reshape(bp * tp, d)
    xs = x_sample.reshape(bs * ts, d)
    tm_p = _row_tile(bp * tp, 512)
    tm_in = _row_tile(tp, 1024)
    tm_s = _row_tile(bs * ts, 512)
    ls = past + ts

    hmean = jnp.asarray(np.kron(np.eye(n_a // D_HEAD), np.full((D_HEAD, D_HEAD), 1.0 / D_HEAD)), BF16)
    tile8 = lambda gvec: jnp.tile(gvec, n_a // D_HEAD).reshape(1, n_a)

    ak_p, av_p, ak_s, av_s, bv_s = [], [], [], [], []
    ck_p, cv_p, ci_p, ck_s, cv_s, ci_s, dc_p, dc_s = [], [], [], [], [], [], [], []

    tri = _stick_tri()

    channel_major = lambda a: jnp.transpose(a, (0, 1, 3, 4, 2)).reshape(a.shape[0], a.shape[1], -1, a.shape[2])
    past_a_k = channel_major(cache_a_k)
    past_a_v = channel_major(cache_a_v)
    past_c_k = channel_major(cache_c_k)
    past_c_v = channel_major(cache_c_v)
    past_c_ki = jnp.transpose(cache_c_kidx, (0, 1, 3, 2))

    w1_all, w2_all = w_ffn1.astype(BF16), w_ffn2.astype(BF16)
    w_in_even_all, w_out_even_all = w_in_even.astype(BF16), w_out_even.astype(BF16)
    offs = np.cumsum([0, n_q, n_kv, n_kv, H_IDX * D_IDX, D_IDX, H_IDX, n_d, n_d, n_d])
    col = lambda s: w_in_odd[:, :, int(offs[s]):int(offs[s + 1])]
    w_in_odd_all = jnp.concatenate(
        [_pair_heads(col(0), 2), col(1), col(2), col(3), col(4), col(4), col(5),
         jnp.zeros((w_in_odd.shape[0], d, LANES - H_IDX), F32), col(6), col(7), col(8)], axis=2).astype(BF16)
    assert w_in_odd_all.shape[2] == ODD_COLS
    w_out_odd_all = jnp.concatenate([_pair_heads(w_out_odd[:, :n_q], 1), w_out_odd[:, n_q:]], axis=1).astype(BF16)

    for i in range(depth):
        j = i // 2
        g_mix = norm_mix_g[i].reshape(1, d)
        ffn = (norm_ffn_g[i].reshape(1, d), (w1_all, i), (w2_all, i))
        if i % 2 == 0:
            w_in = (w_in_even_all, j)
            w_out = (w_out_even_all, j)
            gq, gk, gb = tile8(gq_a[j]), tile8(gk_a[j]), g_b[j].reshape(1, n_b)
            q, kt, vt, u, vb, kb, vm = _even_in(xp, g_mix, w_in, hmean, gq, gk, gb, tm_in, t_len=tp)
            att = _stick(q.reshape(bp, tp, n_a), kb.reshape(bp, tp, n_a), vm.reshape(bp, tp, 2 * n_a), tri,
                         min(tp, LANES), 0)
            xp = _even_out_ffn(xp, att.reshape(bp * tp, n_a), u, vb, ws_b[j], bs_b[j].T, w_out, *ffn, tm_p, B_CHUNK)
            ak_p.append(kt)
            av_p.append(vt)
            q, k, v, u, vb, _, _ = _even_in(xs, g_mix, w_in, hmean, gq, gk, gb, tm_s)
            new_rows = lambda a: jnp.pad(a.reshape(bs, ts, n_a), ((0, 0), (0, STICK_TK - ts), (0, 0)))
            att = _stick(q.reshape(bs, ts, n_a), (past_a_k, j), (past_a_v, j), tri, ts, past,
                         new=(new_rows(k), new_rows(v)))
            xs = _even_out_ffn(xs, att.reshape(bs * ts, n_a), u, vb, ws_b[j][:, :ts, :ts], bs_b[j][:, :ts].T,
                               w_out, *ffn, tm_s, ts)
            ak_s.append(k.reshape(bs, ts, n_a // D_HEAD, D_HEAD))
            av_s.append(v.reshape(bs, ts, n_a // D_HEAD, D_HEAD))
            bv_s.append(vb.reshape(bs, ts, n_b))
        else:
            w_in = (w_in_odd_all, j)
            w_out = (w_out_odd_all, j)
            gq, gk = tile8(gq_c[j]), jnp.tile(gk_c[j], n_kv // D_HEAD).reshape(1, n_kv)
            cw = conv_w_d[j]
            q, k, v, qi, ki, wi, gbt, cin, kt, vt, kit = _odd_in(xp, g_mix, w_in, hmean, gq, gk, tm_in, t_len=tp)
            r3 = lambda a: a.reshape(bp, tp, a.shape[-1])
            att = _dsa(r3(q), r3(qi), r3(wi), r3(k), r3(v), r3(ki), min(tp, 2 * LANES), 0, tp)
            xp = _odd_out_ffn(xp, att.reshape(bp * tp, n_q), gbt, cin, None, cw, w_out, *ffn, tm_p, tp)
            ck_p.append(kt)
            cv_p.append(vt)
            ci_p.append(kit)
            dc_p.append(cin.reshape(bp, tp, n_d)[:, tp - (CONV_W - 1):])
            q, k, v, qi, ki, wi, gbt, cin = _odd_in(xs, g_mix, w_in, hmean, gq, gk, tm_s)
            r3 = lambda a: a.reshape(bs, ts, a.shape[-1])
            pad_q = lambda a: jnp.pad(r3(a), ((0, 0), (0, LANES - ts), (0, 0)))
            new_rows = lambda a: jnp.pad(r3(a), ((0, 0), (0, DSA_CK - ts), (0, 0)))
            att = _dsa(pad_q(q), pad_q(qi), pad_q(wi), (past_c_k, j), (past_c_v, j), (past_c_ki, j),
                       LANES, past, ls, new=(new_rows(k), new_rows(v), new_rows(ki)))[:, :ts]
            prev = jnp.concatenate([jnp.zeros((bs, 8 - (CONV_W - 1), n_d), F32), state_d_conv[j]], axis=1)
            xs = _odd_out_ffn(xs, att.reshape(bs * ts, n_q), gbt, cin, prev.reshape(bs * 8, n_d), cw, w_out, *ffn,
                              tm_s, ts)
            ck_s.append(k.reshape(bs, ts, n_kv // D_HEAD, D_HEAD))
            cv_s.append(v.reshape(bs, ts, n_kv // D_HEAD, D_HEAD))
            ci_s.append(ki[:, :D_IDX].reshape(bs, ts, D_IDX))
            dc_s.append(cin.reshape(bs, ts, n_d)[:, ts - (CONV_W - 1):])

    def heads_last(parts):
        a = jnp.stack(parts)
        return jnp.transpose(a.reshape(a.shape[0], a.shape[1], -1, D_HEAD, a.shape[3]), (0, 1, 4, 2, 3))

    return (xp.reshape(bp, tp, d), xs.reshape(bs, ts, d),
            heads_last(ak_p), heads_last(av_p), jnp.stack(ak_s), jnp.stack(av_s),
            jnp.stack(bv_s),
            heads_last(ck_p), heads_last(cv_p), jnp.transpose(jnp.stack(ci_p), (0, 1, 3, 2)),
            jnp.stack(ck_s), jnp.stack(cv_s), jnp.stack(ci_s),
            jnp.stack(dc_p), jnp.stack(dc_s))
```

```python
import functools

import numpy as np
import jax
import jax.numpy as jnp
from jax import lax
from jax.experimental import pallas as pl
from jax.experimental.pallas import tpu as pltpu

F32 = jnp.float32
BF16 = jnp.bfloat16

EPS = 1e-6
NEG = -1e30
LOG2_E = 1.4426950408889634
CHUNK = 64
CHUNK_SHIFT = CHUNK.bit_length() - 1
D_HEAD = 64
G_B = 4
B_CHUNK = 128
H_IDX = 4
D_IDX = 64
TOPK_MAX = 256
CONV_W = 3
LANES = 128
VMEM_LIMIT = 56 * 1024 * 1024


def _cparams(sem):
    return pltpu.CompilerParams(dimension_semantics=sem, vmem_limit_bytes=VMEM_LIMIT)


def _resident(a):
    if isinstance(a, tuple):
        arr, layer = a
        shape = arr.shape[1:]
        return pl.BlockSpec((None,) + shape, lambda *_: (layer,) + (0,) * len(shape),
                            pipeline_mode=pl.Buffered(1))
    return pl.BlockSpec(a.shape, lambda *_: (0,) * a.ndim, pipeline_mode=pl.Buffered(1))


def _operand(a):
    return a[0] if isinstance(a, tuple) else a


def _dot(a, b):
    return jnp.dot(a, b, preferred_element_type=F32)


def _dot_t(a, b):
    return lax.dot_general(a, b, (((1,), (1,)), ((), ())), preferred_element_type=F32)


def _split(x):
    hi = x.astype(BF16)
    lo = (x - hi.astype(F32)).astype(BF16)
    return hi, lo


def _rms(x, g):
    return x * lax.rsqrt(jnp.mean(x * x, axis=-1, keepdims=True) + EPS) * g


def _head_rms(x, hmean, g):
    ms = _dot((x * x).astype(BF16), hmean)
    return x * lax.rsqrt(ms + EPS) * g


def _gelu(x):
    return jax.nn.gelu(x)


def _even_in_kernel(x_ref, g_ref, w_ref, hm_ref, gq_ref, gk_ref, gb_ref,
                    q_ref, k_ref, v_ref, u_ref, vb_ref, kb_ref, vm_ref, *, channel_major):
    h = _rms(x_ref[...], g_ref[...]).astype(BF16)
    n = q_ref.shape[-1]

    def seg(i):
        return _dot(h, w_ref[:, i * n:(i + 1) * n])

    hm = hm_ref[...]
    q_ref[...] = _head_rms(seg(0), hm, gq_ref[...])
    k = _head_rms(seg(1), hm, gk_ref[...])
    k_ref[...] = k.T if channel_major else k
    kb_ref[...] = k.astype(BF16)
    v = seg(2)
    v_ref[...] = v.T if channel_major else v
    first = lax.broadcasted_iota(jnp.int32, (1, LANES), 1) < D_HEAD
    for p in range(n // LANES):
        vp = v[:, p * LANES:(p + 1) * LANES]
        vm_ref[:, 2 * p * LANES:(2 * p + 1) * LANES] = jnp.where(first, vp, 0.0).astype(BF16)
        vm_ref[:, (2 * p + 1) * LANES:(2 * p + 2) * LANES] = jnp.where(first, 0.0, vp).astype(BF16)
    u_ref[...] = _gelu(seg(3))
    vb_ref[...] = _rms(_gelu(seg(4)), gb_ref[...])


def _channel_major_out(n_rows, tm, t_len, channels):
    per_stream = t_len // tm
    spec = pl.BlockSpec((None, channels, tm), lambda i: (i // per_stream, 0, i % per_stream))
    return spec, jax.ShapeDtypeStruct((n_rows // t_len, channels, t_len), F32)


def _even_in(x, g, w, hm, gq, gk, gb, tm, t_len=None):
    n_rows, d = x.shape
    n = _operand(w).shape[-1] // 5
    row = lambda i: (i, 0)
    nat = (pl.BlockSpec((tm, n), row), jax.ShapeDtypeStruct((n_rows, n), F32))
    kv = nat if t_len is None else _channel_major_out(n_rows, tm, t_len, n)
    outs = [nat, kv, kv, nat, nat,
            (pl.BlockSpec((tm, n), row), jax.ShapeDtypeStruct((n_rows, n), BF16)),
            (pl.BlockSpec((tm, 2 * n), row), jax.ShapeDtypeStruct((n_rows, 2 * n), BF16))]
    return pl.pallas_call(
        functools.partial(_even_in_kernel, channel_major=t_len is not None),
        grid=(n_rows // tm,),
        in_specs=[pl.BlockSpec((tm, d), row), _resident(g), _resident(w), _resident(hm),
                  _resident(gq), _resident(gk), _resident(gb)],
        out_specs=[o[0] for o in outs],
        out_shape=[o[1] for o in outs],
        compiler_params=_cparams(("parallel",)),
        name="even_in",
    )(x, g, _operand(w), hm, gq, gk, gb)


ODD_Q, ODD_K, ODD_V, ODD_QI, ODD_KI, ODD_WI, ODD_GB, ODD_GC, ODD_HD = (
    0, 512, 640, 768, 1024, 1152, 1280, 1792, 2304)
ODD_COLS = 2816


def _odd_in_kernel(x_ref, g_ref, w_ref, hm_ref, gq_ref, gk_ref,
                   q_ref, k_ref, v_ref, qi_ref, ki_ref, wi_ref, gb_ref, cin_ref, *cm_refs):
    h = _rms(x_ref[...], g_ref[...]).astype(BF16)

    def seg(lo, width):
        return _dot(h, w_ref[:, lo:lo + width])

    hm = hm_ref[...]
    q_ref[...] = _head_rms(seg(ODD_Q, 512), hm, gq_ref[...])
    k = _head_rms(seg(ODD_K, 128), hm[:128, :128], gk_ref[...])
    k_ref[...] = k
    v = seg(ODD_V, 128)
    v_ref[...] = v
    qi_ref[...] = seg(ODD_QI, 256)
    ki = seg(ODD_KI, 128)
    ki_ref[...] = ki
    wi_ref[...] = seg(ODD_WI, 128)
    gb_ref[...] = seg(ODD_GB, 512)
    cin_ref[...] = seg(ODD_GC, 512) * seg(ODD_HD, 512)
    if cm_refs:
        kt_ref, vt_ref, kit_ref = cm_refs
        kt_ref[...] = k.T
        vt_ref[...] = v.T
        kit_ref[...] = ki.T[:D_IDX]


def _odd_in(x, g, w, hm, gq, gk, tm, t_len=None):
    n_rows, d = x.shape
    row = lambda i: (i, 0)
    widths = (512, 128, 128, 256, 128, 128, 512, 512)
    outs = [(pl.BlockSpec((tm, n), row), jax.ShapeDtypeStruct((n_rows, n), F32)) for n in widths]
    if t_len is not None:
        outs += [_channel_major_out(n_rows, tm, t_len, n) for n in (128, 128, D_IDX)]
    return pl.pallas_call(
        _odd_in_kernel,
        grid=(n_rows // tm,),
        in_specs=[pl.BlockSpec((tm, d), row), _resident(g), _resident(w), _resident(hm),
                  _resident(gq), _resident(gk)],
        out_specs=[o[0] for o in outs],
        out_shape=[o[1] for o in outs],
        compiler_params=_cparams(("parallel",)),
        name="odd_in",
    )(x, g, _operand(w), hm, gq, gk)


STICK_TK = 256
F32_EXP2_ZERO = -150.0


def _stick_tri():
    half = STICK_TK // 2
    later = (np.arange(half)[:, None] > np.arange(half)[None, :]).astype(np.float32)
    blk = np.concatenate([later, np.ones((half, half), np.float32)], axis=1)
    return jnp.asarray(np.concatenate([blk, blk], axis=0), BF16)


def _stick_kernel(q_ref, k_ref, v_ref, *rest, tq, q_pos0, has_new):
    if has_new:
        kn_ref, vn_ref, tri_ref, o_ref, carry_sc, acc_sc = rest
    else:
        tri_ref, o_ref, carry_sc, acc_sc = rest
    tk = STICK_TK
    half = tk // 2
    n_pairs = q_ref.shape[1] // LANES
    v_split = not has_new and v_ref.shape[1] == 2 * q_ref.shape[1]
    top_rows = lax.broadcasted_iota(jnp.int32, (LANES, 1), 0) < D_HEAD
    q_start = q_pos0 + pl.program_id(1) * tq
    n_full = q_start // tk
    lane = lax.broadcasted_iota(jnp.int32, (1, LANES), 1)
    first = lane < D_HEAD
    q = q_ref[...] * (D_HEAD ** -0.5 * LOG2_E)
    q2 = []
    for p in range(n_pairs):
        qp = q[:, p * LANES:(p + 1) * LANES]
        q2.append(jnp.concatenate([jnp.where(first, qp, 0.0), jnp.where(first, 0.0, qp)], axis=0).astype(BF16))
    tri = tri_ref[...]
    carry_sc[...] = jnp.zeros_like(carry_sc)
    acc_sc[...] = jnp.zeros_like(acc_sc)

    def block(kb, masked):
        channel_major = has_new and not masked
        ks = kb * tk if channel_major else pl.multiple_of(kb * tk, tk)
        if masked:
            row = lax.broadcasted_iota(jnp.int32, (2 * tq, tk), 0)
            qpos = q_start + jnp.where(row >= tq, row - tq, row)
            kpos = ks + lax.broadcasted_iota(jnp.int32, (2 * tq, tk), 1)
            ok = kpos < qpos
        ksrc, vsrc, rows = (kn_ref, vn_ref, pl.ds(0, tk)) if masked and has_new else (k_ref, v_ref, pl.ds(ks, tk))
        for p in range(n_pairs):
            cols = slice(p * LANES, (p + 1) * LANES)
            if channel_major:
                z = _dot(q2[p], k_ref[cols, ks:ks + tk].astype(BF16))
            else:
                z = _dot_t(q2[p], ksrc[rows, cols].astype(BF16))
            nz = -z
            stay = jnp.minimum(nz, 0.0) - jnp.log2(1.0 + jnp.exp2(jnp.minimum(z, nz)))
            if masked:
                stay = jnp.where(ok, stay, 0.0)
            carry = carry_sc[p]
            after = [None, None]
            for h in (1, 0):
                hi, lo = _split(stay[:, h * half:(h + 1) * half])
                r = _dot(jnp.concatenate([hi, lo], axis=1), tri)
                after[h] = r[:, :half] + carry
                carry = carry + r[:, half:]
            carry_sc[p] = carry
            w = jnp.exp2(z + stay + jnp.concatenate(after, axis=1))
            if masked:
                w = jnp.where(ok, w, 0.0)
            w = w.astype(BF16)
            w2 = jnp.concatenate([w[:tq], w[tq:]], axis=1)
            if channel_major:
                vt = v_ref[cols, ks:ks + tk]
                vt2 = jnp.concatenate([jnp.where(top_rows, vt, 0.0), jnp.where(top_rows, 0.0, vt)], axis=1)
                acc_sc[p] += _dot_t(w2, vt2.astype(BF16))
                continue
            if v_split:
                v0 = vsrc[rows, 2 * p * LANES:(2 * p + 1) * LANES]
                v1 = vsrc[rows, (2 * p + 1) * LANES:(2 * p + 2) * LANES]
            else:
                vblk = vsrc[rows, cols]
                v0 = jnp.where(first, vblk, 0.0).astype(BF16)
                v1 = jnp.where(first, 0.0, vblk).astype(BF16)
            acc_sc[p] += _dot(w2, jnp.concatenate([v0, v1], axis=0))

    block(n_full, True)

    def any_weight_left():
        m = jnp.max(jnp.max(carry_sc[...], axis=0), axis=0, keepdims=True)
        return jnp.max(m, axis=1, keepdims=True)[0, 0] > F32_EXP2_ZERO

    if has_new:
        go = any_weight_left()
        for kb in reversed(range(q_pos0 // tk)):
            pl.when(go)(functools.partial(block, kb, False))
            go = go & any_weight_left()
    else:
        pl.when(n_full > 0)(functools.partial(block, n_full - 1, False))

        def body(s):
            block(s[0], False)
            return s[0] - 1, any_weight_left()

        lax.while_loop(lambda s: (s[0] >= 0) & s[1], body, (n_full - 2, any_weight_left()))
    for p in range(n_pairs):
        o_ref[:, p * LANES:(p + 1) * LANES] = acc_sc[p]


def _stick(q, k, v, tri, tq, q_pos0, new=None):
    b, t, c = q.shape
    tk = STICK_TK
    has_new = new is not None
    kmap = lambda bi, i: (bi, 0, 0)
    if has_new:
        (k_arr, layer), (v_arr, _) = k, v
        lp = k_arr.shape[3]
        assert lp == q_pos0 and t == tq <= tk and new[0].shape[1] == tk and k_arr.shape[2] == c == v_arr.shape[2]
        past_map = lambda bi, i: (layer, bi, 0, 0)
        kv_specs = [pl.BlockSpec((None, None, c, lp), past_map), pl.BlockSpec((None, None, c, lp), past_map),
                    pl.BlockSpec((None, tk, c), kmap), pl.BlockSpec((None, tk, c), kmap)]
        kv_args = (k_arr, v_arr) + tuple(new)
    else:
        lp = k.shape[1]
        assert q_pos0 + t <= lp
        kv_specs = [pl.BlockSpec((None, lp, c), kmap), pl.BlockSpec((None, lp, v.shape[2]), kmap)]
        kv_args = (k, v)
    assert lp % tk == 0 and q_pos0 % tq == 0 and tk % tq == 0
    kern = functools.partial(_stick_kernel, tq=tq, q_pos0=q_pos0, has_new=has_new)
    return pl.pallas_call(
        kern,
        grid=(b, t // tq),
        in_specs=[pl.BlockSpec((None, tq, c), lambda bi, i: (bi, i, 0))] + kv_specs
                 + [pl.BlockSpec(tri.shape, lambda bi, i: (0, 0))],
        out_specs=pl.BlockSpec((None, tq, c), lambda bi, i: (bi, i, 0)),
        out_shape=jax.ShapeDtypeStruct((b, t, c), F32),
        scratch_shapes=[pltpu.VMEM((c // LANES, 2 * tq, LANES), F32), pltpu.VMEM((c // LANES, tq, LANES), F32)],
        compiler_params=_cparams(("parallel", "arbitrary")),
        name="stick_attn",
    )(q, *kv_args, tri)


DSA_CK = 256


def _dsa_kernel(*refs, tq, l_valid, q_pos0, top_k, levels, has_new):
    if len(levels) == 1:
        _dsa_tile(*refs, tq=tq, l_valid=l_valid, q_pos0=q_pos0, top_k=top_k, n_chunks=levels[0], has_new=has_new)
        return
    assert not has_new
    q_last = q_pos0 + pl.program_id(1) * tq + tq - 1
    vis_end = jnp.left_shift(jnp.right_shift(q_last, CHUNK_SHIFT) + 1, CHUNK_SHIFT)
    step = levels[0] * DSA_CK
    for i, n in enumerate(levels):
        pl.when((vis_end + step - 1) // step == i + 1)(functools.partial(
            _dsa_tile, *refs, tq=tq, l_valid=l_valid, q_pos0=q_pos0, top_k=top_k, n_chunks=n, has_new=False))


def _dsa_tile(q_ref, qi_ref, wi_ref, k_ref, v_ref, ki_ref, *rest, tq, l_valid, q_pos0, top_k, n_chunks, has_new):
    if has_new:
        kn_ref, vn_ref, kin_ref, o_ref, kic_sc, kb_sc, vt_sc, key_sc, bias_sc = rest
    else:
        o_ref, kic_sc, kb_sc, vt_sc, key_sc, bias_sc = rest
    ck = DSA_CK
    n_cached = k_ref.shape[1 if has_new else 0] // ck
    lp = (n_cached + 1) * ck if has_new else k_ref.shape[0]
    n_rep = q_ref.shape[1] // LANES
    lane = lax.broadcasted_iota(jnp.int32, (1, LANES), 1)
    first = lane < D_HEAD
    q_start = q_pos0 + pl.program_id(1) * tq
    tl = LANES
    parts = range(tq // tl)
    lanes_of = lambda pt: slice(pt * tl, (pt + 1) * tl)
    qchunk = [jnp.right_shift(q_start + pt * tl + lax.broadcasted_iota(jnp.int32, (1, tl), 1), CHUNK_SHIFT)
              for pt in parts]
    kf = jnp.float32(top_k)
    top_rows = lax.broadcasted_iota(jnp.int32, (LANES, 1), 0) < D_HEAD

    def rows_of(c):
        return pl.ds(c * ck, ck)

    def admissible(c, pt):
        kpos = c * ck + lax.broadcasted_iota(jnp.int32, (ck, 1), 0)
        return (jnp.right_shift(kpos, CHUNK_SHIFT) <= qchunk[pt]) & (kpos < l_valid)

    def chunks(n, step, carry):
        for c in range(n):
            carry = step(c, carry)
        return carry

    @pl.when(pl.program_id(1) == 0)
    def _():
        def prep(c, _):
            rows = rows_of(c)
            if has_new and c < n_cached:
                cols = slice(c * ck, (c + 1) * ck)
                kit = ki_ref[:, cols]
                ki2 = jnp.concatenate([kit, kit], axis=0).T
                kblk = k_ref[:, cols].T
                vt = v_ref[:, cols]
            else:
                ksrc, vsrc, kisrc, src = (kn_ref, vn_ref, kin_ref, rows_of(0)) if has_new else (
                    k_ref, v_ref, ki_ref, rows)
                ki2 = kisrc[src, :]
                kblk = ksrc[src, :]
                vt = vsrc[src, :].T
            hi, lo = _split(ki2)
            kic_sc[rows, :LANES] = jnp.where(first, hi, lo)
            kic_sc[rows, LANES:] = jnp.where(first, hi, jnp.zeros_like(hi))
            kb_sc[rows, :] = kblk.astype(BF16)
            vt_sc[c, :, :ck] = jnp.where(top_rows, vt, 0.0).astype(BF16)
            vt_sc[c, :, ck:] = jnp.where(top_rows, 0.0, vt).astype(BF16)
            return 0
        chunks(lp // ck, prep, 0)

    qic_t, wit = [], []
    for pt in parts:
        qi = qi_ref[lanes_of(pt), :]
        blocks = []
        for h in range(H_IDX):
            grp = qi[:, (h // 2) * LANES:(h // 2 + 1) * LANES]
            swapped = pltpu.roll(grp, D_IDX, 1)
            twice = jnp.where(first, grp, swapped) if h % 2 == 0 else jnp.where(first, swapped, grp)
            x3 = jnp.concatenate([twice, jnp.where(first, twice, 0.0)], axis=1)
            hi = x3.astype(BF16).astype(F32)
            blocks.append(jnp.concatenate([hi[:, :LANES], (x3 - hi)[:, LANES:]], axis=1))
        qic_t.append(jnp.concatenate(blocks, axis=0).T.astype(BF16))
        wit.append(wi_ref[lanes_of(pt), :].T)

    def score_chunk(c, _):
        for pt in parts:
            s = _dot(kic_sc[rows_of(c), :], qic_t[pt])
            score = jnp.zeros((ck, tl), F32)
            for h in range(H_IDX):
                score = score + jnp.maximum(s[:, h * tl:(h + 1) * tl], 0.0) * wit[pt][h:h + 1, :]
            score = score * ((D_IDX ** -0.5) * (H_IDX ** -0.5)) + 0.0
            score = jnp.where(admissible(c, pt), score, NEG)
            bits = lax.bitcast_convert_type(score, jnp.int32)
            key_sc[rows_of(c), lanes_of(pt)] = jnp.where(bits < 0, bits ^ jnp.int32(0x7FFFFFFF), bits)
        return 0

    chunks(n_chunks, score_chunk, 0)

    def count(pred, trials):
        def body(c, accs):
            out = []
            for pt in parts:
                hit = jnp.where(pred(key_sc[rows_of(c), lanes_of(pt)], trials[pt]), 1.0, 0.0)
                out.append(accs[pt] + jnp.sum(hit.reshape(ck // 8, 8, tl), axis=0))
            return out
        accs = chunks(n_chunks, body, [jnp.zeros((8, tl), F32) for _ in parts])
        return [jnp.sum(a, axis=0, keepdims=True) for a in accs]

    ge = lambda a, b: a >= b
    above = count(ge, [jnp.zeros((1, tl), jnp.int32) for _ in parts])
    thr0 = tuple(jnp.where(n >= kf, jnp.int32(0), jnp.int32(-2 ** 31)) for n in above)

    def bit_step(i, thr):
        bit = jnp.left_shift(jnp.int32(1), jnp.int32(30) - i)
        trials = [t + bit for t in thr]
        return tuple(jnp.where(n >= kf, tr, t) for n, tr, t in zip(count(ge, trials), trials, thr))

    thr = lax.fori_loop(0, 31, bit_step, thr0)

    room = [kf - n for n in count(lambda a, b: a > b, thr)]
    rr = lax.broadcasted_iota(jnp.int32, (ck, ck), 0)
    cc = lax.broadcasted_iota(jnp.int32, (ck, ck), 1)
    earlier = jnp.where(cc < rr, 1.0, 0.0).astype(BF16)

    def tie_chunk(c, runs):
        out = []
        for pt in parts:
            keyc = key_sc[rows_of(c), lanes_of(pt)]
            eqf = jnp.where(keyc == thr[pt], 1.0, 0.0)
            rank = _dot(earlier, eqf.astype(BF16)) + runs[pt]
            tie_ok = jnp.where(keyc == thr[pt], jnp.where(rank < room[pt], 0.0, NEG), NEG)
            sel = jnp.where(keyc > thr[pt], 0.0, tie_ok)
            bias_sc[rows_of(c), lanes_of(pt)] = jnp.where(admissible(c, pt), sel, NEG)
            out.append(runs[pt] + jnp.sum(eqf, axis=0, keepdims=True))
        return out

    chunks(n_chunks, tie_chunk, [jnp.zeros((1, tl), F32) for _ in parts])

    qs_t = []
    for pt in parts:
        q = q_ref[lanes_of(pt), :] * (D_HEAD ** -0.5 * LOG2_E)
        per_group = []
        for g in range(2):
            keep = first if g == 0 else jnp.logical_not(first)
            per_group.append(jnp.concatenate(
                [jnp.where(keep, q[:, r * LANES:(r + 1) * LANES], 0.0) for r in range(n_rep)],
                axis=0).T.astype(BF16))
        qs_t.append(per_group)
    width = n_rep * tl

    def att_chunk(c, carries):
        kb = kb_sc[rows_of(c), :]
        out = []
        for pt in parts:
            m0, m1, d0, d1, acc = carries[pt]
            bias = bias_sc[rows_of(c), lanes_of(pt)]
            bias = jnp.concatenate([bias] * n_rep, axis=1)
            l0 = _dot(kb, qs_t[pt][0]) + bias
            l1 = _dot(kb, qs_t[pt][1]) + bias
            n0 = jnp.maximum(m0, jnp.max(l0, axis=0, keepdims=True))
            n1 = jnp.maximum(m1, jnp.max(l1, axis=0, keepdims=True))
            a0, a1 = jnp.exp2(m0 - n0), jnp.exp2(m1 - n1)
            p0, p1 = jnp.exp2(l0 - n0), jnp.exp2(l1 - n1)
            d0 = d0 * a0 + jnp.sum(p0, axis=0, keepdims=True)
            d1 = d1 * a1 + jnp.sum(p1, axis=0, keepdims=True)
            pv = _dot(vt_sc[c], jnp.concatenate([p0.astype(BF16), p1.astype(BF16)], axis=0))
            out.append((n0, n1, d0, d1, acc * jnp.where(top_rows, a0, a1) + pv))
        return out

    neg = jnp.full((1, width), NEG, F32)
    zero = jnp.zeros((1, width), F32)
    done = chunks(n_chunks, att_chunk, [(neg, neg, zero, zero, jnp.zeros((LANES, width), F32)) for _ in parts])
    for pt in parts:
        _, _, d0, d1, acc = done[pt]
        out_t = acc / jnp.where(top_rows, d0, d1)
        for r in range(n_rep):
            o_ref[lanes_of(pt), r * LANES:(r + 1) * LANES] = out_t[:, r * tl:(r + 1) * tl].T


def _dsa(q, qi, wi, k, v, ki, tq, q_pos0, l_valid, new=None):
    b, t, c = q.shape
    has_new = new is not None
    qmap = lambda bi, i: (bi, i, 0)
    kmap = lambda bi, i: (bi, 0, 0)
    if has_new:
        layer = k[1]
        past = k[0].shape[3]
        lp = past + DSA_CK
        assert past == q_pos0 and past % DSA_CK == 0
        past_map = lambda bi, i: (layer, bi, 0, 0)
        key_specs = [pl.BlockSpec((None, None, a[0].shape[2], past), past_map) for a in (k, v, ki)]
        key_specs += [pl.BlockSpec((None, DSA_CK, LANES), kmap)] * 3
        key_args = (k[0], v[0], ki[0]) + tuple(new)
    else:
        lp = k.shape[1]
        key_specs = [pl.BlockSpec((None, lp, LANES), kmap)] * 3
        key_args = (k, v, ki)
    assert lp % DSA_CK == 0 and l_valid <= lp and t % tq == 0 and tq % LANES == 0
    top_k = min(TOPK_MAX, l_valid // 4)
    assert DSA_CK >= top_k
    n_all = lp // DSA_CK
    if q_pos0 == 0 and n_all % 4 == 0:
        levels = tuple(range(n_all // 4, n_all + 1, n_all // 4))
    else:
        levels = (n_all,)
    kern = functools.partial(_dsa_kernel, tq=tq, l_valid=l_valid, q_pos0=q_pos0, top_k=top_k, levels=levels,
                             has_new=has_new)
    return pl.pallas_call(
        kern,
        grid=(b, t // tq),
        in_specs=[pl.BlockSpec((None, tq, c), qmap), pl.BlockSpec((None, tq, qi.shape[2]), qmap),
                  pl.BlockSpec((None, tq, LANES), qmap)] + key_specs,
        out_specs=pl.BlockSpec((None, tq, c), qmap),
        out_shape=jax.ShapeDtypeStruct((b, t, c), F32),
        scratch_shapes=[pltpu.VMEM((lp, 2 * LANES), BF16), pltpu.VMEM((lp, LANES), BF16),
                        pltpu.VMEM((lp // DSA_CK, LANES, 2 * DSA_CK), BF16), pltpu.VMEM((lp, tq), jnp.int32),
                        pltpu.VMEM((lp, tq), F32)],
        compiler_params=_cparams(("parallel", "arbitrary")),
        name="dsa_attn",
    )(q, qi, wi, *key_args)


FFN_CHUNK = 1024


def _ffn_tail(x1, gf_ref, w1_ref, w2_ref, o_ref):
    h = _rms(x1, gf_ref[...]).astype(BF16)
    acc = None
    for c in range(w1_ref.shape[1] // FFN_CHUNK):
        cols = slice(c * FFN_CHUNK, (c + 1) * FFN_CHUNK)
        a = jnp.square(jnp.maximum(_dot(h, w1_ref[:, cols]), 0.0))
        part = _dot(a.astype(BF16), w2_ref[cols, :])
        acc = part if acc is None else acc + part
    o_ref[...] = x1 + acc


def _even_out_kernel(x_ref, att_ref, u_ref, vb_ref, ws_ref, bst_ref, w_ref, gf_ref, w1_ref, w2_ref, o_ref,
                     *, p_len):
    tm = x_ref.shape[0]
    cg = vb_ref.shape[1] // G_B
    rr = lax.broadcasted_iota(jnp.int32, (p_len, p_len), 0)
    cc = lax.broadcasted_iota(jnp.int32, (p_len, p_len), 1)
    causal = jnp.right_shift(cc, CHUNK_SHIFT) <= jnp.right_shift(rr, CHUNK_SHIFT)
    u = u_ref[...]
    vb = vb_ref[...].astype(BF16)
    bst = bst_ref[...]
    mixed_rows = []
    for c in range(tm // p_len):
        cols = []
        for g in range(G_B):
            wg = jnp.where(causal, ws_ref[g], 0.0).astype(BF16)
            mg = _dot(wg, vb[c * p_len:(c + 1) * p_len, g * cg:(g + 1) * cg]) + bst[:, g:g + 1]
            cols.append(mg)
        mixed_rows.append(jnp.concatenate(cols, axis=1))
    gated = u * jnp.concatenate(mixed_rows, axis=0)
    na = att_ref.shape[1]
    y = _dot(att_ref[...].astype(BF16), w_ref[:na, :]) + _dot(gated.astype(BF16), w_ref[na:, :])
    _ffn_tail(x_ref[...] + y, gf_ref, w1_ref, w2_ref, o_ref)


def _even_out_ffn(x, att, u, vb, ws, bst, w, gf, w1, w2, tm, p_len):
    n_rows, d = x.shape
    n = att.shape[1]
    row = lambda i: (i, 0)
    kern = functools.partial(_even_out_kernel, p_len=p_len)
    return pl.pallas_call(
        kern,
        grid=(n_rows // tm,),
        in_specs=[pl.BlockSpec((tm, d), row), pl.BlockSpec((tm, n), row), pl.BlockSpec((tm, n), row),
                  pl.BlockSpec((tm, n), row), _resident(ws), _resident(bst), _resident(w),
                  _resident(gf), _resident(w1), _resident(w2)],
        out_specs=pl.BlockSpec((tm, d), row),
        out_shape=jax.ShapeDtypeStruct((n_rows, d), F32),
        compiler_params=_cparams(("parallel",)),
        name="even_out_ffn",
    )(x, att, u, vb, ws, bst, _operand(w), gf, _operand(w1), _operand(w2))


def _odd_out_kernel(x_ref, att_ref, gb_ref, cin_ref, prev_ref, cw_ref, w_ref, gf_ref, w1_ref, w2_ref, o_ref,
                    *, t_len, has_prev):
    tm = x_ref.shape[0]
    prev = prev_ref[...]
    if not has_prev:
        start = (pl.program_id(0) * tm) % t_len == 0
        prev = jnp.where(start, 0.0, prev)
    cw = cw_ref[...]
    seg = t_len if has_prev else tm
    convs = []
    for s in range(tm // seg):
        cin = cin_ref[s * seg:(s + 1) * seg, :]
        ext = jnp.concatenate([prev[8 * s:8 * (s + 1)], cin], axis=0)
        conv = cw[CONV_W - 1:CONV_W] * cin
        for j in range(1, CONV_W):
            conv = conv + cw[CONV_W - 1 - j:CONV_W - j] * pltpu.roll(ext, j, 0)[8:]
        convs.append(conv)
    gated = gb_ref[...] * (convs[0] if len(convs) == 1 else jnp.concatenate(convs, axis=0))
    na = att_ref.shape[1]
    y = _dot(att_ref[...].astype(BF16), w_ref[:na, :]) + _dot(gated.astype(BF16), w_ref[na:, :])
    _ffn_tail(x_ref[...] + y, gf_ref, w1_ref, w2_ref, o_ref)


def _odd_out_ffn(x, att, gb, cin, prev, cw, w, gf, w1, w2, tm, t_len):
    n_rows, d = x.shape
    n = att.shape[1]
    row = lambda i: (i, 0)
    has_prev = prev is not None
    if has_prev:
        assert tm % t_len == 0 and t_len % 8 == 0
        prev_arr, prev_map, prev_rows = prev, row, 8 * (tm // t_len)
    else:
        assert t_len % tm == 0 and tm % 8 == 0
        prev_arr, prev_map, prev_rows = cin, (lambda i: (jnp.maximum(i * (tm // 8) - 1, 0), 0)), 8
    kern = functools.partial(_odd_out_kernel, t_len=t_len, has_prev=has_prev)
    return pl.pallas_call(
        kern,
        grid=(n_rows // tm,),
        in_specs=[pl.BlockSpec((tm, d), row), pl.BlockSpec((tm, n), row), pl.BlockSpec((tm, n), row),
                  pl.BlockSpec((tm, n), row), pl.BlockSpec((prev_rows, n), prev_map),
                  _resident(cw), _resident(w), _resident(gf), _resident(w1), _resident(w2)],
        out_specs=pl.BlockSpec((tm, d), row),
        out_shape=jax.ShapeDtypeStruct((n_rows, d), F32),
        compiler_params=_cparams(("parallel",)),
        name="odd_out_ffn",
    )(x, att, gb, cin, prev_arr, cw, _operand(w), gf, _operand(w1), _operand(w2))


def _row_tile(n_rows, cap):
    t = min(cap, n_rows)
    while n_rows % t:
        t //= 2
    return t


def _pair_heads(a, axis):
    shape = a.shape
    rep = shape[axis] // (2 * D_HEAD)
    a = a.reshape(shape[:axis] + (2, rep, D_HEAD) + shape[axis + 1:])
    return jnp.swapaxes(a, axis, axis + 1).reshape(shape)


def kernel(x_prompt, x_sample, cache_a_k, cache_a_v, cache_c_k, cache_c_v, cache_c_kidx, state_d_conv,
           norm_mix_g, norm_ffn_g, w_in_even, gq_a, gk_a, g_b, ws_b, bs_b, w_out_even,
           w_in_odd, gq_c, gk_c, conv_w_d, w_out_odd, w_ffn1, w_ffn2):
    bp, tp, d = x_prompt.shape
    bs, ts, _ = x_sample.shape
    depth = norm_mix_g.shape[0]
    past = cache_a_k.shape[2]
    n_a = cache_a_k.shape[3] * cache_a_k.shape[4]
    n_kv = cache_c_k.shape[3] * cache_c_k.shape[4]
    n_b = g_b.shape[1]
    n_d = conv_w_d.shape[2]
    n_q = w_out_odd.shape[1] - n_d
    assert n_a == 512 and n_b == 512 and n_kv == 128 and n_q == 512 and n_d == 512 and d == 1024

    xp = x_prompt.reshape(bp * tp, d)
    xs = x_sample.reshape(bs * ts, d)
    tm_p = _row_tile(bp * tp, 512)
    tm_in = _row_tile(tp, 1024)
    tm_s = _row_tile(bs * ts, 512)
    ls = past + ts

    hmean = jnp.asarray(np.kron(np.eye(n_a // D_HEAD), np.full((D_HEAD, D_HEAD), 1.0 / D_HEAD)), BF16)
    tile8 = lambda gvec: jnp.tile(gvec, n_a // D_HEAD).reshape(1, n_a)

    ak_p, av_p, ak_s, av_s, bv_s = [], [], [], [], []
    ck_p, cv_p, ci_p, ck_s, cv_s, ci_s, dc_p, dc_s = [], [], [], [], [], [], [], []

    tri = _stick_tri()

    channel_major = lambda a: jnp.transpose(a, (0, 1, 3, 4, 2)).reshape(a.shape[0], a.shape[1], -1, a.shape[2])
    past_a_k = channel_major(cache_a_k)
    past_a_v = channel_major(cache_a_v)
    past_c_k = channel_major(cache_c_k)
    past_c_v = channel_major(cache_c_v)
    past_c_ki = jnp.transpose(cache_c_kidx, (0, 1, 3, 2))

    w1_all, w2_all = w_ffn1.astype(BF16), w_ffn2.astype(BF16)
    w_in_even_all, w_out_even_all = w_in_even.astype(BF16), w_out_even.astype(BF16)
    offs = np.cumsum([0, n_q, n_kv, n_kv, H_IDX * D_IDX, D_IDX, H_IDX, n_d, n_d, n_d])
    col = lambda s: w_in_odd[:, :, int(offs[s]):int(offs[s + 1])]
    w_in_odd_all = jnp.concatenate(
        [_pair_heads(col(0), 2), col(1), col(2), col(3), col(4), col(4), col(5),
         jnp.zeros((w_in_odd.shape[0], d, LANES - H_IDX), F32), col(6), col(7), col(8)], axis=2).astype(BF16)
    assert w_in_odd_all.shape[2] == ODD_COLS
    w_out_odd_all = jnp.concatenate([_pair_heads(w_out_odd[:, :n_q], 1), w_out_odd[:, n_q:]], axis=1).astype(BF16)

    for i in range(depth):
        j = i // 2
        g_mix = norm_mix_g[i].reshape(1, d)
        ffn = (norm_ffn_g[i].reshape(1, d), (w1_all, i), (w2_all, i))
        if i % 2 == 0:
            w_in = (w_in_even_all, j)
            w_out = (w_out_even_all, j)
            gq, gk, gb = tile8(gq_a[j]), tile8(gk_a[j]), g_b[j].reshape(1, n_b)
            q, kt, vt, u, vb, kb, vm = _even_in(xp, g_mix, w_in, hmean, gq, gk, gb, tm_in, t_len=tp)
            att = _stick(q.reshape(bp, tp, n_a), kb.reshape(bp, tp, n_a), vm.reshape(bp, tp, 2 * n_a), tri,
                         min(tp, LANES), 0)
            xp = _even_out_ffn(xp, att.reshape(bp * tp, n_a), u, vb, ws_b[j], bs_b[j].T, w_out, *ffn, tm_p, B_CHUNK)
            ak_p.append(kt)
            av_p.append(vt)
            q, k, v, u, vb, _, _ = _even_in(xs, g_mix, w_in, hmean, gq, gk, gb, tm_s)
            new_rows = lambda a: jnp.pad(a.reshape(bs, ts, n_a), ((0, 0), (0, STICK_TK - ts), (0, 0)))
            att = _stick(q.reshape(bs, ts, n_a), (past_a_k, j), (past_a_v, j), tri, ts, past,
                         new=(new_rows(k), new_rows(v)))
            xs = _even_out_ffn(xs, att.reshape(bs * ts, n_a), u, vb, ws_b[j][:, :ts, :ts], bs_b[j][:, :ts].T,
                               w_out, *ffn, tm_s, ts)
            ak_s.append(k.reshape(bs, ts, n_a // D_HEAD, D_HEAD))
            av_s.append(v.reshape(bs, ts, n_a // D_HEAD, D_HEAD))
            bv_s.append(vb.reshape(bs, ts, n_b))
        else:
            w_in = (w_in_odd_all, j)
            w_out = (w_out_odd_all, j)
            gq, gk = tile8(gq_c[j]), jnp.tile(gk_c[j], n_kv // D_HEAD).reshape(1, n_kv)
            cw = conv_w_d[j]
            q, k, v, qi, ki, wi, gbt, cin, kt, vt, kit = _odd_in(xp, g_mix, w_in, hmean, gq, gk, tm_in, t_len=tp)
            r3 = lambda a: a.reshape(bp, tp, a.shape[-1])
            att = _dsa(r3(q), r3(qi), r3(wi), r3(k), r3(v), r3(ki), min(tp, 2 * LANES), 0, tp)
            xp = _odd_out_ffn(xp, att.reshape(bp * tp, n_q), gbt, cin, None, cw, w_out, *ffn, tm_p, tp)
            ck_p.append(kt)
            cv_p.append(vt)
            ci_p.append(kit)
            dc_p.append(cin.reshape(bp, tp, n_d)[:, tp - (CONV_W - 1):])
            q, k, v, qi, ki, wi, gbt, cin = _odd_in(xs, g_mix, w_in, hmean, gq, gk, tm_s)
            r3 = lambda a: a.reshape(bs, ts, a.shape[-1])
            pad_q = lambda a: jnp.pad(r3(a), ((0, 0), (0, LANES - ts), (0, 0)))
            new_rows = lambda a: jnp.pad(r3(a), ((0, 0), (0, DSA_CK - ts), (0, 0)))
            att = _dsa(pad_q(q), pad_q(qi), pad_q(wi), (past_c_k, j), (past_c_v, j), (past_c_ki, j),
                       LANES, past, ls, new=(new_rows(k), new_rows(v), new_rows(ki)))[:, :ts]
            prev = jnp.concatenate([jnp.zeros((bs, 8 - (CONV_W - 1), n_d), F32), state_d_conv[j]], axis=1)
            xs = _odd_out_ffn(xs, att.reshape(bs * ts, n_q), gbt, cin, prev.reshape(bs * 8, n_d), cw, w_out, *ffn,
                              tm_s, ts)
            ck_s.append(k.reshape(bs, ts, n_kv // D_HEAD, D_HEAD))
            cv_s.append(v.reshape(bs, ts, n_kv // D_HEAD, D_HEAD))
            ci_s.append(ki[:, :D_IDX].reshape(bs, ts, D_IDX))
            dc_s.append(cin.reshape(bs, ts, n_d)[:, ts - (CONV_W - 1):])

    def heads_last(parts):
        a = jnp.stack(parts)
        return jnp.transpose(a.reshape(a.shape[0], a.shape[1], -1, D_HEAD, a.shape[3]), (0, 1, 4, 2, 3))

    return (xp.reshape(bp, tp, d), xs.reshape(bs, ts, d),
            heads_last(ak_p), heads_last(av_p), jnp.stack(ak_s), jnp.stack(av_s),
            jnp.stack(bv_s),
            heads_last(ck_p), heads_last(cv_p), jnp.transpose(jnp.stack(ci_p), (0, 1, 3, 2)),
            jnp.stack(ck_s), jnp.stack(cv_s), jnp.stack(ci_s),
            jnp.stack(dc_p), jnp.stack(dc_s))
```
